```python
import math
import jax
import jax.numpy as jnp
from jax import lax
import numpy as np

D_MODEL = 4096
BATCH = 2
SEQ = 4096
DEPTH = 2

GRID_W = 64
CTX_LEN = 256

N_MIXERS = 4
MIX_WIDTH = D_MODEL
GROUP_WIDTH = MIX_WIDTH // N_MIXERS

NA_HEAD_DIM = 128
NA_HEADS = GROUP_WIDTH // NA_HEAD_DIM
NA_WIN_ROWS = 8
NA_WIN_COLS = 16

MLA_NOPE_DIM = 128
MLA_ROPE_DIM = 64
MLA_V_DIM = 128
MLA_QK_DIM = MLA_NOPE_DIM + MLA_ROPE_DIM
MLA_HEADS = GROUP_WIDTH // MLA_V_DIM
MLA_Q_RANK = 896
MLA_KV_RANK = 320
ROPE_THETA = 10000.0
Q_BLOCK = 128

CONV_WIDTH = GROUP_WIDTH
CONV_K = 3

S5_WIDTH = GROUP_WIDTH
S5_GROUP = 16
S5_GROUPS = S5_WIDTH // S5_GROUP
S5_STATE = 64
S5_DT_MIN = 0.001
S5_DT_MAX = 0.1

FFN_HIDDEN = ((8 * D_MODEL + 3 * 256 - 1) // (3 * 256)) * 256

IN_SPLITS = (GROUP_WIDTH, GROUP_WIDTH, GROUP_WIDTH,
             MLA_Q_RANK, MLA_KV_RANK, MLA_ROPE_DIM,
             CONV_WIDTH, CONV_WIDTH, CONV_WIDTH,
             S5_WIDTH)
IN_WIDTH = 3 * GROUP_WIDTH + MLA_Q_RANK + MLA_KV_RANK + MLA_ROPE_DIM + 3 * CONV_WIDTH + S5_WIDTH

NORM_EPS = 1e-6
NEG_INF = -1e30

kernel_name = "hybrid_dit_natten_mla_conv_s5"


def rms_norm(x, w):
    xf = x.astype(jnp.float32)
    y = xf * lax.rsqrt(jnp.mean(xf * xf, axis=-1, keepdims=True) + NORM_EPS)
    return (y * w.astype(jnp.float32)).astype(x.dtype)


def group_rms_norm(y, w):
    shp = y.shape
    yg = y.reshape(shp[:-1] + (N_MIXERS, shp[-1] // N_MIXERS)).astype(jnp.float32)
    yg = yg * lax.rsqrt(jnp.mean(yg * yg, axis=-1, keepdims=True) + NORM_EPS)
    return (yg.reshape(shp) * w.astype(jnp.float32)).astype(y.dtype)


def modulate(h, shift, scale):
    return h * (1.0 + scale) + shift


def split_cols(p):
    outs = []
    start = 0
    for width in IN_SPLITS:
        outs.append(p[..., start:start + width])
        start += width
    return outs


def flatten_heads(y):
    return y.reshape(y.shape[0], y.shape[1], -1)


def rope_tables(n_tokens, dtype):
    t = jnp.arange(n_tokens, dtype=jnp.int32)
    row = (t // GRID_W).astype(jnp.float32)
    col = (t % GRID_W).astype(jnp.float32)
    n_freq = MLA_ROPE_DIM // 4
    inv_freq = ROPE_THETA ** (-jnp.arange(n_freq, dtype=jnp.float32) / n_freq)
    ang_r = row[:, None] * inv_freq[None, :]
    ang_c = col[:, None] * inv_freq[None, :]
    return (jnp.cos(ang_r).astype(dtype), jnp.sin(ang_r).astype(dtype),
            jnp.cos(ang_c).astype(dtype), jnp.sin(ang_c).astype(dtype))


def rope_1d(x, cos, sin):
    half = x.shape[-1] // 2
    x1, x2 = x[..., :half], x[..., half:]
    cos, sin = cos[:, None, :], sin[:, None, :]
    return jnp.concatenate([x1 * cos - x2 * sin, x1 * sin + x2 * cos], axis=-1)


def rope_tail(x, tables):
    cos_r, sin_r, cos_c, sin_c = tables
    x_rope = x[..., MLA_NOPE_DIM:]
    half = MLA_ROPE_DIM // 2
    rot = jnp.concatenate([rope_1d(x_rope[..., :half], cos_r, sin_r),
                           rope_1d(x_rope[..., half:], cos_c, sin_c)], axis=-1)
    return jnp.concatenate([x[..., :MLA_NOPE_DIM], rot], axis=-1)


def softmax_attend(q, k, v, scale):
    s = jnp.einsum('bqhd,bkhd->bhqk', q, k).astype(jnp.float32) * scale
    p = jax.nn.softmax(s, axis=-1).astype(v.dtype)
    return jnp.einsum('bhqk,bkhd->bqhd', p, v)


def blocked_attention(q, k, v, scale):
    b, t, h, d = q.shape
    nb = t // Q_BLOCK
    qb = q.reshape(b, nb, Q_BLOCK, h, d).transpose(1, 0, 2, 3, 4)
    out = lax.map(lambda qi: softmax_attend(qi, k, v, scale), qb)
    return out.transpose(1, 0, 2, 3, 4).reshape(b, t, h * v.shape[-1])


def na_qkv(p, q_norm_w, k_norm_w):
    b, t = p[0].shape[:2]
    q = rms_norm(p[0].reshape(b, t, NA_HEADS, NA_HEAD_DIM), q_norm_w)
    k = rms_norm(p[1].reshape(b, t, NA_HEADS, NA_HEAD_DIM), k_norm_w)
    v = p[2].reshape(b, t, NA_HEADS, NA_HEAD_DIM)
    return q, k, v


def neighbourhood_attention(q, k, v, k_ctx, v_ctx, rpb):
    b, t, h, dh = q.shape
    rows = t // GRID_W
    win_r = min(NA_WIN_ROWS, rows)
    scale = dh ** -0.5
    qg = q.reshape(b, rows, GRID_W, h, dh)
    kg = k.reshape(b, rows, GRID_W, h, dh)
    vg = v.reshape(b, rows, GRID_W, h, dh)
    r = jnp.arange(rows)
    row_start = jnp.clip(r - win_r // 2, 0, rows - win_r)
    row_idx = row_start[:, None] + jnp.arange(win_r)[None, :]
    k_band = kg[:, row_idx]
    v_band = vg[:, row_idx]
    s_win = jnp.einsum('brqhd,brkwhd->bhrqkw', qg, k_band).astype(jnp.float32) * scale
    cq = jnp.arange(GRID_W)
    col_start = jnp.clip(cq - NA_WIN_COLS // 2, 0, GRID_W - NA_WIN_COLS)
    in_win = (cq[None, :] >= col_start[:, None]) & (cq[None, :] < col_start[:, None] + NA_WIN_COLS)
    d_row = row_idx - r[:, None] + (NA_WIN_ROWS - 1)
    d_col = jnp.clip(cq[None, :] - cq[:, None], -(NA_WIN_COLS - 1), NA_WIN_COLS - 1) + (NA_WIN_COLS - 1)
    bias = rpb[:, d_row[:, None, :, None], d_col[None, :, None, :]]
    s_win = jnp.where(in_win[:, None, :], s_win + bias.astype(jnp.float32), NEG_INF)
    s_ctx = jnp.einsum('brqhd,bjhd->bhrqj', qg, k_ctx).astype(jnp.float32) * scale
    n_win = win_r * GRID_W
    s_all = jnp.concatenate([s_win.reshape(b, h, rows, GRID_W, n_win), s_ctx], axis=-1)
    p = jax.nn.softmax(s_all, axis=-1).astype(v.dtype)
    p_win = p[..., :n_win].reshape(b, h, rows, GRID_W, win_r, GRID_W)
    p_ctx = p[..., n_win:]
    out = (jnp.einsum('bhrqkw,brkwhd->brqhd', p_win, v_band)
           + jnp.einsum('bhrqj,bjhd->brqhd', p_ctx, v_ctx))
    return out.reshape(b, t, h * dh)


def mla_qkv(p_cq, p_ckv, p_kr, cq_norm_w, ckv_norm_w, w_uq, w_ukv, q_norm_w, k_norm_w):
    b, t = p_cq.shape[:2]
    q = (rms_norm(p_cq, cq_norm_w) @ w_uq).reshape(b, t, MLA_HEADS, MLA_QK_DIM)
    kv = (rms_norm(p_ckv, ckv_norm_w) @ w_ukv).reshape(b, t, MLA_HEADS, MLA_NOPE_DIM + MLA_V_DIM)
    k_rope = jnp.broadcast_to(p_kr[:, :, None, :], (b, t, MLA_HEADS, MLA_ROPE_DIM))
    k = jnp.concatenate([kv[..., :MLA_NOPE_DIM], k_rope], axis=-1)
    return rms_norm(q, q_norm_w), rms_norm(k, k_norm_w), kv[..., MLA_NOPE_DIM:]


def depthwise_conv3(u, w, b):
    ch = u.shape[-1]
    y = lax.conv_general_dilated(u, w[:, None, :], window_strides=(1,),
                                 padding=((CONV_K // 2, CONV_K // 2),),
                                 dimension_numbers=('NWC', 'WIO', 'NWC'),
                                 feature_group_count=ch)
    return y + b


def short_conv_mixer(gate_b, gate_c, u, w, b):
    return gate_b * depthwise_conv3(gate_c * u, w, b)


def s5_discretise(lam_re, lam_im, log_dt, b_re, b_im):
    lam = lax.complex(lam_re.astype(jnp.float32), lam_im.astype(jnp.float32))
    dt = jnp.exp(log_dt.astype(jnp.float32))[:, None]
    lam_bar = jnp.exp(lam * dt)
    bmat = lax.complex(b_re.astype(jnp.float32), b_im.astype(jnp.float32))
    b_bar = ((lam_bar - 1.0) / lam)[..., None] * bmat
    return lam_bar, b_bar


def linear_scan(bu, lam_bar, h0):
    if h0 is not None:
        bu = bu.at[:, 0].add(lam_bar * h0)
    a = jnp.broadcast_to(lam_bar, bu.shape)

    def combine(e1, e2):
        a1, b1 = e1
        a2, b2 = e2
        return a1 * a2, a2 * b1 + b2

    _, h = lax.associative_scan(combine, (a, bu), axis=1)
    return h


def s5_mixer(u_lat, u_ctx, lam_re, lam_im, log_dt, b_re, b_im, c_re, c_im, d_skip,
             glu_w, glu_b, need_ctx):
    bsz, t_lat = u_lat.shape[:2]
    t_ctx = u_ctx.shape[1]
    ug_lat = u_lat.astype(jnp.float32).reshape(bsz, t_lat, S5_GROUPS, S5_GROUP)
    ug_ctx = u_ctx.astype(jnp.float32).reshape(bsz, t_ctx, S5_GROUPS, S5_GROUP)
    d_g = d_skip.astype(jnp.float32).reshape(S5_GROUPS, S5_GROUP)
    y_lat = ug_lat * d_g
    y_ctx = ug_ctx * d_g
    for direction in range(2):
        lam_bar, b_bar = s5_discretise(lam_re[direction], lam_im[direction], log_dt[direction],
                                       b_re[direction], b_im[direction])
        cmat = lax.complex(c_re[direction].astype(jnp.float32), c_im[direction].astype(jnp.float32))
        bu_ctx = jnp.einsum('gpi,btgi->btgp', b_bar, ug_ctx.astype(jnp.complex64))
        bu_lat = jnp.einsum('gpi,btgi->btgp', b_bar, ug_lat.astype(jnp.complex64))
        if direction == 1:
            bu_ctx = jnp.flip(bu_ctx, axis=1)
            bu_lat = jnp.flip(bu_lat, axis=1)
        h_ctx = linear_scan(bu_ctx, lam_bar, None)
        h_lat = linear_scan(bu_lat, lam_bar, h_ctx[:, -1])
        if direction == 1:
            h_ctx = jnp.flip(h_ctx, axis=1)
            h_lat = jnp.flip(h_lat, axis=1)
        y_lat = y_lat + jnp.real(jnp.einsum('gip,btgp->btgi', cmat, h_lat))
        if need_ctx:
            y_ctx = y_ctx + jnp.real(jnp.einsum('gip,btgp->btgi', cmat, h_ctx))

    def glu(y, n_tok):
        y = jax.nn.gelu(y.reshape(bsz, n_tok, S5_WIDTH))
        return (y * jax.nn.sigmoid(y @ glu_w.astype(jnp.float32) + glu_b.astype(jnp.float32))).astype(u_lat.dtype)

    out_ctx = glu(y_ctx, t_ctx) if need_ctx else None
    return glu(y_lat, t_lat), out_ctx


def swiglu(h, w1, w3, w2):
    return (jax.nn.silu(h @ w1) * (h @ w3)) @ w2


def setup_inputs(seed: int = 0) -> dict:
    key = jax.random.key(seed)
    ks = jax.random.split(key, 40)
    f32 = jnp.float32
    L, D = DEPTH, D_MODEL

    def nrm(k, shape, scale):
        return jax.random.normal(k, shape, f32) * scale

    n_idx = jnp.arange(S5_STATE, dtype=f32)
    return {
        'x': nrm(ks[0], (BATCH, SEQ, D), 1.0),
        'c': nrm(ks[1], (BATCH, D), 1.0),
        'ctx': nrm(ks[2], (BATCH, CTX_LEN, D), 1.0),
        'c_ctx': nrm(ks[3], (D,), 1.0),
        'ada_w': nrm(ks[4], (L, D, 6 * D), 0.5 * D ** -0.5),
        'ada_b': nrm(ks[5], (L, 6 * D), 0.02),
        'norm1_w': 1.0 + nrm(ks[6], (L, D), 0.05),
        'norm2_w': 1.0 + nrm(ks[7], (L, D), 0.05),
        'w_in': nrm(ks[8], (L, D, IN_WIDTH), D ** -0.5),
        'na_q_norm_w': 1.0 + nrm(ks[9], (L, NA_HEAD_DIM), 0.05),
        'na_k_norm_w': 1.0 + nrm(ks[10], (L, NA_HEAD_DIM), 0.05),
        'na_rpb': nrm(ks[11], (L, NA_HEADS, 2 * NA_WIN_ROWS - 1, 2 * NA_WIN_COLS - 1), 0.2),
        'mla_cq_norm_w': 1.0 + nrm(ks[12], (L, MLA_Q_RANK), 0.05),
        'mla_ckv_norm_w': 1.0 + nrm(ks[13], (L, MLA_KV_RANK), 0.05),
        'mla_w_uq': nrm(ks[14], (L, MLA_Q_RANK, MLA_HEADS * MLA_QK_DIM), MLA_Q_RANK ** -0.5),
        'mla_w_ukv': nrm(ks[15], (L, MLA_KV_RANK, MLA_HEADS * (MLA_NOPE_DIM + MLA_V_DIM)), MLA_KV_RANK ** -0.5),
        'mla_q_norm_w': 1.0 + nrm(ks[16], (L, MLA_QK_DIM), 0.05),
        'mla_k_norm_w': 1.0 + nrm(ks[17], (L, MLA_QK_DIM), 0.05),
        'conv_w': nrm(ks[18], (L, CONV_K, CONV_WIDTH), CONV_K ** -0.5),
        'conv_b': nrm(ks[19], (L, CONV_WIDTH), 0.02),
        's5_lambda_re': -0.5 + nrm(ks[20], (L, 2, S5_GROUPS, S5_STATE), 0.01),
        's5_lambda_im': math.pi * n_idx + nrm(ks[21], (L, 2, S5_GROUPS, S5_STATE), 0.01),
        's5_log_dt': jax.random.uniform(ks[22], (L, 2, S5_GROUPS), f32,
                                        math.log(S5_DT_MIN), math.log(S5_DT_MAX)),
        's5_b_re': nrm(ks[23], (L, 2, S5_GROUPS, S5_STATE, S5_GROUP), (2 * S5_GROUP) ** -0.5),
        's5_b_im': nrm(ks[24], (L, 2, S5_GROUPS, S5_STATE, S5_GROUP), (2 * S5_GROUP) ** -0.5),
        's5_c_re': nrm(ks[25], (L, 2, S5_GROUPS, S5_GROUP, S5_STATE), S5_STATE ** -0.5),
        's5_c_im': nrm(ks[26], (L, 2, S5_GROUPS, S5_GROUP, S5_STATE), S5_STATE ** -0.5),
        's5_d': nrm(ks[27], (L, S5_WIDTH), 1.0),
        's5_glu_w': nrm(ks[28], (L, S5_WIDTH, S5_WIDTH), S5_WIDTH ** -0.5),
        's5_glu_b': nrm(ks[29], (L, S5_WIDTH), 0.02),
        'mix_norm_w': 1.0 + nrm(ks[30], (L, MIX_WIDTH), 0.05),
        'w_out': nrm(ks[31], (L, MIX_WIDTH, D), MIX_WIDTH ** -0.5),
        'ffn_w1': nrm(ks[32], (L, D, FFN_HIDDEN), D ** -0.5),
        'ffn_w3': nrm(ks[33], (L, D, FFN_HIDDEN), D ** -0.5),
        'ffn_w2': nrm(ks[34], (L, FFN_HIDDEN, D), FFN_HIDDEN ** -0.5),
    }


def reference(x, c, ctx, c_ctx, ada_w, ada_b, norm1_w, norm2_w, w_in,
              na_q_norm_w, na_k_norm_w, na_rpb,
              mla_cq_norm_w, mla_ckv_norm_w, mla_w_uq, mla_w_ukv, mla_q_norm_w, mla_k_norm_w,
              conv_w, conv_b,
              s5_lambda_re, s5_lambda_im, s5_log_dt, s5_b_re, s5_b_im, s5_c_re, s5_c_im,
              s5_d, s5_glu_w, s5_glu_b,
              mix_norm_w, w_out, ffn_w1, ffn_w3, ffn_w2):
    rope = rope_tables(x.shape[1], x.dtype)
    h_lat, h_ctx = x, ctx
    for layer in range(DEPTH):
        last = layer == DEPTH - 1
        mod_lat = jnp.split((jax.nn.silu(c) @ ada_w[layer] + ada_b[layer])[:, None, :], 6, axis=-1)
        mod_ctx = jnp.split(jax.nn.silu(c_ctx) @ ada_w[layer] + ada_b[layer], 6, axis=-1)

        a_lat = modulate(rms_norm(h_lat, norm1_w[layer]), mod_lat[0], mod_lat[1])
        a_ctx = modulate(rms_norm(h_ctx, norm1_w[layer]), mod_ctx[0], mod_ctx[1])
        pl = split_cols(a_lat @ w_in[layer])
        pc = split_cols(a_ctx @ w_in[layer])

        qa_l, ka_l, va_l = na_qkv(pl, na_q_norm_w[layer], na_k_norm_w[layer])
        qa_c, ka_c, va_c = na_qkv(pc, na_q_norm_w[layer], na_k_norm_w[layer])
        ya_l = neighbourhood_attention(qa_l, ka_l, va_l, ka_c, va_c, na_rpb[layer])

        mla_p = (mla_cq_norm_w[layer], mla_ckv_norm_w[layer], mla_w_uq[layer], mla_w_ukv[layer],
                 mla_q_norm_w[layer], mla_k_norm_w[layer])
        qb_l, kb_l, vb_l = mla_qkv(pl[3], pl[4], pl[5], *mla_p)
        qb_l = rope_tail(qb_l, rope)
        kb_l = rope_tail(kb_l, rope)
        qb_c, kb_c, vb_c = mla_qkv(pc[3], pc[4], pc[5], *mla_p)
        yb_l = blocked_attention(qb_l, jnp.concatenate([kb_c, kb_l], axis=1),
                                 jnp.concatenate([vb_c, vb_l], axis=1), MLA_QK_DIM ** -0.5)

        yc_l = short_conv_mixer(pl[6], pl[7], pl[8], conv_w[layer], conv_b[layer])

        yd_l, yd_c = s5_mixer(pl[9], pc[9], s5_lambda_re[layer], s5_lambda_im[layer], s5_log_dt[layer],
                              s5_b_re[layer], s5_b_im[layer], s5_c_re[layer], s5_c_im[layer],
                              s5_d[layer], s5_glu_w[layer], s5_glu_b[layer], not last)

        y_lat = jnp.concatenate([ya_l, yb_l, yc_l, yd_l], axis=-1)
        h_lat = h_lat + mod_lat[2] * (group_rms_norm(y_lat, mix_norm_w[layer]) @ w_out[layer])

        if not last:
            ya_c = flatten_heads(softmax_attend(qa_c, ka_c, va_c, NA_HEAD_DIM ** -0.5))
            yb_c = flatten_heads(softmax_attend(qb_c, kb_c, vb_c, MLA_QK_DIM ** -0.5))
            yc_c = short_conv_mixer(pc[6], pc[7], pc[8], conv_w[layer], conv_b[layer])
            y_ctx = jnp.concatenate([ya_c, yb_c, yc_c, yd_c], axis=-1)
            h_ctx = h_ctx + mod_ctx[2] * (group_rms_norm(y_ctx, mix_norm_w[layer]) @ w_out[layer])

        f_lat = modulate(rms_norm(h_lat, norm2_w[layer]), mod_lat[3], mod_lat[4])
        h_lat = h_lat + mod_lat[5] * swiglu(f_lat, ffn_w1[layer], ffn_w3[layer], ffn_w2[layer])
        if not last:
            f_ctx = modulate(rms_norm(h_ctx, norm2_w[layer]), mod_ctx[3], mod_ctx[4])
            h_ctx = h_ctx + mod_ctx[5] * swiglu(f_ctx, ffn_w1[layer], ffn_w3[layer], ffn_w2[layer])
    return h_lat
```

```python
import functools
import math

import jax
import jax.numpy as jnp
from jax import lax
from jax.experimental import pallas as pl
from jax.experimental.pallas import tpu as pltpu

F32 = jnp.float32
BF16 = jnp.bfloat16

NORM_EPS = 1e-6
NEG_INF = -1e30
GRID_W = 64

GROUP_WIDTH = 1024
N_HEADS = 8
NA_HEAD_DIM = 128
NA_WIN_ROWS = 8
NA_WIN_COLS = 16
NA_Q_ROWS = 4
NA_K_ROWS = NA_Q_ROWS + NA_WIN_ROWS

MLA_NOPE = 128
MLA_ROPE = 64
MLA_QK = MLA_NOPE + MLA_ROPE
MLA_V = 128
MLA_Q_RANK = 896
MLA_KV_RANK = 320
MLA_KV_RANK_PAD = 512
MLA_HEAD_PAD = 256
ROPE_THETA = 10000.0

CONV_K = 3

S5_GROUPS = 64
S5_GROUP = 16
S5_STATE = 64
S5_CHUNK = 16
S5_BLOCK = S5_CHUNK * S5_GROUP

FFN_HIDDEN = 11008
FFN_HIDDEN_PAD = 11264

COL_NA = 0
COL_CONV = 3072
COL_S5 = 6144
COL_CQ = 7168
COL_KR = 8064
COL_CKV = 8192
IN_WIDTH_PAD = 8704

LANE = 128
VMEM_LIMIT = 56 * 1024 * 1024


def _cparams(sem, vmem=None):
    return pltpu.CompilerParams(dimension_semantics=sem, vmem_limit_bytes=vmem)


def _sigmoid(x):
    return 1.0 / (1.0 + jnp.exp(-x))


def _rms(x, w):
    return x * lax.rsqrt(jnp.mean(x * x, axis=-1, keepdims=True) + NORM_EPS) * w


def _dot(a, b):
    return jnp.dot(a, b, preferred_element_type=F32)


def _dot_nt(a, b):
    return lax.dot_general(a, b, (((1,), (1,)), ((), ())), preferred_element_type=F32)


def _dot_f32(a, b):
    return jnp.dot(a, b, preferred_element_type=F32, precision=lax.Precision.HIGHEST)


def _ada_kernel(c_ref, w_ref, b_ref, o_ref):
    c = c_ref[...]
    s = (c * _sigmoid(c)).astype(BF16)
    o_ref[...] = _dot(s, w_ref[...].astype(BF16)) + b_ref[...]


def _ada_mod(cvec, ada_w, ada_b):
    n_layers, d, n = ada_w.shape
    tn = 512
    out = pl.pallas_call(
        _ada_kernel,
        grid=(n_layers, n // tn),
        in_specs=[pl.BlockSpec((8, d), lambda l, j: (0, 0)),
                  pl.BlockSpec((None, d, tn), lambda l, j: (l, 0, j)),
                  pl.BlockSpec((None, 1, tn), lambda l, j: (l, 0, j))],
        out_specs=pl.BlockSpec((None, 8, tn), lambda l, j: (l, 0, j)),
        out_shape=jax.ShapeDtypeStruct((n_layers, 8, n), F32),
        compiler_params=_cparams(("arbitrary", "arbitrary"), 40 * 1024 * 1024),
        name="ada_mod",
    )(cvec, ada_w, ada_b.reshape(n_layers, 1, n))
    return out.reshape(n_layers, 8, 6, d).transpose(0, 2, 1, 3)[:, :, :, None, :]


def _normmod_kernel(h_ref, w_ref, shift_ref, scale_ref, o_ref):
    y = _rms(h_ref[...], w_ref[...])
    o_ref[...] = (y * (1.0 + scale_ref[...]) + shift_ref[...]).astype(o_ref.dtype)


def _norm_modulate(h, w, mod, k_shift, k_scale, geo, n_tiles):
    n_tok, d = h.shape
    tm = geo["tm"]
    cls = geo["tile_class"]
    return pl.pallas_call(
        _normmod_kernel,
        grid=(n_tiles,),
        in_specs=[pl.BlockSpec((tm, d), lambda i: (i, 0)),
                  pl.BlockSpec((1, d), lambda i: (0, 0)),
                  pl.BlockSpec((None, None, 1, d), lambda i: (k_shift, cls(i), 0, 0)),
                  pl.BlockSpec((None, None, 1, d), lambda i: (k_scale, cls(i), 0, 0))],
        out_specs=pl.BlockSpec((tm, d), lambda i: (i, 0)),
        out_shape=jax.ShapeDtypeStruct((n_tok, d), BF16),
        compiler_params=_cparams(("arbitrary",), 48 * 1024 * 1024),
        name="norm_modulate",
    )(h, w.reshape(1, d), mod, mod)


def _mm_kernel(a_ref, w_ref, o_ref):
    o_ref[...] = _dot(a_ref[...], w_ref[...]).astype(o_ref.dtype)


def _matmul(a, w, tm, tn, out_dtype):
    m, k = a.shape
    n = w.shape[1]
    return pl.pallas_call(
        _mm_kernel,
        grid=(n // tn, m // tm),
        in_specs=[pl.BlockSpec((tm, k), lambda j, i: (i, 0)),
                  pl.BlockSpec((k, tn), lambda j, i: (0, j))],
        out_specs=pl.BlockSpec((tm, tn), lambda j, i: (i, j)),
        out_shape=jax.ShapeDtypeStruct((m, n), out_dtype),
        compiler_params=_cparams(("arbitrary", "arbitrary"), 40 * 1024 * 1024),
        name="matmul",
    )(a, w)


def _mm_res_kernel(a_ref, w_ref, h_ref, g_ref, o_ref):
    o_ref[...] = h_ref[...] + g_ref[...] * _dot(a_ref[...], w_ref[...])


def _matmul_gated_residual(a, w, h, mod, k_gate, geo, n_tiles, tn):
    m, k = a.shape
    n = w.shape[1]
    tm = geo["tm"]
    cls = geo["tile_class"]
    return pl.pallas_call(
        _mm_res_kernel,
        grid=(n // tn, n_tiles),
        in_specs=[pl.BlockSpec((tm, k), lambda j, i: (i, 0)),
                  pl.BlockSpec((k, tn), lambda j, i: (0, j)),
                  pl.BlockSpec((tm, tn), lambda j, i: (i, j)),
                  pl.BlockSpec((None, None, 1, tn), lambda j, i: (k_gate, cls(i), 0, j))],
        out_specs=pl.BlockSpec((tm, tn), lambda j, i: (i, j)),
        out_shape=jax.ShapeDtypeStruct((m, n), F32),
        compiler_params=_cparams(("arbitrary", "arbitrary"), VMEM_LIMIT),
        name="matmul_gated_residual",
    )(a, w, h, mod)


def _ffn1_kernel(a_ref, w1_ref, w3_ref, o_ref):
    a = a_ref[...]
    u = _dot(a, w1_ref[...])
    v = _dot(a, w3_ref[...])
    o_ref[...] = (u * _sigmoid(u) * v).astype(o_ref.dtype)


def _ffn_in(a, w1, w3, geo, n_tiles, tn):
    m, k = a.shape
    n = w1.shape[1]
    tm = geo["tm"]
    return pl.pallas_call(
        _ffn1_kernel,
        grid=(n // tn, n_tiles),
        in_specs=[pl.BlockSpec((tm, k), lambda j, i: (i, 0)),
                  pl.BlockSpec((k, tn), lambda j, i: (0, j)),
                  pl.BlockSpec((k, tn), lambda j, i: (0, j))],
        out_specs=pl.BlockSpec((tm, tn), lambda j, i: (i, j)),
        out_shape=jax.ShapeDtypeStruct((m, n), BF16),
        compiler_params=_cparams(("arbitrary", "arbitrary"), 48 * 1024 * 1024),
        name="ffn_in",
    )(a, w1, w3)


def _rope_tail(t, cos_ref, sina_ref, sinb_ref):
    return (t * cos_ref[...] + pltpu.roll(t, 96, 1) * sina_ref[...]
            + pltpu.roll(t, 32, 1) * sinb_ref[...])


def _prep_kernel(na_ref, cq_ref, kr_ref, ckv_ref, naqw_ref, nakw_ref, cqw_ref, ckvw_ref,
                 wuq_ref, wukv_ref, mqw_ref, mkw_ref, cos_ref, sina_ref, sinb_ref,
                 naq_ref, nak_ref, nav_ref, mq_ref, mk_ref, mv_ref):
    hd = NA_HEAD_DIM
    for h in range(N_HEADS):
        naq_ref[:, h * hd:(h + 1) * hd] = _rms(na_ref[:, h * hd:(h + 1) * hd], naqw_ref[...]).astype(BF16)
        nak_ref[:, h * hd:(h + 1) * hd] = _rms(
            na_ref[:, GROUP_WIDTH + h * hd:GROUP_WIDTH + (h + 1) * hd], nakw_ref[...]).astype(BF16)
    nav_ref[...] = na_ref[:, 2 * GROUP_WIDTH:].astype(BF16)

    cq = _rms(cq_ref[...], cqw_ref[...]).astype(BF16)
    q = _dot(cq, wuq_ref[...])
    inv_qk = 1.0 / MLA_QK
    mqw = mqw_ref[...]
    for h in range(N_HEADS):
        o = h * MLA_HEAD_PAD
        nope = q[:, o:o + MLA_NOPE]
        tail = q[:, o + MLA_NOPE:o + MLA_HEAD_PAD]
        ss = jnp.sum(nope * nope, axis=-1, keepdims=True) + jnp.sum(tail * tail, axis=-1, keepdims=True)
        r = lax.rsqrt(ss * inv_qk + NORM_EPS)
        mq_ref[:, o:o + MLA_NOPE] = (nope * r * mqw[:, :MLA_NOPE]).astype(BF16)
        mq_ref[:, o + MLA_NOPE:o + MLA_HEAD_PAD] = _rope_tail(
            tail * r * mqw[:, MLA_NOPE:], cos_ref, sina_ref, sinb_ref).astype(BF16)

    ckv = ckv_ref[...]
    ckv_ms = jnp.sum(ckv * ckv, axis=-1, keepdims=True) * (1.0 / MLA_KV_RANK)
    ckv_n = (ckv * lax.rsqrt(ckv_ms + NORM_EPS) * ckvw_ref[...]).astype(BF16)
    kv = _dot(ckv_n, wukv_ref[...])
    mv_ref[...] = kv[:, GROUP_WIDTH:].astype(BF16)
    kr = kr_ref[...]
    kr_ss = jnp.sum(kr * kr, axis=-1, keepdims=True)
    mkw = mkw_ref[...]
    for h in range(N_HEADS):
        o = h * MLA_HEAD_PAD
        kn = kv[:, h * MLA_NOPE:(h + 1) * MLA_NOPE]
        r = lax.rsqrt((jnp.sum(kn * kn, axis=-1, keepdims=True) + kr_ss) * inv_qk + NORM_EPS)
        mk_ref[:, o:o + MLA_NOPE] = (kn * r * mkw[:, :MLA_NOPE]).astype(BF16)
        mk_ref[:, o + MLA_NOPE:o + MLA_HEAD_PAD] = _rope_tail(
            kr * r * mkw[:, MLA_NOPE:], cos_ref, sina_ref, sinb_ref).astype(BF16)


def _prep_qkv(p, lw, rope, tm):
    n_tok = p.shape[0]
    blk = lambda width, col: pl.BlockSpec((tm, width), lambda i: (i, col // width))
    full = lambda a: pl.BlockSpec(a.shape, lambda i: (0,) * a.ndim)
    tab = pl.BlockSpec((tm, LANE), lambda i: (i, 0))
    out_w = [GROUP_WIDTH, GROUP_WIDTH, GROUP_WIDTH, N_HEADS * MLA_HEAD_PAD, N_HEADS * MLA_HEAD_PAD, GROUP_WIDTH]
    weights = [lw["na_q_norm_w"], lw["na_k_norm_w"], lw["mla_cq_norm_w"], lw["mla_ckv_norm_w"],
               lw["mla_w_uq"], lw["mla_w_ukv"], lw["mla_q_norm_w"], lw["mla_k_norm_w"]]
    return pl.pallas_call(
        _prep_kernel,
        grid=(n_tok // tm,),
        in_specs=[blk(3 * GROUP_WIDTH, COL_NA), blk(MLA_Q_RANK, COL_CQ),
                  blk(LANE, COL_KR), blk(MLA_KV_RANK_PAD, COL_CKV)]
        + [full(a) for a in weights] + [tab, tab, tab],
        out_specs=[pl.BlockSpec((tm, w), lambda i: (i, 0)) for w in out_w],
        out_shape=[jax.ShapeDtypeStruct((n_tok, w), BF16) for w in out_w],
        compiler_params=_cparams(("arbitrary",), 48 * 1024 * 1024),
        name="prep_qkv",
    )(p, p, p, p, *weights, *rope)


def _softmax_attend(q, pairs, scale, bias=None):
    scores = []
    for idx, (k, _) in enumerate(pairs):
        s = _dot_nt(q, k) * scale
        if idx == 0 and bias is not None:
            s = s + bias
        scores.append(s)
    m = scores[0].max(axis=-1, keepdims=True)
    for s in scores[1:]:
        m = jnp.maximum(m, s.max(axis=-1, keepdims=True))
    acc = None
    den = None
    for s, (_, v) in zip(scores, pairs):
        e = jnp.exp(s - m)
        d = jnp.sum(e, axis=-1, keepdims=True)
        o = _dot(e.astype(BF16), v)
        acc = o if acc is None else acc + o
        den = d if den is None else den + d
    return acc / den


def _na_kernel(q_ref, k_ref, v_ref, kc_ref, vc_ref, bias_ref, o_ref, *, n_row_blocks, n_rows):
    rb = pl.program_id(2)
    scale = NA_HEAD_DIM ** -0.5

    @pl.when(rb < n_row_blocks)
    def _():
        ks = jnp.clip(rb * NA_Q_ROWS - NA_WIN_ROWS // 2, 0, n_rows - NA_K_ROWS) * GRID_W
        ks = pl.multiple_of(ks, GRID_W)
        kw = k_ref[pl.ds(ks, NA_K_ROWS * GRID_W), :]
        vw = v_ref[pl.ds(ks, NA_K_ROWS * GRID_W), :]
        o_ref[...] = _softmax_attend(q_ref[...], [(kw, vw), (kc_ref[...], vc_ref[...])], scale, bias_ref[...])

    @pl.when(rb == n_row_blocks)
    def _():
        o_ref[...] = _softmax_attend(q_ref[...], [(kc_ref[...], vc_ref[...])], scale)


def _na_attention(naq, nak, nav, bias, geo, with_ctx):
    b, t, ctx = geo["b"], geo["t"], geo["ctx"]
    n_tok = naq.shape[0]
    tq = NA_Q_ROWS * GRID_W
    assert ctx == tq
    n_rows = t // GRID_W
    nrb = n_rows // NA_Q_ROWS
    lat_blocks = b * t // tq
    ctx_block0 = b * t // ctx

    def q_map(bi, h, r):
        return (jnp.where(r < nrb, bi * nrb + r, lat_blocks + bi), h)

    def bias_map(bi, h, r):
        return (h, jnp.where(r == 0, 0, jnp.where(r >= nrb - 1, 2, 1)), 0, 0)

    return pl.pallas_call(
        functools.partial(_na_kernel, n_row_blocks=nrb, n_rows=n_rows),
        grid=(b, N_HEADS, nrb + (1 if with_ctx else 0)),
        in_specs=[pl.BlockSpec((tq, NA_HEAD_DIM), q_map),
                  pl.BlockSpec((t, NA_HEAD_DIM), lambda bi, h, r: (bi, h)),
                  pl.BlockSpec((t, NA_HEAD_DIM), lambda bi, h, r: (bi, h)),
                  pl.BlockSpec((ctx, NA_HEAD_DIM), lambda bi, h, r: (ctx_block0 + bi, h)),
                  pl.BlockSpec((ctx, NA_HEAD_DIM), lambda bi, h, r: (ctx_block0 + bi, h)),
                  pl.BlockSpec((None, None, tq, NA_K_ROWS * GRID_W), bias_map)],
        out_specs=pl.BlockSpec((tq, NA_HEAD_DIM), q_map),
        out_shape=jax.ShapeDtypeStruct((n_tok, GROUP_WIDTH), F32),
        compiler_params=_cparams(("arbitrary", "arbitrary", "arbitrary"), 40 * 1024 * 1024),
        name="na_attention",
    )(naq, nak, nav, nak, nav, bias)


def _mla_kernel(q_ref, k_ref, v_ref, kc_ref, vc_ref, o_ref, *, n_q_blocks):
    i = pl.program_id(2)
    scale = MLA_QK ** -0.5

    @pl.when(i < n_q_blocks)
    def _():
        o_ref[...] = _softmax_attend(q_ref[...], [(k_ref[...], v_ref[...]), (kc_ref[...], vc_ref[...])], scale)

    @pl.when(i == n_q_blocks)
    def _():
        o_ref[...] = _softmax_attend(q_ref[...], [(kc_ref[...], vc_ref[...])], scale)


def _mla_attention(mq, mk, mv, geo, with_ctx):
    b, t, ctx = geo["b"], geo["t"], geo["ctx"]
    n_tok = mq.shape[0]
    tq = ctx
    nqb = t // tq
    lat_blocks = b * t // tq
    ctx_block0 = b * t // ctx

    def q_map(bi, h, i):
        return (jnp.where(i < nqb, bi * nqb + i, lat_blocks + bi), h)

    return pl.pallas_call(
        functools.partial(_mla_kernel, n_q_blocks=nqb),
        grid=(b, N_HEADS, nqb + (1 if with_ctx else 0)),
        in_specs=[pl.BlockSpec((tq, MLA_HEAD_PAD), q_map),
                  pl.BlockSpec((t, MLA_HEAD_PAD), lambda bi, h, i: (bi, h)),
                  pl.BlockSpec((t, MLA_V), lambda bi, h, i: (bi, h)),
                  pl.BlockSpec((ctx, MLA_HEAD_PAD), lambda bi, h, i: (ctx_block0 + bi, h)),
                  pl.BlockSpec((ctx, MLA_V), lambda bi, h, i: (ctx_block0 + bi, h))],
        out_specs=pl.BlockSpec((tq, MLA_V), q_map),
        out_shape=jax.ShapeDtypeStruct((n_tok, GROUP_WIDTH), F32),
        compiler_params=_cparams(("arbitrary", "arbitrary", "arbitrary"), 48 * 1024 * 1024),
        name="mla_attention",
    )(mq, mk, mv, mk, mv)


def _s5_contrib_kernel(u_ref, b_ref, o_ref):
    r = _dot_f32(u_ref[:, :S5_BLOCK], b_ref[0]) + _dot_f32(u_ref[:, S5_BLOCK:], b_ref[1])
    for k in range(4):
        o_ref[k] = r[:, k * LANE:(k + 1) * LANE]


def _s5_scan_kernel(c_ref, lr_ref, li_ref, x_ref, *, n_tiles, n_ctx_tiles, nb):
    sub = 8
    lanes = c_ref.shape[-1]
    lr = jnp.broadcast_to(lr_ref[...], (sub, lanes))
    li = jnp.broadcast_to(li_ref[...], (sub, lanes))
    row = lax.broadcasted_iota(jnp.int32, (sub, lanes), 0)

    def run(reverse):
        order = range(sub - 1, -1, -1) if reverse else range(sub)

        def body(jt, carry):
            if reverse:
                tile = jnp.where(jt < n_ctx_tiles, n_ctx_tiles - 1 - jt, n_tiles - 1 - (jt - n_ctx_tiles))
            else:
                tile = jt
            new = []
            for bi in range(nb):
                xr, xi = carry[2 * bi], carry[2 * bi + 1]
                r0 = pl.multiple_of((bi * n_tiles + tile) * sub, sub)
                cr = c_ref[0, pl.ds(r0, sub), :]
                ci = c_ref[1, pl.ds(r0, sub), :]
                out_r = jnp.zeros_like(cr)
                out_i = jnp.zeros_like(ci)
                for k in order:
                    out_r = jnp.where(row == k, xr, out_r)
                    out_i = jnp.where(row == k, xi, out_i)
                    ck_r = jnp.broadcast_to(cr[k:k + 1, :], (sub, lanes))
                    ck_i = jnp.broadcast_to(ci[k:k + 1, :], (sub, lanes))
                    xr, xi = lr * xr - li * xi + ck_r, lr * xi + li * xr + ck_i
                x_ref[0, pl.ds(r0, sub), :] = out_r
                x_ref[1, pl.ds(r0, sub), :] = out_i
                new += [xr, xi]
            return tuple(new)

        zero = jnp.zeros((sub, lanes), F32)
        lax.fori_loop(0, n_tiles, body, (zero,) * (2 * nb))

    @pl.when(pl.program_id(0) == 0)
    def _():
        run(False)

    @pl.when(pl.program_id(0) == 1)
    def _():
        run(True)


def _s5_out_kernel(u_ref, x_ref, t_ref, c_ref, o_ref):
    x = jnp.concatenate([x_ref[0], x_ref[1], x_ref[2], x_ref[3]], axis=-1)
    y = _dot_f32(x, c_ref[...])
    o_ref[:, :S5_BLOCK] = _dot_f32(u_ref[:, :S5_BLOCK], t_ref[0]) + y[:, :S5_BLOCK]
    o_ref[:, S5_BLOCK:] = _dot_f32(u_ref[:, S5_BLOCK:], t_ref[1]) + y[:, S5_BLOCK:]


def _s5_mixer(p, ops, geo):
    b, t, ctx = geo["b"], geo["t"], geo["ctx"]
    seq = ctx + t
    n_chunks = seq // S5_CHUNK
    rows = n_chunks * b
    n_pairs = S5_GROUPS // 2
    state_w = S5_GROUPS * S5_STATE
    pair_w = 2 * S5_BLOCK

    u = p[:, COL_S5:COL_S5 + GROUP_WIDTH]
    useq = jnp.concatenate([u[b * t:].reshape(b, ctx, GROUP_WIDTH), u[:b * t].reshape(b, t, GROUP_WIDTH)], axis=1)
    u2 = useq.reshape(b, n_chunks, S5_CHUNK, S5_GROUPS, S5_GROUP).transpose(0, 1, 3, 2, 4)
    u2 = u2.reshape(rows, S5_GROUPS * S5_BLOCK)
    sub = 8
    assert n_chunks % sub == 0 and (ctx // S5_CHUNK) % sub == 0

    contrib = pl.pallas_call(
        _s5_contrib_kernel,
        grid=(n_pairs,),
        in_specs=[pl.BlockSpec((rows, pair_w), lambda q: (0, q)),
                  pl.BlockSpec((None, 2, S5_BLOCK, pair_w), lambda q: (q, 0, 0, 0))],
        out_specs=pl.BlockSpec((4, rows, LANE), lambda q: (0, 0, q)),
        out_shape=jax.ShapeDtypeStruct((4, rows, state_w), F32),
        compiler_params=_cparams(("arbitrary",), 40 * 1024 * 1024),
        name="s5_contrib",
    )(u2, ops["b_pair"])

    lane_blk = 512
    states = pl.pallas_call(
        functools.partial(_s5_scan_kernel, n_tiles=n_chunks // sub, n_ctx_tiles=ctx // S5_CHUNK // sub, nb=b),
        grid=(2, state_w // lane_blk),
        in_specs=[pl.BlockSpec((None, 2, rows, lane_blk), lambda d, l: (d, 0, 0, l)),
                  pl.BlockSpec((None, 1, lane_blk), lambda d, l: (d, 0, l)),
                  pl.BlockSpec((None, 1, lane_blk), lambda d, l: (d, 0, l))],
        out_specs=pl.BlockSpec((None, 2, rows, lane_blk), lambda d, l: (d, 0, 0, l)),
        out_shape=jax.ShapeDtypeStruct((2, 2, rows, state_w), F32),
        compiler_params=_cparams(("arbitrary", "arbitrary"), 40 * 1024 * 1024),
        name="s5_scan",
    )(contrib.reshape(2, 2, rows, state_w), ops["decay_re"], ops["decay_im"])

    y2 = pl.pallas_call(
        _s5_out_kernel,
        grid=(n_pairs,),
        in_specs=[pl.BlockSpec((rows, pair_w), lambda q: (0, q)),
                  pl.BlockSpec((4, rows, LANE), lambda q: (0, 0, q)),
                  pl.BlockSpec((2, S5_BLOCK, S5_BLOCK), lambda q: (q, 0, 0)),
                  pl.BlockSpec((None, pair_w, pair_w), lambda q: (q, 0, 0))],
        out_specs=pl.BlockSpec((rows, pair_w), lambda q: (0, q)),
        out_shape=jax.ShapeDtypeStruct((rows, S5_GROUPS * S5_BLOCK), F32),
        compiler_params=_cparams(("arbitrary",), 40 * 1024 * 1024),
        name="s5_out",
    )(u2, states.reshape(4, rows, state_w), ops["t_sum"], ops["c_pair"])

    y = y2.reshape(b, n_chunks, S5_GROUPS, S5_CHUNK, S5_GROUP).transpose(0, 1, 3, 2, 4).reshape(b, seq, GROUP_WIDTH)
    return jnp.concatenate([y[:, ctx:].reshape(b * t, GROUP_WIDTH), y[:, :ctx].reshape(b * ctx, GROUP_WIDTH)], axis=0)


def _s5_operators(lam_re, lam_im, log_dt, b_re, b_im, c_re, c_im, d_skip):
    hp = lax.Precision.HIGHEST
    g, pn, ni, lc = S5_GROUPS, S5_STATE, S5_GROUP, S5_CHUNK
    dt = jnp.exp(log_dt)[..., None]
    zr, zi = lam_re * dt, lam_im * dt
    steps = jnp.arange(lc + 1, dtype=F32)[None, :, None, None]
    mag = jnp.exp(zr[:, None] * steps)
    ang = zi[:, None] * steps
    pw_r, pw_i = mag * jnp.cos(ang), mag * jnp.sin(ang)
    nr, nim = pw_r[:, 1] - 1.0, pw_i[:, 1]
    den = lam_re * lam_re + lam_im * lam_im
    cr_, ci_ = (nr * lam_re + nim * lam_im) / den, (nim * lam_re - nr * lam_im) / den
    bz_r = cr_[..., None] * b_re - ci_[..., None] * b_im
    bz_i = cr_[..., None] * b_im + ci_[..., None] * b_re

    m_r = pw_r[:, :lc, :, :, None] * bz_r[:, None] - pw_i[:, :lc, :, :, None] * bz_i[:, None]
    m_i = pw_r[:, :lc, :, :, None] * bz_i[:, None] + pw_i[:, :lc, :, :, None] * bz_r[:, None]
    kern = (jnp.einsum("xgop,xdgpi->xdgoi", c_re, m_r, precision=hp)
            - jnp.einsum("xgop,xdgpi->xdgoi", c_im, m_i, precision=hp))
    s_idx = jnp.arange(lc)[:, None]
    t_idx = jnp.arange(lc)[None, :]
    lag_f = t_idx - s_idx
    tf = jnp.where((lag_f >= 0)[:, :, None, None, None], kern[0][jnp.clip(lag_f, 0, lc - 1)], 0.0)
    tb = jnp.where((lag_f <= 0)[:, :, None, None, None], kern[1][jnp.clip(-lag_f, 0, lc - 1)], 0.0)
    skip = (jnp.eye(lc, dtype=F32)[:, :, None, None, None]
            * (jnp.eye(ni, dtype=F32)[None] * d_skip.reshape(g, 1, ni))[None, None])
    t_all = tf + tb + skip
    t_sum = t_all.transpose(2, 0, 4, 1, 3).reshape(g, S5_BLOCK, S5_BLOCK)

    e_f = (pw_r[0, :lc][::-1], pw_i[0, :lc][::-1])
    e_b = (pw_r[1, :lc], pw_i[1, :lc])

    def in_op(e, d):
        re = e[0][:, :, :, None] * bz_r[d][None] - e[1][:, :, :, None] * bz_i[d][None]
        im = e[0][:, :, :, None] * bz_i[d][None] + e[1][:, :, :, None] * bz_r[d][None]
        to = lambda a: a.transpose(1, 0, 3, 2).reshape(g, S5_BLOCK, pn)
        return to(re), to(im)

    bf_r, bf_i = in_op(e_f, 0)
    bb_r, bb_i = in_op(e_b, 1)
    b_mat = jnp.stack([bf_r, bf_i, bb_r, bb_i], axis=2)
    eye2 = jnp.eye(2, dtype=F32)
    b_pair = (b_mat.reshape(g // 2, 2, S5_BLOCK, 4, 1, pn) * eye2[None, :, None, None, :, None])
    b_pair = b_pair.reshape(g // 2, 2, S5_BLOCK, 4 * 2 * pn)

    def out_op(pr, pi, d):
        re = c_re[d][None] * pr[:, :, None, :] - c_im[d][None] * pi[:, :, None, :]
        im = c_re[d][None] * pi[:, :, None, :] + c_im[d][None] * pr[:, :, None, :]
        to = lambda a: a.transpose(1, 3, 0, 2).reshape(g, pn, S5_BLOCK)
        return to(re), to(-im)

    cf_r, cf_i = out_op(pw_r[0, 1:], pw_i[0, 1:], 0)
    cb_r, cb_i = out_op(pw_r[1, 1:][::-1], pw_i[1, 1:][::-1], 1)
    c_mat = jnp.stack([cf_r, cf_i, cb_r, cb_i], axis=1)
    c_pair = (c_mat.reshape(g // 2, 2, 4, pn, 1, S5_BLOCK).transpose(0, 2, 1, 3, 4, 5)
              * eye2[None, None, :, None, :, None])
    c_pair = c_pair.reshape(g // 2, 4 * 2 * pn, 2 * S5_BLOCK)

    return {"t_sum": t_sum, "b_pair": b_pair, "c_pair": c_pair,
            "decay_re": pw_r[:, lc].reshape(2, 1, g * pn), "decay_im": pw_i[:, lc].reshape(2, 1, g * pn)}


def _group_norm_store(o_ref, k, y, w_ref):
    cols = slice(k * GROUP_WIDTH, (k + 1) * GROUP_WIDTH)
    o_ref[:, cols] = _rms(y, w_ref[:, cols]).astype(o_ref.dtype)


def _mix_kernel(ya_ref, yb_ref, gb_ref, gc_ref, u_ref, gcp_ref, up_ref, gcn_ref, un_ref,
                cw_ref, cb_ref, ys_ref, gw_ref, gbias_ref, mw_ref, o_ref, *, tm, lat_rows, t, ctx):
    i = pl.program_id(0)
    _group_norm_store(o_ref, 0, ya_ref[...], mw_ref)
    _group_norm_store(o_ref, 1, yb_ref[...], mw_ref)

    r0 = i * tm
    in_lat = r0 < lat_rows
    seq_len = jnp.where(in_lat, t, ctx)
    off = jnp.where(in_lat, r0, r0 - lat_rows) % seq_len
    has_prev = off != 0
    has_next = off + tm != seq_len
    v = gc_ref[...] * u_ref[...]
    v_prev_row = jnp.where(has_prev, gcp_ref[7:8, :] * up_ref[7:8, :], 0.0)
    v_next_row = jnp.where(has_next, gcn_ref[0:1, :] * un_ref[0:1, :], 0.0)
    row = lax.broadcasted_iota(jnp.int32, v.shape, 0)
    v_prev = jnp.where(row == 0, v_prev_row, pltpu.roll(v, 1, 0))
    v_next = jnp.where(row == tm - 1, v_next_row, pltpu.roll(v, tm - 1, 0))
    conv = cw_ref[0:1, :] * v_prev + cw_ref[1:2, :] * v + cw_ref[2:3, :] * v_next + cb_ref[...]
    _group_norm_store(o_ref, 2, gb_ref[...] * conv, mw_ref)

    y = ys_ref[...]
    g = 0.5 * y * (1.0 + jnp.tanh(math.sqrt(2.0 / math.pi) * (y + 0.044715 * (y * y * y))))
    gate = _sigmoid(_dot(g.astype(BF16), gw_ref[...]) + gbias_ref[...])
    _group_norm_store(o_ref, 3, g * gate, mw_ref)


def _mix_outputs(ya, yb, p, ys5, lw, geo, n_tiles):
    n_tok = ya.shape[0]
    tm = 256
    gw = GROUP_WIDTH
    halo = 8
    n_halo_blocks = n_tok // halo
    tile = lambda col: pl.BlockSpec((tm, gw), lambda i: (i, col // gw))
    prev = lambda col: pl.BlockSpec((halo, gw), lambda i: (jnp.maximum(i * (tm // halo) - 1, 0), col // gw))
    nxt = lambda col: pl.BlockSpec(
        (halo, gw), lambda i: (jnp.minimum((i + 1) * (tm // halo), n_halo_blocks - 1), col // gw))
    full = lambda a: pl.BlockSpec(a.shape, lambda i: (0,) * a.ndim)
    c_gb, c_gc, c_u = COL_CONV, COL_CONV + gw, COL_CONV + 2 * gw
    weights = [lw["conv_w"], lw["conv_b"]]
    glu = [lw["s5_glu_w"], lw["s5_glu_b"], lw["mix_norm_w"]]
    return pl.pallas_call(
        functools.partial(_mix_kernel, tm=tm, lat_rows=geo["b"] * geo["t"], t=geo["t"], ctx=geo["ctx"]),
        grid=(n_tiles * geo["tm"] // tm,),
        in_specs=[tile(0), tile(0), tile(c_gb), tile(c_gc), tile(c_u),
                  prev(c_gc), prev(c_u), nxt(c_gc), nxt(c_u)]
        + [full(a) for a in weights] + [tile(0)] + [full(a) for a in glu],
        out_specs=pl.BlockSpec((tm, 4 * gw), lambda i: (i, 0)),
        out_shape=jax.ShapeDtypeStruct((n_tok, 4 * gw), BF16),
        compiler_params=_cparams(("arbitrary",), 48 * 1024 * 1024),
        name="mix_outputs",
    )(ya, yb, p, p, p, p, p, p, p, *weights, ys5, *glu)


def _rope_perm():
    q = MLA_ROPE // 4
    idx = list(range(0, q)) + list(range(2 * q, 3 * q)) + list(range(q, 2 * q)) + list(range(3 * q, 4 * q))
    return jnp.array(idx, dtype=jnp.int32)


def _pad_cols(a, width):
    return jnp.pad(a, ((0, 0), (0, width - a.shape[1])))


def _layer_weights(l, pr):
    perm = _rope_perm()
    w_in = pr["w_in"][l]
    o_cq = 3 * GROUP_WIDTH
    o_ckv = o_cq + MLA_Q_RANK
    o_kr = o_ckv + MLA_KV_RANK
    o_conv = o_kr + MLA_ROPE
    o_s5 = o_conv + 3 * GROUP_WIDTH
    w_in_p = jnp.concatenate([
        w_in[:, :o_cq], w_in[:, o_conv:o_s5], w_in[:, o_s5:o_s5 + GROUP_WIDTH], w_in[:, o_cq:o_ckv],
        _pad_cols(w_in[:, o_kr:o_conv][:, perm], LANE), _pad_cols(w_in[:, o_ckv:o_kr], MLA_KV_RANK_PAD)],
        axis=1).astype(BF16)
    assert w_in_p.shape[1] == IN_WIDTH_PAD

    w_uq = pr["mla_w_uq"][l].reshape(MLA_Q_RANK, N_HEADS, MLA_QK)
    w_uq = jnp.concatenate([w_uq[:, :, :MLA_NOPE], w_uq[:, :, MLA_NOPE:][:, :, perm],
                            jnp.zeros((MLA_Q_RANK, N_HEADS, MLA_HEAD_PAD - MLA_QK), w_uq.dtype)], axis=2)
    w_ukv = pr["mla_w_ukv"][l].reshape(MLA_KV_RANK, N_HEADS, MLA_NOPE + MLA_V)
    w_ukv = jnp.concatenate([w_ukv[:, :, :MLA_NOPE].reshape(MLA_KV_RANK, -1),
                             w_ukv[:, :, MLA_NOPE:].reshape(MLA_KV_RANK, -1)], axis=1)
    w_ukv = jnp.pad(w_ukv, ((0, MLA_KV_RANK_PAD - MLA_KV_RANK), (0, 0)))

    def head_norm_w(w):
        return _pad_cols(jnp.concatenate([w[:MLA_NOPE], w[MLA_NOPE:][perm]])[None, :], MLA_HEAD_PAD)

    pad_h = FFN_HIDDEN_PAD - FFN_HIDDEN
    return {
        "w_in": w_in_p,
        "na_q_norm_w": pr["na_q_norm_w"][l][None, :],
        "na_k_norm_w": pr["na_k_norm_w"][l][None, :],
        "mla_cq_norm_w": pr["mla_cq_norm_w"][l][None, :],
        "mla_ckv_norm_w": _pad_cols(pr["mla_ckv_norm_w"][l][None, :], MLA_KV_RANK_PAD),
        "mla_w_uq": w_uq.reshape(MLA_Q_RANK, N_HEADS * MLA_HEAD_PAD).astype(BF16),
        "mla_w_ukv": w_ukv.astype(BF16),
        "mla_q_norm_w": head_norm_w(pr["mla_q_norm_w"][l]),
        "mla_k_norm_w": head_norm_w(pr["mla_k_norm_w"][l]),
        "conv_w": pr["conv_w"][l],
        "conv_b": pr["conv_b"][l][None, :],
        "s5_glu_w": pr["s5_glu_w"][l].astype(BF16),
        "s5_glu_b": pr["s5_glu_b"][l][None, :],
        "mix_norm_w": pr["mix_norm_w"][l][None, :],
        "w_out": pr["w_out"][l].astype(BF16),
        "ffn_w1": jnp.pad(pr["ffn_w1"][l], ((0, 0), (0, pad_h))).astype(BF16),
        "ffn_w3": jnp.pad(pr["ffn_w3"][l], ((0, 0), (0, pad_h))).astype(BF16),
        "ffn_w2": jnp.pad(pr["ffn_w2"][l], ((0, pad_h), (0, 0))).astype(BF16),
    }


def _rope_tables(b, t, n_ctx):
    pos = jnp.arange(t, dtype=jnp.int32)
    row = (pos // GRID_W).astype(F32)
    col = (pos % GRID_W).astype(F32)
    n_freq = MLA_ROPE // 4
    inv_freq = ROPE_THETA ** (-jnp.arange(n_freq, dtype=F32) / n_freq)
    ang = jnp.concatenate([row[:, None] * inv_freq[None, :], col[:, None] * inv_freq[None, :]], axis=1)
    cos, sin = jnp.cos(ang), jnp.sin(ang)
    half = MLA_ROPE // 2
    zeros = jnp.zeros((t, half), F32)
    cos_t = jnp.concatenate([cos, cos, jnp.ones((t, LANE - 2 * half), F32)], axis=1)
    sina_t = jnp.concatenate([-sin, zeros, zeros, zeros], axis=1)
    sinb_t = jnp.concatenate([zeros, sin, zeros, zeros], axis=1)
    n_c = b * n_ctx
    tables = []
    for tab, fill in ((cos_t, 1.0), (sina_t, 0.0), (sinb_t, 0.0)):
        tables.append(jnp.concatenate([jnp.tile(tab, (b, 1)), jnp.full((n_c, LANE), fill, F32)], axis=0))
    return tables


def _na_bias_tables(rpb, n_rows):
    w = GRID_W
    half = NA_WIN_ROWS // 2
    r0 = jnp.array([0, NA_Q_ROWS, n_rows - NA_Q_ROWS])
    ks = jnp.clip(r0 - half, 0, n_rows - NA_K_ROWS)
    rq = r0[:, None] + jnp.arange(NA_Q_ROWS)[None, :]
    rk = ks[:, None] + jnp.arange(NA_K_ROWS)[None, :]
    rs = jnp.clip(rq - half, 0, n_rows - NA_WIN_ROWS)
    v_row = (rk[:, None, :] >= rs[:, :, None]) & (rk[:, None, :] < rs[:, :, None] + NA_WIN_ROWS)
    d_row = jnp.clip(rk[:, None, :] - rq[:, :, None] + (NA_WIN_ROWS - 1), 0, 2 * NA_WIN_ROWS - 2)
    qc = jnp.arange(w)
    cs = jnp.clip(qc - NA_WIN_COLS // 2, 0, w - NA_WIN_COLS)
    v_col = (qc[None, :] >= cs[:, None]) & (qc[None, :] < cs[:, None] + NA_WIN_COLS)
    d_col = jnp.clip(qc[None, :] - qc[:, None], -(NA_WIN_COLS - 1), NA_WIN_COLS - 1) + (NA_WIN_COLS - 1)
    bias = rpb[:, d_row[:, :, None, :, None], d_col[None, None, :, None, :]]
    valid = v_row[:, :, None, :, None] & v_col[None, None, :, None, :]
    bias = jnp.where(valid[None], bias, NEG_INF)
    return bias.reshape(rpb.shape[0], 3, NA_Q_ROWS * w, NA_K_ROWS * w)


def kernel(x, c, ctx, c_ctx, ada_w, ada_b, norm1_w, norm2_w, w_in, na_q_norm_w, na_k_norm_w, na_rpb, mla_cq_norm_w, mla_ckv_norm_w, mla_w_uq, mla_w_ukv, mla_q_norm_w, mla_k_norm_w, conv_w, conv_b, s5_lambda_re, s5_lambda_im, s5_log_dt, s5_b_re, s5_b_im, s5_c_re, s5_c_im, s5_d, s5_glu_w, s5_glu_b, mix_norm_w, w_out, ffn_w1, ffn_w3, ffn_w2):
    b, t, d = x.shape
    n_ctx = ctx.shape[1]
    n_layers = ada_w.shape[0]
    n_rows = t // GRID_W
    assert t % GRID_W == 0 and n_rows >= NA_K_ROWS and n_rows % NA_Q_ROWS == 0
    assert n_ctx == NA_Q_ROWS * GRID_W and t % n_ctx == 0 and b + 1 <= 8
    tm = 512 if (b * n_ctx) % 512 == 0 and t % 512 == 0 else 256
    tiles_per_batch = t // tm
    geo = {"b": b, "t": t, "ctx": n_ctx, "tm": tm,
           "tile_class": lambda i: jnp.minimum(i // tiles_per_batch, b)}
    lat_tiles = b * t // tm
    all_tiles = lat_tiles + b * n_ctx // tm

    pr = dict(w_in=w_in, na_q_norm_w=na_q_norm_w, na_k_norm_w=na_k_norm_w, mla_cq_norm_w=mla_cq_norm_w,
              mla_ckv_norm_w=mla_ckv_norm_w, mla_w_uq=mla_w_uq, mla_w_ukv=mla_w_ukv,
              mla_q_norm_w=mla_q_norm_w, mla_k_norm_w=mla_k_norm_w, conv_w=conv_w, conv_b=conv_b,
              s5_glu_w=s5_glu_w, s5_glu_b=s5_glu_b, mix_norm_w=mix_norm_w, w_out=w_out,
              ffn_w1=ffn_w1, ffn_w3=ffn_w3, ffn_w2=ffn_w2)

    h = jnp.concatenate([x.reshape(b * t, d), ctx.reshape(b * n_ctx, d)], axis=0)
    cvec = jnp.zeros((8, d), F32).at[:b].set(c).at[b].set(c_ctx)
    mod_all = _ada_mod(cvec, ada_w, ada_b)
    rope = _rope_tables(b, t, n_ctx)

    for l in range(n_layers):
        last = l == n_layers - 1
        n_tiles = lat_tiles if last else all_tiles
        lw = _layer_weights(l, pr)
        mod = mod_all[l]

        a = _norm_modulate(h, norm1_w[l], mod, 0, 1, geo, all_tiles)
        p = _matmul(a, lw["w_in"], tm, 512, F32)
        naq, nak, nav, mq, mk, mv = _prep_qkv(p, lw, rope, 256)
        ya = _na_attention(naq, nak, nav, _na_bias_tables(na_rpb[l], n_rows), geo, not last)
        yb = _mla_attention(mq, mk, mv, geo, not last)
        ops = _s5_operators(s5_lambda_re[l], s5_lambda_im[l], s5_log_dt[l], s5_b_re[l], s5_b_im[l],
                            s5_c_re[l], s5_c_im[l], s5_d[l])
        ys5 = _s5_mixer(p, ops, geo)
        y = _mix_outputs(ya, yb, p, ys5, lw, geo, n_tiles)
        h = _matmul_gated_residual(y, lw["w_out"], h, mod, 2, geo, n_tiles, 1024)

        f = _norm_modulate(h, norm2_w[l], mod, 3, 4, geo, n_tiles)
        g = _ffn_in(f, lw["ffn_w1"], lw["ffn_w3"], geo, n_tiles, 512)
        h = _matmul_gated_residual(g, lw["ffn_w2"], h, mod, 5, geo, n_tiles, 512)

    return h[:b * t].reshape(b, t, d)
```

```python
import functools
import math

import jax
import jax.numpy as jnp
from jax import lax
from jax.experimental import pallas as pl
from jax.experimental.pallas import tpu as pltpu

F32 = jnp.float32
BF16 = jnp.bfloat16

NORM_EPS = 1e-6
NEG_INF = -1e30
GRID_W = 64

GROUP_WIDTH = 1024
N_HEADS = 8
NA_HEAD_DIM = 128
NA_WIN_ROWS = 8
NA_WIN_COLS = 16
NA_Q_ROWS = 4
NA_K_ROWS = NA_Q_ROWS + NA_WIN_ROWS

MLA_NOPE = 128
MLA_ROPE = 64
MLA_QK = MLA_NOPE + MLA_ROPE
MLA_V = 128
MLA_Q_RANK = 896
MLA_KV_RANK = 320
MLA_KV_RANK_PAD = 512
MLA_HEAD_PAD = 256
V_HEAD_PAD = 256
LOG2E = math.log2(math.e)
ROPE_THETA = 10000.0

CONV_K = 3

S5_GROUPS = 64
S5_GROUP = 16
S5_STATE = 64
S5_CHUNK = 16
S5_BLOCK = S5_CHUNK * S5_GROUP


COL_NA = 0
COL_CONV = 3072
COL_S5 = 6144
COL_CQ = 7168
COL_KR = 8064
COL_CKV = 8192
IN_WIDTH_PAD = 8704

LANE = 128
VMEM_LIMIT = 56 * 1024 * 1024


def _cparams(sem, vmem=None):
    return pltpu.CompilerParams(dimension_semantics=sem, vmem_limit_bytes=vmem)


def _sigmoid(x):
    return 1.0 / (1.0 + jnp.exp(-x))


def _rms(x, w):
    return x * lax.rsqrt(jnp.mean(x * x, axis=-1, keepdims=True) + NORM_EPS) * w


def _dot(a, b):
    return jnp.dot(a, b, preferred_element_type=F32)


def _dot_nt(a, b):
    return lax.dot_general(a, b, (((1,), (1,)), ((), ())), preferred_element_type=F32)


def _dot_f32(a, b):
    return jnp.dot(a, b, preferred_element_type=F32, precision=lax.Precision.HIGHEST)


def _ada_kernel(c_ref, w_ref, b_ref, o_ref):
    c = c_ref[...]
    s = (c * _sigmoid(c)).astype(BF16)
    o_ref[...] = _dot(s, w_ref[...].astype(BF16)) + b_ref[...]


def _ada_mod(cvec, ada_w, ada_b):
    n_layers, d, n = ada_w.shape
    tn = 512
    out = pl.pallas_call(
        _ada_kernel,
        grid=(n_layers, n // tn),
        in_specs=[pl.BlockSpec((8, d), lambda l, j: (0, 0)),
                  pl.BlockSpec((None, d, tn), lambda l, j: (l, 0, j)),
                  pl.BlockSpec((None, 1, tn), lambda l, j: (l, 0, j))],
        out_specs=pl.BlockSpec((None, 8, tn), lambda l, j: (l, 0, j)),
        out_shape=jax.ShapeDtypeStruct((n_layers, 8, n), F32),
        compiler_params=_cparams(("arbitrary", "arbitrary"), 40 * 1024 * 1024),
        name="ada_mod",
    )(cvec, ada_w, ada_b.reshape(n_layers, 1, n))
    return out.reshape(n_layers, 8, 6, d).transpose(0, 2, 1, 3)[:, :, :, None, :]


def _normmod_kernel(h_ref, w_ref, shift_ref, scale_ref, o_ref):
    y = _rms(h_ref[...], w_ref[...])
    o_ref[...] = (y * (1.0 + scale_ref[...]) + shift_ref[...]).astype(o_ref.dtype)


def _norm_modulate(h, w, mod, k_shift, k_scale, geo, n_tiles):
    n_tok, d = h.shape
    tm = geo["tm"]
    cls = geo["tile_class"]
    return pl.pallas_call(
        _normmod_kernel,
        grid=(n_tiles,),
        in_specs=[pl.BlockSpec((tm, d), lambda i: (i, 0)),
                  pl.BlockSpec((1, d), lambda i: (0, 0)),
                  pl.BlockSpec((None, None, 1, d), lambda i: (k_shift, cls(i), 0, 0)),
                  pl.BlockSpec((None, None, 1, d), lambda i: (k_scale, cls(i), 0, 0))],
        out_specs=pl.BlockSpec((tm, d), lambda i: (i, 0)),
        out_shape=jax.ShapeDtypeStruct((n_tok, d), BF16),
        compiler_params=_cparams(("arbitrary",), 48 * 1024 * 1024),
        name="norm_modulate",
    )(h, w.reshape(1, d), mod, mod)


def _mm_kernel(a_ref, w_ref, o_ref):
    o_ref[...] = _dot(a_ref[...], w_ref[...]).astype(o_ref.dtype)


def _matmul(a, w, tm, tn, out_dtype):
    m, k = a.shape
    n = w.shape[1]
    return pl.pallas_call(
        _mm_kernel,
        grid=(n // tn, m // tm),
        in_specs=[pl.BlockSpec((tm, k), lambda j, i: (i, 0)),
                  pl.BlockSpec((k, tn), lambda j, i: (0, j))],
        out_specs=pl.BlockSpec((tm, tn), lambda j, i: (i, j)),
        out_shape=jax.ShapeDtypeStruct((m, n), out_dtype),
        compiler_params=_cparams(("arbitrary", "arbitrary"), 40 * 1024 * 1024),
        name="matmul",
    )(a, w)


def _mm_res_kernel(a_ref, w_ref, h_ref, g_ref, o_ref):
    o_ref[...] = h_ref[...] + g_ref[...] * _dot(a_ref[...], w_ref[...])


def _mm_res_f32w_kernel(a_ref, w_ref, h_ref, g_ref, o_ref, wb_ref):
    @pl.when(pl.program_id(1) == 0)
    def _():
        wb_ref[...] = w_ref[...].astype(BF16)

    o_ref[...] = h_ref[...] + g_ref[...] * _dot(a_ref[...], wb_ref[...])


def _matmul_gated_residual(a, w, h, mod, k_gate, geo, n_tiles, tn):
    m, k = a.shape
    n = w.shape[1]
    tm = geo["tm"]
    cls = geo["tile_class"]
    f32w = w.dtype == F32
    return pl.pallas_call(
        _mm_res_f32w_kernel if f32w else _mm_res_kernel,
        grid=(n // tn, n_tiles),
        in_specs=[pl.BlockSpec((tm, k), lambda j, i: (i, 0)),
                  pl.BlockSpec((k, tn), lambda j, i: (0, j)),
                  pl.BlockSpec((tm, tn), lambda j, i: (i, j)),
                  pl.BlockSpec((None, None, 1, tn), lambda j, i: (k_gate, cls(i), 0, j))],
        out_specs=pl.BlockSpec((tm, tn), lambda j, i: (i, j)),
        out_shape=jax.ShapeDtypeStruct((m, n), F32),
        scratch_shapes=[pltpu.VMEM((k, tn), BF16)] if f32w else [],
        compiler_params=_cparams(("arbitrary", "arbitrary"), VMEM_LIMIT),
        name="matmul_gated_residual",
    )(a, w, h, mod)


def _ffn1_kernel(a_ref, w1_ref, w3_ref, o_ref, w1b_ref, w3b_ref):
    @pl.when(pl.program_id(1) == 0)
    def _():
        w1b_ref[...] = w1_ref[...].astype(BF16)
        w3b_ref[...] = w3_ref[...].astype(BF16)

    a = a_ref[...]
    u = _dot(a, w1b_ref[...])
    v = _dot(a, w3b_ref[...])
    o_ref[...] = (u * _sigmoid(u) * v).astype(o_ref.dtype)


def _ffn_in(a, w1, w3, geo, n_tiles, tn):
    m, k = a.shape
    n = w1.shape[1]
    tm = geo["tm"]
    return pl.pallas_call(
        _ffn1_kernel,
        grid=(pl.cdiv(n, tn), n_tiles),
        in_specs=[pl.BlockSpec((tm, k), lambda j, i: (i, 0)),
                  pl.BlockSpec((k, tn), lambda j, i: (0, j)),
                  pl.BlockSpec((k, tn), lambda j, i: (0, j))],
        out_specs=pl.BlockSpec((tm, tn), lambda j, i: (i, j)),
        out_shape=jax.ShapeDtypeStruct((m, n), BF16),
        scratch_shapes=[pltpu.VMEM((k, tn), BF16), pltpu.VMEM((k, tn), BF16)],
        compiler_params=_cparams(("arbitrary", "arbitrary"), VMEM_LIMIT),
        name="ffn_in",
    )(a, w1, w3)


def _rope_tail(t, cos_ref, sina_ref, sinb_ref):
    return (t * cos_ref[...] + pltpu.roll(t, 96, 1) * sina_ref[...]
            + pltpu.roll(t, 32, 1) * sinb_ref[...])


def _prep_kernel(na_ref, cq_ref, kr_ref, ckv_ref, naqw_ref, nakw_ref, cqw_ref, ckvw_ref,
                 wuq_ref, wukv_ref, mqw_ref, mkw_ref, cos_ref, sina_ref, sinb_ref,
                 naq_ref, nak_ref, nav_ref, mq_ref, mk_ref, mv_ref):
    hd = NA_HEAD_DIM
    tm = na_ref.shape[0]
    ones_col = jnp.where(lax.broadcasted_iota(jnp.int32, (tm, LANE), 1) == 0, 1.0, 0.0).astype(BF16)
    na_qs = naqw_ref[...] * (NA_HEAD_DIM ** -0.5 * LOG2E)
    for h in range(N_HEADS):
        naq_ref[:, h * hd:(h + 1) * hd] = _rms(na_ref[:, h * hd:(h + 1) * hd], na_qs).astype(BF16)
        nak_ref[:, h * hd:(h + 1) * hd] = _rms(
            na_ref[:, GROUP_WIDTH + h * hd:GROUP_WIDTH + (h + 1) * hd], nakw_ref[...]).astype(BF16)
        vo = h * V_HEAD_PAD
        nav_ref[:, vo:vo + hd] = na_ref[:, 2 * GROUP_WIDTH + h * hd:2 * GROUP_WIDTH + (h + 1) * hd].astype(BF16)
        nav_ref[:, vo + hd:vo + V_HEAD_PAD] = ones_col

    cq = _rms(cq_ref[...], cqw_ref[...]).astype(BF16)
    q = _dot(cq, wuq_ref[...])
    inv_qk = 1.0 / MLA_QK
    mqw = mqw_ref[...] * (MLA_QK ** -0.5 * LOG2E)
    for h in range(N_HEADS):
        o = h * MLA_HEAD_PAD
        nope = q[:, o:o + MLA_NOPE]
        tail = q[:, o + MLA_NOPE:o + MLA_HEAD_PAD]
        ss = jnp.sum(nope * nope, axis=-1, keepdims=True) + jnp.sum(tail * tail, axis=-1, keepdims=True)
        r = lax.rsqrt(ss * inv_qk + NORM_EPS)
        mq_ref[:, o:o + MLA_NOPE] = (nope * r * mqw[:, :MLA_NOPE]).astype(BF16)
        mq_ref[:, o + MLA_NOPE:o + MLA_HEAD_PAD] = _rope_tail(
            tail * r * mqw[:, MLA_NOPE:], cos_ref, sina_ref, sinb_ref).astype(BF16)

    ckv = ckv_ref[...]
    ckv_ms = jnp.sum(ckv * ckv, axis=-1, keepdims=True) * (1.0 / MLA_KV_RANK)
    ckv_n = (ckv * lax.rsqrt(ckv_ms + NORM_EPS) * ckvw_ref[...]).astype(BF16)
    kv = _dot(ckv_n, wukv_ref[...])
    for h in range(N_HEADS):
        vo = h * V_HEAD_PAD
        mv_ref[:, vo:vo + MLA_V] = kv[:, GROUP_WIDTH + h * MLA_V:GROUP_WIDTH + (h + 1) * MLA_V].astype(BF16)
        mv_ref[:, vo + MLA_V:vo + V_HEAD_PAD] = ones_col
    kr = kr_ref[...]
    kr_ss = jnp.sum(kr * kr, axis=-1, keepdims=True)
    mkw = mkw_ref[...]
    for h in range(N_HEADS):
        o = h * MLA_HEAD_PAD
        kn = kv[:, h * MLA_NOPE:(h + 1) * MLA_NOPE]
        r = lax.rsqrt((jnp.sum(kn * kn, axis=-1, keepdims=True) + kr_ss) * inv_qk + NORM_EPS)
        mk_ref[:, o:o + MLA_NOPE] = (kn * r * mkw[:, :MLA_NOPE]).astype(BF16)
        mk_ref[:, o + MLA_NOPE:o + MLA_HEAD_PAD] = _rope_tail(
            kr * r * mkw[:, MLA_NOPE:], cos_ref, sina_ref, sinb_ref).astype(BF16)


def _prep_qkv(p, lw, rope, tm):
    n_tok = p.shape[0]
    blk = lambda width, col: pl.BlockSpec((tm, width), lambda i: (i, col // width))
    full = lambda a: pl.BlockSpec(a.shape, lambda i: (0,) * a.ndim)
    tab = pl.BlockSpec((tm, LANE), lambda i: (i, 0))
    out_w = [GROUP_WIDTH, GROUP_WIDTH, N_HEADS * V_HEAD_PAD,
             N_HEADS * MLA_HEAD_PAD, N_HEADS * MLA_HEAD_PAD, N_HEADS * V_HEAD_PAD]
    weights = [lw["na_q_norm_w"], lw["na_k_norm_w"], lw["mla_cq_norm_w"], lw["mla_ckv_norm_w"],
               lw["mla_w_uq"], lw["mla_w_ukv"], lw["mla_q_norm_w"], lw["mla_k_norm_w"]]
    return pl.pallas_call(
        _prep_kernel,
        grid=(n_tok // tm,),
        in_specs=[blk(3 * GROUP_WIDTH, COL_NA), blk(MLA_Q_RANK, COL_CQ),
                  blk(LANE, COL_KR), blk(MLA_KV_RANK_PAD, COL_CKV)]
        + [full(a) for a in weights] + [tab, tab, tab],
        out_specs=[pl.BlockSpec((tm, w), lambda i: (i, 0)) for w in out_w],
        out_shape=[jax.ShapeDtypeStruct((n_tok, w), BF16) for w in out_w],
        compiler_params=_cparams(("arbitrary",), 48 * 1024 * 1024),
        name="prep_qkv",
    )(p, p, p, p, *weights, *rope)


def _softmax_attend(q, pairs, bias=None):
    scores = []
    for idx, (k, _) in enumerate(pairs):
        s = _dot_nt(q, k)
        if idx == 0 and bias is not None:
            s = s + bias
        scores.append(s)
    m = scores[0].max(axis=-1, keepdims=True)
    for s in scores[1:]:
        m = jnp.maximum(m, s.max(axis=-1, keepdims=True))
    acc = None
    for s, (_, v) in zip(scores, pairs):
        o = _dot(jnp.exp2(s - m).astype(BF16), v)
        acc = o if acc is None else acc + o
    dv = acc.shape[-1] // 2
    return acc[:, :dv] / acc[:, dv:dv + 1]


def _na_kernel(q_ref, k_ref, v_ref, kc_ref, vc_ref, bias_ref, o_ref, *, n_row_blocks, n_rows):
    rb = pl.program_id(2)

    @pl.when(rb < n_row_blocks)
    def _():
        ks = jnp.clip(rb * NA_Q_ROWS - NA_WIN_ROWS // 2, 0, n_rows - NA_K_ROWS) * GRID_W
        ks = pl.multiple_of(ks, GRID_W)
        kw = k_ref[pl.ds(ks, NA_K_ROWS * GRID_W), :]
        vw = v_ref[pl.ds(ks, NA_K_ROWS * GRID_W), :]
        o_ref[...] = _softmax_attend(q_ref[...], [(kw, vw), (kc_ref[...], vc_ref[...])], bias_ref[...])

    @pl.when(rb == n_row_blocks)
    def _():
        o_ref[...] = _softmax_attend(q_ref[...], [(kc_ref[...], vc_ref[...])])


def _na_attention(naq, nak, nav, bias, geo, with_ctx):
    b, t, ctx = geo["b"], geo["t"], geo["ctx"]
    n_tok = naq.shape[0]
    tq = NA_Q_ROWS * GRID_W
    assert ctx == tq
    n_rows = t // GRID_W
    nrb = n_rows // NA_Q_ROWS
    lat_blocks = b * t // tq
    ctx_block0 = b * t // ctx

    def q_map(bi, h, r):
        return (jnp.where(r < nrb, bi * nrb + r, lat_blocks + bi), h)

    def bias_map(bi, h, r):
        return (h, jnp.where(r == 0, 0, jnp.where(r >= nrb - 1, 2, 1)), 0, 0)

    return pl.pallas_call(
        functools.partial(_na_kernel, n_row_blocks=nrb, n_rows=n_rows),
        grid=(b, N_HEADS, nrb + (1 if with_ctx else 0)),
        in_specs=[pl.BlockSpec((tq, NA_HEAD_DIM), q_map),
                  pl.BlockSpec((t, NA_HEAD_DIM), lambda bi, h, r: (bi, h)),
                  pl.BlockSpec((t, V_HEAD_PAD), lambda bi, h, r: (bi, h)),
                  pl.BlockSpec((ctx, NA_HEAD_DIM), lambda bi, h, r: (ctx_block0 + bi, h)),
                  pl.BlockSpec((ctx, V_HEAD_PAD), lambda bi, h, r: (ctx_block0 + bi, h)),
                  pl.BlockSpec((None, None, tq, NA_K_ROWS * GRID_W), bias_map)],
        out_specs=pl.BlockSpec((tq, NA_HEAD_DIM), q_map),
        out_shape=jax.ShapeDtypeStruct((n_tok, GROUP_WIDTH), F32),
        compiler_params=_cparams(("arbitrary", "arbitrary", "arbitrary"), 40 * 1024 * 1024),
        name="na_attention",
    )(naq, nak, nav, nak, nav, bias)


def _flash_attend(q, loads):
    m = None
    acc = None
    for load in loads:
        k, v = load()
        s = _dot_nt(q, k)
        mc = s.max(axis=-1, keepdims=True)
        if m is None:
            m = mc
            acc = _dot(jnp.exp2(s - m).astype(BF16), v)
        else:
            m_new = jnp.maximum(m, mc)
            acc = acc * jnp.exp2(m - m_new) + _dot(jnp.exp2(s - m_new).astype(BF16), v)
            m = m_new
    dv = acc.shape[-1] // 2
    return acc[:, :dv] / acc[:, dv:dv + 1]


def _mla_kernel(q_ref, k_ref, v_ref, kc_ref, vc_ref, kca_ref, vca_ref, o_ref, *, n_q_blocks, key_chunk, ctx):
    i = pl.program_id(1)

    @pl.when(i < n_q_blocks)
    def _():
        def lat_chunk(j):
            rows = slice(j * key_chunk, (j + 1) * key_chunk)
            return lambda: (k_ref[rows, :], v_ref[rows, :])

        loads = [lat_chunk(j) for j in range(k_ref.shape[0] // key_chunk)]
        loads.append(lambda: (kc_ref[...], vc_ref[...]))
        o_ref[...] = _flash_attend(q_ref[...], loads)

    @pl.when(i == n_q_blocks)
    def _():
        for bb in range(q_ref.shape[0] // ctx):
            rows = slice(bb * ctx, (bb + 1) * ctx)
            o_ref[rows, :] = _flash_attend(q_ref[rows, :], [lambda: (kca_ref[rows, :], vca_ref[rows, :])])


def _mla_attention(mq, mk, mv, geo, with_ctx):
    b, t, ctx = geo["b"], geo["t"], geo["ctx"]
    n_tok = mq.shape[0]
    tq = b * ctx
    assert t % tq == 0
    nqb = t // tq
    lat_blocks = b * nqb
    ctx_block0 = b * t // ctx
    batch = lambda i: jnp.minimum(i // nqb, b - 1)

    return pl.pallas_call(
        functools.partial(_mla_kernel, n_q_blocks=lat_blocks, key_chunk=512, ctx=ctx),
        grid=(N_HEADS, lat_blocks + (1 if with_ctx else 0)),
        in_specs=[pl.BlockSpec((tq, MLA_HEAD_PAD), lambda h, i: (i, h)),
                  pl.BlockSpec((t, MLA_HEAD_PAD), lambda h, i: (batch(i), h)),
                  pl.BlockSpec((t, V_HEAD_PAD), lambda h, i: (batch(i), h)),
                  pl.BlockSpec((ctx, MLA_HEAD_PAD), lambda h, i: (ctx_block0 + batch(i), h)),
                  pl.BlockSpec((ctx, V_HEAD_PAD), lambda h, i: (ctx_block0 + batch(i), h)),
                  pl.BlockSpec((tq, MLA_HEAD_PAD), lambda h, i: (lat_blocks, h)),
                  pl.BlockSpec((tq, V_HEAD_PAD), lambda h, i: (lat_blocks, h))],
        out_specs=pl.BlockSpec((tq, MLA_V), lambda h, i: (i, h)),
        out_shape=jax.ShapeDtypeStruct((n_tok, GROUP_WIDTH), F32),
        compiler_params=_cparams(("arbitrary", "arbitrary"), 48 * 1024 * 1024),
        name="mla_attention",
    )(mq, mk, mv, mk, mv, mk, mv)


def _s5_contrib_kernel(u_ref, b_ref, o_ref):
    r = _dot_f32(u_ref[:, :S5_BLOCK], b_ref[0]) + _dot_f32(u_ref[:, S5_BLOCK:], b_ref[1])
    for k in range(4):
        o_ref[k] = r[:, k * LANE:(k + 1) * LANE]


def _s5_scan_kernel(c_ref, lr_ref, li_ref, x_ref, *, n_tiles, n_ctx_tiles, nb):
    sub = 8
    lanes = c_ref.shape[-1]
    lr = jnp.broadcast_to(lr_ref[...], (sub, lanes))
    li = jnp.broadcast_to(li_ref[...], (sub, lanes))
    row = lax.broadcasted_iota(jnp.int32, (sub, lanes), 0)

    def run(reverse):
        order = range(sub - 1, -1, -1) if reverse else range(sub)

        def body(jt, carry):
            if reverse:
                tile = jnp.where(jt < n_ctx_tiles, n_ctx_tiles - 1 - jt, n_tiles - 1 - (jt - n_ctx_tiles))
            else:
                tile = jt
            new = []
            for bi in range(nb):
                xr, xi = carry[2 * bi], carry[2 * bi + 1]
                r0 = pl.multiple_of((bi * n_tiles + tile) * sub, sub)
                cr = c_ref[0, pl.ds(r0, sub), :]
                ci = c_ref[1, pl.ds(r0, sub), :]
                out_r = jnp.zeros_like(cr)
                out_i = jnp.zeros_like(ci)
                for k in order:
                    out_r = jnp.where(row == k, xr, out_r)
                    out_i = jnp.where(row == k, xi, out_i)
                    ck_r = jnp.broadcast_to(cr[k:k + 1, :], (sub, lanes))
                    ck_i = jnp.broadcast_to(ci[k:k + 1, :], (sub, lanes))
                    xr, xi = lr * xr - li * xi + ck_r, lr * xi + li * xr + ck_i
                x_ref[0, pl.ds(r0, sub), :] = out_r
                x_ref[1, pl.ds(r0, sub), :] = out_i
                new += [xr, xi]
            return tuple(new)

        zero = jnp.zeros((sub, lanes), F32)
        lax.fori_loop(0, n_tiles, body, (zero,) * (2 * nb))

    @pl.when(pl.program_id(0) == 0)
    def _():
        run(False)

    @pl.when(pl.program_id(0) == 1)
    def _():
        run(True)


def _s5_out_kernel(u_ref, x_ref, t_ref, c_ref, o_ref):
    x = jnp.concatenate([x_ref[0], x_ref[1], x_ref[2], x_ref[3]], axis=-1)
    y = _dot_f32(x, c_ref[...])
    o_ref[:, :S5_BLOCK] = _dot_f32(u_ref[:, :S5_BLOCK], t_ref[0]) + y[:, :S5_BLOCK]
    o_ref[:, S5_BLOCK:] = _dot_f32(u_ref[:, S5_BLOCK:], t_ref[1]) + y[:, S5_BLOCK:]


def _s5_mixer(p, ops, geo):
    b, t, ctx = geo["b"], geo["t"], geo["ctx"]
    seq = ctx + t
    n_chunks = seq // S5_CHUNK
    rows = n_chunks * b
    n_pairs = S5_GROUPS // 2
    state_w = S5_GROUPS * S5_STATE
    pair_w = 2 * S5_BLOCK

    u = p[:, COL_S5:COL_S5 + GROUP_WIDTH]
    useq = jnp.concatenate([u[b * t:].reshape(b, ctx, GROUP_WIDTH), u[:b * t].reshape(b, t, GROUP_WIDTH)], axis=1)
    u2 = useq.reshape(b, n_chunks, S5_CHUNK, S5_GROUPS, S5_GROUP).transpose(0, 1, 3, 2, 4)
    u2 = u2.reshape(rows, S5_GROUPS * S5_BLOCK)
    sub = 8
    assert n_chunks % sub == 0 and (ctx // S5_CHUNK) % sub == 0

    contrib = pl.pallas_call(
        _s5_contrib_kernel,
        grid=(n_pairs,),
        in_specs=[pl.BlockSpec((rows, pair_w), lambda q: (0, q)),
                  pl.BlockSpec((None, 2, S5_BLOCK, pair_w), lambda q: (q, 0, 0, 0))],
        out_specs=pl.BlockSpec((4, rows, LANE), lambda q: (0, 0, q)),
        out_shape=jax.ShapeDtypeStruct((4, rows, state_w), F32),
        compiler_params=_cparams(("arbitrary",), 40 * 1024 * 1024),
        name="s5_contrib",
    )(u2, ops["b_pair"])

    lane_blk = 512
    states = pl.pallas_call(
        functools.partial(_s5_scan_kernel, n_tiles=n_chunks // sub, n_ctx_tiles=ctx // S5_CHUNK // sub, nb=b),
        grid=(2, state_w // lane_blk),
        in_specs=[pl.BlockSpec((None, 2, rows, lane_blk), lambda d, l: (d, 0, 0, l)),
                  pl.BlockSpec((None, 1, lane_blk), lambda d, l: (d, 0, l)),
                  pl.BlockSpec((None, 1, lane_blk), lambda d, l: (d, 0, l))],
        out_specs=pl.BlockSpec((None, 2, rows, lane_blk), lambda d, l: (d, 0, 0, l)),
        out_shape=jax.ShapeDtypeStruct((2, 2, rows, state_w), F32),
        compiler_params=_cparams(("arbitrary", "arbitrary"), 40 * 1024 * 1024),
        name="s5_scan",
    )(contrib.reshape(2, 2, rows, state_w), ops["decay_re"], ops["decay_im"])

    y2 = pl.pallas_call(
        _s5_out_kernel,
        grid=(n_pairs,),
        in_specs=[pl.BlockSpec((rows, pair_w), lambda q: (0, q)),
                  pl.BlockSpec((4, rows, LANE), lambda q: (0, 0, q)),
                  pl.BlockSpec((2, S5_BLOCK, S5_BLOCK), lambda q: (q, 0, 0)),
                  pl.BlockSpec((None, pair_w, pair_w), lambda q: (q, 0, 0))],
        out_specs=pl.BlockSpec((rows, pair_w), lambda q: (0, q)),
        out_shape=jax.ShapeDtypeStruct((rows, S5_GROUPS * S5_BLOCK), F32),
        compiler_params=_cparams(("arbitrary",), 40 * 1024 * 1024),
        name="s5_out",
    )(u2, states.reshape(4, rows, state_w), ops["t_sum"], ops["c_pair"])

    y = y2.reshape(b, n_chunks, S5_GROUPS, S5_CHUNK, S5_GROUP).transpose(0, 1, 3, 2, 4).reshape(b, seq, GROUP_WIDTH)
    return jnp.concatenate([y[:, ctx:].reshape(b * t, GROUP_WIDTH), y[:, :ctx].reshape(b * ctx, GROUP_WIDTH)], axis=0)


def _s5_operators(lam_re, lam_im, log_dt, b_re, b_im, c_re, c_im, d_skip):
    hp = lax.Precision.HIGHEST
    g, pn, ni, lc = S5_GROUPS, S5_STATE, S5_GROUP, S5_CHUNK
    dt = jnp.exp(log_dt)[..., None]
    zr, zi = lam_re * dt, lam_im * dt
    steps = jnp.arange(lc + 1, dtype=F32)[None, :, None, None]
    mag = jnp.exp(zr[:, None] * steps)
    ang = zi[:, None] * steps
    pw_r, pw_i = mag * jnp.cos(ang), mag * jnp.sin(ang)
    nr, nim = pw_r[:, 1] - 1.0, pw_i[:, 1]
    den = lam_re * lam_re + lam_im * lam_im
    cr_, ci_ = (nr * lam_re + nim * lam_im) / den, (nim * lam_re - nr * lam_im) / den
    bz_r = cr_[..., None] * b_re - ci_[..., None] * b_im
    bz_i = cr_[..., None] * b_im + ci_[..., None] * b_re

    m_r = pw_r[:, :lc, :, :, None] * bz_r[:, None] - pw_i[:, :lc, :, :, None] * bz_i[:, None]
    m_i = pw_r[:, :lc, :, :, None] * bz_i[:, None] + pw_i[:, :lc, :, :, None] * bz_r[:, None]
    kern = (jnp.einsum("xgop,xdgpi->xdgoi", c_re, m_r, precision=hp)
            - jnp.einsum("xgop,xdgpi->xdgoi", c_im, m_i, precision=hp))
    s_idx = jnp.arange(lc)[:, None]
    t_idx = jnp.arange(lc)[None, :]
    lag_f = t_idx - s_idx
    lags = jnp.arange(lc)[None, None, :]
    sel_f = (lag_f[:, :, None] == lags).astype(F32)
    sel_b = (-lag_f[:, :, None] == lags).astype(F32)
    tf = jnp.einsum("std,dgoi->stgoi", sel_f, kern[0], precision=hp)
    tb = jnp.einsum("std,dgoi->stgoi", sel_b, kern[1], precision=hp)
    skip = (jnp.eye(lc, dtype=F32)[:, :, None, None, None]
            * (jnp.eye(ni, dtype=F32)[None] * d_skip.reshape(g, 1, ni))[None, None])
    t_all = tf + tb + skip
    t_sum = t_all.transpose(2, 0, 4, 1, 3).reshape(g, S5_BLOCK, S5_BLOCK)

    e_f = (pw_r[0, :lc][::-1], pw_i[0, :lc][::-1])
    e_b = (pw_r[1, :lc], pw_i[1, :lc])

    def in_op(e, d):
        re = e[0][:, :, :, None] * bz_r[d][None] - e[1][:, :, :, None] * bz_i[d][None]
        im = e[0][:, :, :, None] * bz_i[d][None] + e[1][:, :, :, None] * bz_r[d][None]
        to = lambda a: a.transpose(1, 0, 3, 2).reshape(g, S5_BLOCK, pn)
        return to(re), to(im)

    bf_r, bf_i = in_op(e_f, 0)
    bb_r, bb_i = in_op(e_b, 1)
    b_mat = jnp.stack([bf_r, bf_i, bb_r, bb_i], axis=2)
    eye2 = jnp.eye(2, dtype=F32)
    b_pair = (b_mat.reshape(g // 2, 2, S5_BLOCK, 4, 1, pn) * eye2[None, :, None, None, :, None])
    b_pair = b_pair.reshape(g // 2, 2, S5_BLOCK, 4 * 2 * pn)

    def out_op(pr, pi, d):
        re = c_re[d][None] * pr[:, :, None, :] - c_im[d][None] * pi[:, :, None, :]
        im = c_re[d][None] * pi[:, :, None, :] + c_im[d][None] * pr[:, :, None, :]
        to = lambda a: a.transpose(1, 3, 0, 2).reshape(g, pn, S5_BLOCK)
        return to(re), to(-im)

    cf_r, cf_i = out_op(pw_r[0, 1:], pw_i[0, 1:], 0)
    cb_r, cb_i = out_op(pw_r[1, 1:][::-1], pw_i[1, 1:][::-1], 1)
    c_mat = jnp.stack([cf_r, cf_i, cb_r, cb_i], axis=1)
    c_pair = (c_mat.reshape(g // 2, 2, 4, pn, 1, S5_BLOCK).transpose(0, 2, 1, 3, 4, 5)
              * eye2[None, None, :, None, :, None])
    c_pair = c_pair.reshape(g // 2, 4 * 2 * pn, 2 * S5_BLOCK)

    return {"t_sum": t_sum, "b_pair": b_pair, "c_pair": c_pair,
            "decay_re": pw_r[:, lc].reshape(2, 1, g * pn), "decay_im": pw_i[:, lc].reshape(2, 1, g * pn)}


def _group_norm_store(o_ref, k, y, w_ref):
    cols = slice(k * GROUP_WIDTH, (k + 1) * GROUP_WIDTH)
    o_ref[:, cols] = _rms(y, w_ref[:, cols]).astype(o_ref.dtype)


def _mix_kernel(ya_ref, yb_ref, gb_ref, gc_ref, u_ref, gcp_ref, up_ref, gcn_ref, un_ref,
                cw_ref, cb_ref, ys_ref, gw_ref, gbias_ref, mw_ref, o_ref, *, tm, lat_rows, t, ctx):
    i = pl.program_id(0)
    _group_norm_store(o_ref, 0, ya_ref[...], mw_ref)
    _group_norm_store(o_ref, 1, yb_ref[...], mw_ref)

    r0 = i * tm
    in_lat = r0 < lat_rows
    seq_len = jnp.where(in_lat, t, ctx)
    off = jnp.where(in_lat, r0, r0 - lat_rows) % seq_len
    has_prev = off != 0
    has_next = off + tm != seq_len
    v = gc_ref[...] * u_ref[...]
    v_prev_row = jnp.where(has_prev, gcp_ref[7:8, :] * up_ref[7:8, :], 0.0)
    v_next_row = jnp.where(has_next, gcn_ref[0:1, :] * un_ref[0:1, :], 0.0)
    row = lax.broadcasted_iota(jnp.int32, v.shape, 0)
    v_prev = jnp.where(row == 0, v_prev_row, pltpu.roll(v, 1, 0))
    v_next = jnp.where(row == tm - 1, v_next_row, pltpu.roll(v, tm - 1, 0))
    conv = cw_ref[0:1, :] * v_prev + cw_ref[1:2, :] * v + cw_ref[2:3, :] * v_next + cb_ref[...]
    _group_norm_store(o_ref, 2, gb_ref[...] * conv, mw_ref)

    y = ys_ref[...]
    g = 0.5 * y * (1.0 + jnp.tanh(math.sqrt(2.0 / math.pi) * (y + 0.044715 * (y * y * y))))
    gate = _sigmoid(_dot(g.astype(BF16), gw_ref[...]) + gbias_ref[...])
    _group_norm_store(o_ref, 3, g * gate, mw_ref)


def _mix_outputs(ya, yb, p, ys5, lw, geo, n_tiles):
    n_tok = ya.shape[0]
    tm = 256
    gw = GROUP_WIDTH
    halo = 8
    n_halo_blocks = n_tok // halo
    tile = lambda col: pl.BlockSpec((tm, gw), lambda i: (i, col // gw))
    prev = lambda col: pl.BlockSpec((halo, gw), lambda i: (jnp.maximum(i * (tm // halo) - 1, 0), col // gw))
    nxt = lambda col: pl.BlockSpec(
        (halo, gw), lambda i: (jnp.minimum((i + 1) * (tm // halo), n_halo_blocks - 1), col // gw))
    full = lambda a: pl.BlockSpec(a.shape, lambda i: (0,) * a.ndim)
    c_gb, c_gc, c_u = COL_CONV, COL_CONV + gw, COL_CONV + 2 * gw
    weights = [lw["conv_w"], lw["conv_b"]]
    glu = [lw["s5_glu_w"], lw["s5_glu_b"], lw["mix_norm_w"]]
    return pl.pallas_call(
        functools.partial(_mix_kernel, tm=tm, lat_rows=geo["b"] * geo["t"], t=geo["t"], ctx=geo["ctx"]),
        grid=(n_tiles * geo["tm"] // tm,),
        in_specs=[tile(0), tile(0), tile(c_gb), tile(c_gc), tile(c_u),
                  prev(c_gc), prev(c_u), nxt(c_gc), nxt(c_u)]
        + [full(a) for a in weights] + [tile(0)] + [full(a) for a in glu],
        out_specs=pl.BlockSpec((tm, 4 * gw), lambda i: (i, 0)),
        out_shape=jax.ShapeDtypeStruct((n_tok, 4 * gw), BF16),
        compiler_params=_cparams(("arbitrary",), 48 * 1024 * 1024),
        name="mix_outputs",
    )(ya, yb, p, p, p, p, p, p, p, *weights, ys5, *glu)


def _rope_reorder(a):
    q = MLA_ROPE // 4
    return jnp.concatenate([a[..., :q], a[..., 2 * q:3 * q], a[..., q:2 * q], a[..., 3 * q:]], axis=-1)


def _pad_cols(a, width):
    return jnp.pad(a, ((0, 0), (0, width - a.shape[1])))


def _layer_weights(l, pr):
    w_in = pr["w_in"][l]
    o_cq = 3 * GROUP_WIDTH
    o_ckv = o_cq + MLA_Q_RANK
    o_kr = o_ckv + MLA_KV_RANK
    o_conv = o_kr + MLA_ROPE
    o_s5 = o_conv + 3 * GROUP_WIDTH
    w_in_p = jnp.concatenate([
        w_in[:, :o_cq], w_in[:, o_conv:o_s5], w_in[:, o_s5:o_s5 + GROUP_WIDTH], w_in[:, o_cq:o_ckv],
        _pad_cols(_rope_reorder(w_in[:, o_kr:o_conv]), LANE), _pad_cols(w_in[:, o_ckv:o_kr], MLA_KV_RANK_PAD)],
        axis=1).astype(BF16)
    assert w_in_p.shape[1] == IN_WIDTH_PAD

    w_uq = pr["mla_w_uq"][l].reshape(MLA_Q_RANK, N_HEADS, MLA_QK)
    w_uq = jnp.concatenate([w_uq[:, :, :MLA_NOPE], _rope_reorder(w_uq[:, :, MLA_NOPE:]),
                            jnp.zeros((MLA_Q_RANK, N_HEADS, MLA_HEAD_PAD - MLA_QK), w_uq.dtype)], axis=2)
    w_ukv = pr["mla_w_ukv"][l].reshape(MLA_KV_RANK, N_HEADS, MLA_NOPE + MLA_V)
    w_ukv = jnp.concatenate([w_ukv[:, :, :MLA_NOPE].reshape(MLA_KV_RANK, -1),
                             w_ukv[:, :, MLA_NOPE:].reshape(MLA_KV_RANK, -1)], axis=1)
    w_ukv = jnp.pad(w_ukv, ((0, MLA_KV_RANK_PAD - MLA_KV_RANK), (0, 0)))

    def head_norm_w(w):
        return _pad_cols(jnp.concatenate([w[:MLA_NOPE], _rope_reorder(w[MLA_NOPE:])])[None, :], MLA_HEAD_PAD)

    return {
        "w_in": w_in_p,
        "na_q_norm_w": pr["na_q_norm_w"][l][None, :],
        "na_k_norm_w": pr["na_k_norm_w"][l][None, :],
        "mla_cq_norm_w": pr["mla_cq_norm_w"][l][None, :],
        "mla_ckv_norm_w": _pad_cols(pr["mla_ckv_norm_w"][l][None, :], MLA_KV_RANK_PAD),
        "mla_w_uq": w_uq.reshape(MLA_Q_RANK, N_HEADS * MLA_HEAD_PAD).astype(BF16),
        "mla_w_ukv": w_ukv.astype(BF16),
        "mla_q_norm_w": head_norm_w(pr["mla_q_norm_w"][l]),
        "mla_k_norm_w": head_norm_w(pr["mla_k_norm_w"][l]),
        "conv_w": pr["conv_w"][l],
        "conv_b": pr["conv_b"][l][None, :],
        "s5_glu_w": pr["s5_glu_w"][l].astype(BF16),
        "s5_glu_b": pr["s5_glu_b"][l][None, :],
        "mix_norm_w": pr["mix_norm_w"][l][None, :],
        "w_out": pr["w_out"][l],
        "ffn_w1": pr["ffn_w1"][l],
        "ffn_w3": pr["ffn_w3"][l],
        "ffn_w2": pr["ffn_w2"][l].astype(BF16),
    }


def _rope_tables(b, t, n_ctx):
    pos = jnp.arange(t, dtype=jnp.int32)
    row = (pos // GRID_W).astype(F32)
    col = (pos % GRID_W).astype(F32)
    n_freq = MLA_ROPE // 4
    inv_freq = ROPE_THETA ** (-jnp.arange(n_freq, dtype=F32) / n_freq)
    ang = jnp.concatenate([row[:, None] * inv_freq[None, :], col[:, None] * inv_freq[None, :]], axis=1)
    cos, sin = jnp.cos(ang), jnp.sin(ang)
    half = MLA_ROPE // 2
    zeros = jnp.zeros((t, half), F32)
    cos_t = jnp.concatenate([cos, cos, jnp.ones((t, LANE - 2 * half), F32)], axis=1)
    sina_t = jnp.concatenate([-sin, zeros, zeros, zeros], axis=1)
    sinb_t = jnp.concatenate([zeros, sin, zeros, zeros], axis=1)
    n_c = b * n_ctx
    tables = []
    for tab, fill in ((cos_t, 1.0), (sina_t, 0.0), (sinb_t, 0.0)):
        tables.append(jnp.concatenate([jnp.tile(tab, (b, 1)), jnp.full((n_c, LANE), fill, F32)], axis=0))
    return tables


def _na_bias_tables(rpb, n_rows):
    w = GRID_W
    half = NA_WIN_ROWS // 2
    r0 = jnp.array([0, NA_Q_ROWS, n_rows - NA_Q_ROWS])
    ks = jnp.clip(r0 - half, 0, n_rows - NA_K_ROWS)
    rq = r0[:, None] + jnp.arange(NA_Q_ROWS)[None, :]
    rk = ks[:, None] + jnp.arange(NA_K_ROWS)[None, :]
    rs = jnp.clip(rq - half, 0, n_rows - NA_WIN_ROWS)
    v_row = (rk[:, None, :] >= rs[:, :, None]) & (rk[:, None, :] < rs[:, :, None] + NA_WIN_ROWS)
    d_row = jnp.clip(rk[:, None, :] - rq[:, :, None] + (NA_WIN_ROWS - 1), 0, 2 * NA_WIN_ROWS - 2)
    qc = jnp.arange(w)
    cs = jnp.clip(qc - NA_WIN_COLS // 2, 0, w - NA_WIN_COLS)
    v_col = (qc[None, :] >= cs[:, None]) & (qc[None, :] < cs[:, None] + NA_WIN_COLS)
    d_col = jnp.clip(qc[None, :] - qc[:, None], -(NA_WIN_COLS - 1), NA_WIN_COLS - 1) + (NA_WIN_COLS - 1)
    sel_r = jax.nn.one_hot(d_row, 2 * NA_WIN_ROWS - 1, dtype=F32)
    sel_c = jax.nn.one_hot(d_col, 2 * NA_WIN_COLS - 1, dtype=F32)
    bias = jnp.einsum("hab,cija,qkb->hciqjk", rpb * LOG2E, sel_r, sel_c, precision=lax.Precision.HIGHEST)
    valid = v_row[:, :, None, :, None] & v_col[None, None, :, None, :]
    bias = jnp.where(valid[None], bias, NEG_INF)
    return bias.reshape(rpb.shape[0], 3, NA_Q_ROWS * w, NA_K_ROWS * w)


def kernel(x, c, ctx, c_ctx, ada_w, ada_b, norm1_w, norm2_w, w_in, na_q_norm_w, na_k_norm_w, na_rpb, mla_cq_norm_w, mla_ckv_norm_w, mla_w_uq, mla_w_ukv, mla_q_norm_w, mla_k_norm_w, conv_w, conv_b, s5_lambda_re, s5_lambda_im, s5_log_dt, s5_b_re, s5_b_im, s5_c_re, s5_c_im, s5_d, s5_glu_w, s5_glu_b, mix_norm_w, w_out, ffn_w1, ffn_w3, ffn_w2):
    b, t, d = x.shape
    n_ctx = ctx.shape[1]
    n_layers = ada_w.shape[0]
    n_rows = t // GRID_W
    assert t % GRID_W == 0 and n_rows >= NA_K_ROWS and n_rows % NA_Q_ROWS == 0
    assert n_ctx == NA_Q_ROWS * GRID_W and t % n_ctx == 0 and b + 1 <= 8
    tm = 512 if (b * n_ctx) % 512 == 0 and t % 512 == 0 else 256
    tiles_per_batch = t // tm
    geo = {"b": b, "t": t, "ctx": n_ctx, "tm": tm,
           "tile_class": lambda i: jnp.minimum(i // tiles_per_batch, b)}
    lat_tiles = b * t // tm
    all_tiles = lat_tiles + b * n_ctx // tm

    pr = dict(w_in=w_in, na_q_norm_w=na_q_norm_w, na_k_norm_w=na_k_norm_w, mla_cq_norm_w=mla_cq_norm_w,
              mla_ckv_norm_w=mla_ckv_norm_w, mla_w_uq=mla_w_uq, mla_w_ukv=mla_w_ukv,
              mla_q_norm_w=mla_q_norm_w, mla_k_norm_w=mla_k_norm_w, conv_w=conv_w, conv_b=conv_b,
              s5_glu_w=s5_glu_w, s5_glu_b=s5_glu_b, mix_norm_w=mix_norm_w, w_out=w_out,
              ffn_w1=ffn_w1, ffn_w3=ffn_w3, ffn_w2=ffn_w2)

    h = jnp.concatenate([x.reshape(b * t, d), ctx.reshape(b * n_ctx, d)], axis=0)
    cvec = jnp.zeros((8, d), F32).at[:b].set(c).at[b].set(c_ctx)
    mod_all = _ada_mod(cvec, ada_w, ada_b)
    rope = _rope_tables(b, t, n_ctx)

    for l in range(n_layers):
        last = l == n_layers - 1
        n_tiles = lat_tiles if last else all_tiles
        lw = _layer_weights(l, pr)
        mod = mod_all[l]

        a = _norm_modulate(h, norm1_w[l], mod, 0, 1, geo, all_tiles)
        p = _matmul(a, lw["w_in"], tm, 512, F32)
        naq, nak, nav, mq, mk, mv = _prep_qkv(p, lw, rope, 256)
        ya = _na_attention(naq, nak, nav, _na_bias_tables(na_rpb[l], n_rows), geo, not last)
        yb = _mla_attention(mq, mk, mv, geo, not last)
        ops = _s5_operators(s5_lambda_re[l], s5_lambda_im[l], s5_log_dt[l], s5_b_re[l], s5_b_im[l],
                            s5_c_re[l], s5_c_im[l], s5_d[l])
        ys5 = _s5_mixer(p, ops, geo)
        y = _mix_outputs(ya, yb, p, ys5, lw, geo, n_tiles)
        h = _matmul_gated_residual(y, lw["w_out"], h, mod, 2, geo, n_tiles, 512)

        f = _norm_modulate(h, norm2_w[l], mod, 3, 4, geo, n_tiles)
        g = _ffn_in(f, lw["ffn_w1"], lw["ffn_w3"], geo, n_tiles, 512)
        h = _matmul_gated_residual(g, lw["ffn_w2"], h, mod, 5, geo, n_tiles, 512)

    return h[:b * t].reshape(b, t, d)
```

```python
import functools
import math

import jax
import jax.numpy as jnp
from jax import lax
from jax.experimental import pallas as pl
from jax.experimental.pallas import tpu as pltpu

F32 = jnp.float32
BF16 = jnp.bfloat16

NORM_EPS = 1e-6
NEG_INF = -1e30
GRID_W = 64

GROUP_WIDTH = 1024
N_HEADS = 8
NA_HEAD_DIM = 128
NA_WIN_ROWS = 8
NA_WIN_COLS = 16
NA_Q_ROWS = 4
NA_K_ROWS = NA_Q_ROWS + NA_WIN_ROWS

MLA_NOPE = 128
MLA_ROPE = 64
MLA_QK = MLA_NOPE + MLA_ROPE
MLA_V = 128
MLA_Q_RANK = 896
MLA_KV_RANK = 320
MLA_KV_RANK_PAD = 512
MLA_HEAD_PAD = 256
V_HEAD_PAD = 256
LOG2E = math.log2(math.e)
ROPE_THETA = 10000.0

CONV_K = 3

S5_GROUPS = 64
S5_GROUP = 16
S5_STATE = 64
S5_CHUNK = 16
S5_BLOCK = S5_CHUNK * S5_GROUP


COL_NA = 0
COL_CONV = 3072
COL_S5 = 6144
COL_CQ = 7168
COL_KR = 8064
COL_CKV = 8192
IN_WIDTH_PAD = 8704

LANE = 128
VMEM_LIMIT = 56 * 1024 * 1024


def _cparams(sem, vmem=None):
    return pltpu.CompilerParams(dimension_semantics=sem, vmem_limit_bytes=vmem)


def _sigmoid(x):
    return 1.0 / (1.0 + jnp.exp(-x))


def _rms(x, w):
    return x * lax.rsqrt(jnp.mean(x * x, axis=-1, keepdims=True) + NORM_EPS) * w


def _dot(a, b):
    return jnp.dot(a, b, preferred_element_type=F32)


def _dot_nt(a, b):
    return lax.dot_general(a, b, (((1,), (1,)), ((), ())), preferred_element_type=F32)


def _dot_f32(a, b):
    return jnp.dot(a, b, preferred_element_type=F32, precision=lax.Precision.HIGHEST)


def _ada_kernel(c_ref, w_ref, b_ref, o_ref):
    c = c_ref[...]
    s = (c * _sigmoid(c)).astype(BF16)
    o_ref[...] = _dot(s, w_ref[...].astype(BF16)) + b_ref[...]


def _ada_mod(cvec, ada_w, ada_b):
    n_layers, d, n = ada_w.shape
    tn = 512
    out = pl.pallas_call(
        _ada_kernel,
        grid=(n_layers, n // tn),
        in_specs=[pl.BlockSpec((8, d), lambda l, j: (0, 0)),
                  pl.BlockSpec((None, d, tn), lambda l, j: (l, 0, j)),
                  pl.BlockSpec((None, 1, tn), lambda l, j: (l, 0, j))],
        out_specs=pl.BlockSpec((None, 8, tn), lambda l, j: (l, 0, j)),
        out_shape=jax.ShapeDtypeStruct((n_layers, 8, n), F32),
        compiler_params=_cparams(("arbitrary", "arbitrary"), 40 * 1024 * 1024),
        name="ada_mod",
    )(cvec, ada_w, ada_b.reshape(n_layers, 1, n))
    return out.reshape(n_layers, 8, 6, d).transpose(0, 2, 1, 3)[:, :, :, None, :]


def _normmod_kernel(h_ref, w_ref, shift_ref, scale_ref, o_ref):
    y = _rms(h_ref[...], w_ref[...])
    o_ref[...] = (y * (1.0 + scale_ref[...]) + shift_ref[...]).astype(o_ref.dtype)


def _norm_modulate(h, w, mod, k_shift, k_scale, geo, n_tiles):
    n_tok, d = h.shape
    tm = geo["tm"]
    cls = geo["tile_class"]
    return pl.pallas_call(
        _normmod_kernel,
        grid=(n_tiles,),
        in_specs=[pl.BlockSpec((tm, d), lambda i: (i, 0)),
                  pl.BlockSpec((1, d), lambda i: (0, 0)),
                  pl.BlockSpec((None, None, 1, d), lambda i: (k_shift, cls(i), 0, 0)),
                  pl.BlockSpec((None, None, 1, d), lambda i: (k_scale, cls(i), 0, 0))],
        out_specs=pl.BlockSpec((tm, d), lambda i: (i, 0)),
        out_shape=jax.ShapeDtypeStruct((n_tok, d), BF16),
        compiler_params=_cparams(("arbitrary",), 48 * 1024 * 1024),
        name="norm_modulate",
    )(h, w.reshape(1, d), mod, mod)


def _mm_kernel(a_ref, w_ref, o_ref):
    o_ref[...] = _dot(a_ref[...], w_ref[...]).astype(o_ref.dtype)


def _matmul(a, w, tm, tn, out_dtype):
    m, k = a.shape
    n = w.shape[1]
    return pl.pallas_call(
        _mm_kernel,
        grid=(n // tn, m // tm),
        in_specs=[pl.BlockSpec((tm, k), lambda j, i: (i, 0)),
                  pl.BlockSpec((k, tn), lambda j, i: (0, j))],
        out_specs=pl.BlockSpec((tm, tn), lambda j, i: (i, j)),
        out_shape=jax.ShapeDtypeStruct((m, n), out_dtype),
        compiler_params=_cparams(("arbitrary", "arbitrary"), 40 * 1024 * 1024),
        name="matmul",
    )(a, w)


def _mm_res_kernel(a_ref, w_ref, h_ref, g_ref, o_ref):
    o_ref[...] = h_ref[...] + g_ref[...] * _dot(a_ref[...], w_ref[...])


def _mm_res_f32w_kernel(a_ref, w_ref, h_ref, g_ref, o_ref, wb_ref):
    @pl.when(pl.program_id(1) == 0)
    def _():
        wb_ref[...] = w_ref[...].astype(BF16)

    o_ref[...] = h_ref[...] + g_ref[...] * _dot(a_ref[...], wb_ref[...])


def _layer_weight_spec(w, layer, tn):
    if w.ndim == 3:
        return pl.BlockSpec((None, w.shape[1], tn), lambda j, i: (layer, 0, j))
    return pl.BlockSpec((w.shape[0], tn), lambda j, i: (0, j))


def _matmul_gated_residual(a, w, h, mod, k_gate, geo, n_tiles, tn, layer=None):
    m, k = a.shape
    n = w.shape[-1]
    tm = geo["tm"]
    cls = geo["tile_class"]
    f32w = w.dtype == F32
    return pl.pallas_call(
        _mm_res_f32w_kernel if f32w else _mm_res_kernel,
        grid=(n // tn, n_tiles),
        in_specs=[pl.BlockSpec((tm, k), lambda j, i: (i, 0)),
                  _layer_weight_spec(w, layer, tn),
                  pl.BlockSpec((tm, tn), lambda j, i: (i, j)),
                  pl.BlockSpec((None, None, 1, tn), lambda j, i: (k_gate, cls(i), 0, j))],
        out_specs=pl.BlockSpec((tm, tn), lambda j, i: (i, j)),
        out_shape=jax.ShapeDtypeStruct((m, n), F32),
        scratch_shapes=[pltpu.VMEM((k, tn), BF16)] if f32w else [],
        compiler_params=_cparams(("arbitrary", "arbitrary"), VMEM_LIMIT),
        name="matmul_gated_residual",
    )(a, w, h, mod)


def _ffn1_kernel(a_ref, w1_ref, w3_ref, o_ref, w1b_ref, w3b_ref):
    @pl.when(pl.program_id(1) == 0)
    def _():
        w1b_ref[...] = w1_ref[...].astype(BF16)
        w3b_ref[...] = w3_ref[...].astype(BF16)

    a = a_ref[...]
    u = _dot(a, w1b_ref[...])
    v = _dot(a, w3b_ref[...])
    o_ref[...] = (u * _sigmoid(u) * v).astype(o_ref.dtype)


def _ffn_in(a, w1, w3, layer, geo, n_tiles, tn):
    m, k = a.shape
    n = w1.shape[-1]
    tm = geo["tm"]
    return pl.pallas_call(
        _ffn1_kernel,
        grid=(pl.cdiv(n, tn), n_tiles),
        in_specs=[pl.BlockSpec((tm, k), lambda j, i: (i, 0)),
                  _layer_weight_spec(w1, layer, tn),
                  _layer_weight_spec(w3, layer, tn)],
        out_specs=pl.BlockSpec((tm, tn), lambda j, i: (i, j)),
        out_shape=jax.ShapeDtypeStruct((m, n), BF16),
        scratch_shapes=[pltpu.VMEM((k, tn), BF16), pltpu.VMEM((k, tn), BF16)],
        compiler_params=_cparams(("arbitrary", "arbitrary"), VMEM_LIMIT),
        name="ffn_in",
    )(a, w1, w3)


def _rope_tail(t, cos_ref, sina_ref, sinb_ref):
    return (t * cos_ref[...] + pltpu.roll(t, 96, 1) * sina_ref[...]
            + pltpu.roll(t, 32, 1) * sinb_ref[...])


def _prep_kernel(na_ref, cq_ref, kr_ref, ckv_ref, naqw_ref, nakw_ref, cqw_ref, ckvw_ref,
                 wuq_ref, wukv_ref, mqw_ref, mkw_ref, cos_ref, sina_ref, sinb_ref,
                 naq_ref, nak_ref, nav_ref, mq_ref, mk_ref, mv_ref):
    hd = NA_HEAD_DIM
    tm = na_ref.shape[0]
    ones_col = jnp.where(lax.broadcasted_iota(jnp.int32, (tm, LANE), 1) == 0, 1.0, 0.0).astype(BF16)
    na_qs = naqw_ref[...] * (NA_HEAD_DIM ** -0.5 * LOG2E)
    for h in range(N_HEADS):
        naq_ref[:, h * hd:(h + 1) * hd] = _rms(na_ref[:, h * hd:(h + 1) * hd], na_qs).astype(BF16)
        nak_ref[:, h * hd:(h + 1) * hd] = _rms(
            na_ref[:, GROUP_WIDTH + h * hd:GROUP_WIDTH + (h + 1) * hd], nakw_ref[...]).astype(BF16)
        vo = h * V_HEAD_PAD
        nav_ref[:, vo:vo + hd] = na_ref[:, 2 * GROUP_WIDTH + h * hd:2 * GROUP_WIDTH + (h + 1) * hd].astype(BF16)
        nav_ref[:, vo + hd:vo + V_HEAD_PAD] = ones_col

    cq = _rms(cq_ref[...], cqw_ref[...]).astype(BF16)
    q = _dot(cq, wuq_ref[...])
    inv_qk = 1.0 / MLA_QK
    mqw = mqw_ref[...] * (MLA_QK ** -0.5 * LOG2E)
    for h in range(N_HEADS):
        o = h * MLA_HEAD_PAD
        nope = q[:, o:o + MLA_NOPE]
        tail = q[:, o + MLA_NOPE:o + MLA_HEAD_PAD]
        ss = jnp.sum(nope * nope, axis=-1, keepdims=True) + jnp.sum(tail * tail, axis=-1, keepdims=True)
        r = lax.rsqrt(ss * inv_qk + NORM_EPS)
        mq_ref[:, o:o + MLA_NOPE] = (nope * r * mqw[:, :MLA_NOPE]).astype(BF16)
        mq_ref[:, o + MLA_NOPE:o + MLA_HEAD_PAD] = _rope_tail(
            tail * r * mqw[:, MLA_NOPE:], cos_ref, sina_ref, sinb_ref).astype(BF16)

    ckv = ckv_ref[...]
    ckv_ms = jnp.sum(ckv * ckv, axis=-1, keepdims=True) * (1.0 / MLA_KV_RANK)
    ckv_n = (ckv * lax.rsqrt(ckv_ms + NORM_EPS) * ckvw_ref[...]).astype(BF16)
    kv = _dot(ckv_n, wukv_ref[...])
    for h in range(N_HEADS):
        vo = h * V_HEAD_PAD
        mv_ref[:, vo:vo + MLA_V] = kv[:, GROUP_WIDTH + h * MLA_V:GROUP_WIDTH + (h + 1) * MLA_V].astype(BF16)
        mv_ref[:, vo + MLA_V:vo + V_HEAD_PAD] = ones_col
    kr = kr_ref[...]
    kr_ss = jnp.sum(kr * kr, axis=-1, keepdims=True)
    mkw = mkw_ref[...]
    for h in range(N_HEADS):
        o = h * MLA_HEAD_PAD
        kn = kv[:, h * MLA_NOPE:(h + 1) * MLA_NOPE]
        r = lax.rsqrt((jnp.sum(kn * kn, axis=-1, keepdims=True) + kr_ss) * inv_qk + NORM_EPS)
        mk_ref[:, o:o + MLA_NOPE] = (kn * r * mkw[:, :MLA_NOPE]).astype(BF16)
        mk_ref[:, o + MLA_NOPE:o + MLA_HEAD_PAD] = _rope_tail(
            kr * r * mkw[:, MLA_NOPE:], cos_ref, sina_ref, sinb_ref).astype(BF16)


def _prep_qkv(p, lw, rope, tm):
    n_tok = p.shape[0]
    blk = lambda width, col: pl.BlockSpec((tm, width), lambda i: (i, col // width))
    full = lambda a: pl.BlockSpec(a.shape, lambda i: (0,) * a.ndim)
    tab = pl.BlockSpec((tm, LANE), lambda i: (i, 0))
    out_w = [GROUP_WIDTH, GROUP_WIDTH, N_HEADS * V_HEAD_PAD,
             N_HEADS * MLA_HEAD_PAD, N_HEADS * MLA_HEAD_PAD, N_HEADS * V_HEAD_PAD]
    weights = [lw["na_q_norm_w"], lw["na_k_norm_w"], lw["mla_cq_norm_w"], lw["mla_ckv_norm_w"],
               lw["mla_w_uq"], lw["mla_w_ukv"], lw["mla_q_norm_w"], lw["mla_k_norm_w"]]
    return pl.pallas_call(
        _prep_kernel,
        grid=(n_tok // tm,),
        in_specs=[blk(3 * GROUP_WIDTH, COL_NA), blk(MLA_Q_RANK, COL_CQ),
                  blk(LANE, COL_KR), blk(MLA_KV_RANK_PAD, COL_CKV)]
        + [full(a) for a in weights] + [tab, tab, tab],
        out_specs=[pl.BlockSpec((tm, w), lambda i: (i, 0)) for w in out_w],
        out_shape=[jax.ShapeDtypeStruct((n_tok, w), BF16) for w in out_w],
        compiler_params=_cparams(("arbitrary",), 48 * 1024 * 1024),
        name="prep_qkv",
    )(p, p, p, p, *weights, *rope)


def _softmax_attend(q, pairs, bias=None):
    scores = []
    for idx, (k, _) in enumerate(pairs):
        s = _dot_nt(q, k)
        if idx == 0 and bias is not None:
            s = s + bias
        scores.append(s)
    m = scores[0].max(axis=-1, keepdims=True)
    for s in scores[1:]:
        m = jnp.maximum(m, s.max(axis=-1, keepdims=True))
    acc = None
    for s, (_, v) in zip(scores, pairs):
        o = _dot(jnp.exp2(s - m).astype(BF16), v)
        acc = o if acc is None else acc + o
    dv = acc.shape[-1] // 2
    return acc[:, :dv] / acc[:, dv:dv + 1]


def _na_kernel(q_ref, k_ref, v_ref, kc_ref, vc_ref, kca_ref, vca_ref, *rest,
               n_lat_steps, steps_per_batch, n_rows, n_sub, ctx):
    bias_refs, o_ref = rest[:n_sub], rest[n_sub]
    i = pl.program_id(1)
    sq = NA_Q_ROWS * GRID_W

    @pl.when(i < n_lat_steps)
    def _():
        step = i % steps_per_batch
        for sb in range(n_sub):
            rb = step * n_sub + sb
            ks = jnp.clip(rb * NA_Q_ROWS - NA_WIN_ROWS // 2, 0, n_rows - NA_K_ROWS) * GRID_W
            ks = pl.multiple_of(ks, GRID_W)
            kw = k_ref[pl.ds(ks, NA_K_ROWS * GRID_W), :]
            vw = v_ref[pl.ds(ks, NA_K_ROWS * GRID_W), :]
            rows = slice(sb * sq, (sb + 1) * sq)
            o_ref[rows, :] = _softmax_attend(q_ref[rows, :], [(kw, vw), (kc_ref[...], vc_ref[...])],
                                             bias_refs[sb][...])

    @pl.when(i == n_lat_steps)
    def _():
        for bb in range(q_ref.shape[0] // ctx):
            rows = slice(bb * ctx, (bb + 1) * ctx)
            o_ref[rows, :] = _softmax_attend(q_ref[rows, :], [(kca_ref[rows, :], vca_ref[rows, :])])


def _na_attention(naq, nak, nav, bias, geo, with_ctx):
    b, t, ctx = geo["b"], geo["t"], geo["ctx"]
    n_tok = naq.shape[0]
    sq = NA_Q_ROWS * GRID_W
    tq = b * ctx
    assert tq % sq == 0 and t % tq == 0
    n_sub = tq // sq
    n_rows = t // GRID_W
    nrb = n_rows // NA_Q_ROWS
    spb = t // tq
    lat_steps = b * spb
    ctx_block0 = b * t // ctx
    batch = lambda i: jnp.minimum(i // spb, b - 1)

    def bias_spec(sb):
        def bias_map(h, i):
            rb = (i % spb) * n_sub + sb
            return (h, jnp.where(rb == 0, 0, jnp.where(rb >= nrb - 1, 2, 1)), 0, 0)
        return pl.BlockSpec((None, None, sq, NA_K_ROWS * GRID_W), bias_map)

    return pl.pallas_call(
        functools.partial(_na_kernel, n_lat_steps=lat_steps, steps_per_batch=spb, n_rows=n_rows,
                          n_sub=n_sub, ctx=ctx),
        grid=(N_HEADS, lat_steps + (1 if with_ctx else 0)),
        in_specs=[pl.BlockSpec((tq, NA_HEAD_DIM), lambda h, i: (i, h)),
                  pl.BlockSpec((t, NA_HEAD_DIM), lambda h, i: (batch(i), h)),
                  pl.BlockSpec((t, V_HEAD_PAD), lambda h, i: (batch(i), h)),
                  pl.BlockSpec((ctx, NA_HEAD_DIM), lambda h, i: (ctx_block0 + batch(i), h)),
                  pl.BlockSpec((ctx, V_HEAD_PAD), lambda h, i: (ctx_block0 + batch(i), h)),
                  pl.BlockSpec((tq, NA_HEAD_DIM), lambda h, i: (lat_steps, h)),
                  pl.BlockSpec((tq, V_HEAD_PAD), lambda h, i: (lat_steps, h))]
        + [bias_spec(sb) for sb in range(n_sub)],
        out_specs=pl.BlockSpec((tq, NA_HEAD_DIM), lambda h, i: (i, h)),
        out_shape=jax.ShapeDtypeStruct((n_tok, GROUP_WIDTH), F32),
        compiler_params=_cparams(("arbitrary", "arbitrary"), 40 * 1024 * 1024),
        name="na_attention",
    )(naq, nak, nav, nak, nav, nak, nav, *([bias] * n_sub))


def _flash_attend(q, loads):
    m = None
    acc = None
    for load in loads:
        k, v = load()
        s = _dot_nt(q, k)
        mc = s.max(axis=-1, keepdims=True)
        if m is None:
            m = mc
            acc = _dot(jnp.exp2(s - m).astype(BF16), v)
        else:
            m_new = jnp.maximum(m, mc)
            acc = acc * jnp.exp2(m - m_new) + _dot(jnp.exp2(s - m_new).astype(BF16), v)
            m = m_new
    dv = acc.shape[-1] // 2
    return acc[:, :dv] / acc[:, dv:dv + 1]


def _mla_kernel(q_ref, k_ref, v_ref, kc_ref, vc_ref, kca_ref, vca_ref, o_ref, *, n_q_blocks, key_chunk, ctx):
    i = pl.program_id(1)

    @pl.when(i < n_q_blocks)
    def _():
        def lat_chunk(j):
            rows = slice(j * key_chunk, (j + 1) * key_chunk)
            return lambda: (k_ref[rows, :], v_ref[rows, :])

        loads = [lat_chunk(j) for j in range(k_ref.shape[0] // key_chunk)]
        loads.append(lambda: (kc_ref[...], vc_ref[...]))
        o_ref[...] = _flash_attend(q_ref[...], loads)

    @pl.when(i == n_q_blocks)
    def _():
        for bb in range(q_ref.shape[0] // ctx):
            rows = slice(bb * ctx, (bb + 1) * ctx)
            o_ref[rows, :] = _flash_attend(q_ref[rows, :], [lambda: (kca_ref[rows, :], vca_ref[rows, :])])


def _mla_attention(mq, mk, mv, geo, with_ctx):
    b, t, ctx = geo["b"], geo["t"], geo["ctx"]
    n_tok = mq.shape[0]
    tq = b * ctx
    assert t % tq == 0
    nqb = t // tq
    lat_blocks = b * nqb
    ctx_block0 = b * t // ctx
    batch = lambda i: jnp.minimum(i // nqb, b - 1)

    return pl.pallas_call(
        functools.partial(_mla_kernel, n_q_blocks=lat_blocks, key_chunk=512, ctx=ctx),
        grid=(N_HEADS, lat_blocks + (1 if with_ctx else 0)),
        in_specs=[pl.BlockSpec((tq, MLA_HEAD_PAD), lambda h, i: (i, h)),
                  pl.BlockSpec((t, MLA_HEAD_PAD), lambda h, i: (batch(i), h)),
                  pl.BlockSpec((t, V_HEAD_PAD), lambda h, i: (batch(i), h)),
                  pl.BlockSpec((ctx, MLA_HEAD_PAD), lambda h, i: (ctx_block0 + batch(i), h)),
                  pl.BlockSpec((ctx, V_HEAD_PAD), lambda h, i: (ctx_block0 + batch(i), h)),
                  pl.BlockSpec((tq, MLA_HEAD_PAD), lambda h, i: (lat_blocks, h)),
                  pl.BlockSpec((tq, V_HEAD_PAD), lambda h, i: (lat_blocks, h))],
        out_specs=pl.BlockSpec((tq, MLA_V), lambda h, i: (i, h)),
        out_shape=jax.ShapeDtypeStruct((n_tok, GROUP_WIDTH), F32),
        compiler_params=_cparams(("arbitrary", "arbitrary"), 48 * 1024 * 1024),
        name="mla_attention",
    )(mq, mk, mv, mk, mv, mk, mv)


def _s5_contrib_kernel(u_ref, b_ref, o_ref):
    r = [_dot_f32(u_ref[:, m * S5_BLOCK:(m + 1) * S5_BLOCK], b_ref[m]) for m in range(2)]
    ps = S5_STATE
    for k in range(4):
        o_ref[k] = jnp.concatenate([r[0][:, k * ps:(k + 1) * ps], r[1][:, k * ps:(k + 1) * ps]], axis=-1)


def _s5_scan_kernel(c_ref, lr_ref, li_ref, x_ref, *, n_tiles, n_ctx_tiles, nb):
    sub = 8
    n_lat_tiles = n_tiles - n_ctx_tiles
    lanes = c_ref.shape[-1]
    lr = jnp.broadcast_to(lr_ref[...], (sub, lanes))
    li = jnp.broadcast_to(li_ref[...], (sub, lanes))
    row = lax.broadcasted_iota(jnp.int32, (sub, lanes), 0)

    def run(reverse):
        order = range(sub - 1, -1, -1) if reverse else range(sub)

        def body(jt, carry):
            if reverse:
                tile = jnp.where(jt < n_ctx_tiles, n_ctx_tiles - 1 - jt, n_tiles - 1 - (jt - n_ctx_tiles))
            else:
                tile = jt
            new = []
            for bi in range(nb):
                xr, xi = carry[2 * bi], carry[2 * bi + 1]
                row_tile = jnp.where(tile < n_ctx_tiles, nb * n_lat_tiles + bi * n_ctx_tiles + tile,
                                     bi * n_lat_tiles + tile - n_ctx_tiles)
                r0 = pl.multiple_of(row_tile * sub, sub)
                cr = c_ref[0, pl.ds(r0, sub), :]
                ci = c_ref[1, pl.ds(r0, sub), :]
                out_r = jnp.zeros_like(cr)
                out_i = jnp.zeros_like(ci)
                for k in order:
                    out_r = jnp.where(row == k, xr, out_r)
                    out_i = jnp.where(row == k, xi, out_i)
                    ck_r = jnp.broadcast_to(cr[k:k + 1, :], (sub, lanes))
                    ck_i = jnp.broadcast_to(ci[k:k + 1, :], (sub, lanes))
                    xr, xi = lr * xr - li * xi + ck_r, lr * xi + li * xr + ck_i
                x_ref[0, pl.ds(r0, sub), :] = out_r
                x_ref[1, pl.ds(r0, sub), :] = out_i
                new += [xr, xi]
            return tuple(new)

        zero = jnp.zeros((sub, lanes), F32)
        lax.fori_loop(0, n_tiles, body, (zero,) * (2 * nb))

    @pl.when(pl.program_id(0) == 0)
    def _():
        run(False)

    @pl.when(pl.program_id(0) == 1)
    def _():
        run(True)


def _s5_out_kernel(u_ref, x_ref, t_ref, c_ref, o_ref):
    ps = S5_STATE
    for m in range(2):
        x = jnp.concatenate([x_ref[k][:, m * ps:(m + 1) * ps] for k in range(4)], axis=-1)
        cols = slice(m * S5_BLOCK, (m + 1) * S5_BLOCK)
        o_ref[:, cols] = _dot_f32(u_ref[:, cols], t_ref[m]) + _dot_f32(x, c_ref[m])


def _s5_mixer(p, ops, geo):
    b, t, ctx = geo["b"], geo["t"], geo["ctx"]
    n_chunks = (ctx + t) // S5_CHUNK
    rows = n_chunks * b
    n_pairs = S5_GROUPS // 2
    state_w = S5_GROUPS * S5_STATE
    pair_w = 2 * S5_BLOCK

    u2 = p[:, COL_S5:COL_S5 + GROUP_WIDTH].reshape(rows, S5_CHUNK, S5_GROUPS, S5_GROUP)
    u2 = u2.transpose(0, 2, 1, 3).reshape(rows, S5_GROUPS * S5_BLOCK)
    sub = 8
    assert (t // S5_CHUNK) % sub == 0 and (ctx // S5_CHUNK) % sub == 0

    contrib = pl.pallas_call(
        _s5_contrib_kernel,
        grid=(n_pairs,),
        in_specs=[pl.BlockSpec((rows, pair_w), lambda q: (0, q)),
                  pl.BlockSpec((2, S5_BLOCK, S5_BLOCK), lambda q: (q, 0, 0))],
        out_specs=pl.BlockSpec((4, rows, LANE), lambda q: (0, 0, q)),
        out_shape=jax.ShapeDtypeStruct((4, rows, state_w), F32),
        compiler_params=_cparams(("arbitrary",), 40 * 1024 * 1024),
        name="s5_contrib",
    )(u2, ops["b_mat"])

    lane_blk = 512
    states = pl.pallas_call(
        functools.partial(_s5_scan_kernel, n_tiles=n_chunks // sub, n_ctx_tiles=ctx // S5_CHUNK // sub, nb=b),
        grid=(2, state_w // lane_blk),
        in_specs=[pl.BlockSpec((None, 2, rows, lane_blk), lambda d, l: (d, 0, 0, l)),
                  pl.BlockSpec((None, 1, lane_blk), lambda d, l: (d, 0, l)),
                  pl.BlockSpec((None, 1, lane_blk), lambda d, l: (d, 0, l))],
        out_specs=pl.BlockSpec((None, 2, rows, lane_blk), lambda d, l: (d, 0, 0, l)),
        out_shape=jax.ShapeDtypeStruct((2, 2, rows, state_w), F32),
        compiler_params=_cparams(("arbitrary", "arbitrary"), 40 * 1024 * 1024),
        name="s5_scan",
    )(contrib.reshape(2, 2, rows, state_w), ops["decay_re"], ops["decay_im"])

    y2 = pl.pallas_call(
        _s5_out_kernel,
        grid=(n_pairs,),
        in_specs=[pl.BlockSpec((rows, pair_w), lambda q: (0, q)),
                  pl.BlockSpec((4, rows, LANE), lambda q: (0, 0, q)),
                  pl.BlockSpec((2, S5_BLOCK, S5_BLOCK), lambda q: (q, 0, 0)),
                  pl.BlockSpec((2, S5_BLOCK, S5_BLOCK), lambda q: (q, 0, 0))],
        out_specs=pl.BlockSpec((rows, pair_w), lambda q: (0, q)),
        out_shape=jax.ShapeDtypeStruct((rows, S5_GROUPS * S5_BLOCK), F32),
        compiler_params=_cparams(("arbitrary",), 40 * 1024 * 1024),
        name="s5_out",
    )(u2, states.reshape(4, rows, state_w), ops["t_sum"], ops["c_mat"])

    y = y2.reshape(rows, S5_GROUPS, S5_CHUNK, S5_GROUP).transpose(0, 2, 1, 3)
    return y.reshape(rows * S5_CHUNK, GROUP_WIDTH)


def _s5_operators(lam_re, lam_im, log_dt, b_re, b_im, c_re, c_im, d_skip):
    hp = lax.Precision.HIGHEST
    g, pn, ni, lc = S5_GROUPS, S5_STATE, S5_GROUP, S5_CHUNK
    dt = jnp.exp(log_dt)[..., None]
    zr, zi = lam_re * dt, lam_im * dt
    up = jnp.arange(lc, dtype=F32)
    down = (lc - 1) - up

    def powers(d, steps):
        mag = jnp.exp(zr[d][None] * steps[:, None, None])
        ang = zi[d][None] * steps[:, None, None]
        return mag * jnp.cos(ang), mag * jnp.sin(ang)

    one = jnp.ones((1,), F32)
    z1 = [powers(d, one) for d in range(2)]
    nr = jnp.stack([z1[0][0][0], z1[1][0][0]]) - 1.0
    nim = jnp.stack([z1[0][1][0], z1[1][1][0]])
    den = lam_re * lam_re + lam_im * lam_im
    cr_, ci_ = (nr * lam_re + nim * lam_im) / den, (nim * lam_re - nr * lam_im) / den
    bz_r = cr_[..., None] * b_re - ci_[..., None] * b_im
    bz_i = cr_[..., None] * b_im + ci_[..., None] * b_re

    def lag_kernel(d, steps):
        pr, pi = powers(d, steps)
        m_r = pr[..., None] * bz_r[d][None] - pi[..., None] * bz_i[d][None]
        m_i = pr[..., None] * bz_i[d][None] + pi[..., None] * bz_r[d][None]
        k = (jnp.einsum("gop,dgpi->dgoi", c_re[d], m_r, precision=hp)
             - jnp.einsum("gop,dgpi->dgoi", c_im[d], m_i, precision=hp))
        return k.transpose(1, 3, 0, 2)

    k_f = lag_kernel(0, up)
    k_b = lag_kernel(1, down)
    skip = (jnp.eye(ni, dtype=F32)[None] * d_skip.reshape(g, ni, 1))[:, :, None]
    two_sided = jnp.concatenate([k_b[:, :, :lc - 1], k_f[:, :, :1] + k_b[:, :, lc - 1:] + skip, k_f[:, :, 1:]],
                                axis=2)
    t_sum = jnp.stack([two_sided[:, :, lc - 1 - s:2 * lc - 1 - s] for s in range(lc)], axis=1)
    t_sum = t_sum.reshape(g, S5_BLOCK, S5_BLOCK)

    def in_op(d, steps):
        er, ei = powers(d, steps)
        er, ei = er.transpose(1, 0, 2)[:, :, None, :], ei.transpose(1, 0, 2)[:, :, None, :]
        br, bi = bz_r[d].transpose(0, 2, 1)[:, None], bz_i[d].transpose(0, 2, 1)[:, None]
        return er * br - ei * bi, er * bi + ei * br

    bf_r, bf_i = in_op(0, down)
    bb_r, bb_i = in_op(1, up)
    b_mat = jnp.stack([bf_r, bf_i, bb_r, bb_i], axis=3).reshape(g, S5_BLOCK, 4 * pn)

    def out_op(d, steps):
        pr, pi = powers(d, steps)
        pr, pi = pr.transpose(1, 2, 0)[:, :, :, None], pi.transpose(1, 2, 0)[:, :, :, None]
        cr, ci = c_re[d].transpose(0, 2, 1)[:, :, None, :], c_im[d].transpose(0, 2, 1)[:, :, None, :]
        return cr * pr - ci * pi, -(cr * pi + ci * pr)

    cf_r, cf_i = out_op(0, up + 1.0)
    cb_r, cb_i = out_op(1, lc - up)
    c_mat = jnp.stack([cf_r, cf_i, cb_r, cb_i], axis=1).reshape(g, 4 * pn, S5_BLOCK)

    full = jnp.full((1,), float(lc), F32)
    decay = [powers(d, full) for d in range(2)]
    return {"t_sum": t_sum, "b_mat": b_mat, "c_mat": c_mat,
            "decay_re": jnp.stack([decay[0][0], decay[1][0]]).reshape(2, 1, g * pn),
            "decay_im": jnp.stack([decay[0][1], decay[1][1]]).reshape(2, 1, g * pn)}


def _group_norm_store(o_ref, k, y, w_ref):
    cols = slice(k * GROUP_WIDTH, (k + 1) * GROUP_WIDTH)
    o_ref[:, cols] = _rms(y, w_ref[:, cols]).astype(o_ref.dtype)


def _mix_kernel(ya_ref, yb_ref, gb_ref, gc_ref, u_ref, gcp_ref, up_ref, gcn_ref, un_ref,
                cw_ref, cb_ref, ys_ref, gw_ref, gbias_ref, mw_ref, o_ref, *, tm, lat_rows, t, ctx):
    i = pl.program_id(0)
    _group_norm_store(o_ref, 0, ya_ref[...], mw_ref)
    _group_norm_store(o_ref, 1, yb_ref[...], mw_ref)

    r0 = i * tm
    in_lat = r0 < lat_rows
    seq_len = jnp.where(in_lat, t, ctx)
    off = jnp.where(in_lat, r0, r0 - lat_rows) % seq_len
    has_prev = off != 0
    has_next = off + tm != seq_len
    v = gc_ref[...] * u_ref[...]
    v_prev_row = jnp.where(has_prev, gcp_ref[7:8, :] * up_ref[7:8, :], 0.0)
    v_next_row = jnp.where(has_next, gcn_ref[0:1, :] * un_ref[0:1, :], 0.0)
    row = lax.broadcasted_iota(jnp.int32, v.shape, 0)
    v_prev = jnp.where(row == 0, v_prev_row, pltpu.roll(v, 1, 0))
    v_next = jnp.where(row == tm - 1, v_next_row, pltpu.roll(v, tm - 1, 0))
    conv = cw_ref[0:1, :] * v_prev + cw_ref[1:2, :] * v + cw_ref[2:3, :] * v_next + cb_ref[...]
    _group_norm_store(o_ref, 2, gb_ref[...] * conv, mw_ref)

    y = ys_ref[...]
    g = 0.5 * y * (1.0 + jnp.tanh(math.sqrt(2.0 / math.pi) * (y + 0.044715 * (y * y * y))))
    gate = _sigmoid(_dot(g.astype(BF16), gw_ref[...]) + gbias_ref[...])
    _group_norm_store(o_ref, 3, g * gate, mw_ref)


def _mix_outputs(ya, yb, p, ys5, lw, geo, n_tiles):
    n_tok = ya.shape[0]
    tm = 256
    gw = GROUP_WIDTH
    halo = 8
    n_halo_blocks = n_tok // halo
    tile = lambda col: pl.BlockSpec((tm, gw), lambda i: (i, col // gw))
    prev = lambda col: pl.BlockSpec((halo, gw), lambda i: (jnp.maximum(i * (tm // halo) - 1, 0), col // gw))
    nxt = lambda col: pl.BlockSpec(
        (halo, gw), lambda i: (jnp.minimum((i + 1) * (tm // halo), n_halo_blocks - 1), col // gw))
    full = lambda a: pl.BlockSpec(a.shape, lambda i: (0,) * a.ndim)
    c_gb, c_gc, c_u = COL_CONV, COL_CONV + gw, COL_CONV + 2 * gw
    weights = [lw["conv_w"], lw["conv_b"]]
    glu = [lw["s5_glu_w"], lw["s5_glu_b"], lw["mix_norm_w"]]
    return pl.pallas_call(
        functools.partial(_mix_kernel, tm=tm, lat_rows=geo["b"] * geo["t"], t=geo["t"], ctx=geo["ctx"]),
        grid=(n_tiles * geo["tm"] // tm,),
        in_specs=[tile(0), tile(0), tile(c_gb), tile(c_gc), tile(c_u),
                  prev(c_gc), prev(c_u), nxt(c_gc), nxt(c_u)]
        + [full(a) for a in weights] + [tile(0)] + [full(a) for a in glu],
        out_specs=pl.BlockSpec((tm, 4 * gw), lambda i: (i, 0)),
        out_shape=jax.ShapeDtypeStruct((n_tok, 4 * gw), BF16),
        compiler_params=_cparams(("arbitrary",), 48 * 1024 * 1024),
        name="mix_outputs",
    )(ya, yb, p, p, p, p, p, p, p, *weights, ys5, *glu)


def _rope_reorder(a):
    q = MLA_ROPE // 4
    return jnp.concatenate([a[..., :q], a[..., 2 * q:3 * q], a[..., q:2 * q], a[..., 3 * q:]], axis=-1)


def _pad_cols(a, width):
    return jnp.pad(a, ((0, 0), (0, width - a.shape[1])))


def _layer_weights(l, pr):
    w_in = pr["w_in"][l]
    o_cq = 3 * GROUP_WIDTH
    o_ckv = o_cq + MLA_Q_RANK
    o_kr = o_ckv + MLA_KV_RANK
    o_conv = o_kr + MLA_ROPE
    o_s5 = o_conv + 3 * GROUP_WIDTH
    w_in_p = jnp.concatenate([
        w_in[:, :o_cq], w_in[:, o_conv:o_s5], w_in[:, o_s5:o_s5 + GROUP_WIDTH], w_in[:, o_cq:o_ckv],
        _pad_cols(_rope_reorder(w_in[:, o_kr:o_conv]), LANE), _pad_cols(w_in[:, o_ckv:o_kr], MLA_KV_RANK_PAD)],
        axis=1).astype(BF16)
    assert w_in_p.shape[1] == IN_WIDTH_PAD

    w_uq = pr["mla_w_uq"][l].reshape(MLA_Q_RANK, N_HEADS, MLA_QK)
    w_uq = jnp.concatenate([w_uq[:, :, :MLA_NOPE], _rope_reorder(w_uq[:, :, MLA_NOPE:]),
                            jnp.zeros((MLA_Q_RANK, N_HEADS, MLA_HEAD_PAD - MLA_QK), w_uq.dtype)], axis=2)
    w_ukv = pr["mla_w_ukv"][l].reshape(MLA_KV_RANK, N_HEADS, MLA_NOPE + MLA_V)
    w_ukv = jnp.concatenate([w_ukv[:, :, :MLA_NOPE].reshape(MLA_KV_RANK, -1),
                             w_ukv[:, :, MLA_NOPE:].reshape(MLA_KV_RANK, -1)], axis=1)
    w_ukv = jnp.pad(w_ukv, ((0, MLA_KV_RANK_PAD - MLA_KV_RANK), (0, 0)))

    def head_norm_w(w):
        return _pad_cols(jnp.concatenate([w[:MLA_NOPE], _rope_reorder(w[MLA_NOPE:])])[None, :], MLA_HEAD_PAD)

    return {
        "w_in": w_in_p,
        "na_q_norm_w": pr["na_q_norm_w"][l][None, :],
        "na_k_norm_w": pr["na_k_norm_w"][l][None, :],
        "mla_cq_norm_w": pr["mla_cq_norm_w"][l][None, :],
        "mla_ckv_norm_w": _pad_cols(pr["mla_ckv_norm_w"][l][None, :], MLA_KV_RANK_PAD),
        "mla_w_uq": w_uq.reshape(MLA_Q_RANK, N_HEADS * MLA_HEAD_PAD).astype(BF16),
        "mla_w_ukv": w_ukv.astype(BF16),
        "mla_q_norm_w": head_norm_w(pr["mla_q_norm_w"][l]),
        "mla_k_norm_w": head_norm_w(pr["mla_k_norm_w"][l]),
        "conv_w": pr["conv_w"][l],
        "conv_b": pr["conv_b"][l][None, :],
        "s5_glu_w": pr["s5_glu_w"][l].astype(BF16),
        "s5_glu_b": pr["s5_glu_b"][l][None, :],
        "mix_norm_w": pr["mix_norm_w"][l][None, :],
        "ffn_w2": pr["ffn_w2"][l].astype(BF16),
    }


def _rope_tables(b, t, n_ctx):
    pos = jnp.arange(t, dtype=jnp.int32)
    row = (pos // GRID_W).astype(F32)
    col = (pos % GRID_W).astype(F32)
    n_freq = MLA_ROPE // 4
    inv_freq = ROPE_THETA ** (-jnp.arange(n_freq, dtype=F32) / n_freq)
    ang = jnp.concatenate([row[:, None] * inv_freq[None, :], col[:, None] * inv_freq[None, :]], axis=1)
    cos, sin = jnp.cos(ang), jnp.sin(ang)
    half = MLA_ROPE // 2
    zeros = jnp.zeros((t, half), F32)
    cos_t = jnp.concatenate([cos, cos, jnp.ones((t, LANE - 2 * half), F32)], axis=1)
    sina_t = jnp.concatenate([-sin, zeros, zeros, zeros], axis=1)
    sinb_t = jnp.concatenate([zeros, sin, zeros, zeros], axis=1)
    n_c = b * n_ctx
    tables = []
    for tab, fill in ((cos_t, 1.0), (sina_t, 0.0), (sinb_t, 0.0)):
        tables.append(jnp.concatenate([jnp.tile(tab, (b, 1)), jnp.full((n_c, LANE), fill, F32)], axis=0))
    return tables


def _na_bias_tables(rpb, n_rows):
    w = GRID_W
    half = NA_WIN_ROWS // 2
    r0 = jnp.array([0, NA_Q_ROWS, n_rows - NA_Q_ROWS])
    ks = jnp.clip(r0 - half, 0, n_rows - NA_K_ROWS)
    rq = r0[:, None] + jnp.arange(NA_Q_ROWS)[None, :]
    rk = ks[:, None] + jnp.arange(NA_K_ROWS)[None, :]
    rs = jnp.clip(rq - half, 0, n_rows - NA_WIN_ROWS)
    v_row = (rk[:, None, :] >= rs[:, :, None]) & (rk[:, None, :] < rs[:, :, None] + NA_WIN_ROWS)
    d_row = jnp.clip(rk[:, None, :] - rq[:, :, None] + (NA_WIN_ROWS - 1), 0, 2 * NA_WIN_ROWS - 2)
    qc = jnp.arange(w)
    cs = jnp.clip(qc - NA_WIN_COLS // 2, 0, w - NA_WIN_COLS)
    v_col = (qc[None, :] >= cs[:, None]) & (qc[None, :] < cs[:, None] + NA_WIN_COLS)
    d_col = jnp.clip(qc[None, :] - qc[:, None], -(NA_WIN_COLS - 1), NA_WIN_COLS - 1) + (NA_WIN_COLS - 1)
    sel_r = jax.nn.one_hot(d_row, 2 * NA_WIN_ROWS - 1, dtype=F32)
    sel_c = jax.nn.one_hot(d_col, 2 * NA_WIN_COLS - 1, dtype=F32)
    bias = jnp.einsum("hab,cija,qkb->hciqjk", rpb * LOG2E, sel_r, sel_c, precision=lax.Precision.HIGHEST)
    valid = v_row[:, :, None, :, None] & v_col[None, None, :, None, :]
    bias = jnp.where(valid[None], bias, NEG_INF)
    return bias.reshape(rpb.shape[0], 3, NA_Q_ROWS * w, NA_K_ROWS * w)


def kernel(x, c, ctx, c_ctx, ada_w, ada_b, norm1_w, norm2_w, w_in, na_q_norm_w, na_k_norm_w, na_rpb, mla_cq_norm_w, mla_ckv_norm_w, mla_w_uq, mla_w_ukv, mla_q_norm_w, mla_k_norm_w, conv_w, conv_b, s5_lambda_re, s5_lambda_im, s5_log_dt, s5_b_re, s5_b_im, s5_c_re, s5_c_im, s5_d, s5_glu_w, s5_glu_b, mix_norm_w, w_out, ffn_w1, ffn_w3, ffn_w2):
    b, t, d = x.shape
    n_ctx = ctx.shape[1]
    n_layers = ada_w.shape[0]
    n_rows = t // GRID_W
    assert t % GRID_W == 0 and n_rows >= NA_K_ROWS and n_rows % NA_Q_ROWS == 0
    assert n_ctx == NA_Q_ROWS * GRID_W and t % n_ctx == 0 and b + 1 <= 8
    tm = 512 if (b * n_ctx) % 512 == 0 and t % 512 == 0 else 256
    tiles_per_batch = t // tm
    geo = {"b": b, "t": t, "ctx": n_ctx, "tm": tm,
           "tile_class": lambda i: jnp.minimum(i // tiles_per_batch, b)}
    lat_tiles = b * t // tm
    all_tiles = lat_tiles + b * n_ctx // tm

    pr = dict(w_in=w_in, na_q_norm_w=na_q_norm_w, na_k_norm_w=na_k_norm_w, mla_cq_norm_w=mla_cq_norm_w,
              mla_ckv_norm_w=mla_ckv_norm_w, mla_w_uq=mla_w_uq, mla_w_ukv=mla_w_ukv,
              mla_q_norm_w=mla_q_norm_w, mla_k_norm_w=mla_k_norm_w, conv_w=conv_w, conv_b=conv_b,
              s5_glu_w=s5_glu_w, s5_glu_b=s5_glu_b, mix_norm_w=mix_norm_w, ffn_w2=ffn_w2)

    h = jnp.concatenate([x.reshape(b * t, d), ctx.reshape(b * n_ctx, d)], axis=0)
    cvec = jnp.zeros((8, d), F32).at[:b].set(c).at[b].set(c_ctx)
    mod_all = _ada_mod(cvec, ada_w, ada_b)
    rope = _rope_tables(b, t, n_ctx)

    for l in range(n_layers):
        last = l == n_layers - 1
        n_tiles = lat_tiles if last else all_tiles
        lw = _layer_weights(l, pr)
        mod = mod_all[l]

        a = _norm_modulate(h, norm1_w[l], mod, 0, 1, geo, all_tiles)
        p = _matmul(a, lw["w_in"], tm, 512, F32)
        naq, nak, nav, mq, mk, mv = _prep_qkv(p, lw, rope, 256)
        ya = _na_attention(naq, nak, nav, _na_bias_tables(na_rpb[l], n_rows), geo, not last)
        yb = _mla_attention(mq, mk, mv, geo, not last)
        ops = _s5_operators(s5_lambda_re[l], s5_lambda_im[l], s5_log_dt[l], s5_b_re[l], s5_b_im[l],
                            s5_c_re[l], s5_c_im[l], s5_d[l])
        ys5 = _s5_mixer(p, ops, geo)
        y = _mix_outputs(ya, yb, p, ys5, lw, geo, n_tiles)
        h = _matmul_gated_residual(y, w_out, h, mod, 2, geo, n_tiles, 512, layer=l)

        f = _norm_modulate(h, norm2_w[l], mod, 3, 4, geo, n_tiles)
        g = _ffn_in(f, ffn_w1, ffn_w3, l, geo, n_tiles, 512)
        h = _matmul_gated_residual(g, lw["ffn_w2"], h, mod, 5, geo, n_tiles, 512)

    return h[:b * t].reshape(b, t, d)
```

```python
import functools
import math

import jax
import jax.numpy as jnp
from jax import lax
from jax.experimental import pallas as pl
from jax.experimental.pallas import tpu as pltpu

F32 = jnp.float32
BF16 = jnp.bfloat16

NORM_EPS = 1e-6
NEG_INF = -1e30
GRID_W = 64

GROUP_WIDTH = 1024
N_HEADS = 8
NA_HEAD_DIM = 128
NA_WIN_ROWS = 8
NA_WIN_COLS = 16
NA_Q_ROWS = 4
NA_K_ROWS = NA_Q_ROWS + NA_WIN_ROWS

MLA_NOPE = 128
MLA_ROPE = 64
MLA_QK = MLA_NOPE + MLA_ROPE
MLA_V = 128
MLA_Q_RANK = 896
MLA_KV_RANK = 320
MLA_KV_RANK_PAD = 512
MLA_HEAD_PAD = 256
V_HEAD_PAD = 256
LOG2E = math.log2(math.e)
ROPE_THETA = 10000.0

CONV_K = 3

S5_GROUPS = 64
S5_GROUP = 16
S5_STATE = 64
S5_CHUNK = 16
S5_BLOCK = S5_CHUNK * S5_GROUP


COL_NA = 0
COL_CONV = 3072
COL_S5 = 6144
COL_CQ = 7168
COL_CKV = 8192
IN_WIDTH_PAD = 8704
IN_TILE = 512
SRC_CQ = 3072
SRC_CKV = SRC_CQ + MLA_Q_RANK
SRC_CONV = SRC_CKV + MLA_KV_RANK + MLA_ROPE
SRC_S5 = SRC_CONV + 3 * GROUP_WIDTH

LANE = 128
VMEM_LIMIT = 56 * 1024 * 1024


def _cparams(sem, vmem=None):
    return pltpu.CompilerParams(dimension_semantics=sem, vmem_limit_bytes=vmem)


def _sigmoid(x):
    return 1.0 / (1.0 + jnp.exp(-x))


def _rms(x, w):
    return x * lax.rsqrt(jnp.mean(x * x, axis=-1, keepdims=True) + NORM_EPS) * w


def _dot(a, b):
    return jnp.dot(a, b, preferred_element_type=F32)


def _dot_nt(a, b):
    return lax.dot_general(a, b, (((1,), (1,)), ((), ())), preferred_element_type=F32)


def _dot_f32(a, b):
    return jnp.dot(a, b, preferred_element_type=F32, precision=lax.Precision.HIGHEST)


def _ada_kernel(c_ref, w_ref, b_ref, o_ref):
    c = c_ref[...]
    s = (c * _sigmoid(c)).astype(BF16)
    o_ref[...] = _dot(s, w_ref[...].astype(BF16)) + b_ref[...]


def _ada_mod(cvec, ada_w, ada_b):
    n_layers, d, n = ada_w.shape
    tn = 512
    out = pl.pallas_call(
        _ada_kernel,
        grid=(n_layers, n // tn),
        in_specs=[pl.BlockSpec((8, d), lambda l, j: (0, 0)),
                  pl.BlockSpec((None, d, tn), lambda l, j: (l, 0, j)),
                  pl.BlockSpec((None, 1, tn), lambda l, j: (l, 0, j))],
        out_specs=pl.BlockSpec((None, 8, tn), lambda l, j: (l, 0, j)),
        out_shape=jax.ShapeDtypeStruct((n_layers, 8, n), F32),
        compiler_params=_cparams(("arbitrary", "arbitrary"), 40 * 1024 * 1024),
        name="ada_mod",
    )(cvec, ada_w, ada_b.reshape(n_layers, 1, n))
    return out.reshape(n_layers, 8, 6, d).transpose(0, 2, 1, 3)[:, :, :, None, :]


def _normmod_kernel(h_ref, w_ref, shift_ref, scale_ref, o_ref):
    y = _rms(h_ref[...], w_ref[...])
    o_ref[...] = (y * (1.0 + scale_ref[...]) + shift_ref[...]).astype(o_ref.dtype)


def _norm_modulate(h, w, mod, k_shift, k_scale, geo, n_tiles):
    n_tok, d = h.shape
    tm = geo["tm"]
    cls = geo["tile_class"]
    return pl.pallas_call(
        _normmod_kernel,
        grid=(n_tiles,),
        in_specs=[pl.BlockSpec((tm, d), lambda i: (i, 0)),
                  pl.BlockSpec((1, d), lambda i: (0, 0)),
                  pl.BlockSpec((None, None, 1, d), lambda i: (k_shift, cls(i), 0, 0)),
                  pl.BlockSpec((None, None, 1, d), lambda i: (k_scale, cls(i), 0, 0))],
        out_specs=pl.BlockSpec((tm, d), lambda i: (i, 0)),
        out_shape=jax.ShapeDtypeStruct((n_tok, d), BF16),
        compiler_params=_cparams(("arbitrary",), 48 * 1024 * 1024),
        name="norm_modulate",
    )(h, w.reshape(1, d), mod, mod)


def _mm_f32w_kernel(a_ref, w_ref, o_ref, wb_ref):
    @pl.when(pl.program_id(1) == 0)
    def _():
        wb_ref[...] = w_ref[...].astype(BF16)

    o_ref[...] = _dot(a_ref[...], wb_ref[...])


def _in_proj_source_col(j):
    t = IN_TILE // LANE
    unit = jnp.where(j < 6, j * t,
           jnp.where(j < 12, SRC_CONV // LANE + (j - 6) * t,
           jnp.where(j < 14, SRC_S5 // LANE + (j - 12) * t,
           jnp.where(j < 16, SRC_CQ // LANE + (j - 14) * t, SRC_CKV // LANE))))
    return unit * LANE


def _input_projection(a, w_in, layer, tm):
    m, k = a.shape
    tn = IN_TILE
    assert m % tm == 0 and COL_CKV + tn == IN_WIDTH_PAD and SRC_CKV + tn <= w_in.shape[-1]
    return pl.pallas_call(
        _mm_f32w_kernel,
        grid=(IN_WIDTH_PAD // tn, m // tm),
        in_specs=[pl.BlockSpec((tm, k), lambda j, i: (i, 0)),
                  pl.BlockSpec((pl.Element(k), pl.Element(tn)),
                               lambda j, i: (layer * k, _in_proj_source_col(j)))],
        out_specs=pl.BlockSpec((tm, tn), lambda j, i: (i, j)),
        out_shape=jax.ShapeDtypeStruct((m, IN_WIDTH_PAD), F32),
        scratch_shapes=[pltpu.VMEM((k, tn), BF16)],
        compiler_params=_cparams(("arbitrary", "arbitrary"), VMEM_LIMIT),
        name="input_projection",
    )(a, w_in.reshape(-1, w_in.shape[-1]))


def _mm_res_kernel(a_ref, w_ref, h_ref, g_ref, o_ref):
    o_ref[...] = h_ref[...] + g_ref[...] * _dot(a_ref[...], w_ref[...])


def _mm_res_f32w_kernel(a_ref, w_ref, h_ref, g_ref, o_ref, wb_ref):
    @pl.when(pl.program_id(1) == 0)
    def _():
        wb_ref[...] = w_ref[...].astype(BF16)

    o_ref[...] = h_ref[...] + g_ref[...] * _dot(a_ref[...], wb_ref[...])


def _layer_weight_spec(w, layer, tn):
    if w.ndim == 3:
        return pl.BlockSpec((None, w.shape[1], tn), lambda j, i: (layer, 0, j))
    return pl.BlockSpec((w.shape[0], tn), lambda j, i: (0, j))


def _matmul_gated_residual(a, w, h, mod, k_gate, geo, n_tiles, tn, layer=None):
    m, k = a.shape
    n = w.shape[-1]
    tm = geo["tm"]
    cls = geo["tile_class"]
    f32w = w.dtype == F32
    return pl.pallas_call(
        _mm_res_f32w_kernel if f32w else _mm_res_kernel,
        grid=(n // tn, n_tiles),
        in_specs=[pl.BlockSpec((tm, k), lambda j, i: (i, 0)),
                  _layer_weight_spec(w, layer, tn),
                  pl.BlockSpec((tm, tn), lambda j, i: (i, j)),
                  pl.BlockSpec((None, None, 1, tn), lambda j, i: (k_gate, cls(i), 0, j))],
        out_specs=pl.BlockSpec((tm, tn), lambda j, i: (i, j)),
        out_shape=jax.ShapeDtypeStruct((m, n), F32),
        scratch_shapes=[pltpu.VMEM((k, tn), BF16)] if f32w else [],
        compiler_params=_cparams(("arbitrary", "arbitrary"), VMEM_LIMIT),
        name="matmul_gated_residual",
    )(a, w, h, mod)


def _ffn1_kernel(a_ref, w1_ref, w3_ref, o_ref, w1b_ref, w3b_ref):
    @pl.when(pl.program_id(1) == 0)
    def _():
        w1b_ref[...] = w1_ref[...].astype(BF16)
        w3b_ref[...] = w3_ref[...].astype(BF16)

    a = a_ref[...]
    u = _dot(a, w1b_ref[...])
    v = _dot(a, w3b_ref[...])
    o_ref[...] = (u * _sigmoid(u) * v).astype(o_ref.dtype)


def _ffn_in(a, w1, w3, layer, geo, n_tiles, tn):
    m, k = a.shape
    n = w1.shape[-1]
    tm = geo["tm"]
    return pl.pallas_call(
        _ffn1_kernel,
        grid=(pl.cdiv(n, tn), n_tiles),
        in_specs=[pl.BlockSpec((tm, k), lambda j, i: (i, 0)),
                  _layer_weight_spec(w1, layer, tn),
                  _layer_weight_spec(w3, layer, tn)],
        out_specs=pl.BlockSpec((tm, tn), lambda j, i: (i, j)),
        out_shape=jax.ShapeDtypeStruct((m, n), BF16),
        scratch_shapes=[pltpu.VMEM((k, tn), BF16), pltpu.VMEM((k, tn), BF16)],
        compiler_params=_cparams(("arbitrary", "arbitrary"), VMEM_LIMIT),
        name="ffn_in",
    )(a, w1, w3)


def _rope_tail(t, cos_ref, sina_ref, sinb_ref):
    q = MLA_ROPE // 4
    return (t * cos_ref[...] + pltpu.roll(t, LANE - q, 1) * sina_ref[...]
            + pltpu.roll(t, q, 1) * sinb_ref[...])


def _prep_kernel(na_ref, cq_ref, ckvkr_ref, naqw_ref, nakw_ref, cqw_ref, ckvw_ref,
                 wuq_ref, wukv_ref, mqw_ref, mkw_ref, cos_ref, sina_ref, sinb_ref,
                 naq_ref, nak_ref, nav_ref, mq_ref, mk_ref, mv_ref):
    hd = NA_HEAD_DIM
    tm = na_ref.shape[0]
    ones_col = jnp.where(lax.broadcasted_iota(jnp.int32, (tm, LANE), 1) == 0, 1.0, 0.0).astype(BF16)
    na_qs = naqw_ref[...] * (NA_HEAD_DIM ** -0.5 * LOG2E)
    for h in range(N_HEADS):
        naq_ref[:, h * hd:(h + 1) * hd] = _rms(na_ref[:, h * hd:(h + 1) * hd], na_qs).astype(BF16)
        nak_ref[:, h * hd:(h + 1) * hd] = _rms(
            na_ref[:, GROUP_WIDTH + h * hd:GROUP_WIDTH + (h + 1) * hd], nakw_ref[...]).astype(BF16)
        vo = h * V_HEAD_PAD
        nav_ref[:, vo:vo + hd] = na_ref[:, 2 * GROUP_WIDTH + h * hd:2 * GROUP_WIDTH + (h + 1) * hd].astype(BF16)
        nav_ref[:, vo + hd:vo + V_HEAD_PAD] = ones_col

    cq = _rms(cq_ref[...], cqw_ref[...]).astype(BF16)
    q = _dot(cq, wuq_ref[...])
    inv_qk = 1.0 / MLA_QK
    mqw = mqw_ref[...] * (MLA_QK ** -0.5 * LOG2E)
    for h in range(N_HEADS):
        o = h * MLA_HEAD_PAD
        nope = q[:, o:o + MLA_NOPE]
        tail = q[:, o + MLA_NOPE:o + MLA_HEAD_PAD]
        ss = jnp.sum(nope * nope, axis=-1, keepdims=True) + jnp.sum(tail * tail, axis=-1, keepdims=True)
        r = lax.rsqrt(ss * inv_qk + NORM_EPS)
        mq_ref[:, o:o + MLA_NOPE] = (nope * r * mqw[:, :MLA_NOPE]).astype(BF16)
        mq_ref[:, o + MLA_NOPE:o + MLA_HEAD_PAD] = _rope_tail(
            tail * r * mqw[:, MLA_NOPE:], cos_ref, sina_ref, sinb_ref).astype(BF16)

    blk = ckvkr_ref[...]
    lane = lax.broadcasted_iota(jnp.int32, blk.shape, 1)
    ckv = jnp.where(lane < MLA_KV_RANK, blk, 0.0)
    ckv_ms = jnp.sum(ckv * ckv, axis=-1, keepdims=True) * (1.0 / MLA_KV_RANK)
    ckv_n = (ckv * lax.rsqrt(ckv_ms + NORM_EPS) * ckvw_ref[...]).astype(BF16)
    kv = _dot(ckv_n, wukv_ref[...])
    for h in range(N_HEADS):
        vo = h * V_HEAD_PAD
        mv_ref[:, vo:vo + MLA_V] = kv[:, GROUP_WIDTH + h * MLA_V:GROUP_WIDTH + (h + 1) * MLA_V].astype(BF16)
        mv_ref[:, vo + MLA_V:vo + V_HEAD_PAD] = ones_col
    kr = pltpu.roll(blk[:, 2 * LANE:3 * LANE], LANE - MLA_ROPE, 1)
    kr = jnp.where(lax.broadcasted_iota(jnp.int32, kr.shape, 1) < MLA_ROPE, kr, 0.0)
    kr_ss = jnp.sum(kr * kr, axis=-1, keepdims=True)
    mkw = mkw_ref[...]
    for h in range(N_HEADS):
        o = h * MLA_HEAD_PAD
        kn = kv[:, h * MLA_NOPE:(h + 1) * MLA_NOPE]
        r = lax.rsqrt((jnp.sum(kn * kn, axis=-1, keepdims=True) + kr_ss) * inv_qk + NORM_EPS)
        mk_ref[:, o:o + MLA_NOPE] = (kn * r * mkw[:, :MLA_NOPE]).astype(BF16)
        mk_ref[:, o + MLA_NOPE:o + MLA_HEAD_PAD] = _rope_tail(
            kr * r * mkw[:, MLA_NOPE:], cos_ref, sina_ref, sinb_ref).astype(BF16)


def _prep_qkv(p, lw, layer, rope, tm):
    n_tok = p.shape[0]
    blk = lambda width, col: pl.BlockSpec((tm, width), lambda i: (i, col // width))
    full = lambda a: _layer_full_spec(a, layer)
    tab = pl.BlockSpec((tm, LANE), lambda i: (i, 0))
    out_w = [GROUP_WIDTH, GROUP_WIDTH, N_HEADS * V_HEAD_PAD,
             N_HEADS * MLA_HEAD_PAD, N_HEADS * MLA_HEAD_PAD, N_HEADS * V_HEAD_PAD]
    weights = [lw["na_q_norm_w"], lw["na_k_norm_w"], lw["mla_cq_norm_w"], lw["mla_ckv_norm_w"],
               lw["mla_w_uq"], lw["mla_w_ukv"], lw["mla_q_norm_w"], lw["mla_k_norm_w"]]
    return pl.pallas_call(
        _prep_kernel,
        grid=(n_tok // tm,),
        in_specs=[blk(3 * GROUP_WIDTH, COL_NA), blk(MLA_Q_RANK, COL_CQ), blk(MLA_KV_RANK_PAD, COL_CKV)]
        + [full(a) for a in weights] + [tab, tab, tab],
        out_specs=[pl.BlockSpec((tm, w), lambda i: (i, 0)) for w in out_w],
        out_shape=[jax.ShapeDtypeStruct((n_tok, w), BF16) for w in out_w],
        compiler_params=_cparams(("arbitrary",), 48 * 1024 * 1024),
        name="prep_qkv",
    )(p, p, p, *weights, *rope)


def _softmax_attend(q, pairs, bias=None):
    scores = []
    for idx, (k, _) in enumerate(pairs):
        s = _dot_nt(q, k)
        if idx == 0 and bias is not None:
            s = s + bias
        scores.append(s)
    m = scores[0].max(axis=-1, keepdims=True)
    for s in scores[1:]:
        m = jnp.maximum(m, s.max(axis=-1, keepdims=True))
    acc = None
    for s, (_, v) in zip(scores, pairs):
        o = _dot(jnp.exp2(s - m).astype(BF16), v)
        acc = o if acc is None else acc + o
    dv = acc.shape[-1] // 2
    return acc[:, :dv] / acc[:, dv:dv + 1]


def _na_kernel(q_ref, k_ref, v_ref, kc_ref, vc_ref, kca_ref, vca_ref, *rest,
               n_lat_steps, steps_per_batch, n_rows, n_sub, ctx):
    bias_refs, o_ref = rest[:n_sub], rest[n_sub]
    i = pl.program_id(1)
    sq = NA_Q_ROWS * GRID_W

    @pl.when(i < n_lat_steps)
    def _():
        step = i % steps_per_batch
        for sb in range(n_sub):
            rb = step * n_sub + sb
            ks = jnp.clip(rb * NA_Q_ROWS - NA_WIN_ROWS // 2, 0, n_rows - NA_K_ROWS) * GRID_W
            ks = pl.multiple_of(ks, GRID_W)
            kw = k_ref[pl.ds(ks, NA_K_ROWS * GRID_W), :]
            vw = v_ref[pl.ds(ks, NA_K_ROWS * GRID_W), :]
            rows = slice(sb * sq, (sb + 1) * sq)
            o_ref[rows, :] = _softmax_attend(q_ref[rows, :], [(kw, vw), (kc_ref[...], vc_ref[...])],
                                             bias_refs[sb][...])

    @pl.when(i == n_lat_steps)
    def _():
        for bb in range(q_ref.shape[0] // ctx):
            rows = slice(bb * ctx, (bb + 1) * ctx)
            o_ref[rows, :] = _softmax_attend(q_ref[rows, :], [(kca_ref[rows, :], vca_ref[rows, :])])


def _na_attention(naq, nak, nav, bias, layer, geo, with_ctx):
    b, t, ctx = geo["b"], geo["t"], geo["ctx"]
    n_tok = naq.shape[0]
    sq = NA_Q_ROWS * GRID_W
    tq = b * ctx
    assert tq % sq == 0 and t % tq == 0
    n_sub = tq // sq
    n_rows = t // GRID_W
    nrb = n_rows // NA_Q_ROWS
    spb = t // tq
    lat_steps = b * spb
    ctx_block0 = b * t // ctx
    batch = lambda i: jnp.minimum(i // spb, b - 1)

    def bias_spec(sb):
        def bias_map(h, i):
            rb = (i % spb) * n_sub + sb
            return (layer, h, jnp.where(rb == 0, 0, jnp.where(rb >= nrb - 1, 2, 1)), 0, 0)
        return pl.BlockSpec((None, None, None, sq, NA_K_ROWS * GRID_W), bias_map)

    return pl.pallas_call(
        functools.partial(_na_kernel, n_lat_steps=lat_steps, steps_per_batch=spb, n_rows=n_rows,
                          n_sub=n_sub, ctx=ctx),
        grid=(N_HEADS, lat_steps + (1 if with_ctx else 0)),
        in_specs=[pl.BlockSpec((tq, NA_HEAD_DIM), lambda h, i: (i, h)),
                  pl.BlockSpec((t, NA_HEAD_DIM), lambda h, i: (batch(i), h)),
                  pl.BlockSpec((t, V_HEAD_PAD), lambda h, i: (batch(i), h)),
                  pl.BlockSpec((ctx, NA_HEAD_DIM), lambda h, i: (ctx_block0 + batch(i), h)),
                  pl.BlockSpec((ctx, V_HEAD_PAD), lambda h, i: (ctx_block0 + batch(i), h)),
                  pl.BlockSpec((tq, NA_HEAD_DIM), lambda h, i: (lat_steps, h)),
                  pl.BlockSpec((tq, V_HEAD_PAD), lambda h, i: (lat_steps, h))]
        + [bias_spec(sb) for sb in range(n_sub)],
        out_specs=pl.BlockSpec((tq, NA_HEAD_DIM), lambda h, i: (i, h)),
        out_shape=jax.ShapeDtypeStruct((n_tok, GROUP_WIDTH), F32),
        compiler_params=_cparams(("arbitrary", "arbitrary"), 40 * 1024 * 1024),
        name="na_attention",
    )(naq, nak, nav, nak, nav, nak, nav, *([bias] * n_sub))


def _flash_attend(q, loads):
    m = None
    acc = None
    for load in loads:
        k, v = load()
        s = _dot_nt(q, k)
        mc = s.max(axis=-1, keepdims=True)
        if m is None:
            m = mc
            acc = _dot(jnp.exp2(s - m).astype(BF16), v)
        else:
            m_new = jnp.maximum(m, mc)
            acc = acc * jnp.exp2(m - m_new) + _dot(jnp.exp2(s - m_new).astype(BF16), v)
            m = m_new
    dv = acc.shape[-1] // 2
    return acc[:, :dv] / acc[:, dv:dv + 1]


def _mla_kernel(q_ref, k_ref, v_ref, kc_ref, vc_ref, kca_ref, vca_ref, o_ref, *, n_q_blocks, key_chunk, ctx):
    i = pl.program_id(1)

    @pl.when(i < n_q_blocks)
    def _():
        def lat_chunk(j):
            rows = slice(j * key_chunk, (j + 1) * key_chunk)
            return lambda: (k_ref[rows, :], v_ref[rows, :])

        loads = [lat_chunk(j) for j in range(k_ref.shape[0] // key_chunk)]
        loads.append(lambda: (kc_ref[...], vc_ref[...]))
        o_ref[...] = _flash_attend(q_ref[...], loads)

    @pl.when(i == n_q_blocks)
    def _():
        for bb in range(q_ref.shape[0] // ctx):
            rows = slice(bb * ctx, (bb + 1) * ctx)
            o_ref[rows, :] = _flash_attend(q_ref[rows, :], [lambda: (kca_ref[rows, :], vca_ref[rows, :])])


def _mla_attention(mq, mk, mv, geo, with_ctx):
    b, t, ctx = geo["b"], geo["t"], geo["ctx"]
    n_tok = mq.shape[0]
    tq = b * ctx
    assert t % tq == 0
    nqb = t // tq
    lat_blocks = b * nqb
    ctx_block0 = b * t // ctx
    batch = lambda i: jnp.minimum(i // nqb, b - 1)

    return pl.pallas_call(
        functools.partial(_mla_kernel, n_q_blocks=lat_blocks, key_chunk=512, ctx=ctx),
        grid=(N_HEADS, lat_blocks + (1 if with_ctx else 0)),
        in_specs=[pl.BlockSpec((tq, MLA_HEAD_PAD), lambda h, i: (i, h)),
                  pl.BlockSpec((t, MLA_HEAD_PAD), lambda h, i: (batch(i), h)),
                  pl.BlockSpec((t, V_HEAD_PAD), lambda h, i: (batch(i), h)),
                  pl.BlockSpec((ctx, MLA_HEAD_PAD), lambda h, i: (ctx_block0 + batch(i), h)),
                  pl.BlockSpec((ctx, V_HEAD_PAD), lambda h, i: (ctx_block0 + batch(i), h)),
                  pl.BlockSpec((tq, MLA_HEAD_PAD), lambda h, i: (lat_blocks, h)),
                  pl.BlockSpec((tq, V_HEAD_PAD), lambda h, i: (lat_blocks, h))],
        out_specs=pl.BlockSpec((tq, MLA_V), lambda h, i: (i, h)),
        out_shape=jax.ShapeDtypeStruct((n_tok, GROUP_WIDTH), F32),
        compiler_params=_cparams(("arbitrary", "arbitrary"), 48 * 1024 * 1024),
        name="mla_attention",
    )(mq, mk, mv, mk, mv, mk, mv)


def _s5_contrib_kernel(u_ref, b_ref, o_ref):
    r = [_dot_f32(u_ref[:, m * S5_BLOCK:(m + 1) * S5_BLOCK], b_ref[m]) for m in range(2)]
    ps = S5_STATE
    for k in range(4):
        o_ref[k] = jnp.concatenate([r[0][:, k * ps:(k + 1) * ps], r[1][:, k * ps:(k + 1) * ps]], axis=-1)


def _s5_scan_kernel(c_ref, lr_ref, li_ref, x_ref, *, n_tiles, n_ctx_tiles, nb):
    sub = 8
    n_lat_tiles = n_tiles - n_ctx_tiles
    lanes = c_ref.shape[-1]
    lr = jnp.broadcast_to(lr_ref[...], (sub, lanes))
    li = jnp.broadcast_to(li_ref[...], (sub, lanes))
    row = lax.broadcasted_iota(jnp.int32, (sub, lanes), 0)

    def run(reverse):
        order = range(sub - 1, -1, -1) if reverse else range(sub)

        def body(jt, carry):
            if reverse:
                tile = jnp.where(jt < n_ctx_tiles, n_ctx_tiles - 1 - jt, n_tiles - 1 - (jt - n_ctx_tiles))
            else:
                tile = jt
            new = []
            for bi in range(nb):
                xr, xi = carry[2 * bi], carry[2 * bi + 1]
                row_tile = jnp.where(tile < n_ctx_tiles, nb * n_lat_tiles + bi * n_ctx_tiles + tile,
                                     bi * n_lat_tiles + tile - n_ctx_tiles)
                r0 = pl.multiple_of(row_tile * sub, sub)
                cr = c_ref[0, pl.ds(r0, sub), :]
                ci = c_ref[1, pl.ds(r0, sub), :]
                out_r = jnp.zeros_like(cr)
                out_i = jnp.zeros_like(ci)
                for k in order:
                    out_r = jnp.where(row == k, xr, out_r)
                    out_i = jnp.where(row == k, xi, out_i)
                    ck_r = jnp.broadcast_to(cr[k:k + 1, :], (sub, lanes))
                    ck_i = jnp.broadcast_to(ci[k:k + 1, :], (sub, lanes))
                    xr, xi = lr * xr - li * xi + ck_r, lr * xi + li * xr + ck_i
                x_ref[0, pl.ds(r0, sub), :] = out_r
                x_ref[1, pl.ds(r0, sub), :] = out_i
                new += [xr, xi]
            return tuple(new)

        zero = jnp.zeros((sub, lanes), F32)
        lax.fori_loop(0, n_tiles, body, (zero,) * (2 * nb))

    @pl.when(pl.program_id(0) == 0)
    def _():
        run(False)

    @pl.when(pl.program_id(0) == 1)
    def _():
        run(True)


def _s5_out_kernel(u_ref, x_ref, t_ref, c_ref, o_ref):
    ps = S5_STATE
    for m in range(2):
        x = jnp.concatenate([x_ref[k][:, m * ps:(m + 1) * ps] for k in range(4)], axis=-1)
        cols = slice(m * S5_BLOCK, (m + 1) * S5_BLOCK)
        o_ref[:, cols] = _dot_f32(u_ref[:, cols], t_ref[m]) + _dot_f32(x, c_ref[m])


def _s5_mixer(p, ops, layer, geo):
    b, t, ctx = geo["b"], geo["t"], geo["ctx"]
    n_chunks = (ctx + t) // S5_CHUNK
    rows = n_chunks * b
    n_pairs = S5_GROUPS // 2
    state_w = S5_GROUPS * S5_STATE
    pair_w = 2 * S5_BLOCK

    u2 = p[:, COL_S5:COL_S5 + GROUP_WIDTH].reshape(rows, S5_CHUNK, S5_GROUPS, S5_GROUP)
    u2 = u2.transpose(0, 2, 1, 3).reshape(rows, S5_GROUPS * S5_BLOCK)
    sub = 8
    assert (t // S5_CHUNK) % sub == 0 and (ctx // S5_CHUNK) % sub == 0
    op_spec = pl.BlockSpec((None, 2, S5_BLOCK, S5_BLOCK), lambda q: (layer, q, 0, 0))

    contrib = pl.pallas_call(
        _s5_contrib_kernel,
        grid=(n_pairs,),
        in_specs=[pl.BlockSpec((rows, pair_w), lambda q: (0, q)), op_spec],
        out_specs=pl.BlockSpec((4, rows, LANE), lambda q: (0, 0, q)),
        out_shape=jax.ShapeDtypeStruct((4, rows, state_w), F32),
        compiler_params=_cparams(("arbitrary",), 40 * 1024 * 1024),
        name="s5_contrib",
    )(u2, ops["b_mat"])

    lane_blk = 512
    states = pl.pallas_call(
        functools.partial(_s5_scan_kernel, n_tiles=n_chunks // sub, n_ctx_tiles=ctx // S5_CHUNK // sub, nb=b),
        grid=(2, state_w // lane_blk),
        in_specs=[pl.BlockSpec((None, 2, rows, lane_blk), lambda d, l: (d, 0, 0, l)),
                  pl.BlockSpec((None, None, 1, lane_blk), lambda d, l: (layer, d, 0, l)),
                  pl.BlockSpec((None, None, 1, lane_blk), lambda d, l: (layer, d, 0, l))],
        out_specs=pl.BlockSpec((None, 2, rows, lane_blk), lambda d, l: (d, 0, 0, l)),
        out_shape=jax.ShapeDtypeStruct((2, 2, rows, state_w), F32),
        compiler_params=_cparams(("arbitrary", "arbitrary"), 40 * 1024 * 1024),
        name="s5_scan",
    )(contrib.reshape(2, 2, rows, state_w), ops["decay_re"], ops["decay_im"])

    y2 = pl.pallas_call(
        _s5_out_kernel,
        grid=(n_pairs,),
        in_specs=[pl.BlockSpec((rows, pair_w), lambda q: (0, q)),
                  pl.BlockSpec((4, rows, LANE), lambda q: (0, 0, q)), op_spec, op_spec],
        out_specs=pl.BlockSpec((rows, pair_w), lambda q: (0, q)),
        out_shape=jax.ShapeDtypeStruct((rows, S5_GROUPS * S5_BLOCK), F32),
        compiler_params=_cparams(("arbitrary",), 40 * 1024 * 1024),
        name="s5_out",
    )(u2, states.reshape(4, rows, state_w), ops["t_sum"], ops["c_mat"])

    y = y2.reshape(rows, S5_GROUPS, S5_CHUNK, S5_GROUP).transpose(0, 2, 1, 3)
    return y.reshape(rows * S5_CHUNK, GROUP_WIDTH)


def _s5_operators(lam_re, lam_im, log_dt, b_re, b_im, c_re, c_im, d_skip):
    hp = lax.Precision.HIGHEST
    g, pn, ni, lc = S5_GROUPS, S5_STATE, S5_GROUP, S5_CHUNK
    dt = jnp.exp(log_dt)[..., None]
    zr, zi = lam_re * dt, lam_im * dt
    up = jnp.arange(lc, dtype=F32)
    down = (lc - 1) - up

    def powers(d, steps):
        mag = jnp.exp(zr[d][None] * steps[:, None, None])
        ang = zi[d][None] * steps[:, None, None]
        return mag * jnp.cos(ang), mag * jnp.sin(ang)

    one = jnp.ones((1,), F32)
    z1 = [powers(d, one) for d in range(2)]
    nr = jnp.stack([z1[0][0][0], z1[1][0][0]]) - 1.0
    nim = jnp.stack([z1[0][1][0], z1[1][1][0]])
    den = lam_re * lam_re + lam_im * lam_im
    cr_, ci_ = (nr * lam_re + nim * lam_im) / den, (nim * lam_re - nr * lam_im) / den
    bz_r = cr_[..., None] * b_re - ci_[..., None] * b_im
    bz_i = cr_[..., None] * b_im + ci_[..., None] * b_re

    def lag_kernel(d, steps):
        pr, pi = powers(d, steps)
        m_r = pr[..., None] * bz_r[d][None] - pi[..., None] * bz_i[d][None]
        m_i = pr[..., None] * bz_i[d][None] + pi[..., None] * bz_r[d][None]
        k = (jnp.einsum("gop,dgpi->dgoi", c_re[d], m_r, precision=hp)
             - jnp.einsum("gop,dgpi->dgoi", c_im[d], m_i, precision=hp))
        return k.transpose(1, 3, 0, 2)

    k_f = lag_kernel(0, up)
    k_b = lag_kernel(1, down)
    skip = (jnp.eye(ni, dtype=F32)[None] * d_skip.reshape(g, ni, 1))[:, :, None]
    two_sided = jnp.concatenate([k_b[:, :, :lc - 1], k_f[:, :, :1] + k_b[:, :, lc - 1:] + skip, k_f[:, :, 1:]],
                                axis=2)
    t_sum = jnp.stack([two_sided[:, :, lc - 1 - s:2 * lc - 1 - s] for s in range(lc)], axis=1)
    t_sum = t_sum.reshape(g, S5_BLOCK, S5_BLOCK)

    def in_op(d, steps):
        er, ei = powers(d, steps)
        er, ei = er.transpose(1, 0, 2)[:, :, None, :], ei.transpose(1, 0, 2)[:, :, None, :]
        br, bi = bz_r[d].transpose(0, 2, 1)[:, None], bz_i[d].transpose(0, 2, 1)[:, None]
        return er * br - ei * bi, er * bi + ei * br

    bf_r, bf_i = in_op(0, down)
    bb_r, bb_i = in_op(1, up)
    b_mat = jnp.stack([bf_r, bf_i, bb_r, bb_i], axis=3).reshape(g, S5_BLOCK, 4 * pn)

    def out_op(d, steps):
        pr, pi = powers(d, steps)
        pr, pi = pr.transpose(1, 2, 0)[:, :, :, None], pi.transpose(1, 2, 0)[:, :, :, None]
        cr, ci = c_re[d].transpose(0, 2, 1)[:, :, None, :], c_im[d].transpose(0, 2, 1)[:, :, None, :]
        return cr * pr - ci * pi, -(cr * pi + ci * pr)

    cf_r, cf_i = out_op(0, up + 1.0)
    cb_r, cb_i = out_op(1, lc - up)
    c_mat = jnp.stack([cf_r, cf_i, cb_r, cb_i], axis=1).reshape(g, 4 * pn, S5_BLOCK)

    full = jnp.full((1,), float(lc), F32)
    decay = [powers(d, full) for d in range(2)]
    return {"t_sum": t_sum, "b_mat": b_mat, "c_mat": c_mat,
            "decay_re": jnp.stack([decay[0][0], decay[1][0]]).reshape(2, 1, g * pn),
            "decay_im": jnp.stack([decay[0][1], decay[1][1]]).reshape(2, 1, g * pn)}


def _group_norm_store(o_ref, k, y, w_ref):
    cols = slice(k * GROUP_WIDTH, (k + 1) * GROUP_WIDTH)
    o_ref[:, cols] = _rms(y, w_ref[:, cols]).astype(o_ref.dtype)


def _mix_kernel(ya_ref, yb_ref, gb_ref, gc_ref, u_ref, gcp_ref, up_ref, gcn_ref, un_ref,
                cw_ref, cb_ref, ys_ref, gw_ref, gbias_ref, mw_ref, o_ref, *, tm, lat_rows, t, ctx):
    i = pl.program_id(0)
    _group_norm_store(o_ref, 0, ya_ref[...], mw_ref)
    _group_norm_store(o_ref, 1, yb_ref[...], mw_ref)

    r0 = i * tm
    in_lat = r0 < lat_rows
    seq_len = jnp.where(in_lat, t, ctx)
    off = jnp.where(in_lat, r0, r0 - lat_rows) % seq_len
    has_prev = off != 0
    has_next = off + tm != seq_len
    v = gc_ref[...] * u_ref[...]
    v_prev_row = jnp.where(has_prev, gcp_ref[7:8, :] * up_ref[7:8, :], 0.0)
    v_next_row = jnp.where(has_next, gcn_ref[0:1, :] * un_ref[0:1, :], 0.0)
    row = lax.broadcasted_iota(jnp.int32, v.shape, 0)
    v_prev = jnp.where(row == 0, v_prev_row, pltpu.roll(v, 1, 0))
    v_next = jnp.where(row == tm - 1, v_next_row, pltpu.roll(v, tm - 1, 0))
    conv = cw_ref[0:1, :] * v_prev + cw_ref[1:2, :] * v + cw_ref[2:3, :] * v_next + cb_ref[...]
    _group_norm_store(o_ref, 2, gb_ref[...] * conv, mw_ref)

    y = ys_ref[...]
    g = 0.5 * y * (1.0 + jnp.tanh(math.sqrt(2.0 / math.pi) * (y + 0.044715 * (y * y * y))))
    gate = _sigmoid(_dot(g.astype(BF16), gw_ref[...]) + gbias_ref[...])
    _group_norm_store(o_ref, 3, g * gate, mw_ref)


def _mix_outputs(ya, yb, p, ys5, lw, layer, geo, n_tiles):
    n_tok = ya.shape[0]
    tm = 256
    gw = GROUP_WIDTH
    halo = 8
    n_halo_blocks = n_tok // halo
    tile = lambda col: pl.BlockSpec((tm, gw), lambda i: (i, col // gw))
    prev = lambda col: pl.BlockSpec((halo, gw), lambda i: (jnp.maximum(i * (tm // halo) - 1, 0), col // gw))
    nxt = lambda col: pl.BlockSpec(
        (halo, gw), lambda i: (jnp.minimum((i + 1) * (tm // halo), n_halo_blocks - 1), col // gw))
    full = lambda a: _layer_full_spec(a, layer)
    c_gb, c_gc, c_u = COL_CONV, COL_CONV + gw, COL_CONV + 2 * gw
    weights = [lw["conv_w"], lw["conv_b"]]
    glu = [lw["s5_glu_w"], lw["s5_glu_b"], lw["mix_norm_w"]]
    return pl.pallas_call(
        functools.partial(_mix_kernel, tm=tm, lat_rows=geo["b"] * geo["t"], t=geo["t"], ctx=geo["ctx"]),
        grid=(n_tiles * geo["tm"] // tm,),
        in_specs=[tile(0), tile(0), tile(c_gb), tile(c_gc), tile(c_u),
                  prev(c_gc), prev(c_u), nxt(c_gc), nxt(c_u)]
        + [full(a) for a in weights] + [tile(0)] + [full(a) for a in glu],
        out_specs=pl.BlockSpec((tm, 4 * gw), lambda i: (i, 0)),
        out_shape=jax.ShapeDtypeStruct((n_tok, 4 * gw), BF16),
        compiler_params=_cparams(("arbitrary",), 48 * 1024 * 1024),
        name="mix_outputs",
    )(ya, yb, p, p, p, p, p, p, p, *weights, ys5, *glu)


def _pad_last(a, width):
    return jnp.pad(a, [(0, 0)] * (a.ndim - 1) + [(0, width - a.shape[-1])])


def _stacked_weights(pr):
    n_layers = pr["mla_w_uq"].shape[0]
    w_uq = pr["mla_w_uq"].reshape(n_layers, MLA_Q_RANK, N_HEADS, MLA_QK)
    w_uq = _pad_last(w_uq, MLA_HEAD_PAD).reshape(n_layers, MLA_Q_RANK, N_HEADS * MLA_HEAD_PAD)
    w_ukv = pr["mla_w_ukv"].reshape(n_layers, MLA_KV_RANK, N_HEADS, MLA_NOPE + MLA_V)
    w_ukv = jnp.concatenate([w_ukv[..., :MLA_NOPE].reshape(n_layers, MLA_KV_RANK, -1),
                             w_ukv[..., MLA_NOPE:].reshape(n_layers, MLA_KV_RANK, -1)], axis=2)
    w_ukv = jnp.pad(w_ukv, ((0, 0), (0, MLA_KV_RANK_PAD - MLA_KV_RANK), (0, 0)))
    row = lambda a: a[:, None, :]
    return {
        "na_q_norm_w": row(pr["na_q_norm_w"]),
        "na_k_norm_w": row(pr["na_k_norm_w"]),
        "mla_cq_norm_w": row(pr["mla_cq_norm_w"]),
        "mla_ckv_norm_w": row(_pad_last(pr["mla_ckv_norm_w"], MLA_KV_RANK_PAD)),
        "mla_w_uq": w_uq.astype(BF16),
        "mla_w_ukv": w_ukv.astype(BF16),
        "mla_q_norm_w": row(_pad_last(pr["mla_q_norm_w"], MLA_HEAD_PAD)),
        "mla_k_norm_w": row(_pad_last(pr["mla_k_norm_w"], MLA_HEAD_PAD)),
        "conv_w": pr["conv_w"],
        "conv_b": row(pr["conv_b"]),
        "s5_glu_w": pr["s5_glu_w"].astype(BF16),
        "s5_glu_b": row(pr["s5_glu_b"]),
        "mix_norm_w": row(pr["mix_norm_w"]),
        "ffn_w2": pr["ffn_w2"].astype(BF16),
    }


def _layer_full_spec(a, layer):
    return pl.BlockSpec((None,) + a.shape[1:], lambda i: (layer,) + (0,) * (a.ndim - 1))


def _rope_tables(b, t, n_ctx):
    pos = jnp.arange(t, dtype=jnp.int32)
    row = (pos // GRID_W).astype(F32)
    col = (pos % GRID_W).astype(F32)
    n_freq = MLA_ROPE // 4
    inv_freq = ROPE_THETA ** (-jnp.arange(n_freq, dtype=F32) / n_freq)
    ang_r, ang_c = row[:, None] * inv_freq[None, :], col[:, None] * inv_freq[None, :]
    cos_r, sin_r, cos_c, sin_c = jnp.cos(ang_r), jnp.sin(ang_r), jnp.cos(ang_c), jnp.sin(ang_c)
    zeros = jnp.zeros((t, n_freq), F32)
    rest = LANE - MLA_ROPE
    cos_t = jnp.concatenate([cos_r, cos_r, cos_c, cos_c, jnp.ones((t, rest), F32)], axis=1)
    sina_t = jnp.concatenate([-sin_r, zeros, -sin_c, zeros, jnp.zeros((t, rest), F32)], axis=1)
    sinb_t = jnp.concatenate([zeros, sin_r, zeros, sin_c, jnp.zeros((t, rest), F32)], axis=1)
    n_c = b * n_ctx
    tables = []
    for tab, fill in ((cos_t, 1.0), (sina_t, 0.0), (sinb_t, 0.0)):
        tables.append(jnp.concatenate([jnp.tile(tab, (b, 1)), jnp.full((n_c, LANE), fill, F32)], axis=0))
    return tables


def _na_bias_tables(rpb, n_rows):
    w = GRID_W
    half = NA_WIN_ROWS // 2
    r0 = jnp.array([0, NA_Q_ROWS, n_rows - NA_Q_ROWS])
    ks = jnp.clip(r0 - half, 0, n_rows - NA_K_ROWS)
    rq = r0[:, None] + jnp.arange(NA_Q_ROWS)[None, :]
    rk = ks[:, None] + jnp.arange(NA_K_ROWS)[None, :]
    rs = jnp.clip(rq - half, 0, n_rows - NA_WIN_ROWS)
    v_row = (rk[:, None, :] >= rs[:, :, None]) & (rk[:, None, :] < rs[:, :, None] + NA_WIN_ROWS)
    d_row = jnp.clip(rk[:, None, :] - rq[:, :, None] + (NA_WIN_ROWS - 1), 0, 2 * NA_WIN_ROWS - 2)
    qc = jnp.arange(w)
    cs = jnp.clip(qc - NA_WIN_COLS // 2, 0, w - NA_WIN_COLS)
    v_col = (qc[None, :] >= cs[:, None]) & (qc[None, :] < cs[:, None] + NA_WIN_COLS)
    d_col = jnp.clip(qc[None, :] - qc[:, None], -(NA_WIN_COLS - 1), NA_WIN_COLS - 1) + (NA_WIN_COLS - 1)
    sel_r = jax.nn.one_hot(d_row, 2 * NA_WIN_ROWS - 1, dtype=F32)
    sel_c = jax.nn.one_hot(d_col, 2 * NA_WIN_COLS - 1, dtype=F32)
    bias = jnp.einsum("hab,cija,qkb->hciqjk", rpb * LOG2E, sel_r, sel_c, precision=lax.Precision.HIGHEST)
    valid = v_row[:, :, None, :, None] & v_col[None, None, :, None, :]
    bias = jnp.where(valid[None], bias, NEG_INF)
    return bias.reshape(rpb.shape[0], 3, NA_Q_ROWS * w, NA_K_ROWS * w)


def kernel(x, c, ctx, c_ctx, ada_w, ada_b, norm1_w, norm2_w, w_in, na_q_norm_w, na_k_norm_w, na_rpb, mla_cq_norm_w, mla_ckv_norm_w, mla_w_uq, mla_w_ukv, mla_q_norm_w, mla_k_norm_w, conv_w, conv_b, s5_lambda_re, s5_lambda_im, s5_log_dt, s5_b_re, s5_b_im, s5_c_re, s5_c_im, s5_d, s5_glu_w, s5_glu_b, mix_norm_w, w_out, ffn_w1, ffn_w3, ffn_w2):
    b, t, d = x.shape
    n_ctx = ctx.shape[1]
    n_layers = ada_w.shape[0]
    n_rows = t // GRID_W
    assert t % GRID_W == 0 and n_rows >= NA_K_ROWS and n_rows % NA_Q_ROWS == 0
    assert n_ctx == NA_Q_ROWS * GRID_W and t % n_ctx == 0 and b + 1 <= 8
    tm = 512 if (b * n_ctx) % 512 == 0 and t % 512 == 0 else 256
    tiles_per_batch = t // tm
    geo = {"b": b, "t": t, "ctx": n_ctx, "tm": tm,
           "tile_class": lambda i: jnp.minimum(i // tiles_per_batch, b)}
    lat_tiles = b * t // tm
    all_tiles = lat_tiles + b * n_ctx // tm

    pr = dict(na_q_norm_w=na_q_norm_w, na_k_norm_w=na_k_norm_w, mla_cq_norm_w=mla_cq_norm_w,
              mla_ckv_norm_w=mla_ckv_norm_w, mla_w_uq=mla_w_uq, mla_w_ukv=mla_w_ukv,
              mla_q_norm_w=mla_q_norm_w, mla_k_norm_w=mla_k_norm_w, conv_w=conv_w, conv_b=conv_b,
              s5_glu_w=s5_glu_w, s5_glu_b=s5_glu_b, mix_norm_w=mix_norm_w, ffn_w2=ffn_w2)

    h = jnp.concatenate([x.reshape(b * t, d), ctx.reshape(b * n_ctx, d)], axis=0)
    cvec = jnp.zeros((8, d), F32).at[:b].set(c).at[b].set(c_ctx)
    mod_all = _ada_mod(cvec, ada_w, ada_b)
    rope = _rope_tables(b, t, n_ctx)
    lw = _stacked_weights(pr)
    na_bias = jax.vmap(lambda rpb: _na_bias_tables(rpb, n_rows))(na_rpb)
    s5_ops = jax.vmap(_s5_operators)(s5_lambda_re, s5_lambda_im, s5_log_dt, s5_b_re, s5_b_im,
                                     s5_c_re, s5_c_im, s5_d)
    m_in = 1088 if (b * t + b * n_ctx) % 1088 == 0 else tm

    for l in range(n_layers):
        last = l == n_layers - 1
        n_tiles = lat_tiles if last else all_tiles
        mod = mod_all[l]

        a = _norm_modulate(h, norm1_w[l], mod, 0, 1, geo, all_tiles)
        p = _input_projection(a, w_in, l, m_in)
        naq, nak, nav, mq, mk, mv = _prep_qkv(p, lw, l, rope, 256)
        ya = _na_attention(naq, nak, nav, na_bias, l, geo, not last)
        yb = _mla_attention(mq, mk, mv, geo, not last)
        ys5 = _s5_mixer(p, s5_ops, l, geo)
        y = _mix_outputs(ya, yb, p, ys5, lw, l, geo, n_tiles)
        h = _matmul_gated_residual(y, w_out, h, mod, 2, geo, n_tiles, 512, layer=l)

        f = _norm_modulate(h, norm2_w[l], mod, 3, 4, geo, n_tiles)
        g = _ffn_in(f, ffn_w1, ffn_w3, l, geo, n_tiles, 512)
        h = _matmul_gated_residual(g, lw["ffn_w2"], h, mod, 5, geo, n_tiles, 512, layer=l)

    return h[:b * t].reshape(b, t, d)
```

```python
import functools
import math

import jax
import jax.numpy as jnp
from jax import lax
from jax.experimental import pallas as pl
from jax.experimental.pallas import tpu as pltpu

F32 = jnp.float32
BF16 = jnp.bfloat16

NORM_EPS = 1e-6
NEG_INF = -1e30
GRID_W = 64

GROUP_WIDTH = 1024
N_HEADS = 8
NA_HEAD_DIM = 128
NA_WIN_ROWS = 8
NA_WIN_COLS = 16
NA_Q_ROWS = 4
NA_K_ROWS = NA_Q_ROWS + NA_WIN_ROWS

MLA_NOPE = 128
MLA_ROPE = 64
MLA_QK = MLA_NOPE + MLA_ROPE
MLA_V = 128
MLA_Q_RANK = 896
MLA_KV_RANK = 320
MLA_KV_RANK_PAD = 512
MLA_HEAD_PAD = 256
V_HEAD_PAD = 256
LOG2E = math.log2(math.e)
ROPE_THETA = 10000.0

CONV_K = 3

S5_GROUPS = 64
S5_GROUP = 16
S5_STATE = 64
S5_CHUNK = 16
S5_BLOCK = S5_CHUNK * S5_GROUP


COL_NA = 0
COL_CONV = 3072
COL_S5 = 6144
COL_CQ = 7168
COL_CKV = 8192
IN_WIDTH_PAD = 8704
IN_TILE = 512
SRC_CQ = 3072
SRC_CKV = SRC_CQ + MLA_Q_RANK
SRC_CONV = SRC_CKV + MLA_KV_RANK + MLA_ROPE
SRC_S5 = SRC_CONV + 3 * GROUP_WIDTH

LANE = 128
VMEM_LIMIT = 56 * 1024 * 1024


def _cparams(sem, vmem=None):
    return pltpu.CompilerParams(dimension_semantics=sem, vmem_limit_bytes=vmem)


def _sigmoid(x):
    return 1.0 / (1.0 + jnp.exp(-x))


def _rms(x, w):
    return x * lax.rsqrt(jnp.mean(x * x, axis=-1, keepdims=True) + NORM_EPS) * w


def _dot(a, b):
    return jnp.dot(a, b, preferred_element_type=F32)


def _dot_nt(a, b):
    return lax.dot_general(a, b, (((1,), (1,)), ((), ())), preferred_element_type=F32)


def _dot_f32(a, b):
    return jnp.dot(a, b, preferred_element_type=F32, precision=lax.Precision.HIGHEST)


def _ada_kernel(c_ref, w_ref, b_ref, o_ref):
    c = c_ref[...]
    s = (c * _sigmoid(c)).astype(BF16)
    o_ref[...] = _dot(s, w_ref[...].astype(BF16)) + b_ref[...]


def _ada_mod(cvec, ada_w, ada_b):
    n_layers, d, n = ada_w.shape
    tn = 512
    out = pl.pallas_call(
        _ada_kernel,
        grid=(n_layers, n // tn),
        in_specs=[pl.BlockSpec((8, d), lambda l, j: (0, 0)),
                  pl.BlockSpec((None, d, tn), lambda l, j: (l, 0, j)),
                  pl.BlockSpec((None, 1, tn), lambda l, j: (l, 0, j))],
        out_specs=pl.BlockSpec((None, 8, tn), lambda l, j: (l, 0, j)),
        out_shape=jax.ShapeDtypeStruct((n_layers, 8, n), F32),
        compiler_params=_cparams(("arbitrary", "arbitrary"), 40 * 1024 * 1024),
        name="ada_mod",
    )(cvec, ada_w, ada_b.reshape(n_layers, 1, n))
    return out.reshape(n_layers, 8, 6, d).transpose(0, 2, 1, 3)[:, :, :, None, :]


def _stream_parts(h):
    return tuple(h) if isinstance(h, (tuple, list)) else (h,)


def _stream_specs(parts, tm, width, row_of, col_of):
    if len(parts) == 1:
        return [pl.BlockSpec((tm, width), lambda *g: (row_of(*g), col_of(*g)))]
    lat_tiles = parts[0].shape[0] // tm
    return [pl.BlockSpec((tm, width), lambda *g: (jnp.minimum(row_of(*g), lat_tiles - 1), col_of(*g))),
            pl.BlockSpec((tm, width), lambda *g: (jnp.maximum(row_of(*g) - lat_tiles, 0), col_of(*g)))]


def _read_stream(h_refs, row_tile, lat_tiles):
    if len(h_refs) == 1:
        return h_refs[0][...]
    return jnp.where(row_tile < lat_tiles, h_refs[0][...], h_refs[1][...])


def _normmod_kernel(*refs, n_h, lat_tiles):
    h_refs = refs[:n_h]
    w_ref, shift_ref, scale_ref, o_ref = refs[n_h:]
    y = _rms(_read_stream(h_refs, pl.program_id(0), lat_tiles), w_ref[...])
    o_ref[...] = (y * (1.0 + scale_ref[...]) + shift_ref[...]).astype(o_ref.dtype)


def _norm_modulate(h, w, mod, k_shift, k_scale, geo, n_tiles):
    parts = _stream_parts(h)
    d = parts[0].shape[1]
    tm = geo["tm"]
    cls = geo["tile_class"]
    return pl.pallas_call(
        functools.partial(_normmod_kernel, n_h=len(parts), lat_tiles=parts[0].shape[0] // tm),
        grid=(n_tiles,),
        in_specs=_stream_specs(parts, tm, d, lambda i: i, lambda i: 0)
        + [pl.BlockSpec((1, d), lambda i: (0, 0)),
           pl.BlockSpec((None, None, 1, d), lambda i: (k_shift, cls(i), 0, 0)),
           pl.BlockSpec((None, None, 1, d), lambda i: (k_scale, cls(i), 0, 0))],
        out_specs=pl.BlockSpec((tm, d), lambda i: (i, 0)),
        out_shape=jax.ShapeDtypeStruct((n_tiles * tm, d), BF16),
        compiler_params=_cparams(("arbitrary",), VMEM_LIMIT),
        name="norm_modulate",
    )(*parts, w.reshape(1, d), mod, mod)


def _mm_f32w_kernel(a_ref, w_ref, o_ref, wb_ref):
    @pl.when(pl.program_id(1) == 0)
    def _():
        wb_ref[...] = w_ref[...].astype(BF16)

    o_ref[...] = _dot(a_ref[...], wb_ref[...])


def _in_proj_source_col(j):
    t = IN_TILE // LANE
    unit = jnp.where(j < 6, j * t,
           jnp.where(j < 12, SRC_CONV // LANE + (j - 6) * t,
           jnp.where(j < 14, SRC_S5 // LANE + (j - 12) * t,
           jnp.where(j < 16, SRC_CQ // LANE + (j - 14) * t, SRC_CKV // LANE))))
    return unit * LANE


def _input_projection(a, w_in, layer, tm):
    m, k = a.shape
    tn = IN_TILE
    assert m % tm == 0 and COL_CKV + tn == IN_WIDTH_PAD and SRC_CKV + tn <= w_in.shape[-1]
    return pl.pallas_call(
        _mm_f32w_kernel,
        grid=(IN_WIDTH_PAD // tn, m // tm),
        in_specs=[pl.BlockSpec((tm, k), lambda j, i: (i, 0)),
                  pl.BlockSpec((pl.Element(k), pl.Element(tn)),
                               lambda j, i: (layer * k, _in_proj_source_col(j)))],
        out_specs=pl.BlockSpec((tm, tn), lambda j, i: (i, j)),
        out_shape=jax.ShapeDtypeStruct((m, IN_WIDTH_PAD), F32),
        scratch_shapes=[pltpu.VMEM((k, tn), BF16)],
        compiler_params=_cparams(("arbitrary", "arbitrary"), VMEM_LIMIT),
        name="input_projection",
    )(a, w_in.reshape(-1, w_in.shape[-1]))


def _mm_res_kernel(a_ref, w_ref, *refs, n_h, lat_tiles, f32w):
    h_refs = refs[:n_h]
    g_ref, o_ref = refs[n_h], refs[n_h + 1]
    if f32w:
        wb_ref = refs[n_h + 2]

        @pl.when(pl.program_id(1) == 0)
        def _():
            wb_ref[...] = w_ref[...].astype(BF16)

        w = wb_ref[...]
    else:
        w = w_ref[...]
    h = _read_stream(h_refs, pl.program_id(1), lat_tiles)
    o_ref[...] = h + g_ref[...] * _dot(a_ref[...], w)


def _layer_weight_spec(w, layer, tn):
    if w.ndim == 3:
        return pl.BlockSpec((None, w.shape[1], tn), lambda j, i: (layer, 0, j))
    return pl.BlockSpec((w.shape[0], tn), lambda j, i: (0, j))


def _matmul_gated_residual(a, w, h, mod, k_gate, geo, n_tiles, tn, layer=None):
    m, k = a.shape
    n = w.shape[-1]
    tm = geo["tm"]
    cls = geo["tile_class"]
    f32w = w.dtype == F32
    parts = _stream_parts(h)
    return pl.pallas_call(
        functools.partial(_mm_res_kernel, n_h=len(parts), lat_tiles=parts[0].shape[0] // tm, f32w=f32w),
        grid=(n // tn, n_tiles),
        in_specs=[pl.BlockSpec((tm, k), lambda j, i: (i, 0)),
                  _layer_weight_spec(w, layer, tn)]
        + _stream_specs(parts, tm, tn, lambda j, i: i, lambda j, i: j)
        + [pl.BlockSpec((None, None, 1, tn), lambda j, i: (k_gate, cls(i), 0, j))],
        out_specs=pl.BlockSpec((tm, tn), lambda j, i: (i, j)),
        out_shape=jax.ShapeDtypeStruct((n_tiles * tm, n), F32),
        scratch_shapes=[pltpu.VMEM((k, tn), BF16)] if f32w else [],
        compiler_params=_cparams(("arbitrary", "arbitrary"), VMEM_LIMIT),
        name="matmul_gated_residual",
    )(a, w, *parts, mod)


def _ffn1_kernel(a_ref, w1_ref, w3_ref, o_ref, w1b_ref, w3b_ref):
    @pl.when(pl.program_id(1) == 0)
    def _():
        w1b_ref[...] = w1_ref[...].astype(BF16)
        w3b_ref[...] = w3_ref[...].astype(BF16)

    a = a_ref[...]
    u = _dot(a, w1b_ref[...])
    v = _dot(a, w3b_ref[...])
    o_ref[...] = (u * _sigmoid(u) * v).astype(o_ref.dtype)


def _ffn_in(a, w1, w3, layer, geo, n_tiles, tn):
    m, k = a.shape
    n = w1.shape[-1]
    tm = geo["tm"]
    return pl.pallas_call(
        _ffn1_kernel,
        grid=(pl.cdiv(n, tn), n_tiles),
        in_specs=[pl.BlockSpec((tm, k), lambda j, i: (i, 0)),
                  _layer_weight_spec(w1, layer, tn),
                  _layer_weight_spec(w3, layer, tn)],
        out_specs=pl.BlockSpec((tm, tn), lambda j, i: (i, j)),
        out_shape=jax.ShapeDtypeStruct((m, n), BF16),
        scratch_shapes=[pltpu.VMEM((k, tn), BF16), pltpu.VMEM((k, tn), BF16)],
        compiler_params=_cparams(("arbitrary", "arbitrary"), VMEM_LIMIT),
        name="ffn_in",
    )(a, w1, w3)


def _rope_tail(t, cos_ref, sina_ref, sinb_ref):
    q = MLA_ROPE // 4
    return (t * cos_ref[...] + pltpu.roll(t, LANE - q, 1) * sina_ref[...]
            + pltpu.roll(t, q, 1) * sinb_ref[...])


def _prep_kernel(na_ref, cq_ref, ckvkr_ref, naqw_ref, nakw_ref, cqw_ref, ckvw_ref,
                 wuq_ref, wukv_ref, mqw_ref, mkw_ref, cos_ref, sina_ref, sinb_ref,
                 naq_ref, nak_ref, nav_ref, mq_ref, mk_ref, mv_ref):
    hd = NA_HEAD_DIM
    tm = na_ref.shape[0]
    ones_col = jnp.where(lax.broadcasted_iota(jnp.int32, (tm, LANE), 1) == 0, 1.0, 0.0).astype(BF16)
    na_qs = naqw_ref[...] * (NA_HEAD_DIM ** -0.5 * LOG2E)
    for h in range(N_HEADS):
        naq_ref[:, h * hd:(h + 1) * hd] = _rms(na_ref[:, h * hd:(h + 1) * hd], na_qs).astype(BF16)
        nak_ref[:, h * hd:(h + 1) * hd] = _rms(
            na_ref[:, GROUP_WIDTH + h * hd:GROUP_WIDTH + (h + 1) * hd], nakw_ref[...]).astype(BF16)
        vo = h * V_HEAD_PAD
        nav_ref[:, vo:vo + hd] = na_ref[:, 2 * GROUP_WIDTH + h * hd:2 * GROUP_WIDTH + (h + 1) * hd].astype(BF16)
        nav_ref[:, vo + hd:vo + V_HEAD_PAD] = ones_col

    cq = _rms(cq_ref[...], cqw_ref[...]).astype(BF16)
    q = _dot(cq, wuq_ref[...])
    inv_qk = 1.0 / MLA_QK
    mqw = mqw_ref[...] * (MLA_QK ** -0.5 * LOG2E)
    for h in range(N_HEADS):
        o = h * MLA_HEAD_PAD
        nope = q[:, o:o + MLA_NOPE]
        tail = q[:, o + MLA_NOPE:o + MLA_HEAD_PAD]
        ss = jnp.sum(nope * nope, axis=-1, keepdims=True) + jnp.sum(tail * tail, axis=-1, keepdims=True)
        r = lax.rsqrt(ss * inv_qk + NORM_EPS)
        mq_ref[:, o:o + MLA_NOPE] = (nope * r * mqw[:, :MLA_NOPE]).astype(BF16)
        mq_ref[:, o + MLA_NOPE:o + MLA_HEAD_PAD] = _rope_tail(
            tail * r * mqw[:, MLA_NOPE:], cos_ref, sina_ref, sinb_ref).astype(BF16)

    blk = ckvkr_ref[...]
    lane = lax.broadcasted_iota(jnp.int32, blk.shape, 1)
    ckv = jnp.where(lane < MLA_KV_RANK, blk, 0.0)
    ckv_ms = jnp.sum(ckv * ckv, axis=-1, keepdims=True) * (1.0 / MLA_KV_RANK)
    ckv_n = (ckv * lax.rsqrt(ckv_ms + NORM_EPS) * ckvw_ref[...]).astype(BF16)
    kv = _dot(ckv_n, wukv_ref[...])
    for h in range(N_HEADS):
        vo = h * V_HEAD_PAD
        mv_ref[:, vo:vo + MLA_V] = kv[:, GROUP_WIDTH + h * MLA_V:GROUP_WIDTH + (h + 1) * MLA_V].astype(BF16)
        mv_ref[:, vo + MLA_V:vo + V_HEAD_PAD] = ones_col
    kr = pltpu.roll(blk[:, 2 * LANE:3 * LANE], LANE - MLA_ROPE, 1)
    kr = jnp.where(lax.broadcasted_iota(jnp.int32, kr.shape, 1) < MLA_ROPE, kr, 0.0)
    kr_ss = jnp.sum(kr * kr, axis=-1, keepdims=True)
    mkw = mkw_ref[...]
    for h in range(N_HEADS):
        o = h * MLA_HEAD_PAD
        kn = kv[:, h * MLA_NOPE:(h + 1) * MLA_NOPE]
        r = lax.rsqrt((jnp.sum(kn * kn, axis=-1, keepdims=True) + kr_ss) * inv_qk + NORM_EPS)
        mk_ref[:, o:o + MLA_NOPE] = (kn * r * mkw[:, :MLA_NOPE]).astype(BF16)
        mk_ref[:, o + MLA_NOPE:o + MLA_HEAD_PAD] = _rope_tail(
            kr * r * mkw[:, MLA_NOPE:], cos_ref, sina_ref, sinb_ref).astype(BF16)


def _prep_qkv(p, lw, layer, rope, tm):
    n_tok = p.shape[0]
    blk = lambda width, col: pl.BlockSpec((tm, width), lambda i: (i, col // width))
    full = lambda a: _layer_full_spec(a, layer)
    tab = pl.BlockSpec((tm, LANE), lambda i: (i, 0))
    out_w = [GROUP_WIDTH, GROUP_WIDTH, N_HEADS * V_HEAD_PAD,
             N_HEADS * MLA_HEAD_PAD, N_HEADS * MLA_HEAD_PAD, N_HEADS * V_HEAD_PAD]
    weights = [lw["na_q_norm_w"], lw["na_k_norm_w"], lw["mla_cq_norm_w"], lw["mla_ckv_norm_w"],
               lw["mla_w_uq"], lw["mla_w_ukv"], lw["mla_q_norm_w"], lw["mla_k_norm_w"]]
    return pl.pallas_call(
        _prep_kernel,
        grid=(n_tok // tm,),
        in_specs=[blk(3 * GROUP_WIDTH, COL_NA), blk(MLA_Q_RANK, COL_CQ), blk(MLA_KV_RANK_PAD, COL_CKV)]
        + [full(a) for a in weights] + [tab, tab, tab],
        out_specs=[pl.BlockSpec((tm, w), lambda i: (i, 0)) for w in out_w],
        out_shape=[jax.ShapeDtypeStruct((n_tok, w), BF16) for w in out_w],
        compiler_params=_cparams(("arbitrary",), 48 * 1024 * 1024),
        name="prep_qkv",
    )(p, p, p, *weights, *rope)


def _softmax_attend(q, pairs, bias=None):
    scores = []
    for idx, (k, _) in enumerate(pairs):
        s = _dot_nt(q, k)
        if idx == 0 and bias is not None:
            s = s + bias
        scores.append(s)
    m = scores[0].max(axis=-1, keepdims=True)
    for s in scores[1:]:
        m = jnp.maximum(m, s.max(axis=-1, keepdims=True))
    acc = None
    for s, (_, v) in zip(scores, pairs):
        o = _dot(jnp.exp2(s - m).astype(BF16), v)
        acc = o if acc is None else acc + o
    dv = acc.shape[-1] // 2
    return acc[:, :dv] / acc[:, dv:dv + 1]


def _na_kernel(q_ref, k_ref, v_ref, kc_ref, vc_ref, kca_ref, vca_ref, *rest,
               n_lat_steps, steps_per_batch, n_rows, n_sub, ctx):
    bias_refs, o_ref = rest[:n_sub], rest[n_sub]
    i = pl.program_id(1)
    sq = NA_Q_ROWS * GRID_W

    @pl.when(i < n_lat_steps)
    def _():
        step = i % steps_per_batch
        for sb in range(n_sub):
            rb = step * n_sub + sb
            ks = jnp.clip(rb * NA_Q_ROWS - NA_WIN_ROWS // 2, 0, n_rows - NA_K_ROWS) * GRID_W
            ks = pl.multiple_of(ks, GRID_W)
            kw = k_ref[pl.ds(ks, NA_K_ROWS * GRID_W), :]
            vw = v_ref[pl.ds(ks, NA_K_ROWS * GRID_W), :]
            rows = slice(sb * sq, (sb + 1) * sq)
            o_ref[rows, :] = _softmax_attend(q_ref[rows, :], [(kw, vw), (kc_ref[...], vc_ref[...])],
                                             bias_refs[sb][...])

    @pl.when(i == n_lat_steps)
    def _():
        for bb in range(q_ref.shape[0] // ctx):
            rows = slice(bb * ctx, (bb + 1) * ctx)
            o_ref[rows, :] = _softmax_attend(q_ref[rows, :], [(kca_ref[rows, :], vca_ref[rows, :])])


def _na_attention(naq, nak, nav, bias, layer, geo, with_ctx):
    b, t, ctx = geo["b"], geo["t"], geo["ctx"]
    n_tok = naq.shape[0]
    sq = NA_Q_ROWS * GRID_W
    tq = b * ctx
    assert tq % sq == 0 and t % tq == 0
    n_sub = tq // sq
    n_rows = t // GRID_W
    nrb = n_rows // NA_Q_ROWS
    spb = t // tq
    lat_steps = b * spb
    ctx_block0 = b * t // ctx
    batch = lambda i: jnp.minimum(i // spb, b - 1)

    def bias_spec(sb):
        def bias_map(h, i):
            rb = (i % spb) * n_sub + sb
            return (layer, h, jnp.where(rb == 0, 0, jnp.where(rb >= nrb - 1, 2, 1)), 0, 0)
        return pl.BlockSpec((None, None, None, sq, NA_K_ROWS * GRID_W), bias_map)

    return pl.pallas_call(
        functools.partial(_na_kernel, n_lat_steps=lat_steps, steps_per_batch=spb, n_rows=n_rows,
                          n_sub=n_sub, ctx=ctx),
        grid=(N_HEADS, lat_steps + (1 if with_ctx else 0)),
        in_specs=[pl.BlockSpec((tq, NA_HEAD_DIM), lambda h, i: (i, h)),
                  pl.BlockSpec((t, NA_HEAD_DIM), lambda h, i: (batch(i), h)),
                  pl.BlockSpec((t, V_HEAD_PAD), lambda h, i: (batch(i), h)),
                  pl.BlockSpec((ctx, NA_HEAD_DIM), lambda h, i: (ctx_block0 + batch(i), h)),
                  pl.BlockSpec((ctx, V_HEAD_PAD), lambda h, i: (ctx_block0 + batch(i), h)),
                  pl.BlockSpec((tq, NA_HEAD_DIM), lambda h, i: (lat_steps, h)),
                  pl.BlockSpec((tq, V_HEAD_PAD), lambda h, i: (lat_steps, h))]
        + [bias_spec(sb) for sb in range(n_sub)],
        out_specs=pl.BlockSpec((tq, NA_HEAD_DIM), lambda h, i: (i, h)),
        out_shape=jax.ShapeDtypeStruct((n_tok, GROUP_WIDTH), F32),
        compiler_params=_cparams(("arbitrary", "arbitrary"), 40 * 1024 * 1024),
        name="na_attention",
    )(naq, nak, nav, nak, nav, nak, nav, *([bias] * n_sub))


def _flash_attend(q, loads):
    m = None
    acc = None
    for load in loads:
        k, v = load()
        s = _dot_nt(q, k)
        mc = s.max(axis=-1, keepdims=True)
        if m is None:
            m = mc
            acc = _dot(jnp.exp2(s - m).astype(BF16), v)
        else:
            m_new = jnp.maximum(m, mc)
            acc = acc * jnp.exp2(m - m_new) + _dot(jnp.exp2(s - m_new).astype(BF16), v)
            m = m_new
    dv = acc.shape[-1] // 2
    return acc[:, :dv] / acc[:, dv:dv + 1]


def _mla_kernel(q_ref, k_ref, v_ref, kc_ref, vc_ref, kca_ref, vca_ref, o_ref, *, n_q_blocks, key_chunk, ctx):
    i = pl.program_id(1)

    @pl.when(i < n_q_blocks)
    def _():
        def lat_chunk(j):
            rows = slice(j * key_chunk, (j + 1) * key_chunk)
            return lambda: (k_ref[rows, :], v_ref[rows, :])

        loads = [lat_chunk(j) for j in range(k_ref.shape[0] // key_chunk)]
        loads.append(lambda: (kc_ref[...], vc_ref[...]))
        o_ref[...] = _flash_attend(q_ref[...], loads)

    @pl.when(i == n_q_blocks)
    def _():
        for bb in range(q_ref.shape[0] // ctx):
            rows = slice(bb * ctx, (bb + 1) * ctx)
            o_ref[rows, :] = _flash_attend(q_ref[rows, :], [lambda: (kca_ref[rows, :], vca_ref[rows, :])])


def _mla_attention(mq, mk, mv, geo, with_ctx):
    b, t, ctx = geo["b"], geo["t"], geo["ctx"]
    n_tok = mq.shape[0]
    tq = b * ctx
    assert t % tq == 0
    nqb = t // tq
    lat_blocks = b * nqb
    ctx_block0 = b * t // ctx
    batch = lambda i: jnp.minimum(i // nqb, b - 1)

    return pl.pallas_call(
        functools.partial(_mla_kernel, n_q_blocks=lat_blocks, key_chunk=512, ctx=ctx),
        grid=(N_HEADS, lat_blocks + (1 if with_ctx else 0)),
        in_specs=[pl.BlockSpec((tq, MLA_HEAD_PAD), lambda h, i: (i, h)),
                  pl.BlockSpec((t, MLA_HEAD_PAD), lambda h, i: (batch(i), h)),
                  pl.BlockSpec((t, V_HEAD_PAD), lambda h, i: (batch(i), h)),
                  pl.BlockSpec((ctx, MLA_HEAD_PAD), lambda h, i: (ctx_block0 + batch(i), h)),
                  pl.BlockSpec((ctx, V_HEAD_PAD), lambda h, i: (ctx_block0 + batch(i), h)),
                  pl.BlockSpec((tq, MLA_HEAD_PAD), lambda h, i: (lat_blocks, h)),
                  pl.BlockSpec((tq, V_HEAD_PAD), lambda h, i: (lat_blocks, h))],
        out_specs=pl.BlockSpec((tq, MLA_V), lambda h, i: (i, h)),
        out_shape=jax.ShapeDtypeStruct((n_tok, GROUP_WIDTH), F32),
        compiler_params=_cparams(("arbitrary", "arbitrary"), 48 * 1024 * 1024),
        name="mla_attention",
    )(mq, mk, mv, mk, mv, mk, mv)


S5_OCTET = LANE // S5_GROUP


def _s5_gather_chunks(p_ref, u_ref):
    rows = u_ref.shape[0]
    for s in range(S5_CHUNK):
        xs = p_ref[pl.ds(s, rows, stride=S5_CHUNK), :]
        for k in range(S5_OCTET):
            dst = k * S5_BLOCK + s * S5_GROUP
            u_ref[:, dst:dst + S5_GROUP] = xs[:, k * S5_GROUP:(k + 1) * S5_GROUP]


def _s5_contrib_kernel(p_ref, b_ref, o_ref, u_ref):
    _s5_gather_chunks(p_ref, u_ref)
    r = [_dot_f32(u_ref[:, k * S5_BLOCK:(k + 1) * S5_BLOCK], b_ref[k]) for k in range(S5_OCTET)]
    ps = S5_STATE
    for plane in range(4):
        o_ref[plane] = jnp.concatenate([rk[:, plane * ps:(plane + 1) * ps] for rk in r], axis=-1)


def _s5_scan_kernel(c_ref, lr_ref, li_ref, x_ref, *, n_tiles, n_ctx_tiles, nb):
    sub = 8
    n_lat_tiles = n_tiles - n_ctx_tiles
    lanes = c_ref.shape[-1]
    lr = jnp.broadcast_to(lr_ref[...], (sub, lanes))
    li = jnp.broadcast_to(li_ref[...], (sub, lanes))
    row = lax.broadcasted_iota(jnp.int32, (sub, lanes), 0)

    def run(reverse):
        order = range(sub - 1, -1, -1) if reverse else range(sub)

        def body(jt, carry):
            if reverse:
                tile = jnp.where(jt < n_ctx_tiles, n_ctx_tiles - 1 - jt, n_tiles - 1 - (jt - n_ctx_tiles))
            else:
                tile = jt
            new = []
            for bi in range(nb):
                xr, xi = carry[2 * bi], carry[2 * bi + 1]
                row_tile = jnp.where(tile < n_ctx_tiles, nb * n_lat_tiles + bi * n_ctx_tiles + tile,
                                     bi * n_lat_tiles + tile - n_ctx_tiles)
                r0 = pl.multiple_of(row_tile * sub, sub)
                cr = c_ref[0, pl.ds(r0, sub), :]
                ci = c_ref[1, pl.ds(r0, sub), :]
                out_r = jnp.zeros_like(cr)
                out_i = jnp.zeros_like(ci)
                for k in order:
                    out_r = jnp.where(row == k, xr, out_r)
                    out_i = jnp.where(row == k, xi, out_i)
                    ck_r = jnp.broadcast_to(cr[k:k + 1, :], (sub, lanes))
                    ck_i = jnp.broadcast_to(ci[k:k + 1, :], (sub, lanes))
                    xr, xi = lr * xr - li * xi + ck_r, lr * xi + li * xr + ck_i
                x_ref[0, pl.ds(r0, sub), :] = out_r
                x_ref[1, pl.ds(r0, sub), :] = out_i
                new += [xr, xi]
            return tuple(new)

        zero = jnp.zeros((sub, lanes), F32)
        lax.fori_loop(0, n_tiles, body, (zero,) * (2 * nb))

    @pl.when(pl.program_id(0) == 0)
    def _():
        run(False)

    @pl.when(pl.program_id(0) == 1)
    def _():
        run(True)


def _s5_out_kernel(p_ref, x_ref, t_ref, c_ref, o_ref, u_ref, y_ref):
    _s5_gather_chunks(p_ref, u_ref)
    rows = u_ref.shape[0]
    ps = S5_STATE
    for k in range(S5_OCTET):
        x = jnp.concatenate([x_ref[plane][:, k * ps:(k + 1) * ps] for plane in range(4)], axis=-1)
        cols = slice(k * S5_BLOCK, (k + 1) * S5_BLOCK)
        y_ref[:, cols] = _dot_f32(u_ref[:, cols], t_ref[k]) + _dot_f32(x, c_ref[k])
    for t in range(S5_CHUNK):
        piece = jnp.concatenate([y_ref[:, k * S5_BLOCK + t * S5_GROUP:k * S5_BLOCK + (t + 1) * S5_GROUP]
                                 for k in range(S5_OCTET)], axis=-1)
        o_ref[pl.ds(t, rows, stride=S5_CHUNK), :] = piece


def _s5_mixer(p, ops, layer, geo):
    b, t, ctx = geo["b"], geo["t"], geo["ctx"]
    n_chunks = (ctx + t) // S5_CHUNK
    rows = n_chunks * b
    n_tok = p.shape[0]
    assert rows * S5_CHUNK == n_tok
    n_oct = S5_GROUPS // S5_OCTET
    state_w = S5_GROUPS * S5_STATE
    oct_w = S5_OCTET * S5_BLOCK
    oct_states = S5_OCTET * S5_STATE
    sub = 8
    assert (t // S5_CHUNK) % sub == 0 and (ctx // S5_CHUNK) % sub == 0
    p_spec = pl.BlockSpec((n_tok, LANE), lambda o: (0, COL_S5 // LANE + o))
    op_spec = pl.BlockSpec((None, S5_OCTET, S5_BLOCK, S5_BLOCK), lambda o: (layer, o, 0, 0))
    plane_spec = pl.BlockSpec((4, rows, oct_states), lambda o: (0, 0, o))

    contrib = pl.pallas_call(
        _s5_contrib_kernel,
        grid=(n_oct,),
        in_specs=[p_spec, op_spec],
        out_specs=plane_spec,
        out_shape=jax.ShapeDtypeStruct((4, rows, state_w), F32),
        scratch_shapes=[pltpu.VMEM((rows, oct_w), F32)],
        compiler_params=_cparams(("arbitrary",), 48 * 1024 * 1024),
        name="s5_contrib",
    )(p, ops["b_mat"])

    lane_blk = 512
    states = pl.pallas_call(
        functools.partial(_s5_scan_kernel, n_tiles=n_chunks // sub, n_ctx_tiles=ctx // S5_CHUNK // sub, nb=b),
        grid=(2, state_w // lane_blk),
        in_specs=[pl.BlockSpec((None, 2, rows, lane_blk), lambda d, l: (d, 0, 0, l)),
                  pl.BlockSpec((None, None, 1, lane_blk), lambda d, l: (layer, d, 0, l)),
                  pl.BlockSpec((None, None, 1, lane_blk), lambda d, l: (layer, d, 0, l))],
        out_specs=pl.BlockSpec((None, 2, rows, lane_blk), lambda d, l: (d, 0, 0, l)),
        out_shape=jax.ShapeDtypeStruct((2, 2, rows, state_w), F32),
        compiler_params=_cparams(("arbitrary", "arbitrary"), 40 * 1024 * 1024),
        name="s5_scan",
    )(contrib.reshape(2, 2, rows, state_w), ops["decay_re"], ops["decay_im"])

    return pl.pallas_call(
        _s5_out_kernel,
        grid=(n_oct,),
        in_specs=[p_spec, plane_spec, op_spec, op_spec],
        out_specs=pl.BlockSpec((n_tok, LANE), lambda o: (0, o)),
        out_shape=jax.ShapeDtypeStruct((n_tok, GROUP_WIDTH), F32),
        scratch_shapes=[pltpu.VMEM((rows, oct_w), F32), pltpu.VMEM((rows, oct_w), F32)],
        compiler_params=_cparams(("arbitrary",), VMEM_LIMIT),
        name="s5_out",
    )(p, states.reshape(4, rows, state_w), ops["t_sum"], ops["c_mat"])


def _s5_operators(lam_re, lam_im, log_dt, b_re, b_im, c_re, c_im, d_skip):
    hp = lax.Precision.HIGHEST
    g, pn, ni, lc = S5_GROUPS, S5_STATE, S5_GROUP, S5_CHUNK
    dt = jnp.exp(log_dt)[..., None]
    zr, zi = lam_re * dt, lam_im * dt
    up = jnp.arange(lc, dtype=F32)
    down = (lc - 1) - up

    def powers(d, steps):
        mag = jnp.exp(zr[d][None] * steps[:, None, None])
        ang = zi[d][None] * steps[:, None, None]
        return mag * jnp.cos(ang), mag * jnp.sin(ang)

    one = jnp.ones((1,), F32)
    z1 = [powers(d, one) for d in range(2)]
    nr = jnp.stack([z1[0][0][0], z1[1][0][0]]) - 1.0
    nim = jnp.stack([z1[0][1][0], z1[1][1][0]])
    den = lam_re * lam_re + lam_im * lam_im
    cr_, ci_ = (nr * lam_re + nim * lam_im) / den, (nim * lam_re - nr * lam_im) / den
    bz_r = cr_[..., None] * b_re - ci_[..., None] * b_im
    bz_i = cr_[..., None] * b_im + ci_[..., None] * b_re

    def lag_kernel(d, steps):
        pr, pi = powers(d, steps)
        m_r = pr[..., None] * bz_r[d][None] - pi[..., None] * bz_i[d][None]
        m_i = pr[..., None] * bz_i[d][None] + pi[..., None] * bz_r[d][None]
        k = (jnp.einsum("gop,dgpi->dgoi", c_re[d], m_r, precision=hp)
             - jnp.einsum("gop,dgpi->dgoi", c_im[d], m_i, precision=hp))
        return k.transpose(1, 3, 0, 2)

    k_f = lag_kernel(0, up)
    k_b = lag_kernel(1, down)
    skip = (jnp.eye(ni, dtype=F32)[None] * d_skip.reshape(g, ni, 1))[:, :, None]
    two_sided = jnp.concatenate([k_b[:, :, :lc - 1], k_f[:, :, :1] + k_b[:, :, lc - 1:] + skip, k_f[:, :, 1:]],
                                axis=2)
    t_sum = jnp.stack([two_sided[:, :, lc - 1 - s:2 * lc - 1 - s] for s in range(lc)], axis=1)
    t_sum = t_sum.reshape(g, S5_BLOCK, S5_BLOCK)

    per_dir = lambda a: jnp.stack([a[0], a[0], a[1], a[1]])

    ef_r, ef_i = powers(0, down)
    eb_r, eb_i = powers(1, up)
    x_in = jnp.stack([ef_r, ef_i, eb_r, eb_i]).transpose(2, 1, 0, 3)[:, :, None]
    y_in = jnp.stack([-ef_i, ef_r, -eb_i, eb_r]).transpose(2, 1, 0, 3)[:, :, None]
    bzr_t = per_dir(bz_r).transpose(1, 3, 0, 2)[:, None]
    bzi_t = per_dir(bz_i).transpose(1, 3, 0, 2)[:, None]
    b_mat = (x_in * bzr_t + y_in * bzi_t).reshape(g, S5_BLOCK, 4 * pn)

    pf_r, pf_i = powers(0, up + 1.0)
    pb_r, pb_i = powers(1, lc - up)
    x_out = jnp.stack([pf_r, -pf_i, pb_r, -pb_i]).transpose(2, 0, 3, 1)[..., None]
    y_out = jnp.stack([-pf_i, -pf_r, -pb_i, -pb_r]).transpose(2, 0, 3, 1)[..., None]
    cr_t = per_dir(c_re).transpose(1, 0, 3, 2)[:, :, :, None, :]
    ci_t = per_dir(c_im).transpose(1, 0, 3, 2)[:, :, :, None, :]
    c_mat = (x_out * cr_t + y_out * ci_t).reshape(g, 4 * pn, S5_BLOCK)

    full = jnp.full((1,), float(lc), F32)
    decay = [powers(d, full) for d in range(2)]
    return {"t_sum": t_sum, "b_mat": b_mat, "c_mat": c_mat,
            "decay_re": jnp.stack([decay[0][0], decay[1][0]]).reshape(2, 1, g * pn),
            "decay_im": jnp.stack([decay[0][1], decay[1][1]]).reshape(2, 1, g * pn)}


def _group_norm_store(o_ref, k, y, w_ref):
    cols = slice(k * GROUP_WIDTH, (k + 1) * GROUP_WIDTH)
    o_ref[:, cols] = _rms(y, w_ref[:, cols]).astype(o_ref.dtype)


def _mix_kernel(ya_ref, yb_ref, gb_ref, gc_ref, u_ref, gcp_ref, up_ref, gcn_ref, un_ref,
                cw_ref, cb_ref, ys_ref, gw_ref, gbias_ref, mw_ref, o_ref, *, tm, lat_rows, t, ctx):
    i = pl.program_id(0)
    _group_norm_store(o_ref, 0, ya_ref[...], mw_ref)
    _group_norm_store(o_ref, 1, yb_ref[...], mw_ref)

    r0 = i * tm
    in_lat = r0 < lat_rows
    seq_len = jnp.where(in_lat, t, ctx)
    off = jnp.where(in_lat, r0, r0 - lat_rows) % seq_len
    has_prev = off != 0
    has_next = off + tm != seq_len
    v = gc_ref[...] * u_ref[...]
    v_prev_row = jnp.where(has_prev, gcp_ref[7:8, :] * up_ref[7:8, :], 0.0)
    v_next_row = jnp.where(has_next, gcn_ref[0:1, :] * un_ref[0:1, :], 0.0)
    row = lax.broadcasted_iota(jnp.int32, v.shape, 0)
    v_prev = jnp.where(row == 0, v_prev_row, pltpu.roll(v, 1, 0))
    v_next = jnp.where(row == tm - 1, v_next_row, pltpu.roll(v, tm - 1, 0))
    conv = cw_ref[0:1, :] * v_prev + cw_ref[1:2, :] * v + cw_ref[2:3, :] * v_next + cb_ref[...]
    _group_norm_store(o_ref, 2, gb_ref[...] * conv, mw_ref)

    y = ys_ref[...]
    g = 0.5 * y * (1.0 + jnp.tanh(math.sqrt(2.0 / math.pi) * (y + 0.044715 * (y * y * y))))
    gate = _sigmoid(_dot(g.astype(BF16), gw_ref[...]) + gbias_ref[...])
    _group_norm_store(o_ref, 3, g * gate, mw_ref)


def _mix_outputs(ya, yb, p, ys5, lw, layer, geo, n_tiles):
    n_tok = ya.shape[0]
    tm = 256
    gw = GROUP_WIDTH
    halo = 8
    n_halo_blocks = n_tok // halo
    tile = lambda col: pl.BlockSpec((tm, gw), lambda i: (i, col // gw))
    prev = lambda col: pl.BlockSpec((halo, gw), lambda i: (jnp.maximum(i * (tm // halo) - 1, 0), col // gw))
    nxt = lambda col: pl.BlockSpec(
        (halo, gw), lambda i: (jnp.minimum((i + 1) * (tm // halo), n_halo_blocks - 1), col // gw))
    full = lambda a: _layer_full_spec(a, layer)
    c_gb, c_gc, c_u = COL_CONV, COL_CONV + gw, COL_CONV + 2 * gw
    weights = [lw["conv_w"], lw["conv_b"]]
    glu = [lw["s5_glu_w"], lw["s5_glu_b"], lw["mix_norm_w"]]
    return pl.pallas_call(
        functools.partial(_mix_kernel, tm=tm, lat_rows=geo["b"] * geo["t"], t=geo["t"], ctx=geo["ctx"]),
        grid=(n_tiles * geo["tm"] // tm,),
        in_specs=[tile(0), tile(0), tile(c_gb), tile(c_gc), tile(c_u),
                  prev(c_gc), prev(c_u), nxt(c_gc), nxt(c_u)]
        + [full(a) for a in weights] + [tile(0)] + [full(a) for a in glu],
        out_specs=pl.BlockSpec((tm, 4 * gw), lambda i: (i, 0)),
        out_shape=jax.ShapeDtypeStruct((n_tiles * geo["tm"], 4 * gw), BF16),
        compiler_params=_cparams(("arbitrary",), 48 * 1024 * 1024),
        name="mix_outputs",
    )(ya, yb, p, p, p, p, p, p, p, *weights, ys5, *glu)


def _pad_last(a, width):
    return jnp.pad(a, [(0, 0)] * (a.ndim - 1) + [(0, width - a.shape[-1])])


def _stacked_weights(pr):
    n_layers = pr["mla_w_uq"].shape[0]
    w_uq = pr["mla_w_uq"].reshape(n_layers, MLA_Q_RANK, N_HEADS, MLA_QK)
    w_uq = _pad_last(w_uq, MLA_HEAD_PAD).reshape(n_layers, MLA_Q_RANK, N_HEADS * MLA_HEAD_PAD)
    w_ukv = pr["mla_w_ukv"].reshape(n_layers, MLA_KV_RANK, N_HEADS, MLA_NOPE + MLA_V)
    w_ukv = jnp.concatenate([w_ukv[..., :MLA_NOPE].reshape(n_layers, MLA_KV_RANK, -1),
                             w_ukv[..., MLA_NOPE:].reshape(n_layers, MLA_KV_RANK, -1)], axis=2)
    w_ukv = jnp.pad(w_ukv, ((0, 0), (0, MLA_KV_RANK_PAD - MLA_KV_RANK), (0, 0)))
    row = lambda a: a[:, None, :]
    return {
        "na_q_norm_w": row(pr["na_q_norm_w"]),
        "na_k_norm_w": row(pr["na_k_norm_w"]),
        "mla_cq_norm_w": row(pr["mla_cq_norm_w"]),
        "mla_ckv_norm_w": row(_pad_last(pr["mla_ckv_norm_w"], MLA_KV_RANK_PAD)),
        "mla_w_uq": w_uq.astype(BF16),
        "mla_w_ukv": w_ukv.astype(BF16),
        "mla_q_norm_w": row(_pad_last(pr["mla_q_norm_w"], MLA_HEAD_PAD)),
        "mla_k_norm_w": row(_pad_last(pr["mla_k_norm_w"], MLA_HEAD_PAD)),
        "conv_w": pr["conv_w"],
        "conv_b": row(pr["conv_b"]),
        "s5_glu_w": pr["s5_glu_w"].astype(BF16),
        "s5_glu_b": row(pr["s5_glu_b"]),
        "mix_norm_w": row(pr["mix_norm_w"]),
        "ffn_w2": pr["ffn_w2"].astype(BF16),
    }


def _layer_full_spec(a, layer):
    return pl.BlockSpec((None,) + a.shape[1:], lambda i: (layer,) + (0,) * (a.ndim - 1))


def _rope_tables(b, t, n_ctx):
    pos = jnp.arange(t, dtype=jnp.int32)
    row = (pos // GRID_W).astype(F32)
    col = (pos % GRID_W).astype(F32)
    n_freq = MLA_ROPE // 4
    inv_freq = ROPE_THETA ** (-jnp.arange(n_freq, dtype=F32) / n_freq)
    ang_r, ang_c = row[:, None] * inv_freq[None, :], col[:, None] * inv_freq[None, :]
    cos_r, sin_r, cos_c, sin_c = jnp.cos(ang_r), jnp.sin(ang_r), jnp.cos(ang_c), jnp.sin(ang_c)
    zeros = jnp.zeros((t, n_freq), F32)
    rest = LANE - MLA_ROPE
    cos_t = jnp.concatenate([cos_r, cos_r, cos_c, cos_c, jnp.ones((t, rest), F32)], axis=1)
    sina_t = jnp.concatenate([-sin_r, zeros, -sin_c, zeros, jnp.zeros((t, rest), F32)], axis=1)
    sinb_t = jnp.concatenate([zeros, sin_r, zeros, sin_c, jnp.zeros((t, rest), F32)], axis=1)
    n_c = b * n_ctx
    tables = []
    for tab, fill in ((cos_t, 1.0), (sina_t, 0.0), (sinb_t, 0.0)):
        tables.append(jnp.concatenate([jnp.tile(tab, (b, 1)), jnp.full((n_c, LANE), fill, F32)], axis=0))
    return tables


def _na_bias_tables(rpb, n_rows):
    w = GRID_W
    half = NA_WIN_ROWS // 2
    r0 = jnp.array([0, NA_Q_ROWS, n_rows - NA_Q_ROWS])
    ks = jnp.clip(r0 - half, 0, n_rows - NA_K_ROWS)
    rq = r0[:, None] + jnp.arange(NA_Q_ROWS)[None, :]
    rk = ks[:, None] + jnp.arange(NA_K_ROWS)[None, :]
    rs = jnp.clip(rq - half, 0, n_rows - NA_WIN_ROWS)
    v_row = (rk[:, None, :] >= rs[:, :, None]) & (rk[:, None, :] < rs[:, :, None] + NA_WIN_ROWS)
    d_row = jnp.clip(rk[:, None, :] - rq[:, :, None] + (NA_WIN_ROWS - 1), 0, 2 * NA_WIN_ROWS - 2)
    qc = jnp.arange(w)
    cs = jnp.clip(qc - NA_WIN_COLS // 2, 0, w - NA_WIN_COLS)
    v_col = (qc[None, :] >= cs[:, None]) & (qc[None, :] < cs[:, None] + NA_WIN_COLS)
    d_col = jnp.clip(qc[None, :] - qc[:, None], -(NA_WIN_COLS - 1), NA_WIN_COLS - 1) + (NA_WIN_COLS - 1)
    sel_r = jax.nn.one_hot(d_row, 2 * NA_WIN_ROWS - 1, dtype=F32)
    sel_c = jax.nn.one_hot(d_col, 2 * NA_WIN_COLS - 1, dtype=F32)
    bias = jnp.einsum("hab,cija,qkb->hciqjk", rpb * LOG2E, sel_r, sel_c, precision=lax.Precision.HIGHEST)
    valid = v_row[:, :, None, :, None] & v_col[None, None, :, None, :]
    bias = jnp.where(valid[None], bias, NEG_INF)
    return bias.reshape(rpb.shape[0], 3, NA_Q_ROWS * w, NA_K_ROWS * w)


def kernel(x, c, ctx, c_ctx, ada_w, ada_b, norm1_w, norm2_w, w_in, na_q_norm_w, na_k_norm_w, na_rpb, mla_cq_norm_w, mla_ckv_norm_w, mla_w_uq, mla_w_ukv, mla_q_norm_w, mla_k_norm_w, conv_w, conv_b, s5_lambda_re, s5_lambda_im, s5_log_dt, s5_b_re, s5_b_im, s5_c_re, s5_c_im, s5_d, s5_glu_w, s5_glu_b, mix_norm_w, w_out, ffn_w1, ffn_w3, ffn_w2):
    b, t, d = x.shape
    n_ctx = ctx.shape[1]
    n_layers = ada_w.shape[0]
    n_rows = t // GRID_W
    assert t % GRID_W == 0 and n_rows >= NA_K_ROWS and n_rows % NA_Q_ROWS == 0
    assert n_ctx == NA_Q_ROWS * GRID_W and t % n_ctx == 0 and b + 1 <= 8
    tm = 512 if (b * n_ctx) % 512 == 0 and t % 512 == 0 else 256
    tiles_per_batch = t // tm
    geo = {"b": b, "t": t, "ctx": n_ctx, "tm": tm,
           "tile_class": lambda i: jnp.minimum(i // tiles_per_batch, b)}
    lat_tiles = b * t // tm
    all_tiles = lat_tiles + b * n_ctx // tm

    pr = dict(na_q_norm_w=na_q_norm_w, na_k_norm_w=na_k_norm_w, mla_cq_norm_w=mla_cq_norm_w,
              mla_ckv_norm_w=mla_ckv_norm_w, mla_w_uq=mla_w_uq, mla_w_ukv=mla_w_ukv,
              mla_q_norm_w=mla_q_norm_w, mla_k_norm_w=mla_k_norm_w, conv_w=conv_w, conv_b=conv_b,
              s5_glu_w=s5_glu_w, s5_glu_b=s5_glu_b, mix_norm_w=mix_norm_w, ffn_w2=ffn_w2)

    h = (x.reshape(b * t, d), ctx.reshape(b * n_ctx, d))
    cvec = jnp.zeros((8, d), F32).at[:b].set(c).at[b].set(c_ctx)
    mod_all = _ada_mod(cvec, ada_w, ada_b)
    rope = _rope_tables(b, t, n_ctx)
    lw = _stacked_weights(pr)
    na_bias = jax.vmap(lambda rpb: _na_bias_tables(rpb, n_rows))(na_rpb)
    s5_ops = jax.vmap(_s5_operators)(s5_lambda_re, s5_lambda_im, s5_log_dt, s5_b_re, s5_b_im,
                                     s5_c_re, s5_c_im, s5_d)
    m_in = 1088 if (b * t + b * n_ctx) % 1088 == 0 else tm

    for l in range(n_layers):
        last = l == n_layers - 1
        n_tiles = lat_tiles if last else all_tiles
        mod = mod_all[l]

        a = _norm_modulate(h, norm1_w[l], mod, 0, 1, geo, all_tiles)
        p = _input_projection(a, w_in, l, m_in)
        naq, nak, nav, mq, mk, mv = _prep_qkv(p, lw, l, rope, 256)
        ya = _na_attention(naq, nak, nav, na_bias, l, geo, not last)
        yb = _mla_attention(mq, mk, mv, geo, not last)
        ys5 = _s5_mixer(p, s5_ops, l, geo)
        y = _mix_outputs(ya, yb, p, ys5, lw, l, geo, n_tiles)
        h = _matmul_gated_residual(y, w_out, h, mod, 2, geo, n_tiles, 512, layer=l)

        f = _norm_modulate(h, norm2_w[l], mod, 3, 4, geo, n_tiles)
        g = _ffn_in(f, ffn_w1, ffn_w3, l, geo, n_tiles, 512)
        h = _matmul_gated_residual(g, lw["ffn_w2"], h, mod, 5, geo, n_tiles, 512, layer=l)

    return h.reshape(b, t, d)
```

```python
import functools
import math

import jax
import jax.numpy as jnp
from jax import lax
from jax.experimental import pallas as pl
from jax.experimental.pallas import tpu as pltpu

F32 = jnp.float32
BF16 = jnp.bfloat16

NORM_EPS = 1e-6
NEG_INF = -1e30
GRID_W = 64

GROUP_WIDTH = 1024
N_HEADS = 8
NA_HEAD_DIM = 128
NA_WIN_ROWS = 8
NA_WIN_COLS = 16
NA_Q_ROWS = 4
NA_K_ROWS = NA_Q_ROWS + NA_WIN_ROWS

MLA_NOPE = 128
MLA_ROPE = 64
MLA_QK = MLA_NOPE + MLA_ROPE
MLA_V = 128
MLA_Q_RANK = 896
MLA_KV_RANK = 320
MLA_KV_RANK_PAD = 512
MLA_HEAD_PAD = 256
V_HEAD_PAD = 256
LOG2E = math.log2(math.e)
ROPE_THETA = 10000.0

CONV_K = 3

S5_GROUPS = 64
S5_GROUP = 16
S5_STATE = 64
S5_CHUNK = 16
S5_BLOCK = S5_CHUNK * S5_GROUP


COL_NA = 0
COL_CONV = 3072
COL_S5 = 6144
COL_CQ = 7168
COL_CKV = 8192
IN_WIDTH_PAD = 8704
IN_TILE = 512
SRC_CQ = 3072
SRC_CKV = SRC_CQ + MLA_Q_RANK
SRC_CONV = SRC_CKV + MLA_KV_RANK + MLA_ROPE
SRC_S5 = SRC_CONV + 3 * GROUP_WIDTH

LANE = 128
VMEM_LIMIT = 56 * 1024 * 1024


def _cparams(sem, vmem=None):
    return pltpu.CompilerParams(dimension_semantics=sem, vmem_limit_bytes=vmem)


def _sigmoid(x):
    return 1.0 / (1.0 + jnp.exp(-x))


def _rms(x, w):
    return x * lax.rsqrt(jnp.mean(x * x, axis=-1, keepdims=True) + NORM_EPS) * w


def _dot(a, b):
    return jnp.dot(a, b, preferred_element_type=F32)


def _dot_nt(a, b):
    return lax.dot_general(a, b, (((1,), (1,)), ((), ())), preferred_element_type=F32)


def _ada_kernel(c_ref, w_ref, b_ref, o_ref):
    c = c_ref[...]
    s = (c * _sigmoid(c)).astype(BF16)
    o_ref[...] = _dot(s, w_ref[...].astype(BF16)) + b_ref[...]


def _ada_mod(cvec, ada_w, ada_b):
    n_layers, d, n = ada_w.shape
    tn = 512
    out = pl.pallas_call(
        _ada_kernel,
        grid=(n_layers, n // tn),
        in_specs=[pl.BlockSpec((8, d), lambda l, j: (0, 0)),
                  pl.BlockSpec((None, d, tn), lambda l, j: (l, 0, j)),
                  pl.BlockSpec((None, 1, tn), lambda l, j: (l, 0, j))],
        out_specs=pl.BlockSpec((None, 8, tn), lambda l, j: (l, 0, j)),
        out_shape=jax.ShapeDtypeStruct((n_layers, 8, n), F32),
        compiler_params=_cparams(("arbitrary", "arbitrary"), 40 * 1024 * 1024),
        name="ada_mod",
    )(cvec, ada_w, ada_b.reshape(n_layers, 1, n))
    return out.reshape(n_layers, 8, 6, d).transpose(0, 2, 1, 3)[:, :, :, None, :]


def _stream_parts(h):
    return tuple(h) if isinstance(h, (tuple, list)) else (h,)


def _stream_specs(parts, tm, width, row_of, col_of):
    if len(parts) == 1:
        return [pl.BlockSpec((tm, width), lambda *g: (row_of(*g), col_of(*g)))]
    lat_tiles = parts[0].shape[0] // tm
    return [pl.BlockSpec((tm, width), lambda *g: (jnp.minimum(row_of(*g), lat_tiles - 1), col_of(*g))),
            pl.BlockSpec((tm, width), lambda *g: (jnp.maximum(row_of(*g) - lat_tiles, 0), col_of(*g)))]


def _read_stream(h_refs, row_tile, lat_tiles):
    if len(h_refs) == 1:
        return h_refs[0][...]
    return jnp.where(row_tile < lat_tiles, h_refs[0][...], h_refs[1][...])


def _normmod_kernel(*refs, n_h, lat_tiles):
    h_refs = refs[:n_h]
    w_ref, shift_ref, scale_ref, o_ref = refs[n_h:]
    y = _rms(_read_stream(h_refs, pl.program_id(0), lat_tiles), w_ref[...])
    o_ref[...] = (y * (1.0 + scale_ref[...]) + shift_ref[...]).astype(o_ref.dtype)


def _norm_modulate(h, w, mod, k_shift, k_scale, geo, n_tiles):
    parts = _stream_parts(h)
    d = parts[0].shape[1]
    tm = geo["tm"]
    cls = geo["tile_class"]
    return pl.pallas_call(
        functools.partial(_normmod_kernel, n_h=len(parts), lat_tiles=parts[0].shape[0] // tm),
        grid=(n_tiles,),
        in_specs=_stream_specs(parts, tm, d, lambda i: i, lambda i: 0)
        + [pl.BlockSpec((1, d), lambda i: (0, 0)),
           pl.BlockSpec((None, None, 1, d), lambda i: (k_shift, cls(i), 0, 0)),
           pl.BlockSpec((None, None, 1, d), lambda i: (k_scale, cls(i), 0, 0))],
        out_specs=pl.BlockSpec((tm, d), lambda i: (i, 0)),
        out_shape=jax.ShapeDtypeStruct((n_tiles * tm, d), BF16),
        compiler_params=_cparams(("arbitrary",), VMEM_LIMIT),
        name="norm_modulate",
    )(*parts, w.reshape(1, d), mod, mod)


def _mm_f32w_kernel(a_ref, w_ref, o_ref, wb_ref):
    @pl.when(pl.program_id(1) == 0)
    def _():
        wb_ref[...] = w_ref[...].astype(BF16)

    o_ref[...] = _dot(a_ref[...], wb_ref[...])


def _in_proj_source_col(j):
    t = IN_TILE // LANE
    unit = jnp.where(j < 6, j * t,
           jnp.where(j < 12, SRC_CONV // LANE + (j - 6) * t,
           jnp.where(j < 14, SRC_S5 // LANE + (j - 12) * t,
           jnp.where(j < 16, SRC_CQ // LANE + (j - 14) * t, SRC_CKV // LANE))))
    return unit * LANE


def _input_projection(a, w_in, layer, tm):
    m, k = a.shape
    tn = IN_TILE
    assert m % tm == 0 and COL_CKV + tn == IN_WIDTH_PAD and SRC_CKV + tn <= w_in.shape[-1]
    return pl.pallas_call(
        _mm_f32w_kernel,
        grid=(IN_WIDTH_PAD // tn, m // tm),
        in_specs=[pl.BlockSpec((tm, k), lambda j, i: (i, 0)),
                  pl.BlockSpec((pl.Element(k), pl.Element(tn)),
                               lambda j, i: (layer * k, _in_proj_source_col(j)))],
        out_specs=pl.BlockSpec((tm, tn), lambda j, i: (i, j)),
        out_shape=jax.ShapeDtypeStruct((m, IN_WIDTH_PAD), F32),
        scratch_shapes=[pltpu.VMEM((k, tn), BF16)],
        compiler_params=_cparams(("arbitrary", "arbitrary"), VMEM_LIMIT),
        name="input_projection",
    )(a, w_in.reshape(-1, w_in.shape[-1]))


def _mm_res_kernel(a_ref, w_ref, *refs, n_h, lat_tiles, f32w):
    h_refs = refs[:n_h]
    g_ref, o_ref = refs[n_h], refs[n_h + 1]
    if f32w:
        wb_ref = refs[n_h + 2]

        @pl.when(pl.program_id(1) == 0)
        def _():
            wb_ref[...] = w_ref[...].astype(BF16)

        w = wb_ref[...]
    else:
        w = w_ref[...]
    h = _read_stream(h_refs, pl.program_id(1), lat_tiles)
    o_ref[...] = h + g_ref[...] * _dot(a_ref[...], w)


def _layer_weight_spec(w, layer, tn):
    if w.ndim == 3:
        return pl.BlockSpec((None, w.shape[1], tn), lambda j, i: (layer, 0, j))
    return pl.BlockSpec((w.shape[0], tn), lambda j, i: (0, j))


def _matmul_gated_residual(a, w, h, mod, k_gate, geo, n_tiles, tn, layer=None):
    m, k = a.shape
    n = w.shape[-1]
    tm = geo["tm"]
    cls = geo["tile_class"]
    f32w = w.dtype == F32
    parts = _stream_parts(h)
    return pl.pallas_call(
        functools.partial(_mm_res_kernel, n_h=len(parts), lat_tiles=parts[0].shape[0] // tm, f32w=f32w),
        grid=(n // tn, n_tiles),
        in_specs=[pl.BlockSpec((tm, k), lambda j, i: (i, 0)),
                  _layer_weight_spec(w, layer, tn)]
        + _stream_specs(parts, tm, tn, lambda j, i: i, lambda j, i: j)
        + [pl.BlockSpec((None, None, 1, tn), lambda j, i: (k_gate, cls(i), 0, j))],
        out_specs=pl.BlockSpec((tm, tn), lambda j, i: (i, j)),
        out_shape=jax.ShapeDtypeStruct((n_tiles * tm, n), F32),
        scratch_shapes=[pltpu.VMEM((k, tn), BF16)] if f32w else [],
        compiler_params=_cparams(("arbitrary", "arbitrary"), VMEM_LIMIT),
        name="matmul_gated_residual",
    )(a, w, *parts, mod)


def _ffn1_kernel(a_ref, w1_ref, w3_ref, o_ref, w1b_ref, w3b_ref):
    @pl.when(pl.program_id(1) == 0)
    def _():
        w1b_ref[...] = w1_ref[...].astype(BF16)
        w3b_ref[...] = w3_ref[...].astype(BF16)

    a = a_ref[...]
    u = _dot(a, w1b_ref[...])
    v = _dot(a, w3b_ref[...])
    o_ref[...] = (u * _sigmoid(u) * v).astype(o_ref.dtype)


def _ffn_in(a, w1, w3, layer, geo, n_tiles, tn):
    m, k = a.shape
    n = w1.shape[-1]
    tm = geo["tm"]
    return pl.pallas_call(
        _ffn1_kernel,
        grid=(pl.cdiv(n, tn), n_tiles),
        in_specs=[pl.BlockSpec((tm, k), lambda j, i: (i, 0)),
                  _layer_weight_spec(w1, layer, tn),
                  _layer_weight_spec(w3, layer, tn)],
        out_specs=pl.BlockSpec((tm, tn), lambda j, i: (i, j)),
        out_shape=jax.ShapeDtypeStruct((m, n), BF16),
        scratch_shapes=[pltpu.VMEM((k, tn), BF16), pltpu.VMEM((k, tn), BF16)],
        compiler_params=_cparams(("arbitrary", "arbitrary"), VMEM_LIMIT),
        name="ffn_in",
    )(a, w1, w3)


def _rope_tail(t, cos_ref, sina_ref, sinb_ref):
    q = MLA_ROPE // 4
    return (t * cos_ref[...] + pltpu.roll(t, LANE - q, 1) * sina_ref[...]
            + pltpu.roll(t, q, 1) * sinb_ref[...])


def _prep_kernel(na_ref, cq_ref, ckvkr_ref, naqw_ref, nakw_ref, cqw_ref, ckvw_ref,
                 wuq_ref, wukv_ref, mqw_ref, mkw_ref, cos_ref, sina_ref, sinb_ref,
                 naq_ref, nak_ref, nav_ref, mq_ref, mk_ref, mv_ref):
    hd = NA_HEAD_DIM
    tm = na_ref.shape[0]
    ones_col = jnp.where(lax.broadcasted_iota(jnp.int32, (tm, LANE), 1) == 0, 1.0, 0.0).astype(BF16)
    na_qs = naqw_ref[...] * (NA_HEAD_DIM ** -0.5 * LOG2E)
    for h in range(N_HEADS):
        naq_ref[:, h * hd:(h + 1) * hd] = _rms(na_ref[:, h * hd:(h + 1) * hd], na_qs).astype(BF16)
        nak_ref[:, h * hd:(h + 1) * hd] = _rms(
            na_ref[:, GROUP_WIDTH + h * hd:GROUP_WIDTH + (h + 1) * hd], nakw_ref[...]).astype(BF16)
        vo = h * V_HEAD_PAD
        nav_ref[:, vo:vo + hd] = na_ref[:, 2 * GROUP_WIDTH + h * hd:2 * GROUP_WIDTH + (h + 1) * hd].astype(BF16)
        nav_ref[:, vo + hd:vo + V_HEAD_PAD] = ones_col

    cq = _rms(cq_ref[...], cqw_ref[...]).astype(BF16)
    q = _dot(cq, wuq_ref[...])
    inv_qk = 1.0 / MLA_QK
    mqw = mqw_ref[...] * (MLA_QK ** -0.5 * LOG2E)
    for h in range(N_HEADS):
        o = h * MLA_HEAD_PAD
        nope = q[:, o:o + MLA_NOPE]
        tail = q[:, o + MLA_NOPE:o + MLA_HEAD_PAD]
        ss = jnp.sum(nope * nope, axis=-1, keepdims=True) + jnp.sum(tail * tail, axis=-1, keepdims=True)
        r = lax.rsqrt(ss * inv_qk + NORM_EPS)
        mq_ref[:, o:o + MLA_NOPE] = (nope * r * mqw[:, :MLA_NOPE]).astype(BF16)
        mq_ref[:, o + MLA_NOPE:o + MLA_HEAD_PAD] = _rope_tail(
            tail * r * mqw[:, MLA_NOPE:], cos_ref, sina_ref, sinb_ref).astype(BF16)

    blk = ckvkr_ref[...]
    lane = lax.broadcasted_iota(jnp.int32, blk.shape, 1)
    ckv = jnp.where(lane < MLA_KV_RANK, blk, 0.0)
    ckv_ms = jnp.sum(ckv * ckv, axis=-1, keepdims=True) * (1.0 / MLA_KV_RANK)
    ckv_n = (ckv * lax.rsqrt(ckv_ms + NORM_EPS) * ckvw_ref[...]).astype(BF16)
    kv = _dot(ckv_n, wukv_ref[...])
    for h in range(N_HEADS):
        vo = h * V_HEAD_PAD
        mv_ref[:, vo:vo + MLA_V] = kv[:, GROUP_WIDTH + h * MLA_V:GROUP_WIDTH + (h + 1) * MLA_V].astype(BF16)
        mv_ref[:, vo + MLA_V:vo + V_HEAD_PAD] = ones_col
    kr = pltpu.roll(blk[:, 2 * LANE:3 * LANE], LANE - MLA_ROPE, 1)
    kr = jnp.where(lax.broadcasted_iota(jnp.int32, kr.shape, 1) < MLA_ROPE, kr, 0.0)
    kr_ss = jnp.sum(kr * kr, axis=-1, keepdims=True)
    mkw = mkw_ref[...]
    for h in range(N_HEADS):
        o = h * MLA_HEAD_PAD
        kn = kv[:, h * MLA_NOPE:(h + 1) * MLA_NOPE]
        r = lax.rsqrt((jnp.sum(kn * kn, axis=-1, keepdims=True) + kr_ss) * inv_qk + NORM_EPS)
        mk_ref[:, o:o + MLA_NOPE] = (kn * r * mkw[:, :MLA_NOPE]).astype(BF16)
        mk_ref[:, o + MLA_NOPE:o + MLA_HEAD_PAD] = _rope_tail(
            kr * r * mkw[:, MLA_NOPE:], cos_ref, sina_ref, sinb_ref).astype(BF16)


def _prep_qkv(p, lw, layer, rope, tm):
    n_tok = p.shape[0]
    blk = lambda width, col: pl.BlockSpec((tm, width), lambda i: (i, col // width))
    full = lambda a: _layer_full_spec(a, layer)
    tab = pl.BlockSpec((tm, LANE), lambda i: (i, 0))
    out_w = [GROUP_WIDTH, GROUP_WIDTH, N_HEADS * V_HEAD_PAD,
             N_HEADS * MLA_HEAD_PAD, N_HEADS * MLA_HEAD_PAD, N_HEADS * V_HEAD_PAD]
    weights = [lw["na_q_norm_w"], lw["na_k_norm_w"], lw["mla_cq_norm_w"], lw["mla_ckv_norm_w"],
               lw["mla_w_uq"], lw["mla_w_ukv"], lw["mla_q_norm_w"], lw["mla_k_norm_w"]]
    return pl.pallas_call(
        _prep_kernel,
        grid=(n_tok // tm,),
        in_specs=[blk(3 * GROUP_WIDTH, COL_NA), blk(MLA_Q_RANK, COL_CQ), blk(MLA_KV_RANK_PAD, COL_CKV)]
        + [full(a) for a in weights] + [tab, tab, tab],
        out_specs=[pl.BlockSpec((tm, w), lambda i: (i, 0)) for w in out_w],
        out_shape=[jax.ShapeDtypeStruct((n_tok, w), BF16) for w in out_w],
        compiler_params=_cparams(("arbitrary",), 48 * 1024 * 1024),
        name="prep_qkv",
    )(p, p, p, *weights, *rope)


def _softmax_attend(q, pairs, bias=None):
    scores = []
    for idx, (k, _) in enumerate(pairs):
        s = _dot_nt(q, k)
        if idx == 0 and bias is not None:
            s = s + bias
        scores.append(s)
    m = scores[0].max(axis=-1, keepdims=True)
    for s in scores[1:]:
        m = jnp.maximum(m, s.max(axis=-1, keepdims=True))
    acc = None
    for s, (_, v) in zip(scores, pairs):
        o = _dot(jnp.exp2(s - m).astype(BF16), v)
        acc = o if acc is None else acc + o
    dv = acc.shape[-1] // 2
    return acc[:, :dv] / acc[:, dv:dv + 1]


def _na_kernel(q_ref, k_ref, v_ref, kc_ref, vc_ref, kca_ref, vca_ref, *rest,
               n_lat_steps, steps_per_batch, n_rows, n_sub, ctx):
    bias_refs, o_ref = rest[:n_sub], rest[n_sub]
    i = pl.program_id(1)
    sq = NA_Q_ROWS * GRID_W

    @pl.when(i < n_lat_steps)
    def _():
        step = i % steps_per_batch
        for sb in range(n_sub):
            rb = step * n_sub + sb
            ks = jnp.clip(rb * NA_Q_ROWS - NA_WIN_ROWS // 2, 0, n_rows - NA_K_ROWS) * GRID_W
            ks = pl.multiple_of(ks, GRID_W)
            kw = k_ref[pl.ds(ks, NA_K_ROWS * GRID_W), :]
            vw = v_ref[pl.ds(ks, NA_K_ROWS * GRID_W), :]
            rows = slice(sb * sq, (sb + 1) * sq)
            o_ref[rows, :] = _softmax_attend(q_ref[rows, :], [(kw, vw), (kc_ref[...], vc_ref[...])],
                                             bias_refs[sb][...])

    @pl.when(i == n_lat_steps)
    def _():
        for bb in range(q_ref.shape[0] // ctx):
            rows = slice(bb * ctx, (bb + 1) * ctx)
            o_ref[rows, :] = _softmax_attend(q_ref[rows, :], [(kca_ref[rows, :], vca_ref[rows, :])])


def _na_attention(naq, nak, nav, bias, layer, geo, with_ctx):
    b, t, ctx = geo["b"], geo["t"], geo["ctx"]
    n_tok = naq.shape[0]
    sq = NA_Q_ROWS * GRID_W
    tq = b * ctx
    assert tq % sq == 0 and t % tq == 0
    n_sub = tq // sq
    n_rows = t // GRID_W
    nrb = n_rows // NA_Q_ROWS
    spb = t // tq
    lat_steps = b * spb
    ctx_block0 = b * t // ctx
    batch = lambda i: jnp.minimum(i // spb, b - 1)

    def bias_spec(sb):
        def bias_map(h, i):
            rb = (i % spb) * n_sub + sb
            return (layer, h, jnp.where(rb == 0, 0, jnp.where(rb >= nrb - 1, 2, 1)), 0, 0)
        return pl.BlockSpec((None, None, None, sq, NA_K_ROWS * GRID_W), bias_map)

    return pl.pallas_call(
        functools.partial(_na_kernel, n_lat_steps=lat_steps, steps_per_batch=spb, n_rows=n_rows,
                          n_sub=n_sub, ctx=ctx),
        grid=(N_HEADS, lat_steps + (1 if with_ctx else 0)),
        in_specs=[pl.BlockSpec((tq, NA_HEAD_DIM), lambda h, i: (i, h)),
                  pl.BlockSpec((t, NA_HEAD_DIM), lambda h, i: (batch(i), h)),
                  pl.BlockSpec((t, V_HEAD_PAD), lambda h, i: (batch(i), h)),
                  pl.BlockSpec((ctx, NA_HEAD_DIM), lambda h, i: (ctx_block0 + batch(i), h)),
                  pl.BlockSpec((ctx, V_HEAD_PAD), lambda h, i: (ctx_block0 + batch(i), h)),
                  pl.BlockSpec((tq, NA_HEAD_DIM), lambda h, i: (lat_steps, h)),
                  pl.BlockSpec((tq, V_HEAD_PAD), lambda h, i: (lat_steps, h))]
        + [bias_spec(sb) for sb in range(n_sub)],
        out_specs=pl.BlockSpec((tq, NA_HEAD_DIM), lambda h, i: (i, h)),
        out_shape=jax.ShapeDtypeStruct((n_tok, GROUP_WIDTH), F32),
        compiler_params=_cparams(("arbitrary", "arbitrary"), 40 * 1024 * 1024),
        name="na_attention",
    )(naq, nak, nav, nak, nav, nak, nav, *([bias] * n_sub))


def _flash_attend(q, loads):
    m = None
    acc = None
    for load in loads:
        k, v = load()
        s = _dot_nt(q, k)
        mc = s.max(axis=-1, keepdims=True)
        if m is None:
            m = mc
            acc = _dot(jnp.exp2(s - m).astype(BF16), v)
        else:
            m_new = jnp.maximum(m, mc)
            acc = acc * jnp.exp2(m - m_new) + _dot(jnp.exp2(s - m_new).astype(BF16), v)
            m = m_new
    dv = acc.shape[-1] // 2
    return acc[:, :dv] / acc[:, dv:dv + 1]


def _mla_kernel(q_ref, k_ref, v_ref, kc_ref, vc_ref, kca_ref, vca_ref, o_ref, *, n_q_blocks, key_chunk, ctx, hps):
    i = pl.program_id(1)
    dq, dv = MLA_HEAD_PAD, V_HEAD_PAD

    @pl.when(i < n_q_blocks)
    def _():
        for hh in range(hps):
            qc, vc = slice(hh * dq, (hh + 1) * dq), slice(hh * dv, (hh + 1) * dv)

            def lat_chunk(j, qc=qc, vc=vc):
                rows = slice(j * key_chunk, (j + 1) * key_chunk)
                return lambda: (k_ref[rows, qc], v_ref[rows, vc])

            loads = [lat_chunk(j) for j in range(k_ref.shape[0] // key_chunk)]
            loads.append(lambda qc=qc, vc=vc: (kc_ref[:, qc], vc_ref[:, vc]))
            o_ref[:, hh * MLA_V:(hh + 1) * MLA_V] = _flash_attend(q_ref[:, qc], loads)

    @pl.when(i == n_q_blocks)
    def _():
        for hh in range(hps):
            qc, vc = slice(hh * dq, (hh + 1) * dq), slice(hh * dv, (hh + 1) * dv)
            for bb in range(q_ref.shape[0] // ctx):
                rows = slice(bb * ctx, (bb + 1) * ctx)
                o_ref[rows, hh * MLA_V:(hh + 1) * MLA_V] = _flash_attend(
                    q_ref[rows, qc], [lambda: (kca_ref[rows, qc], vca_ref[rows, vc])])


def _mla_attention(mq, mk, mv, geo, with_ctx):
    b, t, ctx = geo["b"], geo["t"], geo["ctx"]
    n_tok = mq.shape[0]
    tq = b * ctx
    assert t % tq == 0
    nqb = t // tq
    lat_blocks = b * nqb
    ctx_block0 = b * t // ctx
    batch = lambda i: jnp.minimum(i // nqb, b - 1)
    hps = 2
    dq, dv = hps * MLA_HEAD_PAD, hps * V_HEAD_PAD

    return pl.pallas_call(
        functools.partial(_mla_kernel, n_q_blocks=lat_blocks, key_chunk=512, ctx=ctx, hps=hps),
        grid=(N_HEADS // hps, lat_blocks + (1 if with_ctx else 0)),
        in_specs=[pl.BlockSpec((tq, dq), lambda h, i: (i, h)),
                  pl.BlockSpec((t, dq), lambda h, i: (batch(i), h)),
                  pl.BlockSpec((t, dv), lambda h, i: (batch(i), h)),
                  pl.BlockSpec((ctx, dq), lambda h, i: (ctx_block0 + batch(i), h)),
                  pl.BlockSpec((ctx, dv), lambda h, i: (ctx_block0 + batch(i), h)),
                  pl.BlockSpec((tq, dq), lambda h, i: (lat_blocks, h)),
                  pl.BlockSpec((tq, dv), lambda h, i: (lat_blocks, h))],
        out_specs=pl.BlockSpec((tq, hps * MLA_V), lambda h, i: (i, h)),
        out_shape=jax.ShapeDtypeStruct((n_tok, GROUP_WIDTH), F32),
        compiler_params=_cparams(("arbitrary", "arbitrary"), 48 * 1024 * 1024),
        name="mla_attention",
    )(mq, mk, mv, mk, mv, mk, mv)


S5_OCTET = LANE // S5_GROUP


def _s5_gather_chunks(p_ref, u_ref):
    rows = u_ref.shape[0]
    for s in range(S5_CHUNK):
        xs = p_ref[pl.ds(s, rows, stride=S5_CHUNK), :]
        for k in range(S5_OCTET):
            dst = k * S5_BLOCK + s * S5_GROUP
            u_ref[:, dst:dst + S5_GROUP] = xs[:, k * S5_GROUP:(k + 1) * S5_GROUP]


def _s5_contrib_kernel(p_ref, b_ref, o_ref, u_ref):
    _s5_gather_chunks(p_ref, u_ref)
    r = [_dot(u_ref[:, k * S5_BLOCK:(k + 1) * S5_BLOCK].astype(BF16), b_ref[k]) for k in range(S5_OCTET)]
    ps = S5_STATE
    for plane in range(4):
        o_ref[plane] = jnp.concatenate([rk[:, plane * ps:(plane + 1) * ps] for rk in r], axis=-1)


def _s5_scan_kernel(c_ref, lr_ref, li_ref, x_ref, *, n_tiles, n_ctx_tiles, nb):
    sub = 8
    n_lat_tiles = n_tiles - n_ctx_tiles
    lanes = c_ref.shape[-1]
    lr = jnp.broadcast_to(lr_ref[...], (sub, lanes))
    li = jnp.broadcast_to(li_ref[...], (sub, lanes))
    row = lax.broadcasted_iota(jnp.int32, (sub, lanes), 0)

    def run(reverse):
        order = range(sub - 1, -1, -1) if reverse else range(sub)

        def body(jt, carry):
            if reverse:
                tile = jnp.where(jt < n_ctx_tiles, n_ctx_tiles - 1 - jt, n_tiles - 1 - (jt - n_ctx_tiles))
            else:
                tile = jt
            new = []
            for bi in range(nb):
                xr, xi = carry[2 * bi], carry[2 * bi + 1]
                row_tile = jnp.where(tile < n_ctx_tiles, nb * n_lat_tiles + bi * n_ctx_tiles + tile,
                                     bi * n_lat_tiles + tile - n_ctx_tiles)
                r0 = pl.multiple_of(row_tile * sub, sub)
                cr = c_ref[0, pl.ds(r0, sub), :]
                ci = c_ref[1, pl.ds(r0, sub), :]
                out_r = jnp.zeros_like(cr)
                out_i = jnp.zeros_like(ci)
                for k in order:
                    out_r = jnp.where(row == k, xr, out_r)
                    out_i = jnp.where(row == k, xi, out_i)
                    ck_r = jnp.broadcast_to(cr[k:k + 1, :], (sub, lanes))
                    ck_i = jnp.broadcast_to(ci[k:k + 1, :], (sub, lanes))
                    xr, xi = lr * xr - li * xi + ck_r, lr * xi + li * xr + ck_i
                x_ref[0, pl.ds(r0, sub), :] = out_r
                x_ref[1, pl.ds(r0, sub), :] = out_i
                new += [xr, xi]
            return tuple(new)

        zero = jnp.zeros((sub, lanes), F32)
        lax.fori_loop(0, n_tiles, body, (zero,) * (2 * nb))

    @pl.when(pl.program_id(0) == 0)
    def _():
        run(False)

    @pl.when(pl.program_id(0) == 1)
    def _():
        run(True)


def _s5_out_kernel(p_ref, x_ref, t_ref, ct_ref, o_ref, u_ref, y_ref):
    _s5_gather_chunks(p_ref, u_ref)
    rows = u_ref.shape[0]
    ps = S5_STATE
    for k in range(S5_OCTET):
        x = jnp.concatenate([x_ref[plane][:, k * ps:(k + 1) * ps] for plane in range(4)], axis=-1)
        cols = slice(k * S5_BLOCK, (k + 1) * S5_BLOCK)
        y_ref[:, cols] = _dot(u_ref[:, cols].astype(BF16), t_ref[k]) + _dot_nt(x.astype(BF16), ct_ref[k])
    for t in range(S5_CHUNK):
        piece = jnp.concatenate([y_ref[:, k * S5_BLOCK + t * S5_GROUP:k * S5_BLOCK + (t + 1) * S5_GROUP]
                                 for k in range(S5_OCTET)], axis=-1)
        o_ref[pl.ds(t, rows, stride=S5_CHUNK), :] = piece


def _s5_mixer(p, ops, layer, geo):
    b, t, ctx = geo["b"], geo["t"], geo["ctx"]
    n_chunks = (ctx + t) // S5_CHUNK
    rows = n_chunks * b
    n_tok = p.shape[0]
    assert rows * S5_CHUNK == n_tok
    n_oct = S5_GROUPS // S5_OCTET
    state_w = S5_GROUPS * S5_STATE
    oct_w = S5_OCTET * S5_BLOCK
    oct_states = S5_OCTET * S5_STATE
    sub = 8
    assert (t // S5_CHUNK) % sub == 0 and (ctx // S5_CHUNK) % sub == 0
    p_spec = pl.BlockSpec((n_tok, LANE), lambda o: (0, COL_S5 // LANE + o))
    op_spec = pl.BlockSpec((None, S5_OCTET, S5_BLOCK, S5_BLOCK), lambda o: (layer, o, 0, 0))
    plane_spec = pl.BlockSpec((4, rows, oct_states), lambda o: (0, 0, o))

    contrib = pl.pallas_call(
        _s5_contrib_kernel,
        grid=(n_oct,),
        in_specs=[p_spec, op_spec],
        out_specs=plane_spec,
        out_shape=jax.ShapeDtypeStruct((4, rows, state_w), F32),
        scratch_shapes=[pltpu.VMEM((rows, oct_w), F32)],
        compiler_params=_cparams(("arbitrary",), 48 * 1024 * 1024),
        name="s5_contrib",
    )(p, ops["b_mat"])

    lane_blk = 512
    states = pl.pallas_call(
        functools.partial(_s5_scan_kernel, n_tiles=n_chunks // sub, n_ctx_tiles=ctx // S5_CHUNK // sub, nb=b),
        grid=(2, state_w // lane_blk),
        in_specs=[pl.BlockSpec((None, 2, rows, lane_blk), lambda d, l: (d, 0, 0, l)),
                  pl.BlockSpec((None, None, 1, lane_blk), lambda d, l: (layer, d, 0, l)),
                  pl.BlockSpec((None, None, 1, lane_blk), lambda d, l: (layer, d, 0, l))],
        out_specs=pl.BlockSpec((None, 2, rows, lane_blk), lambda d, l: (d, 0, 0, l)),
        out_shape=jax.ShapeDtypeStruct((2, 2, rows, state_w), F32),
        compiler_params=_cparams(("arbitrary", "arbitrary"), 40 * 1024 * 1024),
        name="s5_scan",
    )(contrib.reshape(2, 2, rows, state_w), ops["decay_re"], ops["decay_im"])

    return pl.pallas_call(
        _s5_out_kernel,
        grid=(n_oct,),
        in_specs=[p_spec, plane_spec, op_spec, op_spec],
        out_specs=pl.BlockSpec((n_tok, LANE), lambda o: (0, o)),
        out_shape=jax.ShapeDtypeStruct((n_tok, GROUP_WIDTH), F32),
        scratch_shapes=[pltpu.VMEM((rows, oct_w), F32), pltpu.VMEM((rows, oct_w), F32)],
        compiler_params=_cparams(("arbitrary",), VMEM_LIMIT),
        name="s5_out",
    )(p, states.reshape(4, rows, state_w), ops["t_sum"], ops["c_mat_t"])


def _s5_operators(lam_re, lam_im, log_dt, b_re, b_im, c_re, c_im):
    hp = lax.Precision.HIGHEST
    g, pn, ni, lc = S5_GROUPS, S5_STATE, S5_GROUP, S5_CHUNK
    dt = jnp.exp(log_dt)[..., None]
    zr, zi = lam_re * dt, lam_im * dt
    up = jnp.arange(lc, dtype=F32)
    down = (lc - 1) - up

    def powers(d, steps):
        mag = jnp.exp(zr[d][None] * steps[:, None, None])
        ang = zi[d][None] * steps[:, None, None]
        return mag * jnp.cos(ang), mag * jnp.sin(ang)

    one = jnp.ones((1,), F32)
    z1 = [powers(d, one) for d in range(2)]
    nr = jnp.stack([z1[0][0][0], z1[1][0][0]]) - 1.0
    nim = jnp.stack([z1[0][1][0], z1[1][1][0]])
    den = lam_re * lam_re + lam_im * lam_im
    cr_, ci_ = (nr * lam_re + nim * lam_im) / den, (nim * lam_re - nr * lam_im) / den
    bz_r = cr_[..., None] * b_re - ci_[..., None] * b_im
    bz_i = cr_[..., None] * b_im + ci_[..., None] * b_re

    def lag_kernel(d, steps):
        pr, pi = powers(d, steps)
        m_r = pr[..., None] * bz_r[d][None] - pi[..., None] * bz_i[d][None]
        m_i = pr[..., None] * bz_i[d][None] + pi[..., None] * bz_r[d][None]
        k = (jnp.einsum("gop,dgpi->dgoi", c_re[d], m_r, precision=hp)
             - jnp.einsum("gop,dgpi->dgoi", c_im[d], m_i, precision=hp))
        return k.transpose(1, 3, 0, 2)

    k_f = lag_kernel(0, up)
    k_b = lag_kernel(1, down)
    two_sided = jnp.concatenate([k_b[:, :, :lc - 1], k_f[:, :, :1] + k_b[:, :, lc - 1:], k_f[:, :, 1:]], axis=2)
    two_sided = two_sided.reshape(g, ni, (2 * lc - 1) * ni)
    t_sum = jnp.stack([two_sided[:, :, (lc - 1 - s) * ni:(lc - 1 - s) * ni + S5_BLOCK] for s in range(lc)], axis=1)
    t_sum = t_sum.reshape(g, S5_BLOCK, S5_BLOCK)

    per_dir = lambda a: jnp.stack([a[0], a[0], a[1], a[1]])
    by_pos = lambda planes: jnp.stack(planes).transpose(2, 1, 0, 3).reshape(g, lc, 1, 4 * pn)
    by_chan = lambda a: per_dir(a).transpose(1, 2, 0, 3).reshape(g, 1, ni, 4 * pn)

    ef_r, ef_i = powers(0, down)
    eb_r, eb_i = powers(1, up)
    bz_ri = (bz_r.transpose(0, 1, 3, 2), bz_i.transpose(0, 1, 3, 2))
    b_mat = (by_pos([ef_r, ef_i, eb_r, eb_i]) * by_chan(bz_ri[0])
             + by_pos([-ef_i, ef_r, -eb_i, eb_r]) * by_chan(bz_ri[1])).reshape(g, S5_BLOCK, 4 * pn)

    pf_r, pf_i = powers(0, up + 1.0)
    pb_r, pb_i = powers(1, lc - up)
    c_mat_t = (by_pos([pf_r, -pf_i, pb_r, -pb_i]) * by_chan(c_re)
               + by_pos([-pf_i, -pf_r, -pb_i, -pb_r]) * by_chan(c_im)).reshape(g, S5_BLOCK, 4 * pn)

    full = jnp.full((1,), float(lc), F32)
    decay = [powers(d, full) for d in range(2)]
    return {"t_sum": t_sum.astype(BF16), "b_mat": b_mat.astype(BF16), "c_mat_t": c_mat_t.astype(BF16),
            "decay_re": jnp.stack([decay[0][0], decay[1][0]]).reshape(2, 1, g * pn),
            "decay_im": jnp.stack([decay[0][1], decay[1][1]]).reshape(2, 1, g * pn)}


def _group_norm_store(o_ref, k, y, w_ref):
    cols = slice(k * GROUP_WIDTH, (k + 1) * GROUP_WIDTH)
    o_ref[:, cols] = _rms(y, w_ref[:, cols]).astype(o_ref.dtype)


def _mix_kernel(ya_ref, yb_ref, gb_ref, gc_ref, u_ref, gcp_ref, up_ref, gcn_ref, un_ref,
                cw_ref, cb_ref, ys_ref, us_ref, ds_ref, gw_ref, gbias_ref, mw_ref, o_ref, *, tm, lat_rows, t, ctx):
    i = pl.program_id(0)
    _group_norm_store(o_ref, 0, ya_ref[...], mw_ref)
    _group_norm_store(o_ref, 1, yb_ref[...], mw_ref)

    r0 = i * tm
    in_lat = r0 < lat_rows
    seq_len = jnp.where(in_lat, t, ctx)
    off = jnp.where(in_lat, r0, r0 - lat_rows) % seq_len
    has_prev = off != 0
    has_next = off + tm != seq_len
    v = gc_ref[...] * u_ref[...]
    v_prev_row = jnp.where(has_prev, gcp_ref[7:8, :] * up_ref[7:8, :], 0.0)
    v_next_row = jnp.where(has_next, gcn_ref[0:1, :] * un_ref[0:1, :], 0.0)
    row = lax.broadcasted_iota(jnp.int32, v.shape, 0)
    v_prev = jnp.where(row == 0, v_prev_row, pltpu.roll(v, 1, 0))
    v_next = jnp.where(row == tm - 1, v_next_row, pltpu.roll(v, tm - 1, 0))
    conv = cw_ref[0:1, :] * v_prev + cw_ref[1:2, :] * v + cw_ref[2:3, :] * v_next + cb_ref[...]
    _group_norm_store(o_ref, 2, gb_ref[...] * conv, mw_ref)

    y = ys_ref[...] + ds_ref[...] * us_ref[...]
    g = 0.5 * y * (1.0 + jnp.tanh(math.sqrt(2.0 / math.pi) * (y + 0.044715 * (y * y * y))))
    gate = _sigmoid(_dot(g.astype(BF16), gw_ref[...]) + gbias_ref[...])
    _group_norm_store(o_ref, 3, g * gate, mw_ref)


def _mix_outputs(ya, yb, p, ys5, lw, layer, geo, n_tiles):
    n_tok = ya.shape[0]
    tm = 256
    gw = GROUP_WIDTH
    halo = 8
    n_halo_blocks = n_tok // halo
    tile = lambda col: pl.BlockSpec((tm, gw), lambda i: (i, col // gw))
    prev = lambda col: pl.BlockSpec((halo, gw), lambda i: (jnp.maximum(i * (tm // halo) - 1, 0), col // gw))
    nxt = lambda col: pl.BlockSpec(
        (halo, gw), lambda i: (jnp.minimum((i + 1) * (tm // halo), n_halo_blocks - 1), col // gw))
    full = lambda a: _layer_full_spec(a, layer)
    c_gb, c_gc, c_u = COL_CONV, COL_CONV + gw, COL_CONV + 2 * gw
    weights = [lw["conv_w"], lw["conv_b"]]
    glu = [lw["s5_d"], lw["s5_glu_w"], lw["s5_glu_b"], lw["mix_norm_w"]]
    return pl.pallas_call(
        functools.partial(_mix_kernel, tm=tm, lat_rows=geo["b"] * geo["t"], t=geo["t"], ctx=geo["ctx"]),
        grid=(n_tiles * geo["tm"] // tm,),
        in_specs=[tile(0), tile(0), tile(c_gb), tile(c_gc), tile(c_u),
                  prev(c_gc), prev(c_u), nxt(c_gc), nxt(c_u)]
        + [full(a) for a in weights] + [tile(0), tile(COL_S5)] + [full(a) for a in glu],
        out_specs=pl.BlockSpec((tm, 4 * gw), lambda i: (i, 0)),
        out_shape=jax.ShapeDtypeStruct((n_tiles * geo["tm"], 4 * gw), BF16),
        compiler_params=_cparams(("arbitrary",), 48 * 1024 * 1024),
        name="mix_outputs",
    )(ya, yb, p, p, p, p, p, p, p, *weights, ys5, p, *glu)


def _pad_last(a, width):
    return jnp.pad(a, [(0, 0)] * (a.ndim - 1) + [(0, width - a.shape[-1])])


def _stacked_weights(pr):
    n_layers = pr["mla_w_uq"].shape[0]
    w_uq = pr["mla_w_uq"].reshape(n_layers, MLA_Q_RANK, N_HEADS, MLA_QK)
    w_uq = _pad_last(w_uq, MLA_HEAD_PAD).reshape(n_layers, MLA_Q_RANK, N_HEADS * MLA_HEAD_PAD)
    w_ukv = pr["mla_w_ukv"].reshape(n_layers, MLA_KV_RANK, N_HEADS, MLA_NOPE + MLA_V)
    w_ukv = jnp.concatenate([w_ukv[..., :MLA_NOPE].reshape(n_layers, MLA_KV_RANK, -1),
                             w_ukv[..., MLA_NOPE:].reshape(n_layers, MLA_KV_RANK, -1)], axis=2)
    w_ukv = jnp.pad(w_ukv, ((0, 0), (0, MLA_KV_RANK_PAD - MLA_KV_RANK), (0, 0)))
    row = lambda a: a[:, None, :]
    return {
        "na_q_norm_w": row(pr["na_q_norm_w"]),
        "na_k_norm_w": row(pr["na_k_norm_w"]),
        "mla_cq_norm_w": row(pr["mla_cq_norm_w"]),
        "mla_ckv_norm_w": row(_pad_last(pr["mla_ckv_norm_w"], MLA_KV_RANK_PAD)),
        "mla_w_uq": w_uq.astype(BF16),
        "mla_w_ukv": w_ukv.astype(BF16),
        "mla_q_norm_w": row(_pad_last(pr["mla_q_norm_w"], MLA_HEAD_PAD)),
        "mla_k_norm_w": row(_pad_last(pr["mla_k_norm_w"], MLA_HEAD_PAD)),
        "conv_w": pr["conv_w"],
        "conv_b": row(pr["conv_b"]),
        "s5_d": row(pr["s5_d"]),
        "s5_glu_w": pr["s5_glu_w"].astype(BF16),
        "s5_glu_b": row(pr["s5_glu_b"]),
        "mix_norm_w": row(pr["mix_norm_w"]),
        "ffn_w2": pr["ffn_w2"].astype(BF16),
    }


def _layer_full_spec(a, layer):
    return pl.BlockSpec((None,) + a.shape[1:], lambda i: (layer,) + (0,) * (a.ndim - 1))


def _rope_tables(b, t, n_ctx):
    pos = jnp.arange(t, dtype=jnp.int32)
    row = (pos // GRID_W).astype(F32)
    col = (pos % GRID_W).astype(F32)
    n_freq = MLA_ROPE // 4
    inv_freq = ROPE_THETA ** (-jnp.arange(n_freq, dtype=F32) / n_freq)
    ang_r, ang_c = row[:, None] * inv_freq[None, :], col[:, None] * inv_freq[None, :]
    cos_r, sin_r, cos_c, sin_c = jnp.cos(ang_r), jnp.sin(ang_r), jnp.cos(ang_c), jnp.sin(ang_c)
    zeros = jnp.zeros((t, n_freq), F32)
    rest = LANE - MLA_ROPE
    cos_t = jnp.concatenate([cos_r, cos_r, cos_c, cos_c, jnp.ones((t, rest), F32)], axis=1)
    sina_t = jnp.concatenate([-sin_r, zeros, -sin_c, zeros, jnp.zeros((t, rest), F32)], axis=1)
    sinb_t = jnp.concatenate([zeros, sin_r, zeros, sin_c, jnp.zeros((t, rest), F32)], axis=1)
    n_c = b * n_ctx
    tables = []
    for tab, fill in ((cos_t, 1.0), (sina_t, 0.0), (sinb_t, 0.0)):
        tables.append(jnp.concatenate([jnp.tile(tab, (b, 1)), jnp.full((n_c, LANE), fill, F32)], axis=0))
    return tables


def _na_bias_tables(rpb, n_rows):
    w = GRID_W
    half = NA_WIN_ROWS // 2
    r0 = jnp.array([0, NA_Q_ROWS, n_rows - NA_Q_ROWS])
    ks = jnp.clip(r0 - half, 0, n_rows - NA_K_ROWS)
    rq = r0[:, None] + jnp.arange(NA_Q_ROWS)[None, :]
    rk = ks[:, None] + jnp.arange(NA_K_ROWS)[None, :]
    rs = jnp.clip(rq - half, 0, n_rows - NA_WIN_ROWS)
    v_row = (rk[:, None, :] >= rs[:, :, None]) & (rk[:, None, :] < rs[:, :, None] + NA_WIN_ROWS)
    d_row = jnp.clip(rk[:, None, :] - rq[:, :, None] + (NA_WIN_ROWS - 1), 0, 2 * NA_WIN_ROWS - 2)
    qc = jnp.arange(w)
    cs = jnp.clip(qc - NA_WIN_COLS // 2, 0, w - NA_WIN_COLS)
    v_col = (qc[None, :] >= cs[:, None]) & (qc[None, :] < cs[:, None] + NA_WIN_COLS)
    d_col = jnp.clip(qc[None, :] - qc[:, None], -(NA_WIN_COLS - 1), NA_WIN_COLS - 1) + (NA_WIN_COLS - 1)
    sel_r = jax.nn.one_hot(d_row, 2 * NA_WIN_ROWS - 1, dtype=F32)
    sel_c = jax.nn.one_hot(d_col, 2 * NA_WIN_COLS - 1, dtype=F32)
    bias = jnp.einsum("hab,cija,qkb->hciqjk", rpb * LOG2E, sel_r, sel_c, precision=lax.Precision.HIGHEST)
    valid = v_row[:, :, None, :, None] & v_col[None, None, :, None, :]
    bias = jnp.where(valid[None], bias, NEG_INF)
    return bias.reshape(rpb.shape[0], 3, NA_Q_ROWS * w, NA_K_ROWS * w)


def kernel(x, c, ctx, c_ctx, ada_w, ada_b, norm1_w, norm2_w, w_in, na_q_norm_w, na_k_norm_w, na_rpb, mla_cq_norm_w, mla_ckv_norm_w, mla_w_uq, mla_w_ukv, mla_q_norm_w, mla_k_norm_w, conv_w, conv_b, s5_lambda_re, s5_lambda_im, s5_log_dt, s5_b_re, s5_b_im, s5_c_re, s5_c_im, s5_d, s5_glu_w, s5_glu_b, mix_norm_w, w_out, ffn_w1, ffn_w3, ffn_w2):
    b, t, d = x.shape
    n_ctx = ctx.shape[1]
    n_layers = ada_w.shape[0]
    n_rows = t // GRID_W
    assert t % GRID_W == 0 and n_rows >= NA_K_ROWS and n_rows % NA_Q_ROWS == 0
    assert n_ctx == NA_Q_ROWS * GRID_W and t % n_ctx == 0 and b + 1 <= 8
    tm = 512 if (b * n_ctx) % 512 == 0 and t % 512 == 0 else 256
    tiles_per_batch = t // tm
    geo = {"b": b, "t": t, "ctx": n_ctx, "tm": tm,
           "tile_class": lambda i: jnp.minimum(i // tiles_per_batch, b)}
    lat_tiles = b * t // tm
    all_tiles = lat_tiles + b * n_ctx // tm

    pr = dict(na_q_norm_w=na_q_norm_w, na_k_norm_w=na_k_norm_w, mla_cq_norm_w=mla_cq_norm_w,
              mla_ckv_norm_w=mla_ckv_norm_w, mla_w_uq=mla_w_uq, mla_w_ukv=mla_w_ukv,
              mla_q_norm_w=mla_q_norm_w, mla_k_norm_w=mla_k_norm_w, conv_w=conv_w, conv_b=conv_b,
              s5_d=s5_d, s5_glu_w=s5_glu_w, s5_glu_b=s5_glu_b, mix_norm_w=mix_norm_w, ffn_w2=ffn_w2)

    h = (x.reshape(b * t, d), ctx.reshape(b * n_ctx, d))
    cvec = jnp.zeros((8, d), F32).at[:b].set(c).at[b].set(c_ctx)
    mod_all = _ada_mod(cvec, ada_w, ada_b)
    rope = _rope_tables(b, t, n_ctx)
    lw = _stacked_weights(pr)
    na_bias = jax.vmap(lambda rpb: _na_bias_tables(rpb, n_rows))(na_rpb)
    s5_ops = jax.vmap(_s5_operators)(s5_lambda_re, s5_lambda_im, s5_log_dt, s5_b_re, s5_b_im, s5_c_re, s5_c_im)
    m_in = 1088 if (b * t + b * n_ctx) % 1088 == 0 else tm

    for l in range(n_layers):
        last = l == n_layers - 1
        n_tiles = lat_tiles if last else all_tiles
        mod = mod_all[l]

        a = _norm_modulate(h, norm1_w[l], mod, 0, 1, geo, all_tiles)
        p = _input_projection(a, w_in, l, m_in)
        naq, nak, nav, mq, mk, mv = _prep_qkv(p, lw, l, rope, 256)
        ya = _na_attention(naq, nak, nav, na_bias, l, geo, not last)
        yb = _mla_attention(mq, mk, mv, geo, not last)
        ys5 = _s5_mixer(p, s5_ops, l, geo)
        y = _mix_outputs(ya, yb, p, ys5, lw, l, geo, n_tiles)
        h = _matmul_gated_residual(y, w_out, h, mod, 2, geo, n_tiles, 512, layer=l)

        f = _norm_modulate(h, norm2_w[l], mod, 3, 4, geo, n_tiles)
        g = _ffn_in(f, ffn_w1, ffn_w3, l, geo, n_tiles, 512)
        h = _matmul_gated_residual(g, lw["ffn_w2"], h, mod, 5, geo, n_tiles, 512, layer=l)

    return h.reshape(b, t, d)
```

```python
import functools
import math

import jax
import jax.numpy as jnp
from jax import lax
from jax.experimental import pallas as pl
from jax.experimental.pallas import tpu as pltpu

F32 = jnp.float32
BF16 = jnp.bfloat16

NORM_EPS = 1e-6
NEG_INF = -1e30
GRID_W = 64

GROUP_WIDTH = 1024
N_HEADS = 8
NA_HEAD_DIM = 128
NA_WIN_ROWS = 8
NA_WIN_COLS = 16
NA_Q_ROWS = 4
NA_K_ROWS = NA_Q_ROWS + NA_WIN_ROWS

MLA_NOPE = 128
MLA_ROPE = 64
MLA_QK = MLA_NOPE + MLA_ROPE
MLA_V = 128
MLA_Q_RANK = 896
MLA_KV_RANK = 320
MLA_KV_RANK_PAD = 512
MLA_HEAD_PAD = 256
V_HEAD_PAD = 256
LOG2E = math.log2(math.e)
ROPE_THETA = 10000.0

CONV_K = 3

S5_GROUPS = 64
S5_GROUP = 16
S5_STATE = 64
S5_CHUNK = 16
S5_BLOCK = S5_CHUNK * S5_GROUP


COL_NA = 0
COL_CONV = 3072
COL_S5 = 6144
COL_CQ = 7168
COL_CKV = 8192
IN_WIDTH_PAD = 8704
IN_TILE = 512
SRC_CQ = 3072
SRC_CKV = SRC_CQ + MLA_Q_RANK
SRC_CONV = SRC_CKV + MLA_KV_RANK + MLA_ROPE
SRC_S5 = SRC_CONV + 3 * GROUP_WIDTH

LANE = 128
VMEM_LIMIT = 56 * 1024 * 1024


def _cparams(sem, vmem=None):
    return pltpu.CompilerParams(dimension_semantics=sem, vmem_limit_bytes=vmem)


def _sigmoid(x):
    return 1.0 / (1.0 + jnp.exp(-x))


def _rms(x, w):
    return x * lax.rsqrt(jnp.mean(x * x, axis=-1, keepdims=True) + NORM_EPS) * w


def _dot(a, b):
    return jnp.dot(a, b, preferred_element_type=F32)


def _dot_nt(a, b):
    return lax.dot_general(a, b, (((1,), (1,)), ((), ())), preferred_element_type=F32)


def _ada_kernel(c_ref, w_ref, b_ref, o_ref):
    c = c_ref[...]
    s = (c * _sigmoid(c)).astype(BF16)
    o_ref[...] = _dot(s, w_ref[...].astype(BF16)) + b_ref[...]


def _ada_mod(cvec, ada_w, ada_b):
    n_layers, d, n = ada_w.shape
    tn = 512
    out = pl.pallas_call(
        _ada_kernel,
        grid=(n_layers, n // tn),
        in_specs=[pl.BlockSpec((8, d), lambda l, j: (0, 0)),
                  pl.BlockSpec((None, d, tn), lambda l, j: (l, 0, j)),
                  pl.BlockSpec((None, 1, tn), lambda l, j: (l, 0, j))],
        out_specs=pl.BlockSpec((None, 8, tn), lambda l, j: (l, 0, j)),
        out_shape=jax.ShapeDtypeStruct((n_layers, 8, n), F32),
        compiler_params=_cparams(("arbitrary", "arbitrary"), 40 * 1024 * 1024),
        name="ada_mod",
    )(cvec, ada_w, ada_b.reshape(n_layers, 1, n))
    return out.reshape(n_layers, 8, 6, d).transpose(0, 2, 1, 3)[:, :, :, None, :]


def _stream_parts(h):
    return tuple(h) if isinstance(h, (tuple, list)) else (h,)


def _stream_specs(parts, tm, width, row_of, col_of):
    if len(parts) == 1:
        return [pl.BlockSpec((tm, width), lambda *g: (row_of(*g), col_of(*g)))]
    lat_tiles = parts[0].shape[0] // tm
    return [pl.BlockSpec((tm, width), lambda *g: (jnp.minimum(row_of(*g), lat_tiles - 1), col_of(*g))),
            pl.BlockSpec((tm, width), lambda *g: (jnp.maximum(row_of(*g) - lat_tiles, 0), col_of(*g)))]


def _read_stream(h_refs, row_tile, lat_tiles):
    if len(h_refs) == 1:
        return h_refs[0][...]
    return jnp.where(row_tile < lat_tiles, h_refs[0][...], h_refs[1][...])


def _normmod_kernel(*refs, n_h, lat_tiles):
    h_refs = refs[:n_h]
    w_ref, shift_ref, scale_ref, o_ref = refs[n_h:]
    y = _rms(_read_stream(h_refs, pl.program_id(0), lat_tiles), w_ref[...])
    o_ref[...] = (y * (1.0 + scale_ref[...]) + shift_ref[...]).astype(o_ref.dtype)


def _norm_modulate(h, w, mod, k_shift, k_scale, geo, n_tiles):
    parts = _stream_parts(h)
    d = parts[0].shape[1]
    tm = geo["tm"]
    cls = geo["tile_class"]
    return pl.pallas_call(
        functools.partial(_normmod_kernel, n_h=len(parts), lat_tiles=parts[0].shape[0] // tm),
        grid=(n_tiles,),
        in_specs=_stream_specs(parts, tm, d, lambda i: i, lambda i: 0)
        + [pl.BlockSpec((1, d), lambda i: (0, 0)),
           pl.BlockSpec((None, None, 1, d), lambda i: (k_shift, cls(i), 0, 0)),
           pl.BlockSpec((None, None, 1, d), lambda i: (k_scale, cls(i), 0, 0))],
        out_specs=pl.BlockSpec((tm, d), lambda i: (i, 0)),
        out_shape=jax.ShapeDtypeStruct((n_tiles * tm, d), BF16),
        compiler_params=_cparams(("arbitrary",), VMEM_LIMIT),
        name="norm_modulate",
    )(*parts, w.reshape(1, d), mod, mod)


def _in_proj_kernel(a_ref, w_ref, w2_ref, o_ref, w2b_ref, wb_ref, *, n_row_tiles, n_w2_blocks):
    @pl.when(pl.program_id(1) == 0)
    def _():
        wb_ref[...] = w_ref[...].astype(BF16)

    o_ref[...] = _dot(a_ref[...], wb_ref[...])

    step = pl.program_id(0) * n_row_tiles + pl.program_id(1)

    @pl.when(step < n_w2_blocks)
    def _():
        w2b_ref[...] = w2_ref[...].astype(BF16)


def _in_proj_source_col(j):
    t = IN_TILE // LANE
    unit = jnp.where(j < 6, j * t,
           jnp.where(j < 12, SRC_CONV // LANE + (j - 6) * t,
           jnp.where(j < 14, SRC_S5 // LANE + (j - 12) * t,
           jnp.where(j < 16, SRC_CQ // LANE + (j - 14) * t, SRC_CKV // LANE))))
    return unit * LANE


def _input_projection(a, w_in, w2, layer, tm):
    m, k = a.shape
    tn = IN_TILE
    assert m % tm == 0 and COL_CKV + tn == IN_WIDTH_PAD and SRC_CKV + tn <= w_in.shape[-1]
    n_col_tiles, n_row_tiles = IN_WIDTH_PAD // tn, m // tm
    hidden, d_out = w2.shape[1:]
    slab = next(r for r in range(16, hidden + 1, 16) if hidden % r == 0 and hidden // r <= n_col_tiles * n_row_tiles)
    n_w2_blocks = hidden // slab
    w2_block = lambda j, i: jnp.minimum(j * n_row_tiles + i, n_w2_blocks - 1)
    return pl.pallas_call(
        functools.partial(_in_proj_kernel, n_row_tiles=n_row_tiles, n_w2_blocks=n_w2_blocks),
        grid=(n_col_tiles, n_row_tiles),
        in_specs=[pl.BlockSpec((tm, k), lambda j, i: (i, 0)),
                  pl.BlockSpec((pl.Element(k), pl.Element(tn)),
                               lambda j, i: (layer * k, _in_proj_source_col(j))),
                  pl.BlockSpec((None, slab, d_out), lambda j, i: (layer, w2_block(j, i), 0))],
        out_specs=[pl.BlockSpec((tm, tn), lambda j, i: (i, j)),
                   pl.BlockSpec((slab, d_out), lambda j, i: (w2_block(j, i), 0))],
        out_shape=[jax.ShapeDtypeStruct((m, IN_WIDTH_PAD), F32), jax.ShapeDtypeStruct((hidden, d_out), BF16)],
        scratch_shapes=[pltpu.VMEM((k, tn), BF16)],
        compiler_params=_cparams(("arbitrary", "arbitrary"), VMEM_LIMIT),
        name="input_projection",
    )(a, w_in.reshape(-1, w_in.shape[-1]), w2)


def _mm_res_kernel(a_ref, w_ref, *refs, n_h, lat_tiles, f32w):
    h_refs = refs[:n_h]
    g_ref, o_ref = refs[n_h], refs[n_h + 1]
    if f32w:
        wb_ref = refs[n_h + 2]

        @pl.when(pl.program_id(1) == 0)
        def _():
            wb_ref[...] = w_ref[...].astype(BF16)

        w = wb_ref[...]
    else:
        w = w_ref[...]
    h = _read_stream(h_refs, pl.program_id(1), lat_tiles)
    o_ref[...] = h + g_ref[...] * _dot(a_ref[...], w)


def _layer_weight_spec(w, layer, tn):
    if w.ndim == 3:
        return pl.BlockSpec((None, w.shape[1], tn), lambda j, i: (layer, 0, j))
    return pl.BlockSpec((w.shape[0], tn), lambda j, i: (0, j))


def _matmul_gated_residual(a, w, h, mod, k_gate, geo, n_tiles, tn, layer=None):
    m, k = a.shape
    n = w.shape[-1]
    tm = geo["tm"]
    cls = geo["tile_class"]
    f32w = w.dtype == F32
    parts = _stream_parts(h)
    return pl.pallas_call(
        functools.partial(_mm_res_kernel, n_h=len(parts), lat_tiles=parts[0].shape[0] // tm, f32w=f32w),
        grid=(n // tn, n_tiles),
        in_specs=[pl.BlockSpec((tm, k), lambda j, i: (i, 0)),
                  _layer_weight_spec(w, layer, tn)]
        + _stream_specs(parts, tm, tn, lambda j, i: i, lambda j, i: j)
        + [pl.BlockSpec((None, None, 1, tn), lambda j, i: (k_gate, cls(i), 0, j))],
        out_specs=pl.BlockSpec((tm, tn), lambda j, i: (i, j)),
        out_shape=jax.ShapeDtypeStruct((n_tiles * tm, n), F32),
        scratch_shapes=[pltpu.VMEM((k, tn), BF16)] if f32w else [],
        compiler_params=_cparams(("arbitrary", "arbitrary"), VMEM_LIMIT),
        name="matmul_gated_residual",
    )(a, w, *parts, mod)


def _ffn1_kernel(a_ref, w1_ref, w3_ref, o_ref, w1b_ref, w3b_ref):
    @pl.when(pl.program_id(1) == 0)
    def _():
        w1b_ref[...] = w1_ref[...].astype(BF16)
        w3b_ref[...] = w3_ref[...].astype(BF16)

    a = a_ref[...]
    u = _dot(a, w1b_ref[...])
    v = _dot(a, w3b_ref[...])
    o_ref[...] = (u * _sigmoid(u) * v).astype(o_ref.dtype)


def _ffn_in(a, w1, w3, layer, geo, n_tiles, tn):
    m, k = a.shape
    n = w1.shape[-1]
    tm = geo["tm"]
    return pl.pallas_call(
        _ffn1_kernel,
        grid=(pl.cdiv(n, tn), n_tiles),
        in_specs=[pl.BlockSpec((tm, k), lambda j, i: (i, 0)),
                  _layer_weight_spec(w1, layer, tn),
                  _layer_weight_spec(w3, layer, tn)],
        out_specs=pl.BlockSpec((tm, tn), lambda j, i: (i, j)),
        out_shape=jax.ShapeDtypeStruct((m, n), BF16),
        scratch_shapes=[pltpu.VMEM((k, tn), BF16), pltpu.VMEM((k, tn), BF16)],
        compiler_params=_cparams(("arbitrary", "arbitrary"), VMEM_LIMIT),
        name="ffn_in",
    )(a, w1, w3)


def _rope_tail(t, cos_ref, sina_ref, sinb_ref):
    q = MLA_ROPE // 4
    return (t * cos_ref[...] + pltpu.roll(t, LANE - q, 1) * sina_ref[...]
            + pltpu.roll(t, q, 1) * sinb_ref[...])


def _prep_kernel(na_ref, cq_ref, ckvkr_ref, naqw_ref, nakw_ref, cqw_ref, ckvw_ref,
                 wuq_ref, wukv_ref, mqw_ref, mkw_ref, cos_ref, sina_ref, sinb_ref,
                 naq_ref, nak_ref, nav_ref, mq_ref, mk_ref, mv_ref):
    hd = NA_HEAD_DIM
    tm = na_ref.shape[0]
    ones_col = jnp.where(lax.broadcasted_iota(jnp.int32, (tm, LANE), 1) == 0, 1.0, 0.0).astype(BF16)
    na_qs = naqw_ref[...] * (NA_HEAD_DIM ** -0.5 * LOG2E)
    for h in range(N_HEADS):
        naq_ref[:, h * hd:(h + 1) * hd] = _rms(na_ref[:, h * hd:(h + 1) * hd], na_qs).astype(BF16)
        nak_ref[:, h * hd:(h + 1) * hd] = _rms(
            na_ref[:, GROUP_WIDTH + h * hd:GROUP_WIDTH + (h + 1) * hd], nakw_ref[...]).astype(BF16)
        vo = h * V_HEAD_PAD
        nav_ref[:, vo:vo + hd] = na_ref[:, 2 * GROUP_WIDTH + h * hd:2 * GROUP_WIDTH + (h + 1) * hd].astype(BF16)
        nav_ref[:, vo + hd:vo + V_HEAD_PAD] = ones_col

    cq = _rms(cq_ref[...], cqw_ref[...]).astype(BF16)
    q = _dot(cq, wuq_ref[...])
    inv_qk = 1.0 / MLA_QK
    mqw = mqw_ref[...] * (MLA_QK ** -0.5 * LOG2E)
    for h in range(N_HEADS):
        o = h * MLA_HEAD_PAD
        nope = q[:, o:o + MLA_NOPE]
        tail = q[:, o + MLA_NOPE:o + MLA_HEAD_PAD]
        ss = jnp.sum(nope * nope, axis=-1, keepdims=True) + jnp.sum(tail * tail, axis=-1, keepdims=True)
        r = lax.rsqrt(ss * inv_qk + NORM_EPS)
        mq_ref[:, o:o + MLA_NOPE] = (nope * r * mqw[:, :MLA_NOPE]).astype(BF16)
        mq_ref[:, o + MLA_NOPE:o + MLA_HEAD_PAD] = _rope_tail(
            tail * r * mqw[:, MLA_NOPE:], cos_ref, sina_ref, sinb_ref).astype(BF16)

    blk = ckvkr_ref[...]
    lane = lax.broadcasted_iota(jnp.int32, blk.shape, 1)
    ckv = jnp.where(lane < MLA_KV_RANK, blk, 0.0)
    ckv_ms = jnp.sum(ckv * ckv, axis=-1, keepdims=True) * (1.0 / MLA_KV_RANK)
    ckv_n = (ckv * lax.rsqrt(ckv_ms + NORM_EPS) * ckvw_ref[...]).astype(BF16)
    kv = _dot(ckv_n, wukv_ref[...])
    for h in range(N_HEADS):
        vo = h * V_HEAD_PAD
        mv_ref[:, vo:vo + MLA_V] = kv[:, GROUP_WIDTH + h * MLA_V:GROUP_WIDTH + (h + 1) * MLA_V].astype(BF16)
        mv_ref[:, vo + MLA_V:vo + V_HEAD_PAD] = ones_col
    kr = pltpu.roll(blk[:, 2 * LANE:3 * LANE], LANE - MLA_ROPE, 1)
    kr = jnp.where(lax.broadcasted_iota(jnp.int32, kr.shape, 1) < MLA_ROPE, kr, 0.0)
    kr_ss = jnp.sum(kr * kr, axis=-1, keepdims=True)
    mkw = mkw_ref[...]
    for h in range(N_HEADS):
        o = h * MLA_HEAD_PAD
        kn = kv[:, h * MLA_NOPE:(h + 1) * MLA_NOPE]
        r = lax.rsqrt((jnp.sum(kn * kn, axis=-1, keepdims=True) + kr_ss) * inv_qk + NORM_EPS)
        mk_ref[:, o:o + MLA_NOPE] = (kn * r * mkw[:, :MLA_NOPE]).astype(BF16)
        mk_ref[:, o + MLA_NOPE:o + MLA_HEAD_PAD] = _rope_tail(
            kr * r * mkw[:, MLA_NOPE:], cos_ref, sina_ref, sinb_ref).astype(BF16)


def _prep_qkv(p, lw, layer, rope, tm):
    n_tok = p.shape[0]
    blk = lambda width, col: pl.BlockSpec((tm, width), lambda i: (i, col // width))
    full = lambda a: _layer_full_spec(a, layer)
    tab = pl.BlockSpec((tm, LANE), lambda i: (i, 0))
    out_w = [GROUP_WIDTH, GROUP_WIDTH, N_HEADS * V_HEAD_PAD,
             N_HEADS * MLA_HEAD_PAD, N_HEADS * MLA_HEAD_PAD, N_HEADS * V_HEAD_PAD]
    weights = [lw["na_q_norm_w"], lw["na_k_norm_w"], lw["mla_cq_norm_w"], lw["mla_ckv_norm_w"],
               lw["mla_w_uq"], lw["mla_w_ukv"], lw["mla_q_norm_w"], lw["mla_k_norm_w"]]
    return pl.pallas_call(
        _prep_kernel,
        grid=(n_tok // tm,),
        in_specs=[blk(3 * GROUP_WIDTH, COL_NA), blk(MLA_Q_RANK, COL_CQ), blk(MLA_KV_RANK_PAD, COL_CKV)]
        + [full(a) for a in weights] + [tab, tab, tab],
        out_specs=[pl.BlockSpec((tm, w), lambda i: (i, 0)) for w in out_w],
        out_shape=[jax.ShapeDtypeStruct((n_tok, w), BF16) for w in out_w],
        compiler_params=_cparams(("arbitrary",), 48 * 1024 * 1024),
        name="prep_qkv",
    )(p, p, p, *weights, *rope)


def _softmax_attend(q, pairs, bias=None):
    scores = []
    for idx, (k, _) in enumerate(pairs):
        s = _dot_nt(q, k)
        if idx == 0 and bias is not None:
            s = s + bias
        scores.append(s)
    m = scores[0].max(axis=-1, keepdims=True)
    for s in scores[1:]:
        m = jnp.maximum(m, s.max(axis=-1, keepdims=True))
    acc = None
    for s, (_, v) in zip(scores, pairs):
        o = _dot(jnp.exp2(s - m).astype(BF16), v)
        acc = o if acc is None else acc + o
    dv = acc.shape[-1] // 2
    return acc[:, :dv] / acc[:, dv:dv + 1]


NA_INVALID_ROW = 2 * NA_WIN_ROWS - 1


def _na_window_bias(bt_ref, rb, n_rows):
    half = NA_WIN_ROWS // 2
    ks_row = jnp.clip(rb * NA_Q_ROWS - half, 0, n_rows - NA_K_ROWS)
    lane_lo = lax.broadcasted_iota(jnp.int32, (GRID_W, LANE), 1) < GRID_W
    strips = []
    for i in range(NA_Q_ROWS):
        rq = rb * NA_Q_ROWS + i
        rs = jnp.clip(rq - half, 0, n_rows - NA_WIN_ROWS)

        def slab(j, rq=rq, rs=rs):
            rk = ks_row + j
            valid = (rk >= rs) & (rk < rs + NA_WIN_ROWS)
            return bt_ref[jnp.where(valid, rk - rq + (NA_WIN_ROWS - 1), NA_INVALID_ROW)]

        pieces = [jnp.where(lane_lo, slab(2 * m), slab(2 * m + 1)) for m in range(NA_K_ROWS // 2)]
        strips.append(jnp.concatenate(pieces, axis=-1))
    return jnp.concatenate(strips, axis=0)


def _na_kernel(q_ref, k_ref, v_ref, kc_ref, vc_ref, kca_ref, vca_ref, bt_ref, o_ref, *,
               n_lat_steps, steps_per_batch, n_rows, n_sub, ctx):
    i = pl.program_id(1)
    sq = NA_Q_ROWS * GRID_W

    @pl.when(i < n_lat_steps)
    def _():
        step = i % steps_per_batch
        for sb in range(n_sub):
            rb = step * n_sub + sb
            ks = jnp.clip(rb * NA_Q_ROWS - NA_WIN_ROWS // 2, 0, n_rows - NA_K_ROWS) * GRID_W
            ks = pl.multiple_of(ks, GRID_W)
            kw = k_ref[pl.ds(ks, NA_K_ROWS * GRID_W), :]
            vw = v_ref[pl.ds(ks, NA_K_ROWS * GRID_W), :]
            rows = slice(sb * sq, (sb + 1) * sq)
            o_ref[rows, :] = _softmax_attend(q_ref[rows, :], [(kw, vw), (kc_ref[...], vc_ref[...])],
                                             _na_window_bias(bt_ref, rb, n_rows))

    @pl.when(i == n_lat_steps)
    def _():
        for bb in range(q_ref.shape[0] // ctx):
            rows = slice(bb * ctx, (bb + 1) * ctx)
            o_ref[rows, :] = _softmax_attend(q_ref[rows, :], [(kca_ref[rows, :], vca_ref[rows, :])])


def _na_attention(naq, nak, nav, bias_tab, layer, geo, with_ctx):
    b, t, ctx = geo["b"], geo["t"], geo["ctx"]
    n_tok = naq.shape[0]
    sq = NA_Q_ROWS * GRID_W
    tq = b * ctx
    assert tq % sq == 0 and t % tq == 0
    n_sub = tq // sq
    n_rows = t // GRID_W
    spb = t // tq
    lat_steps = b * spb
    ctx_block0 = b * t // ctx
    batch = lambda i: jnp.minimum(i // spb, b - 1)

    return pl.pallas_call(
        functools.partial(_na_kernel, n_lat_steps=lat_steps, steps_per_batch=spb, n_rows=n_rows,
                          n_sub=n_sub, ctx=ctx),
        grid=(N_HEADS, lat_steps + (1 if with_ctx else 0)),
        in_specs=[pl.BlockSpec((tq, NA_HEAD_DIM), lambda h, i: (i, h)),
                  pl.BlockSpec((t, NA_HEAD_DIM), lambda h, i: (batch(i), h)),
                  pl.BlockSpec((t, V_HEAD_PAD), lambda h, i: (batch(i), h)),
                  pl.BlockSpec((ctx, NA_HEAD_DIM), lambda h, i: (ctx_block0 + batch(i), h)),
                  pl.BlockSpec((ctx, V_HEAD_PAD), lambda h, i: (ctx_block0 + batch(i), h)),
                  pl.BlockSpec((tq, NA_HEAD_DIM), lambda h, i: (lat_steps, h)),
                  pl.BlockSpec((tq, V_HEAD_PAD), lambda h, i: (lat_steps, h)),
                  pl.BlockSpec((None, None) + bias_tab.shape[2:], lambda h, i: (layer, h, 0, 0, 0))],
        out_specs=pl.BlockSpec((tq, NA_HEAD_DIM), lambda h, i: (i, h)),
        out_shape=jax.ShapeDtypeStruct((n_tok, GROUP_WIDTH), F32),
        compiler_params=_cparams(("arbitrary", "arbitrary"), 40 * 1024 * 1024),
        name="na_attention",
    )(naq, nak, nav, nak, nav, nak, nav, bias_tab)


def _flash_attend(q, loads):
    m = None
    acc = None
    for load in loads:
        k, v = load()
        s = _dot_nt(q, k)
        mc = s.max(axis=-1, keepdims=True)
        if m is None:
            m = mc
            acc = _dot(jnp.exp2(s - m).astype(BF16), v)
        else:
            m_new = jnp.maximum(m, mc)
            acc = acc * jnp.exp2(m - m_new) + _dot(jnp.exp2(s - m_new).astype(BF16), v)
            m = m_new
    dv = acc.shape[-1] // 2
    return acc[:, :dv] / acc[:, dv:dv + 1]


def _mla_kernel(q_ref, k_ref, v_ref, kc_ref, vc_ref, kca_ref, vca_ref, o_ref, *, n_q_blocks, key_chunk, ctx, hps):
    i = pl.program_id(1)
    dq, dv = MLA_HEAD_PAD, V_HEAD_PAD

    @pl.when(i < n_q_blocks)
    def _():
        for hh in range(hps):
            qc, vc = slice(hh * dq, (hh + 1) * dq), slice(hh * dv, (hh + 1) * dv)

            def lat_chunk(j, qc=qc, vc=vc):
                rows = slice(j * key_chunk, (j + 1) * key_chunk)
                return lambda: (k_ref[rows, qc], v_ref[rows, vc])

            loads = [lat_chunk(j) for j in range(k_ref.shape[0] // key_chunk)]
            loads.append(lambda qc=qc, vc=vc: (kc_ref[:, qc], vc_ref[:, vc]))
            o_ref[:, hh * MLA_V:(hh + 1) * MLA_V] = _flash_attend(q_ref[:, qc], loads)

    @pl.when(i == n_q_blocks)
    def _():
        for hh in range(hps):
            qc, vc = slice(hh * dq, (hh + 1) * dq), slice(hh * dv, (hh + 1) * dv)
            for bb in range(q_ref.shape[0] // ctx):
                rows = slice(bb * ctx, (bb + 1) * ctx)
                o_ref[rows, hh * MLA_V:(hh + 1) * MLA_V] = _flash_attend(
                    q_ref[rows, qc], [lambda: (kca_ref[rows, qc], vca_ref[rows, vc])])


def _mla_attention(mq, mk, mv, geo, with_ctx):
    b, t, ctx = geo["b"], geo["t"], geo["ctx"]
    n_tok = mq.shape[0]
    tq = b * ctx
    assert t % tq == 0
    nqb = t // tq
    lat_blocks = b * nqb
    ctx_block0 = b * t // ctx
    batch = lambda i: jnp.minimum(i // nqb, b - 1)
    hps = 2
    dq, dv = hps * MLA_HEAD_PAD, hps * V_HEAD_PAD

    return pl.pallas_call(
        functools.partial(_mla_kernel, n_q_blocks=lat_blocks, key_chunk=512, ctx=ctx, hps=hps),
        grid=(N_HEADS // hps, lat_blocks + (1 if with_ctx else 0)),
        in_specs=[pl.BlockSpec((tq, dq), lambda h, i: (i, h)),
                  pl.BlockSpec((t, dq), lambda h, i: (batch(i), h)),
                  pl.BlockSpec((t, dv), lambda h, i: (batch(i), h)),
                  pl.BlockSpec((ctx, dq), lambda h, i: (ctx_block0 + batch(i), h)),
                  pl.BlockSpec((ctx, dv), lambda h, i: (ctx_block0 + batch(i), h)),
                  pl.BlockSpec((tq, dq), lambda h, i: (lat_blocks, h)),
                  pl.BlockSpec((tq, dv), lambda h, i: (lat_blocks, h))],
        out_specs=pl.BlockSpec((tq, hps * MLA_V), lambda h, i: (i, h)),
        out_shape=jax.ShapeDtypeStruct((n_tok, GROUP_WIDTH), F32),
        compiler_params=_cparams(("arbitrary", "arbitrary"), 48 * 1024 * 1024),
        name="mla_attention",
    )(mq, mk, mv, mk, mv, mk, mv)


S5_OCTET = LANE // S5_GROUP


def _s5_gather_chunks(p_ref, u_ref):
    rows = u_ref.shape[0]
    for s in range(S5_CHUNK):
        xs = p_ref[pl.ds(s, rows, stride=S5_CHUNK), :]
        for k in range(S5_OCTET):
            dst = k * S5_BLOCK + s * S5_GROUP
            u_ref[:, dst:dst + S5_GROUP] = xs[:, k * S5_GROUP:(k + 1) * S5_GROUP]


def _s5_contrib_kernel(p_ref, b_ref, o_ref, u_ref):
    _s5_gather_chunks(p_ref, u_ref)
    r = [_dot(u_ref[:, k * S5_BLOCK:(k + 1) * S5_BLOCK].astype(BF16), b_ref[k]) for k in range(S5_OCTET)]
    ps = S5_STATE
    for plane in range(4):
        o_ref[plane] = jnp.concatenate([rk[:, plane * ps:(plane + 1) * ps] for rk in r], axis=-1)


def _s5_scan_kernel(c_ref, lr_ref, li_ref, x_ref, *, n_tiles, n_ctx_tiles, nb):
    sub = 8
    n_lat_tiles = n_tiles - n_ctx_tiles
    lanes = c_ref.shape[-1]
    lr = jnp.broadcast_to(lr_ref[...], (sub, lanes))
    li = jnp.broadcast_to(li_ref[...], (sub, lanes))
    row = lax.broadcasted_iota(jnp.int32, (sub, lanes), 0)

    def run(reverse):
        order = range(sub - 1, -1, -1) if reverse else range(sub)

        def body(jt, carry):
            if reverse:
                tile = jnp.where(jt < n_ctx_tiles, n_ctx_tiles - 1 - jt, n_tiles - 1 - (jt - n_ctx_tiles))
            else:
                tile = jt
            new = []
            for bi in range(nb):
                xr, xi = carry[2 * bi], carry[2 * bi + 1]
                row_tile = jnp.where(tile < n_ctx_tiles, nb * n_lat_tiles + bi * n_ctx_tiles + tile,
                                     bi * n_lat_tiles + tile - n_ctx_tiles)
                r0 = pl.multiple_of(row_tile * sub, sub)
                cr = c_ref[0, pl.ds(r0, sub), :]
                ci = c_ref[1, pl.ds(r0, sub), :]
                out_r = jnp.zeros_like(cr)
                out_i = jnp.zeros_like(ci)
                for k in order:
                    out_r = jnp.where(row == k, xr, out_r)
                    out_i = jnp.where(row == k, xi, out_i)
                    ck_r = jnp.broadcast_to(cr[k:k + 1, :], (sub, lanes))
                    ck_i = jnp.broadcast_to(ci[k:k + 1, :], (sub, lanes))
                    xr, xi = lr * xr - li * xi + ck_r, lr * xi + li * xr + ck_i
                x_ref[0, pl.ds(r0, sub), :] = out_r
                x_ref[1, pl.ds(r0, sub), :] = out_i
                new += [xr, xi]
            return tuple(new)

        zero = jnp.zeros((sub, lanes), F32)
        lax.fori_loop(0, n_tiles, body, (zero,) * (2 * nb))

    @pl.when(pl.program_id(0) == 0)
    def _():
        run(False)

    @pl.when(pl.program_id(0) == 1)
    def _():
        run(True)


def _s5_out_kernel(p_ref, x_ref, t_ref, ct_ref, o_ref, u_ref, y_ref):
    _s5_gather_chunks(p_ref, u_ref)
    rows = u_ref.shape[0]
    ps = S5_STATE
    for k in range(S5_OCTET):
        x = jnp.concatenate([x_ref[plane][:, k * ps:(k + 1) * ps] for plane in range(4)], axis=-1)
        cols = slice(k * S5_BLOCK, (k + 1) * S5_BLOCK)
        y_ref[:, cols] = _dot(u_ref[:, cols].astype(BF16), t_ref[k]) + _dot_nt(x.astype(BF16), ct_ref[k])
    for t in range(S5_CHUNK):
        piece = jnp.concatenate([y_ref[:, k * S5_BLOCK + t * S5_GROUP:k * S5_BLOCK + (t + 1) * S5_GROUP]
                                 for k in range(S5_OCTET)], axis=-1)
        o_ref[pl.ds(t, rows, stride=S5_CHUNK), :] = piece


def _s5_mixer(p, ops, layer, geo):
    b, t, ctx = geo["b"], geo["t"], geo["ctx"]
    n_chunks = (ctx + t) // S5_CHUNK
    rows = n_chunks * b
    n_tok = p.shape[0]
    assert rows * S5_CHUNK == n_tok
    n_oct = S5_GROUPS // S5_OCTET
    state_w = S5_GROUPS * S5_STATE
    oct_w = S5_OCTET * S5_BLOCK
    oct_states = S5_OCTET * S5_STATE
    sub = 8
    assert (t // S5_CHUNK) % sub == 0 and (ctx // S5_CHUNK) % sub == 0
    p_spec = pl.BlockSpec((n_tok, LANE), lambda o: (0, COL_S5 // LANE + o))
    op_spec = pl.BlockSpec((None, S5_OCTET, S5_BLOCK, S5_BLOCK), lambda o: (layer, o, 0, 0))
    plane_spec = pl.BlockSpec((4, rows, oct_states), lambda o: (0, 0, o))

    contrib = pl.pallas_call(
        _s5_contrib_kernel,
        grid=(n_oct,),
        in_specs=[p_spec, op_spec],
        out_specs=plane_spec,
        out_shape=jax.ShapeDtypeStruct((4, rows, state_w), F32),
        scratch_shapes=[pltpu.VMEM((rows, oct_w), F32)],
        compiler_params=_cparams(("arbitrary",), 48 * 1024 * 1024),
        name="s5_contrib",
    )(p, ops["b_mat"])

    lane_blk = 512
    states = pl.pallas_call(
        functools.partial(_s5_scan_kernel, n_tiles=n_chunks // sub, n_ctx_tiles=ctx // S5_CHUNK // sub, nb=b),
        grid=(2, state_w // lane_blk),
        in_specs=[pl.BlockSpec((None, 2, rows, lane_blk), lambda d, l: (d, 0, 0, l)),
                  pl.BlockSpec((None, None, 1, lane_blk), lambda d, l: (layer, d, 0, l)),
                  pl.BlockSpec((None, None, 1, lane_blk), lambda d, l: (layer, d, 0, l))],
        out_specs=pl.BlockSpec((None, 2, rows, lane_blk), lambda d, l: (d, 0, 0, l)),
        out_shape=jax.ShapeDtypeStruct((2, 2, rows, state_w), F32),
        compiler_params=_cparams(("arbitrary", "arbitrary"), 40 * 1024 * 1024),
        name="s5_scan",
    )(contrib.reshape(2, 2, rows, state_w), ops["decay_re"], ops["decay_im"])

    return pl.pallas_call(
        _s5_out_kernel,
        grid=(n_oct,),
        in_specs=[p_spec, plane_spec, op_spec, op_spec],
        out_specs=pl.BlockSpec((n_tok, LANE), lambda o: (0, o)),
        out_shape=jax.ShapeDtypeStruct((n_tok, GROUP_WIDTH), F32),
        scratch_shapes=[pltpu.VMEM((rows, oct_w), F32), pltpu.VMEM((rows, oct_w), F32)],
        compiler_params=_cparams(("arbitrary",), VMEM_LIMIT),
        name="s5_out",
    )(p, states.reshape(4, rows, state_w), ops["t_sum"], ops["c_mat_t"])


def _s5_operators(lam_re, lam_im, log_dt, b_re, b_im, c_re, c_im):
    hp = lax.Precision.HIGHEST
    g, pn, ni, lc = S5_GROUPS, S5_STATE, S5_GROUP, S5_CHUNK
    dt = jnp.exp(log_dt)[..., None]
    zr, zi = lam_re * dt, lam_im * dt
    up = jnp.arange(lc, dtype=F32)
    down = (lc - 1) - up

    def powers(d, steps):
        mag = jnp.exp(zr[d][None] * steps[:, None, None])
        ang = zi[d][None] * steps[:, None, None]
        return mag * jnp.cos(ang), mag * jnp.sin(ang)

    one = jnp.ones((1,), F32)
    z1 = [powers(d, one) for d in range(2)]
    nr = jnp.stack([z1[0][0][0], z1[1][0][0]]) - 1.0
    nim = jnp.stack([z1[0][1][0], z1[1][1][0]])
    den = lam_re * lam_re + lam_im * lam_im
    cr_, ci_ = (nr * lam_re + nim * lam_im) / den, (nim * lam_re - nr * lam_im) / den
    bz_r = cr_[..., None] * b_re - ci_[..., None] * b_im
    bz_i = cr_[..., None] * b_im + ci_[..., None] * b_re

    def lag_kernel(d, steps):
        pr, pi = powers(d, steps)
        m_r = pr[..., None] * bz_r[d][None] - pi[..., None] * bz_i[d][None]
        m_i = pr[..., None] * bz_i[d][None] + pi[..., None] * bz_r[d][None]
        k = (jnp.einsum("gop,dgpi->dgoi", c_re[d], m_r, precision=hp)
             - jnp.einsum("gop,dgpi->dgoi", c_im[d], m_i, precision=hp))
        return k.transpose(1, 3, 0, 2)

    k_f = lag_kernel(0, up)
    k_b = lag_kernel(1, down)
    two_sided = jnp.concatenate([k_b[:, :, :lc - 1], k_f[:, :, :1] + k_b[:, :, lc - 1:], k_f[:, :, 1:]], axis=2)
    two_sided = two_sided.reshape(g, ni, (2 * lc - 1) * ni)
    t_sum = jnp.stack([two_sided[:, :, (lc - 1 - s) * ni:(lc - 1 - s) * ni + S5_BLOCK] for s in range(lc)], axis=1)
    t_sum = t_sum.reshape(g, S5_BLOCK, S5_BLOCK)

    per_dir = lambda a: jnp.stack([a[0], a[0], a[1], a[1]])
    by_pos = lambda planes: jnp.stack(planes).transpose(2, 1, 0, 3).reshape(g, lc, 1, 4 * pn)
    by_chan = lambda a: per_dir(a).transpose(1, 2, 0, 3).reshape(g, 1, ni, 4 * pn)

    ef_r, ef_i = powers(0, down)
    eb_r, eb_i = powers(1, up)
    bz_ri = (bz_r.transpose(0, 1, 3, 2), bz_i.transpose(0, 1, 3, 2))
    b_mat = (by_pos([ef_r, ef_i, eb_r, eb_i]) * by_chan(bz_ri[0])
             + by_pos([-ef_i, ef_r, -eb_i, eb_r]) * by_chan(bz_ri[1])).reshape(g, S5_BLOCK, 4 * pn)

    pf_r, pf_i = powers(0, up + 1.0)
    pb_r, pb_i = powers(1, lc - up)
    c_mat_t = (by_pos([pf_r, -pf_i, pb_r, -pb_i]) * by_chan(c_re)
               + by_pos([-pf_i, -pf_r, -pb_i, -pb_r]) * by_chan(c_im)).reshape(g, S5_BLOCK, 4 * pn)

    full = jnp.full((1,), float(lc), F32)
    decay = [powers(d, full) for d in range(2)]
    return {"t_sum": t_sum.astype(BF16), "b_mat": b_mat.astype(BF16), "c_mat_t": c_mat_t.astype(BF16),
            "decay_re": jnp.stack([decay[0][0], decay[1][0]]).reshape(2, 1, g * pn),
            "decay_im": jnp.stack([decay[0][1], decay[1][1]]).reshape(2, 1, g * pn)}


def _group_norm_store(o_ref, k, y, w_ref):
    cols = slice(k * GROUP_WIDTH, (k + 1) * GROUP_WIDTH)
    o_ref[:, cols] = _rms(y, w_ref[:, cols]).astype(o_ref.dtype)


def _mix_kernel(ya_ref, yb_ref, gb_ref, gc_ref, u_ref, gcp_ref, up_ref, gcn_ref, un_ref,
                cw_ref, cb_ref, ys_ref, us_ref, ds_ref, gw_ref, gbias_ref, mw_ref, o_ref, *, tm, lat_rows, t, ctx):
    i = pl.program_id(0)
    _group_norm_store(o_ref, 0, ya_ref[...], mw_ref)
    _group_norm_store(o_ref, 1, yb_ref[...], mw_ref)

    r0 = i * tm
    in_lat = r0 < lat_rows
    seq_len = jnp.where(in_lat, t, ctx)
    off = jnp.where(in_lat, r0, r0 - lat_rows) % seq_len
    has_prev = off != 0
    has_next = off + tm != seq_len
    v = gc_ref[...] * u_ref[...]
    v_prev_row = jnp.where(has_prev, gcp_ref[7:8, :] * up_ref[7:8, :], 0.0)
    v_next_row = jnp.where(has_next, gcn_ref[0:1, :] * un_ref[0:1, :], 0.0)
    row = lax.broadcasted_iota(jnp.int32, v.shape, 0)
    v_prev = jnp.where(row == 0, v_prev_row, pltpu.roll(v, 1, 0))
    v_next = jnp.where(row == tm - 1, v_next_row, pltpu.roll(v, tm - 1, 0))
    conv = cw_ref[0:1, :] * v_prev + cw_ref[1:2, :] * v + cw_ref[2:3, :] * v_next + cb_ref[...]
    _group_norm_store(o_ref, 2, gb_ref[...] * conv, mw_ref)

    y = ys_ref[...] + ds_ref[...] * us_ref[...]
    g = 0.5 * y * (1.0 + jnp.tanh(math.sqrt(2.0 / math.pi) * (y + 0.044715 * (y * y * y))))
    gate = _sigmoid(_dot(g.astype(BF16), gw_ref[...]) + gbias_ref[...])
    _group_norm_store(o_ref, 3, g * gate, mw_ref)


def _mix_outputs(ya, yb, p, ys5, lw, layer, geo, n_tiles):
    n_tok = ya.shape[0]
    tm = 256
    gw = GROUP_WIDTH
    halo = 8
    n_halo_blocks = n_tok // halo
    tile = lambda col: pl.BlockSpec((tm, gw), lambda i: (i, col // gw))
    prev = lambda col: pl.BlockSpec((halo, gw), lambda i: (jnp.maximum(i * (tm // halo) - 1, 0), col // gw))
    nxt = lambda col: pl.BlockSpec(
        (halo, gw), lambda i: (jnp.minimum((i + 1) * (tm // halo), n_halo_blocks - 1), col // gw))
    full = lambda a: _layer_full_spec(a, layer)
    c_gb, c_gc, c_u = COL_CONV, COL_CONV + gw, COL_CONV + 2 * gw
    weights = [lw["conv_w"], lw["conv_b"]]
    glu = [lw["s5_d"], lw["s5_glu_w"], lw["s5_glu_b"], lw["mix_norm_w"]]
    return pl.pallas_call(
        functools.partial(_mix_kernel, tm=tm, lat_rows=geo["b"] * geo["t"], t=geo["t"], ctx=geo["ctx"]),
        grid=(n_tiles * geo["tm"] // tm,),
        in_specs=[tile(0), tile(0), tile(c_gb), tile(c_gc), tile(c_u),
                  prev(c_gc), prev(c_u), nxt(c_gc), nxt(c_u)]
        + [full(a) for a in weights] + [tile(0), tile(COL_S5)] + [full(a) for a in glu],
        out_specs=pl.BlockSpec((tm, 4 * gw), lambda i: (i, 0)),
        out_shape=jax.ShapeDtypeStruct((n_tiles * geo["tm"], 4 * gw), BF16),
        compiler_params=_cparams(("arbitrary",), 48 * 1024 * 1024),
        name="mix_outputs",
    )(ya, yb, p, p, p, p, p, p, p, *weights, ys5, p, *glu)


def _pad_last(a, width):
    return jnp.pad(a, [(0, 0)] * (a.ndim - 1) + [(0, width - a.shape[-1])])


def _stacked_weights(pr):
    n_layers = pr["mla_w_uq"].shape[0]
    w_uq = pr["mla_w_uq"].reshape(n_layers, MLA_Q_RANK, N_HEADS, MLA_QK)
    w_uq = _pad_last(w_uq, MLA_HEAD_PAD).reshape(n_layers, MLA_Q_RANK, N_HEADS * MLA_HEAD_PAD)
    w_ukv = pr["mla_w_ukv"].reshape(n_layers, MLA_KV_RANK, N_HEADS, MLA_NOPE + MLA_V)
    w_ukv = jnp.concatenate([w_ukv[..., :MLA_NOPE].reshape(n_layers, MLA_KV_RANK, -1),
                             w_ukv[..., MLA_NOPE:].reshape(n_layers, MLA_KV_RANK, -1)], axis=2)
    w_ukv = jnp.pad(w_ukv, ((0, 0), (0, MLA_KV_RANK_PAD - MLA_KV_RANK), (0, 0)))
    row = lambda a: a[:, None, :]
    return {
        "na_q_norm_w": row(pr["na_q_norm_w"]),
        "na_k_norm_w": row(pr["na_k_norm_w"]),
        "mla_cq_norm_w": row(pr["mla_cq_norm_w"]),
        "mla_ckv_norm_w": row(_pad_last(pr["mla_ckv_norm_w"], MLA_KV_RANK_PAD)),
        "mla_w_uq": w_uq.astype(BF16),
        "mla_w_ukv": w_ukv.astype(BF16),
        "mla_q_norm_w": row(_pad_last(pr["mla_q_norm_w"], MLA_HEAD_PAD)),
        "mla_k_norm_w": row(_pad_last(pr["mla_k_norm_w"], MLA_HEAD_PAD)),
        "conv_w": pr["conv_w"],
        "conv_b": row(pr["conv_b"]),
        "s5_d": row(pr["s5_d"]),
        "s5_glu_w": pr["s5_glu_w"].astype(BF16),
        "s5_glu_b": row(pr["s5_glu_b"]),
        "mix_norm_w": row(pr["mix_norm_w"]),
    }


def _layer_full_spec(a, layer):
    return pl.BlockSpec((None,) + a.shape[1:], lambda i: (layer,) + (0,) * (a.ndim - 1))


def _rope_tables(b, t, n_ctx):
    pos = jnp.arange(t, dtype=jnp.int32)
    row = (pos // GRID_W).astype(F32)
    col = (pos % GRID_W).astype(F32)
    n_freq = MLA_ROPE // 4
    inv_freq = ROPE_THETA ** (-jnp.arange(n_freq, dtype=F32) / n_freq)
    ang_r, ang_c = row[:, None] * inv_freq[None, :], col[:, None] * inv_freq[None, :]
    cos_r, sin_r, cos_c, sin_c = jnp.cos(ang_r), jnp.sin(ang_r), jnp.cos(ang_c), jnp.sin(ang_c)
    zeros = jnp.zeros((t, n_freq), F32)
    rest = LANE - MLA_ROPE
    cos_t = jnp.concatenate([cos_r, cos_r, cos_c, cos_c, jnp.ones((t, rest), F32)], axis=1)
    sina_t = jnp.concatenate([-sin_r, zeros, -sin_c, zeros, jnp.zeros((t, rest), F32)], axis=1)
    sinb_t = jnp.concatenate([zeros, sin_r, zeros, sin_c, jnp.zeros((t, rest), F32)], axis=1)
    n_c = b * n_ctx
    tables = []
    for tab, fill in ((cos_t, 1.0), (sina_t, 0.0), (sinb_t, 0.0)):
        tables.append(jnp.concatenate([jnp.tile(tab, (b, 1)), jnp.full((n_c, LANE), fill, F32)], axis=0))
    return tables


def _na_bias_tables(rpb):
    w = GRID_W
    qc = jnp.arange(w)
    cs = jnp.clip(qc - NA_WIN_COLS // 2, 0, w - NA_WIN_COLS)
    v_col = (qc[None, :] >= cs[:, None]) & (qc[None, :] < cs[:, None] + NA_WIN_COLS)
    d_col = jnp.clip(qc[None, :] - qc[:, None], -(NA_WIN_COLS - 1), NA_WIN_COLS - 1) + (NA_WIN_COLS - 1)
    sel_c = jax.nn.one_hot(d_col, 2 * NA_WIN_COLS - 1, dtype=F32)
    tab = jnp.einsum("hab,qkb->haqk", rpb * LOG2E, sel_c, precision=lax.Precision.HIGHEST)
    tab = jnp.where(v_col[None, None], tab, NEG_INF)
    tab = jnp.concatenate([tab, jnp.full((rpb.shape[0], 1, w, w), NEG_INF, F32)], axis=1)
    return jnp.concatenate([tab, tab], axis=-1)


def kernel(x, c, ctx, c_ctx, ada_w, ada_b, norm1_w, norm2_w, w_in, na_q_norm_w, na_k_norm_w, na_rpb, mla_cq_norm_w, mla_ckv_norm_w, mla_w_uq, mla_w_ukv, mla_q_norm_w, mla_k_norm_w, conv_w, conv_b, s5_lambda_re, s5_lambda_im, s5_log_dt, s5_b_re, s5_b_im, s5_c_re, s5_c_im, s5_d, s5_glu_w, s5_glu_b, mix_norm_w, w_out, ffn_w1, ffn_w3, ffn_w2):
    b, t, d = x.shape
    n_ctx = ctx.shape[1]
    n_layers = ada_w.shape[0]
    n_rows = t // GRID_W
    assert t % GRID_W == 0 and n_rows >= NA_K_ROWS and n_rows % NA_Q_ROWS == 0
    assert n_ctx == NA_Q_ROWS * GRID_W and t % n_ctx == 0 and b + 1 <= 8
    tm = 512 if (b * n_ctx) % 512 == 0 and t % 512 == 0 else 256
    tiles_per_batch = t // tm
    geo = {"b": b, "t": t, "ctx": n_ctx, "tm": tm,
           "tile_class": lambda i: jnp.minimum(i // tiles_per_batch, b)}
    lat_tiles = b * t // tm
    all_tiles = lat_tiles + b * n_ctx // tm

    pr = dict(na_q_norm_w=na_q_norm_w, na_k_norm_w=na_k_norm_w, mla_cq_norm_w=mla_cq_norm_w,
              mla_ckv_norm_w=mla_ckv_norm_w, mla_w_uq=mla_w_uq, mla_w_ukv=mla_w_ukv,
              mla_q_norm_w=mla_q_norm_w, mla_k_norm_w=mla_k_norm_w, conv_w=conv_w, conv_b=conv_b,
              s5_d=s5_d, s5_glu_w=s5_glu_w, s5_glu_b=s5_glu_b, mix_norm_w=mix_norm_w)

    h = (x.reshape(b * t, d), ctx.reshape(b * n_ctx, d))
    cvec = jnp.zeros((8, d), F32).at[:b].set(c).at[b].set(c_ctx)
    mod_all = _ada_mod(cvec, ada_w, ada_b)
    rope = _rope_tables(b, t, n_ctx)
    lw = _stacked_weights(pr)
    na_bias = jax.vmap(_na_bias_tables)(na_rpb)
    s5_ops = jax.vmap(_s5_operators)(s5_lambda_re, s5_lambda_im, s5_log_dt, s5_b_re, s5_b_im, s5_c_re, s5_c_im)
    m_in = 1088 if (b * t + b * n_ctx) % 1088 == 0 else tm

    for l in range(n_layers):
        last = l == n_layers - 1
        n_tiles = lat_tiles if last else all_tiles
        mod = mod_all[l]

        a = _norm_modulate(h, norm1_w[l], mod, 0, 1, geo, all_tiles)
        p, w2_bf16 = _input_projection(a, w_in, ffn_w2, l, m_in)
        naq, nak, nav, mq, mk, mv = _prep_qkv(p, lw, l, rope, 256)
        ya = _na_attention(naq, nak, nav, na_bias, l, geo, not last)
        yb = _mla_attention(mq, mk, mv, geo, not last)
        ys5 = _s5_mixer(p, s5_ops, l, geo)
        y = _mix_outputs(ya, yb, p, ys5, lw, l, geo, n_tiles)
        h = _matmul_gated_residual(y, w_out, h, mod, 2, geo, n_tiles, 512, layer=l)

        f = _norm_modulate(h, norm2_w[l], mod, 3, 4, geo, n_tiles)
        g = _ffn_in(f, ffn_w1, ffn_w3, l, geo, n_tiles, 512)
        h = _matmul_gated_residual(g, w2_bf16, h, mod, 5, geo, n_tiles, 512)

    return h.reshape(b, t, d)
```

```python
import functools
import math

import jax
import jax.numpy as jnp
from jax import lax
from jax.experimental import pallas as pl
from jax.experimental.pallas import tpu as pltpu

F32 = jnp.float32
BF16 = jnp.bfloat16

NORM_EPS = 1e-6
NEG_INF = -1e30
GRID_W = 64

GROUP_WIDTH = 1024
N_HEADS = 8
NA_HEAD_DIM = 128
NA_WIN_ROWS = 8
NA_WIN_COLS = 16
NA_Q_ROWS = 4
NA_K_ROWS = NA_Q_ROWS + NA_WIN_ROWS

MLA_NOPE = 128
MLA_ROPE = 64
MLA_QK = MLA_NOPE + MLA_ROPE
MLA_V = 128
MLA_Q_RANK = 896
MLA_KV_RANK = 320
MLA_KV_RANK_PAD = 512
MLA_HEAD_PAD = 256
V_HEAD_PAD = 256
LOG2E = math.log2(math.e)
ROPE_THETA = 10000.0

CONV_K = 3

S5_GROUPS = 64
S5_GROUP = 16
S5_STATE = 64
S5_CHUNK = 16
S5_BLOCK = S5_CHUNK * S5_GROUP


COL_NA = 0
COL_CONV = 3072
COL_S5 = 6144
COL_CQ = 7168
COL_CKV = 8192
IN_WIDTH_PAD = 8704
IN_TILE = 512
SRC_CQ = 3072
SRC_CKV = SRC_CQ + MLA_Q_RANK
SRC_CONV = SRC_CKV + MLA_KV_RANK + MLA_ROPE
SRC_S5 = SRC_CONV + 3 * GROUP_WIDTH

LANE = 128
VMEM_LIMIT = 56 * 1024 * 1024


def _cparams(sem, vmem=None):
    return pltpu.CompilerParams(dimension_semantics=sem, vmem_limit_bytes=vmem)


def _sigmoid(x):
    return 1.0 / (1.0 + jnp.exp(-x))


def _rms(x, w):
    return x * lax.rsqrt(jnp.mean(x * x, axis=-1, keepdims=True) + NORM_EPS) * w


def _dot(a, b):
    return jnp.dot(a, b, preferred_element_type=F32)


def _dot_nt(a, b):
    return lax.dot_general(a, b, (((1,), (1,)), ((), ())), preferred_element_type=F32)


def _ada_kernel(c_ref, w_ref, b_ref, o_ref):
    c = c_ref[...]
    s = (c * _sigmoid(c)).astype(BF16)
    o_ref[...] = _dot(s, w_ref[...].astype(BF16)) + b_ref[...]


def _ada_rows(out, d):
    return out.reshape(8, 6, d).transpose(1, 0, 2)[:, :, None, :]


def _ada_mod(cvec, ada_w, ada_b, layer):
    n_layers, d, n = ada_w.shape
    tn = 512
    out = pl.pallas_call(
        _ada_kernel,
        grid=(n // tn,),
        in_specs=[pl.BlockSpec((8, d), lambda j: (0, 0)),
                  pl.BlockSpec((None, d, tn), lambda j: (layer, 0, j)),
                  pl.BlockSpec((None, 1, tn), lambda j: (layer, 0, j))],
        out_specs=pl.BlockSpec((8, tn), lambda j: (0, j)),
        out_shape=jax.ShapeDtypeStruct((8, n), F32),
        compiler_params=_cparams(("arbitrary",), 40 * 1024 * 1024),
        name="ada_mod",
    )(cvec, ada_w, ada_b.reshape(n_layers, 1, n))
    return _ada_rows(out, d)


def _stream_parts(h):
    return tuple(h) if isinstance(h, (tuple, list)) else (h,)


def _stream_specs(parts, tm, width, row_of, col_of):
    if len(parts) == 1:
        return [pl.BlockSpec((tm, width), lambda *g: (row_of(*g), col_of(*g)))]
    lat_tiles = parts[0].shape[0] // tm
    return [pl.BlockSpec((tm, width), lambda *g: (jnp.minimum(row_of(*g), lat_tiles - 1), col_of(*g))),
            pl.BlockSpec((tm, width), lambda *g: (jnp.maximum(row_of(*g) - lat_tiles, 0), col_of(*g)))]


def _read_stream(h_refs, row_tile, lat_tiles):
    if len(h_refs) == 1:
        return h_refs[0][...]
    return jnp.where(row_tile < lat_tiles, h_refs[0][...], h_refs[1][...])


def _normmod_kernel(*refs, n_h, lat_tiles):
    h_refs = refs[:n_h]
    w_ref, shift_ref, scale_ref, o_ref = refs[n_h:]
    y = _rms(_read_stream(h_refs, pl.program_id(0), lat_tiles), w_ref[...])
    o_ref[...] = (y * (1.0 + scale_ref[...]) + shift_ref[...]).astype(o_ref.dtype)


def _norm_modulate(h, w, mod, k_shift, k_scale, geo, n_tiles):
    parts = _stream_parts(h)
    d = parts[0].shape[1]
    tm = geo["tm"]
    cls = geo["tile_class"]
    return pl.pallas_call(
        functools.partial(_normmod_kernel, n_h=len(parts), lat_tiles=parts[0].shape[0] // tm),
        grid=(n_tiles,),
        in_specs=_stream_specs(parts, tm, d, lambda i: i, lambda i: 0)
        + [pl.BlockSpec((1, d), lambda i: (0, 0)),
           pl.BlockSpec((None, None, 1, d), lambda i: (k_shift, cls(i), 0, 0)),
           pl.BlockSpec((None, None, 1, d), lambda i: (k_scale, cls(i), 0, 0))],
        out_specs=pl.BlockSpec((tm, d), lambda i: (i, 0)),
        out_shape=jax.ShapeDtypeStruct((n_tiles * tm, d), BF16),
        compiler_params=_cparams(("arbitrary",), VMEM_LIMIT),
        name="norm_modulate",
    )(*parts, w.reshape(1, d), mod, mod)


def _in_proj_kernel(a_ref, w_ref, w2_ref, o_ref, w2b_ref, wb_ref, *, n_row_tiles, n_w2_blocks):
    @pl.when(pl.program_id(1) == 0)
    def _():
        wb_ref[...] = w_ref[...].astype(BF16)

    o_ref[...] = _dot(a_ref[...], wb_ref[...])

    step = pl.program_id(0) * n_row_tiles + pl.program_id(1)

    @pl.when(step < n_w2_blocks)
    def _():
        w2b_ref[...] = w2_ref[...].astype(BF16)


def _in_proj_source_col(j):
    t = IN_TILE // LANE
    unit = jnp.where(j < 6, j * t,
           jnp.where(j < 12, SRC_CONV // LANE + (j - 6) * t,
           jnp.where(j < 14, SRC_S5 // LANE + (j - 12) * t,
           jnp.where(j < 16, SRC_CQ // LANE + (j - 14) * t, SRC_CKV // LANE))))
    return unit * LANE


def _input_projection(a, w_in, w2, layer, tm):
    m, k = a.shape
    tn = IN_TILE
    assert m % tm == 0 and COL_CKV + tn == IN_WIDTH_PAD and SRC_CKV + tn <= w_in.shape[-1]
    n_col_tiles, n_row_tiles = IN_WIDTH_PAD // tn, m // tm
    hidden, d_out = w2.shape[1:]
    slab = next(r for r in range(16, hidden + 1, 16) if hidden % r == 0 and hidden // r <= n_col_tiles * n_row_tiles)
    n_w2_blocks = hidden // slab
    w2_block = lambda j, i: jnp.minimum(j * n_row_tiles + i, n_w2_blocks - 1)
    return pl.pallas_call(
        functools.partial(_in_proj_kernel, n_row_tiles=n_row_tiles, n_w2_blocks=n_w2_blocks),
        grid=(n_col_tiles, n_row_tiles),
        in_specs=[pl.BlockSpec((tm, k), lambda j, i: (i, 0)),
                  pl.BlockSpec((pl.Element(k), pl.Element(tn)),
                               lambda j, i: (layer * k, _in_proj_source_col(j))),
                  pl.BlockSpec((None, slab, d_out), lambda j, i: (layer, w2_block(j, i), 0))],
        out_specs=[pl.BlockSpec((tm, tn), lambda j, i: (i, j)),
                   pl.BlockSpec((slab, d_out), lambda j, i: (w2_block(j, i), 0))],
        out_shape=[jax.ShapeDtypeStruct((m, IN_WIDTH_PAD), F32), jax.ShapeDtypeStruct((hidden, d_out), BF16)],
        scratch_shapes=[pltpu.VMEM((k, tn), BF16)],
        compiler_params=_cparams(("arbitrary", "arbitrary"), VMEM_LIMIT),
        name="input_projection",
    )(a, w_in.reshape(-1, w_in.shape[-1]), w2)


def _mm_res_kernel(a_ref, w_ref, *refs, n_h, lat_tiles, f32w):
    h_refs = refs[:n_h]
    g_ref, o_ref = refs[n_h], refs[n_h + 1]
    if f32w:
        wb_ref = refs[n_h + 2]

        @pl.when(pl.program_id(1) == 0)
        def _():
            wb_ref[...] = w_ref[...].astype(BF16)

        w = wb_ref[...]
    else:
        w = w_ref[...]
    h = _read_stream(h_refs, pl.program_id(1), lat_tiles)
    o_ref[...] = h + g_ref[...] * _dot(a_ref[...], w)


def _layer_weight_spec(w, layer, tn, single_buffer=False):
    mode = pl.Buffered(1) if single_buffer else None
    if w.ndim == 3:
        return pl.BlockSpec((None, w.shape[1], tn), lambda j, i: (layer, 0, j), pipeline_mode=mode)
    return pl.BlockSpec((w.shape[0], tn), lambda j, i: (0, j), pipeline_mode=mode)


def _matmul_gated_residual(a, w, h, mod, k_gate, geo, n_tiles, tn, layer=None, single_buffer_w=False):
    m, k = a.shape
    n = w.shape[-1]
    tm = geo["tm"]
    cls = geo["tile_class"]
    f32w = w.dtype == F32
    parts = _stream_parts(h)
    return pl.pallas_call(
        functools.partial(_mm_res_kernel, n_h=len(parts), lat_tiles=parts[0].shape[0] // tm, f32w=f32w),
        grid=(n // tn, n_tiles),
        in_specs=[pl.BlockSpec((tm, k), lambda j, i: (i, 0)),
                  _layer_weight_spec(w, layer, tn, single_buffer_w)]
        + _stream_specs(parts, tm, tn, lambda j, i: i, lambda j, i: j)
        + [pl.BlockSpec((None, None, 1, tn), lambda j, i: (k_gate, cls(i), 0, j))],
        out_specs=pl.BlockSpec((tm, tn), lambda j, i: (i, j)),
        out_shape=jax.ShapeDtypeStruct((n_tiles * tm, n), F32),
        scratch_shapes=[pltpu.VMEM((k, tn), BF16)] if f32w else [],
        compiler_params=_cparams(("arbitrary", "arbitrary"), VMEM_LIMIT),
        name="matmul_gated_residual",
    )(a, w, *parts, mod)


def _ffn1_kernel(a_ref, w1_ref, w3_ref, o_ref, w1b_ref, w3b_ref):
    @pl.when(pl.program_id(1) == 0)
    def _():
        w1b_ref[...] = w1_ref[...].astype(BF16)
        w3b_ref[...] = w3_ref[...].astype(BF16)

    a = a_ref[...]
    u = _dot(a, w1b_ref[...])
    v = _dot(a, w3b_ref[...])
    o_ref[...] = (u * _sigmoid(u) * v).astype(o_ref.dtype)


def _ffn_in(a, w1, w3, layer, geo, n_tiles, tn):
    m, k = a.shape
    n = w1.shape[-1]
    tm = geo["tm"]
    return pl.pallas_call(
        _ffn1_kernel,
        grid=(pl.cdiv(n, tn), n_tiles),
        in_specs=[pl.BlockSpec((tm, k), lambda j, i: (i, 0)),
                  _layer_weight_spec(w1, layer, tn),
                  _layer_weight_spec(w3, layer, tn)],
        out_specs=pl.BlockSpec((tm, tn), lambda j, i: (i, j)),
        out_shape=jax.ShapeDtypeStruct((m, n), BF16),
        scratch_shapes=[pltpu.VMEM((k, tn), BF16), pltpu.VMEM((k, tn), BF16)],
        compiler_params=_cparams(("arbitrary", "arbitrary"), VMEM_LIMIT),
        name="ffn_in",
    )(a, w1, w3)


def _rope_tail(t, cos_ref, sina_ref, sinb_ref):
    q = MLA_ROPE // 4
    return (t * cos_ref[...] + pltpu.roll(t, LANE - q, 1) * sina_ref[...]
            + pltpu.roll(t, q, 1) * sinb_ref[...])


def _prep_kernel(na_ref, cq_ref, ckvkr_ref, naqw_ref, nakw_ref, cqw_ref, ckvw_ref,
                 wuq_ref, wukv_ref, mqw_ref, mkw_ref, cos_ref, sina_ref, sinb_ref,
                 naq_ref, nak_ref, nav_ref, mq_ref, mk_ref, mv_ref):
    hd = NA_HEAD_DIM
    tm = na_ref.shape[0]
    ones_col = jnp.where(lax.broadcasted_iota(jnp.int32, (tm, LANE), 1) == 0, 1.0, 0.0).astype(BF16)
    na_qs = naqw_ref[...] * (NA_HEAD_DIM ** -0.5 * LOG2E)
    for h in range(N_HEADS):
        naq_ref[:, h * hd:(h + 1) * hd] = _rms(na_ref[:, h * hd:(h + 1) * hd], na_qs).astype(BF16)
        nak_ref[:, h * hd:(h + 1) * hd] = _rms(
            na_ref[:, GROUP_WIDTH + h * hd:GROUP_WIDTH + (h + 1) * hd], nakw_ref[...]).astype(BF16)
        vo = h * V_HEAD_PAD
        nav_ref[:, vo:vo + hd] = na_ref[:, 2 * GROUP_WIDTH + h * hd:2 * GROUP_WIDTH + (h + 1) * hd].astype(BF16)
        nav_ref[:, vo + hd:vo + V_HEAD_PAD] = ones_col

    cq = _rms(cq_ref[...], cqw_ref[...]).astype(BF16)
    q = _dot(cq, wuq_ref[...])
    inv_qk = 1.0 / MLA_QK
    mqw = mqw_ref[...] * (MLA_QK ** -0.5 * LOG2E)
    for h in range(N_HEADS):
        o = h * MLA_HEAD_PAD
        nope = q[:, o:o + MLA_NOPE]
        tail = q[:, o + MLA_NOPE:o + MLA_HEAD_PAD]
        ss = jnp.sum(nope * nope, axis=-1, keepdims=True) + jnp.sum(tail * tail, axis=-1, keepdims=True)
        r = lax.rsqrt(ss * inv_qk + NORM_EPS)
        mq_ref[:, o:o + MLA_NOPE] = (nope * r * mqw[:, :MLA_NOPE]).astype(BF16)
        mq_ref[:, o + MLA_NOPE:o + MLA_HEAD_PAD] = _rope_tail(
            tail * r * mqw[:, MLA_NOPE:], cos_ref, sina_ref, sinb_ref).astype(BF16)

    blk = ckvkr_ref[...]
    lane = lax.broadcasted_iota(jnp.int32, blk.shape, 1)
    ckv = jnp.where(lane < MLA_KV_RANK, blk, 0.0)
    ckv_ms = jnp.sum(ckv * ckv, axis=-1, keepdims=True) * (1.0 / MLA_KV_RANK)
    ckv_n = (ckv * lax.rsqrt(ckv_ms + NORM_EPS) * ckvw_ref[...]).astype(BF16)
    kv = _dot(ckv_n, wukv_ref[...])
    for h in range(N_HEADS):
        vo = h * V_HEAD_PAD
        mv_ref[:, vo:vo + MLA_V] = kv[:, GROUP_WIDTH + h * MLA_V:GROUP_WIDTH + (h + 1) * MLA_V].astype(BF16)
        mv_ref[:, vo + MLA_V:vo + V_HEAD_PAD] = ones_col
    kr = pltpu.roll(blk[:, 2 * LANE:3 * LANE], LANE - MLA_ROPE, 1)
    kr = jnp.where(lax.broadcasted_iota(jnp.int32, kr.shape, 1) < MLA_ROPE, kr, 0.0)
    kr_ss = jnp.sum(kr * kr, axis=-1, keepdims=True)
    mkw = mkw_ref[...]
    for h in range(N_HEADS):
        o = h * MLA_HEAD_PAD
        kn = kv[:, h * MLA_NOPE:(h + 1) * MLA_NOPE]
        r = lax.rsqrt((jnp.sum(kn * kn, axis=-1, keepdims=True) + kr_ss) * inv_qk + NORM_EPS)
        mk_ref[:, o:o + MLA_NOPE] = (kn * r * mkw[:, :MLA_NOPE]).astype(BF16)
        mk_ref[:, o + MLA_NOPE:o + MLA_HEAD_PAD] = _rope_tail(
            kr * r * mkw[:, MLA_NOPE:], cos_ref, sina_ref, sinb_ref).astype(BF16)


def _prep_qkv(p, lw, layer, rope, tm):
    n_tok = p.shape[0]
    blk = lambda width, col: pl.BlockSpec((tm, width), lambda i: (i, col // width))
    full = lambda a: _layer_full_spec(a, layer)
    tab = pl.BlockSpec((tm, LANE), lambda i: (i, 0))
    out_w = [GROUP_WIDTH, GROUP_WIDTH, N_HEADS * V_HEAD_PAD,
             N_HEADS * MLA_HEAD_PAD, N_HEADS * MLA_HEAD_PAD, N_HEADS * V_HEAD_PAD]
    weights = [lw["na_q_norm_w"], lw["na_k_norm_w"], lw["mla_cq_norm_w"], lw["mla_ckv_norm_w"],
               lw["mla_w_uq"], lw["mla_w_ukv"], lw["mla_q_norm_w"], lw["mla_k_norm_w"]]
    return pl.pallas_call(
        _prep_kernel,
        grid=(n_tok // tm,),
        in_specs=[blk(3 * GROUP_WIDTH, COL_NA), blk(MLA_Q_RANK, COL_CQ), blk(MLA_KV_RANK_PAD, COL_CKV)]
        + [full(a) for a in weights] + [tab, tab, tab],
        out_specs=[pl.BlockSpec((tm, w), lambda i: (i, 0)) for w in out_w],
        out_shape=[jax.ShapeDtypeStruct((n_tok, w), BF16) for w in out_w],
        compiler_params=_cparams(("arbitrary",), 48 * 1024 * 1024),
        name="prep_qkv",
    )(p, p, p, *weights, *rope)


def _softmax_attend(q, pairs, bias=None):
    scores = []
    for idx, (k, _) in enumerate(pairs):
        s = _dot_nt(q, k)
        if idx == 0 and bias is not None:
            s = s + bias
        scores.append(s)
    m = scores[0].max(axis=-1, keepdims=True)
    for s in scores[1:]:
        m = jnp.maximum(m, s.max(axis=-1, keepdims=True))
    acc = None
    for s, (_, v) in zip(scores, pairs):
        o = _dot(jnp.exp2(s - m).astype(BF16), v)
        acc = o if acc is None else acc + o
    dv = acc.shape[-1] // 2
    return acc[:, :dv] / acc[:, dv:dv + 1]


NA_INVALID_ROW = 2 * NA_WIN_ROWS - 1


def _na_window_bias(bt_ref, rb, n_rows):
    half = NA_WIN_ROWS // 2
    ks_row = jnp.clip(rb * NA_Q_ROWS - half, 0, n_rows - NA_K_ROWS)
    lane_lo = lax.broadcasted_iota(jnp.int32, (GRID_W, LANE), 1) < GRID_W
    strips = []
    for i in range(NA_Q_ROWS):
        rq = rb * NA_Q_ROWS + i
        rs = jnp.clip(rq - half, 0, n_rows - NA_WIN_ROWS)

        def slab(j, rq=rq, rs=rs):
            rk = ks_row + j
            valid = (rk >= rs) & (rk < rs + NA_WIN_ROWS)
            return bt_ref[jnp.where(valid, rk - rq + (NA_WIN_ROWS - 1), NA_INVALID_ROW)]

        pieces = [jnp.where(lane_lo, slab(2 * m), slab(2 * m + 1)) for m in range(NA_K_ROWS // 2)]
        strips.append(jnp.concatenate(pieces, axis=-1))
    return jnp.concatenate(strips, axis=0)


def _na_kernel(q_ref, k_ref, v_ref, kc_ref, vc_ref, kca_ref, vca_ref, bt_ref, o_ref, *,
               n_lat_steps, steps_per_batch, n_rows, n_sub, ctx):
    i = pl.program_id(1)
    sq = NA_Q_ROWS * GRID_W

    @pl.when(i < n_lat_steps)
    def _():
        step = i % steps_per_batch
        for sb in range(n_sub):
            rb = step * n_sub + sb
            ks = jnp.clip(rb * NA_Q_ROWS - NA_WIN_ROWS // 2, 0, n_rows - NA_K_ROWS) * GRID_W
            ks = pl.multiple_of(ks, GRID_W)
            kw = k_ref[pl.ds(ks, NA_K_ROWS * GRID_W), :]
            vw = v_ref[pl.ds(ks, NA_K_ROWS * GRID_W), :]
            rows = slice(sb * sq, (sb + 1) * sq)
            o_ref[rows, :] = _softmax_attend(q_ref[rows, :], [(kw, vw), (kc_ref[...], vc_ref[...])],
                                             _na_window_bias(bt_ref, rb, n_rows))

    @pl.when(i == n_lat_steps)
    def _():
        for bb in range(q_ref.shape[0] // ctx):
            rows = slice(bb * ctx, (bb + 1) * ctx)
            o_ref[rows, :] = _softmax_attend(q_ref[rows, :], [(kca_ref[rows, :], vca_ref[rows, :])])


def _na_attention(naq, nak, nav, bias_tab, layer, geo, with_ctx):
    b, t, ctx = geo["b"], geo["t"], geo["ctx"]
    n_tok = naq.shape[0]
    sq = NA_Q_ROWS * GRID_W
    tq = b * ctx
    assert tq % sq == 0 and t % tq == 0
    n_sub = tq // sq
    n_rows = t // GRID_W
    spb = t // tq
    lat_steps = b * spb
    ctx_block0 = b * t // ctx
    batch = lambda i: jnp.minimum(i // spb, b - 1)

    return pl.pallas_call(
        functools.partial(_na_kernel, n_lat_steps=lat_steps, steps_per_batch=spb, n_rows=n_rows,
                          n_sub=n_sub, ctx=ctx),
        grid=(N_HEADS, lat_steps + (1 if with_ctx else 0)),
        in_specs=[pl.BlockSpec((tq, NA_HEAD_DIM), lambda h, i: (i, h)),
                  pl.BlockSpec((t, NA_HEAD_DIM), lambda h, i: (batch(i), h)),
                  pl.BlockSpec((t, V_HEAD_PAD), lambda h, i: (batch(i), h)),
                  pl.BlockSpec((ctx, NA_HEAD_DIM), lambda h, i: (ctx_block0 + batch(i), h)),
                  pl.BlockSpec((ctx, V_HEAD_PAD), lambda h, i: (ctx_block0 + batch(i), h)),
                  pl.BlockSpec((tq, NA_HEAD_DIM), lambda h, i: (lat_steps, h)),
                  pl.BlockSpec((tq, V_HEAD_PAD), lambda h, i: (lat_steps, h)),
                  pl.BlockSpec((None, None) + bias_tab.shape[2:], lambda h, i: (layer, h, 0, 0, 0))],
        out_specs=pl.BlockSpec((tq, NA_HEAD_DIM), lambda h, i: (i, h)),
        out_shape=jax.ShapeDtypeStruct((n_tok, GROUP_WIDTH), F32),
        compiler_params=_cparams(("arbitrary", "arbitrary"), 40 * 1024 * 1024),
        name="na_attention",
    )(naq, nak, nav, nak, nav, nak, nav, bias_tab)


def _flash_attend(q, loads):
    m = None
    acc = None
    for load in loads:
        k, v = load()
        s = _dot_nt(q, k)
        mc = s.max(axis=-1, keepdims=True)
        if m is None:
            m = mc
            acc = _dot(jnp.exp2(s - m).astype(BF16), v)
        else:
            m_new = jnp.maximum(m, mc)
            acc = acc * jnp.exp2(m - m_new) + _dot(jnp.exp2(s - m_new).astype(BF16), v)
            m = m_new
    dv = acc.shape[-1] // 2
    return acc[:, :dv] / acc[:, dv:dv + 1]


def _mla_kernel(q_ref, k_ref, v_ref, kc_ref, vc_ref, kca_ref, vca_ref, *refs,
                n_q_blocks, key_chunk, ctx, hps, n_ada_blocks):
    i = pl.program_id(1)
    dq, dv = MLA_HEAD_PAD, V_HEAD_PAD
    if n_ada_blocks:
        c_ref, aw_ref, ab_ref, o_ref, mod_ref = refs
        step = pl.program_id(0) * pl.num_programs(1) + i

        @pl.when(step < n_ada_blocks)
        def _():
            _ada_kernel(c_ref, aw_ref, ab_ref, mod_ref)
    else:
        (o_ref,) = refs

    @pl.when(i < n_q_blocks)
    def _():
        for hh in range(hps):
            qc, vc = slice(hh * dq, (hh + 1) * dq), slice(hh * dv, (hh + 1) * dv)

            def lat_chunk(j, qc=qc, vc=vc):
                rows = slice(j * key_chunk, (j + 1) * key_chunk)
                return lambda: (k_ref[rows, qc], v_ref[rows, vc])

            loads = [lat_chunk(j) for j in range(k_ref.shape[0] // key_chunk)]
            loads.append(lambda qc=qc, vc=vc: (kc_ref[:, qc], vc_ref[:, vc]))
            o_ref[:, hh * MLA_V:(hh + 1) * MLA_V] = _flash_attend(q_ref[:, qc], loads)

    @pl.when(i == n_q_blocks)
    def _():
        for hh in range(hps):
            qc, vc = slice(hh * dq, (hh + 1) * dq), slice(hh * dv, (hh + 1) * dv)
            for bb in range(q_ref.shape[0] // ctx):
                rows = slice(bb * ctx, (bb + 1) * ctx)
                o_ref[rows, hh * MLA_V:(hh + 1) * MLA_V] = _flash_attend(
                    q_ref[rows, qc], [lambda: (kca_ref[rows, qc], vca_ref[rows, vc])])


def _mla_attention(mq, mk, mv, geo, with_ctx, ada=None):
    b, t, ctx = geo["b"], geo["t"], geo["ctx"]
    n_tok = mq.shape[0]
    tq = b * ctx
    assert t % tq == 0
    nqb = t // tq
    lat_blocks = b * nqb
    ctx_block0 = b * t // ctx
    batch = lambda i: jnp.minimum(i // nqb, b - 1)
    hps = 2
    dq, dv = hps * MLA_HEAD_PAD, hps * V_HEAD_PAD
    grid = (N_HEADS // hps, lat_blocks + (1 if with_ctx else 0))

    in_specs = [pl.BlockSpec((tq, dq), lambda h, i: (i, h)),
                pl.BlockSpec((t, dq), lambda h, i: (batch(i), h)),
                pl.BlockSpec((t, dv), lambda h, i: (batch(i), h)),
                pl.BlockSpec((ctx, dq), lambda h, i: (ctx_block0 + batch(i), h)),
                pl.BlockSpec((ctx, dv), lambda h, i: (ctx_block0 + batch(i), h)),
                pl.BlockSpec((tq, dq), lambda h, i: (lat_blocks, h)),
                pl.BlockSpec((tq, dv), lambda h, i: (lat_blocks, h))]
    operands = [mq, mk, mv, mk, mv, mk, mv]
    out_specs = [pl.BlockSpec((tq, hps * MLA_V), lambda h, i: (i, h))]
    out_shape = [jax.ShapeDtypeStruct((n_tok, GROUP_WIDTH), F32)]
    n_ada_blocks = 0
    if ada is not None:
        cvec, ada_w, ada_b, layer = ada
        n_layers, d, n = ada_w.shape
        n_steps = grid[0] * grid[1]
        n_ada_blocks = max(k for k in range(1, n_steps + 1) if n % (k * LANE) == 0)
        tn = n // n_ada_blocks
        slab = lambda h, i: jnp.minimum(h * grid[1] + i, n_ada_blocks - 1)
        in_specs += [pl.BlockSpec((8, d), lambda h, i: (0, 0)),
                     pl.BlockSpec((None, d, tn), lambda h, i: (layer, 0, slab(h, i))),
                     pl.BlockSpec((None, 1, tn), lambda h, i: (layer, 0, slab(h, i)))]
        operands += [cvec, ada_w, ada_b.reshape(n_layers, 1, n)]
        out_specs.append(pl.BlockSpec((8, tn), lambda h, i: (0, slab(h, i))))
        out_shape.append(jax.ShapeDtypeStruct((8, n), F32))

    outs = pl.pallas_call(
        functools.partial(_mla_kernel, n_q_blocks=lat_blocks, key_chunk=512, ctx=ctx, hps=hps,
                          n_ada_blocks=n_ada_blocks),
        grid=grid,
        in_specs=in_specs,
        out_specs=out_specs,
        out_shape=out_shape,
        compiler_params=_cparams(("arbitrary", "arbitrary"), 48 * 1024 * 1024),
        name="mla_attention",
    )(*operands)
    if ada is None:
        return outs[0], None
    return outs[0], _ada_rows(outs[1], ada[1].shape[1])


S5_OCTET = LANE // S5_GROUP


def _s5_gather_chunks(p_ref, u_ref):
    rows = u_ref.shape[0]
    for s in range(S5_CHUNK):
        xs = p_ref[pl.ds(s, rows, stride=S5_CHUNK), :]
        for k in range(S5_OCTET):
            dst = k * S5_BLOCK + s * S5_GROUP
            u_ref[:, dst:dst + S5_GROUP] = xs[:, k * S5_GROUP:(k + 1) * S5_GROUP]


def _s5_contrib_kernel(p_ref, b_ref, o_ref, ub_ref, u_ref):
    _s5_gather_chunks(p_ref, u_ref)
    ub_ref[...] = u_ref[...].astype(BF16)
    r = [_dot(ub_ref[:, k * S5_BLOCK:(k + 1) * S5_BLOCK], b_ref[k]) for k in range(S5_OCTET)]
    ps = S5_STATE
    for plane in range(4):
        o_ref[plane] = jnp.concatenate([rk[:, plane * ps:(plane + 1) * ps] for rk in r], axis=-1)


def _s5_scan_kernel(c_ref, lr_ref, li_ref, x_ref, *, n_tiles, n_ctx_tiles, nb):
    sub = 8
    n_lat_tiles = n_tiles - n_ctx_tiles
    lanes = c_ref.shape[-1]
    lr = jnp.broadcast_to(lr_ref[...], (sub, lanes))
    li = jnp.broadcast_to(li_ref[...], (sub, lanes))
    row = lax.broadcasted_iota(jnp.int32, (sub, lanes), 0)

    def run(reverse):
        order = range(sub - 1, -1, -1) if reverse else range(sub)

        def body(jt, carry):
            if reverse:
                tile = jnp.where(jt < n_ctx_tiles, n_ctx_tiles - 1 - jt, n_tiles - 1 - (jt - n_ctx_tiles))
            else:
                tile = jt
            new = []
            for bi in range(nb):
                xr, xi = carry[2 * bi], carry[2 * bi + 1]
                row_tile = jnp.where(tile < n_ctx_tiles, nb * n_lat_tiles + bi * n_ctx_tiles + tile,
                                     bi * n_lat_tiles + tile - n_ctx_tiles)
                r0 = pl.multiple_of(row_tile * sub, sub)
                cr = c_ref[0, pl.ds(r0, sub), :]
                ci = c_ref[1, pl.ds(r0, sub), :]
                out_r = jnp.zeros_like(cr)
                out_i = jnp.zeros_like(ci)
                for k in order:
                    out_r = jnp.where(row == k, xr, out_r)
                    out_i = jnp.where(row == k, xi, out_i)
                    ck_r = jnp.broadcast_to(cr[k:k + 1, :], (sub, lanes))
                    ck_i = jnp.broadcast_to(ci[k:k + 1, :], (sub, lanes))
                    xr, xi = lr * xr - li * xi + ck_r, lr * xi + li * xr + ck_i
                x_ref[0, pl.ds(r0, sub), :] = out_r
                x_ref[1, pl.ds(r0, sub), :] = out_i
                new += [xr, xi]
            return tuple(new)

        zero = jnp.zeros((sub, lanes), F32)
        lax.fori_loop(0, n_tiles, body, (zero,) * (2 * nb))

    @pl.when(pl.program_id(0) == 0)
    def _():
        run(False)

    @pl.when(pl.program_id(0) == 1)
    def _():
        run(True)


def _s5_out_kernel(u_ref, x_ref, t_ref, ct_ref, o_ref, y_ref):
    rows = u_ref.shape[0]
    ps = S5_STATE
    for k in range(S5_OCTET):
        x = jnp.concatenate([x_ref[plane][:, k * ps:(k + 1) * ps] for plane in range(4)], axis=-1)
        cols = slice(k * S5_BLOCK, (k + 1) * S5_BLOCK)
        y_ref[:, cols] = _dot(u_ref[:, cols], t_ref[k]) + _dot_nt(x.astype(BF16), ct_ref[k])
    for t in range(S5_CHUNK):
        piece = jnp.concatenate([y_ref[:, k * S5_BLOCK + t * S5_GROUP:k * S5_BLOCK + (t + 1) * S5_GROUP]
                                 for k in range(S5_OCTET)], axis=-1)
        o_ref[pl.ds(t, rows, stride=S5_CHUNK), :] = piece


def _s5_mixer(p, ops, layer, geo):
    b, t, ctx = geo["b"], geo["t"], geo["ctx"]
    n_chunks = (ctx + t) // S5_CHUNK
    rows = n_chunks * b
    n_tok = p.shape[0]
    assert rows * S5_CHUNK == n_tok
    n_oct = S5_GROUPS // S5_OCTET
    state_w = S5_GROUPS * S5_STATE
    oct_w = S5_OCTET * S5_BLOCK
    oct_states = S5_OCTET * S5_STATE
    sub = 8
    assert (t // S5_CHUNK) % sub == 0 and (ctx // S5_CHUNK) % sub == 0
    p_spec = pl.BlockSpec((n_tok, LANE), lambda o: (0, COL_S5 // LANE + o))
    op_spec = pl.BlockSpec((None, S5_OCTET, S5_BLOCK, S5_BLOCK), lambda o: (layer, o, 0, 0))
    plane_spec = pl.BlockSpec((4, rows, oct_states), lambda o: (0, 0, o))

    u_spec = pl.BlockSpec((rows, oct_w), lambda o: (0, o))
    contrib, u_chunks = pl.pallas_call(
        _s5_contrib_kernel,
        grid=(n_oct,),
        in_specs=[p_spec, op_spec],
        out_specs=[plane_spec, u_spec],
        out_shape=[jax.ShapeDtypeStruct((4, rows, state_w), F32),
                   jax.ShapeDtypeStruct((rows, S5_GROUPS * S5_BLOCK), BF16)],
        scratch_shapes=[pltpu.VMEM((rows, oct_w), F32)],
        compiler_params=_cparams(("arbitrary",), 48 * 1024 * 1024),
        name="s5_contrib",
    )(p, ops["b_mat"])

    lane_blk = 512
    states = pl.pallas_call(
        functools.partial(_s5_scan_kernel, n_tiles=n_chunks // sub, n_ctx_tiles=ctx // S5_CHUNK // sub, nb=b),
        grid=(2, state_w // lane_blk),
        in_specs=[pl.BlockSpec((None, 2, rows, lane_blk), lambda d, l: (d, 0, 0, l)),
                  pl.BlockSpec((None, None, 1, lane_blk), lambda d, l: (layer, d, 0, l)),
                  pl.BlockSpec((None, None, 1, lane_blk), lambda d, l: (layer, d, 0, l))],
        out_specs=pl.BlockSpec((None, 2, rows, lane_blk), lambda d, l: (d, 0, 0, l)),
        out_shape=jax.ShapeDtypeStruct((2, 2, rows, state_w), F32),
        compiler_params=_cparams(("arbitrary", "arbitrary"), 40 * 1024 * 1024),
        name="s5_scan",
    )(contrib.reshape(2, 2, rows, state_w), ops["decay_re"], ops["decay_im"])

    return pl.pallas_call(
        _s5_out_kernel,
        grid=(n_oct,),
        in_specs=[u_spec, plane_spec, op_spec, op_spec],
        out_specs=pl.BlockSpec((n_tok, LANE), lambda o: (0, o)),
        out_shape=jax.ShapeDtypeStruct((n_tok, GROUP_WIDTH), F32),
        scratch_shapes=[pltpu.VMEM((rows, oct_w), F32)],
        compiler_params=_cparams(("arbitrary",), 48 * 1024 * 1024),
        name="s5_out",
    )(u_chunks, states.reshape(4, rows, state_w), ops["t_sum"], ops["c_mat_t"])


def _s5_operators(lam_re, lam_im, log_dt, b_re, b_im, c_re, c_im):
    hp = lax.Precision.HIGHEST
    g, pn, ni, lc = S5_GROUPS, S5_STATE, S5_GROUP, S5_CHUNK
    dt = jnp.exp(log_dt)[..., None]
    zr, zi = lam_re * dt, lam_im * dt
    up = jnp.arange(lc, dtype=F32)
    down = (lc - 1) - up

    def powers(d, steps):
        mag = jnp.exp(zr[d][None] * steps[:, None, None])
        ang = zi[d][None] * steps[:, None, None]
        return mag * jnp.cos(ang), mag * jnp.sin(ang)

    one = jnp.ones((1,), F32)
    z1 = [powers(d, one) for d in range(2)]
    nr = jnp.stack([z1[0][0][0], z1[1][0][0]]) - 1.0
    nim = jnp.stack([z1[0][1][0], z1[1][1][0]])
    den = lam_re * lam_re + lam_im * lam_im
    cr_, ci_ = (nr * lam_re + nim * lam_im) / den, (nim * lam_re - nr * lam_im) / den
    bz_r = cr_[..., None] * b_re - ci_[..., None] * b_im
    bz_i = cr_[..., None] * b_im + ci_[..., None] * b_re

    def lag_kernel(d, steps):
        pr, pi = powers(d, steps)
        m_r = pr[..., None] * bz_r[d][None] - pi[..., None] * bz_i[d][None]
        m_i = pr[..., None] * bz_i[d][None] + pi[..., None] * bz_r[d][None]
        k = (jnp.einsum("gop,dgpi->dgoi", c_re[d], m_r, precision=hp)
             - jnp.einsum("gop,dgpi->dgoi", c_im[d], m_i, precision=hp))
        return k.transpose(1, 3, 0, 2)

    k_f = lag_kernel(0, up)
    k_b = lag_kernel(1, down)
    two_sided = jnp.concatenate([k_b[:, :, :lc - 1], k_f[:, :, :1] + k_b[:, :, lc - 1:], k_f[:, :, 1:]], axis=2)
    two_sided = two_sided.reshape(g, ni, (2 * lc - 1) * ni)
    t_sum = jnp.stack([two_sided[:, :, (lc - 1 - s) * ni:(lc - 1 - s) * ni + S5_BLOCK] for s in range(lc)], axis=1)
    t_sum = t_sum.reshape(g, S5_BLOCK, S5_BLOCK)

    per_dir = lambda a: jnp.stack([a[0], a[0], a[1], a[1]])
    by_pos = lambda planes: jnp.stack(planes).transpose(2, 1, 0, 3).reshape(g, lc, 1, 4 * pn)
    by_chan = lambda a: per_dir(a).transpose(1, 2, 0, 3).reshape(g, 1, ni, 4 * pn)

    ef_r, ef_i = powers(0, down)
    eb_r, eb_i = powers(1, up)
    bz_ri = (bz_r.transpose(0, 1, 3, 2), bz_i.transpose(0, 1, 3, 2))
    b_mat = (by_pos([ef_r, ef_i, eb_r, eb_i]) * by_chan(bz_ri[0])
             + by_pos([-ef_i, ef_r, -eb_i, eb_r]) * by_chan(bz_ri[1])).reshape(g, S5_BLOCK, 4 * pn)

    pf_r, pf_i = powers(0, up + 1.0)
    pb_r, pb_i = powers(1, lc - up)
    c_mat_t = (by_pos([pf_r, -pf_i, pb_r, -pb_i]) * by_chan(c_re)
               + by_pos([-pf_i, -pf_r, -pb_i, -pb_r]) * by_chan(c_im)).reshape(g, S5_BLOCK, 4 * pn)

    full = jnp.full((1,), float(lc), F32)
    decay = [powers(d, full) for d in range(2)]
    return {"t_sum": t_sum.astype(BF16), "b_mat": b_mat.astype(BF16), "c_mat_t": c_mat_t.astype(BF16),
            "decay_re": jnp.stack([decay[0][0], decay[1][0]]).reshape(2, 1, g * pn),
            "decay_im": jnp.stack([decay[0][1], decay[1][1]]).reshape(2, 1, g * pn)}


def _group_norm_store(o_ref, k, y, w_ref):
    cols = slice(k * GROUP_WIDTH, (k + 1) * GROUP_WIDTH)
    o_ref[:, cols] = _rms(y, w_ref[:, cols]).astype(o_ref.dtype)


def _mix_kernel(ya_ref, yb_ref, gb_ref, gc_ref, u_ref, gcp_ref, up_ref, gcn_ref, un_ref,
                cw_ref, cb_ref, ys_ref, us_ref, ds_ref, gw_ref, gbias_ref, mw_ref, o_ref, *, tm, lat_rows, t, ctx):
    i = pl.program_id(0)
    _group_norm_store(o_ref, 0, ya_ref[...], mw_ref)
    _group_norm_store(o_ref, 1, yb_ref[...], mw_ref)

    r0 = i * tm
    in_lat = r0 < lat_rows
    seq_len = jnp.where(in_lat, t, ctx)
    off = jnp.where(in_lat, r0, r0 - lat_rows) % seq_len
    has_prev = off != 0
    has_next = off + tm != seq_len
    v = gc_ref[...] * u_ref[...]
    v_prev_row = jnp.where(has_prev, gcp_ref[7:8, :] * up_ref[7:8, :], 0.0)
    v_next_row = jnp.where(has_next, gcn_ref[0:1, :] * un_ref[0:1, :], 0.0)
    row = lax.broadcasted_iota(jnp.int32, v.shape, 0)
    v_prev = jnp.where(row == 0, v_prev_row, pltpu.roll(v, 1, 0))
    v_next = jnp.where(row == tm - 1, v_next_row, pltpu.roll(v, tm - 1, 0))
    conv = cw_ref[0:1, :] * v_prev + cw_ref[1:2, :] * v + cw_ref[2:3, :] * v_next + cb_ref[...]
    _group_norm_store(o_ref, 2, gb_ref[...] * conv, mw_ref)

    y = ys_ref[...] + ds_ref[...] * us_ref[...]
    g = 0.5 * y * (1.0 + jnp.tanh(math.sqrt(2.0 / math.pi) * (y + 0.044715 * (y * y * y))))
    gate = _sigmoid(_dot(g.astype(BF16), gw_ref[...]) + gbias_ref[...])
    _group_norm_store(o_ref, 3, g * gate, mw_ref)


def _mix_outputs(ya, yb, p, ys5, lw, layer, geo, n_tiles):
    n_tok = ya.shape[0]
    tm = 256
    gw = GROUP_WIDTH
    halo = 8
    n_halo_blocks = n_tok // halo
    tile = lambda col: pl.BlockSpec((tm, gw), lambda i: (i, col // gw))
    prev = lambda col: pl.BlockSpec((halo, gw), lambda i: (jnp.maximum(i * (tm // halo) - 1, 0), col // gw))
    nxt = lambda col: pl.BlockSpec(
        (halo, gw), lambda i: (jnp.minimum((i + 1) * (tm // halo), n_halo_blocks - 1), col // gw))
    full = lambda a: _layer_full_spec(a, layer)
    c_gb, c_gc, c_u = COL_CONV, COL_CONV + gw, COL_CONV + 2 * gw
    weights = [lw["conv_w"], lw["conv_b"]]
    glu = [lw["s5_d"], lw["s5_glu_w"], lw["s5_glu_b"], lw["mix_norm_w"]]
    return pl.pallas_call(
        functools.partial(_mix_kernel, tm=tm, lat_rows=geo["b"] * geo["t"], t=geo["t"], ctx=geo["ctx"]),
        grid=(n_tiles * geo["tm"] // tm,),
        in_specs=[tile(0), tile(0), tile(c_gb), tile(c_gc), tile(c_u),
                  prev(c_gc), prev(c_u), nxt(c_gc), nxt(c_u)]
        + [full(a) for a in weights] + [tile(0), tile(COL_S5)] + [full(a) for a in glu],
        out_specs=pl.BlockSpec((tm, 4 * gw), lambda i: (i, 0)),
        out_shape=jax.ShapeDtypeStruct((n_tiles * geo["tm"], 4 * gw), BF16),
        compiler_params=_cparams(("arbitrary",), 48 * 1024 * 1024),
        name="mix_outputs",
    )(ya, yb, p, p, p, p, p, p, p, *weights, ys5, p, *glu)


def _pad_last(a, width):
    return jnp.pad(a, [(0, 0)] * (a.ndim - 1) + [(0, width - a.shape[-1])])


def _stacked_weights(pr):
    n_layers = pr["mla_w_uq"].shape[0]
    w_uq = pr["mla_w_uq"].reshape(n_layers, MLA_Q_RANK, N_HEADS, MLA_QK)
    w_uq = _pad_last(w_uq, MLA_HEAD_PAD).reshape(n_layers, MLA_Q_RANK, N_HEADS * MLA_HEAD_PAD)
    w_ukv = pr["mla_w_ukv"].reshape(n_layers, MLA_KV_RANK, N_HEADS, MLA_NOPE + MLA_V)
    w_ukv = jnp.concatenate([w_ukv[..., :MLA_NOPE].reshape(n_layers, MLA_KV_RANK, -1),
                             w_ukv[..., MLA_NOPE:].reshape(n_layers, MLA_KV_RANK, -1)], axis=2)
    w_ukv = jnp.pad(w_ukv, ((0, 0), (0, MLA_KV_RANK_PAD - MLA_KV_RANK), (0, 0)))
    row = lambda a: a[:, None, :]
    return {
        "na_q_norm_w": row(pr["na_q_norm_w"]),
        "na_k_norm_w": row(pr["na_k_norm_w"]),
        "mla_cq_norm_w": row(pr["mla_cq_norm_w"]),
        "mla_ckv_norm_w": row(_pad_last(pr["mla_ckv_norm_w"], MLA_KV_RANK_PAD)),
        "mla_w_uq": w_uq.astype(BF16),
        "mla_w_ukv": w_ukv.astype(BF16),
        "mla_q_norm_w": row(_pad_last(pr["mla_q_norm_w"], MLA_HEAD_PAD)),
        "mla_k_norm_w": row(_pad_last(pr["mla_k_norm_w"], MLA_HEAD_PAD)),
        "conv_w": pr["conv_w"],
        "conv_b": row(pr["conv_b"]),
        "s5_d": row(pr["s5_d"]),
        "s5_glu_w": pr["s5_glu_w"].astype(BF16),
        "s5_glu_b": row(pr["s5_glu_b"]),
        "mix_norm_w": row(pr["mix_norm_w"]),
    }


def _layer_full_spec(a, layer):
    return pl.BlockSpec((None,) + a.shape[1:], lambda i: (layer,) + (0,) * (a.ndim - 1))


def _rope_tables(b, t, n_ctx):
    pos = jnp.arange(t, dtype=jnp.int32)
    row = (pos // GRID_W).astype(F32)
    col = (pos % GRID_W).astype(F32)
    n_freq = MLA_ROPE // 4
    inv_freq = ROPE_THETA ** (-jnp.arange(n_freq, dtype=F32) / n_freq)
    ang_r, ang_c = row[:, None] * inv_freq[None, :], col[:, None] * inv_freq[None, :]
    cos_r, sin_r, cos_c, sin_c = jnp.cos(ang_r), jnp.sin(ang_r), jnp.cos(ang_c), jnp.sin(ang_c)
    zeros = jnp.zeros((t, n_freq), F32)
    rest = LANE - MLA_ROPE
    cos_t = jnp.concatenate([cos_r, cos_r, cos_c, cos_c, jnp.ones((t, rest), F32)], axis=1)
    sina_t = jnp.concatenate([-sin_r, zeros, -sin_c, zeros, jnp.zeros((t, rest), F32)], axis=1)
    sinb_t = jnp.concatenate([zeros, sin_r, zeros, sin_c, jnp.zeros((t, rest), F32)], axis=1)
    n_c = b * n_ctx
    tables = []
    for tab, fill in ((cos_t, 1.0), (sina_t, 0.0), (sinb_t, 0.0)):
        tables.append(jnp.concatenate([jnp.tile(tab, (b, 1)), jnp.full((n_c, LANE), fill, F32)], axis=0))
    return tables


def _na_bias_tables(rpb):
    w = GRID_W
    qc = jnp.arange(w)
    cs = jnp.clip(qc - NA_WIN_COLS // 2, 0, w - NA_WIN_COLS)
    v_col = (qc[None, :] >= cs[:, None]) & (qc[None, :] < cs[:, None] + NA_WIN_COLS)
    d_col = jnp.clip(qc[None, :] - qc[:, None], -(NA_WIN_COLS - 1), NA_WIN_COLS - 1) + (NA_WIN_COLS - 1)
    sel_c = jax.nn.one_hot(d_col, 2 * NA_WIN_COLS - 1, dtype=F32)
    tab = jnp.einsum("hab,qkb->haqk", rpb * LOG2E, sel_c, precision=lax.Precision.HIGHEST)
    tab = jnp.where(v_col[None, None], tab, NEG_INF)
    tab = jnp.concatenate([tab, jnp.full((rpb.shape[0], 1, w, w), NEG_INF, F32)], axis=1)
    return jnp.concatenate([tab, tab], axis=-1)


def kernel(x, c, ctx, c_ctx, ada_w, ada_b, norm1_w, norm2_w, w_in, na_q_norm_w, na_k_norm_w, na_rpb, mla_cq_norm_w, mla_ckv_norm_w, mla_w_uq, mla_w_ukv, mla_q_norm_w, mla_k_norm_w, conv_w, conv_b, s5_lambda_re, s5_lambda_im, s5_log_dt, s5_b_re, s5_b_im, s5_c_re, s5_c_im, s5_d, s5_glu_w, s5_glu_b, mix_norm_w, w_out, ffn_w1, ffn_w3, ffn_w2):
    b, t, d = x.shape
    n_ctx = ctx.shape[1]
    n_layers = ada_w.shape[0]
    n_rows = t // GRID_W
    assert t % GRID_W == 0 and n_rows >= NA_K_ROWS and n_rows % NA_Q_ROWS == 0
    assert n_ctx == NA_Q_ROWS * GRID_W and t % n_ctx == 0 and b + 1 <= 8
    tm = 512 if (b * n_ctx) % 512 == 0 and t % 512 == 0 else 256
    tiles_per_batch = t // tm
    geo = {"b": b, "t": t, "ctx": n_ctx, "tm": tm,
           "tile_class": lambda i: jnp.minimum(i // tiles_per_batch, b)}
    lat_tiles = b * t // tm
    all_tiles = lat_tiles + b * n_ctx // tm

    pr = dict(na_q_norm_w=na_q_norm_w, na_k_norm_w=na_k_norm_w, mla_cq_norm_w=mla_cq_norm_w,
              mla_ckv_norm_w=mla_ckv_norm_w, mla_w_uq=mla_w_uq, mla_w_ukv=mla_w_ukv,
              mla_q_norm_w=mla_q_norm_w, mla_k_norm_w=mla_k_norm_w, conv_w=conv_w, conv_b=conv_b,
              s5_d=s5_d, s5_glu_w=s5_glu_w, s5_glu_b=s5_glu_b, mix_norm_w=mix_norm_w)

    h = (x.reshape(b * t, d), ctx.reshape(b * n_ctx, d))
    cvec = jnp.zeros((8, d), F32).at[:b].set(c).at[b].set(c_ctx)
    mod = _ada_mod(cvec, ada_w, ada_b, 0)
    rope = _rope_tables(b, t, n_ctx)
    lw = _stacked_weights(pr)
    na_bias = jax.vmap(_na_bias_tables)(na_rpb)
    s5_ops = jax.vmap(_s5_operators)(s5_lambda_re, s5_lambda_im, s5_log_dt, s5_b_re, s5_b_im, s5_c_re, s5_c_im)
    m_in = 1088 if (b * t + b * n_ctx) % 1088 == 0 else tm

    for l in range(n_layers):
        last = l == n_layers - 1
        n_tiles = lat_tiles if last else all_tiles

        a = _norm_modulate(h, norm1_w[l], mod, 0, 1, geo, all_tiles)
        p, w2_bf16 = _input_projection(a, w_in, ffn_w2, l, m_in)
        naq, nak, nav, mq, mk, mv = _prep_qkv(p, lw, l, rope, 256)
        ya = _na_attention(naq, nak, nav, na_bias, l, geo, not last)
        yb, mod_next = _mla_attention(mq, mk, mv, geo, not last,
                                      None if last else (cvec, ada_w, ada_b, l + 1))
        ys5 = _s5_mixer(p, s5_ops, l, geo)
        y = _mix_outputs(ya, yb, p, ys5, lw, l, geo, n_tiles)
        h = _matmul_gated_residual(y, w_out, h, mod, 2, geo, n_tiles, 1024, layer=l, single_buffer_w=True)

        f = _norm_modulate(h, norm2_w[l], mod, 3, 4, geo, n_tiles)
        g = _ffn_in(f, ffn_w1, ffn_w3, l, geo, n_tiles, 512)
        h = _matmul_gated_residual(g, w2_bf16, h, mod, 5, geo, n_tiles, 512)
        mod = mod_next

    return h.reshape(b, t, d)
```

```python
import functools
import math

import jax
import jax.numpy as jnp
from jax import lax
from jax.experimental import pallas as pl
from jax.experimental.pallas import tpu as pltpu

F32 = jnp.float32
BF16 = jnp.bfloat16

NORM_EPS = 1e-6
NEG_INF = -1e30
GRID_W = 64

GROUP_WIDTH = 1024
N_HEADS = 8
NA_HEAD_DIM = 128
NA_WIN_ROWS = 8
NA_WIN_COLS = 16
NA_Q_ROWS = 4
NA_K_ROWS = NA_Q_ROWS + NA_WIN_ROWS

MLA_NOPE = 128
MLA_ROPE = 64
MLA_QK = MLA_NOPE + MLA_ROPE
MLA_V = 128
MLA_Q_RANK = 896
MLA_KV_RANK = 320
MLA_KV_RANK_PAD = 512
MLA_HEAD_PAD = 256
V_HEAD_PAD = 256
LOG2E = math.log2(math.e)
ROPE_THETA = 10000.0

CONV_K = 3

S5_GROUPS = 64
S5_GROUP = 16
S5_STATE = 64
S5_CHUNK = 16
S5_BLOCK = S5_CHUNK * S5_GROUP


COL_NA = 0
COL_CONV = 3072
COL_S5 = 6144
COL_CQ = 7168
COL_CKV = 8192
IN_WIDTH_PAD = 8704
IN_TILE = 512
SRC_CQ = 3072
SRC_CKV = SRC_CQ + MLA_Q_RANK
SRC_CONV = SRC_CKV + MLA_KV_RANK + MLA_ROPE
SRC_S5 = SRC_CONV + 3 * GROUP_WIDTH

LANE = 128
VMEM_LIMIT = 56 * 1024 * 1024


def _cparams(sem, vmem=None):
    return pltpu.CompilerParams(dimension_semantics=sem, vmem_limit_bytes=vmem)


def _sigmoid(x):
    return 1.0 / (1.0 + jnp.exp(-x))


def _rms(x, w):
    return x * lax.rsqrt(jnp.mean(x * x, axis=-1, keepdims=True) + NORM_EPS) * w


def _dot(a, b):
    return jnp.dot(a, b, preferred_element_type=F32)


def _dot_nt(a, b):
    return lax.dot_general(a, b, (((1,), (1,)), ((), ())), preferred_element_type=F32)


def _ada_kernel(c_ref, w_ref, b_ref, o_ref):
    c = c_ref[...]
    s = (c * _sigmoid(c)).astype(BF16)
    o_ref[...] = _dot(s, w_ref[...].astype(BF16)) + b_ref[...]


def _ada_rows(out, d):
    return out.reshape(8, 6, d).transpose(1, 0, 2)[:, :, None, :]


def _ada_mod(cvec, ada_w, ada_b, layer):
    n_layers, d, n = ada_w.shape
    tn = 512
    out = pl.pallas_call(
        _ada_kernel,
        grid=(n // tn,),
        in_specs=[pl.BlockSpec((8, d), lambda j: (0, 0)),
                  pl.BlockSpec((None, d, tn), lambda j: (layer, 0, j)),
                  pl.BlockSpec((None, 1, tn), lambda j: (layer, 0, j))],
        out_specs=pl.BlockSpec((8, tn), lambda j: (0, j)),
        out_shape=jax.ShapeDtypeStruct((8, n), F32),
        compiler_params=_cparams(("arbitrary",), 40 * 1024 * 1024),
        name="ada_mod",
    )(cvec, ada_w, ada_b.reshape(n_layers, 1, n))
    return _ada_rows(out, d)


def _stream_parts(h):
    return tuple(h) if isinstance(h, (tuple, list)) else (h,)


def _stream_specs(parts, tm, width, row_of, col_of):
    if len(parts) == 1:
        return [pl.BlockSpec((tm, width), lambda *g: (row_of(*g), col_of(*g)))]
    lat_tiles = parts[0].shape[0] // tm
    return [pl.BlockSpec((tm, width), lambda *g: (jnp.minimum(row_of(*g), lat_tiles - 1), col_of(*g))),
            pl.BlockSpec((tm, width), lambda *g: (jnp.maximum(row_of(*g) - lat_tiles, 0), col_of(*g)))]


def _read_stream(h_refs, row_tile, lat_tiles):
    if len(h_refs) == 1:
        return h_refs[0][...]
    return jnp.where(row_tile < lat_tiles, h_refs[0][...], h_refs[1][...])


def _normmod_kernel(*refs, n_h, lat_tiles):
    h_refs = refs[:n_h]
    w_ref, shift_ref, scale_ref, o_ref = refs[n_h:]
    y = _rms(_read_stream(h_refs, pl.program_id(0), lat_tiles), w_ref[...])
    o_ref[...] = (y * (1.0 + scale_ref[...]) + shift_ref[...]).astype(o_ref.dtype)


def _norm_modulate(h, w, mod, k_shift, k_scale, geo, n_tiles):
    parts = _stream_parts(h)
    d = parts[0].shape[1]
    tm = geo["tm"]
    cls = geo["tile_class"]
    return pl.pallas_call(
        functools.partial(_normmod_kernel, n_h=len(parts), lat_tiles=parts[0].shape[0] // tm),
        grid=(n_tiles,),
        in_specs=_stream_specs(parts, tm, d, lambda i: i, lambda i: 0)
        + [pl.BlockSpec((1, d), lambda i: (0, 0)),
           pl.BlockSpec((None, None, 1, d), lambda i: (k_shift, cls(i), 0, 0)),
           pl.BlockSpec((None, None, 1, d), lambda i: (k_scale, cls(i), 0, 0))],
        out_specs=pl.BlockSpec((tm, d), lambda i: (i, 0)),
        out_shape=jax.ShapeDtypeStruct((n_tiles * tm, d), BF16),
        compiler_params=_cparams(("arbitrary",), VMEM_LIMIT),
        name="norm_modulate",
    )(*parts, w.reshape(1, d), mod, mod)


def _in_proj_kernel(a_ref, w_ref, w2_ref, o_ref, w2b_ref, wb_ref, *, n_row_tiles, n_w2_blocks):
    @pl.when(pl.program_id(1) == 0)
    def _():
        wb_ref[...] = w_ref[...].astype(BF16)

    o_ref[...] = _dot(a_ref[...], wb_ref[...])

    step = pl.program_id(0) * n_row_tiles + pl.program_id(1)

    @pl.when(step < n_w2_blocks)
    def _():
        w2b_ref[...] = w2_ref[...].astype(BF16)


def _in_proj_source_col(j):
    t = IN_TILE // LANE
    unit = jnp.where(j < 6, j * t,
           jnp.where(j < 12, SRC_CONV // LANE + (j - 6) * t,
           jnp.where(j < 14, SRC_S5 // LANE + (j - 12) * t,
           jnp.where(j < 16, SRC_CQ // LANE + (j - 14) * t, SRC_CKV // LANE))))
    return unit * LANE


def _input_projection(a, w_in, w2, layer, tm):
    m, k = a.shape
    tn = IN_TILE
    assert m % tm == 0 and COL_CKV + tn == IN_WIDTH_PAD and SRC_CKV + tn <= w_in.shape[-1]
    n_col_tiles, n_row_tiles = IN_WIDTH_PAD // tn, m // tm
    hidden, d_out = w2.shape[1:]
    slab = next(r for r in range(16, hidden + 1, 16) if hidden % r == 0 and hidden // r <= n_col_tiles * n_row_tiles)
    n_w2_blocks = hidden // slab
    w2_block = lambda j, i: jnp.minimum(j * n_row_tiles + i, n_w2_blocks - 1)
    return pl.pallas_call(
        functools.partial(_in_proj_kernel, n_row_tiles=n_row_tiles, n_w2_blocks=n_w2_blocks),
        grid=(n_col_tiles, n_row_tiles),
        in_specs=[pl.BlockSpec((tm, k), lambda j, i: (i, 0)),
                  pl.BlockSpec((pl.Element(k), pl.Element(tn)),
                               lambda j, i: (layer * k, _in_proj_source_col(j))),
                  pl.BlockSpec((None, slab, d_out), lambda j, i: (layer, w2_block(j, i), 0))],
        out_specs=[pl.BlockSpec((tm, tn), lambda j, i: (i, j)),
                   pl.BlockSpec((slab, d_out), lambda j, i: (w2_block(j, i), 0))],
        out_shape=[jax.ShapeDtypeStruct((m, IN_WIDTH_PAD), F32), jax.ShapeDtypeStruct((hidden, d_out), BF16)],
        scratch_shapes=[pltpu.VMEM((k, tn), BF16)],
        compiler_params=_cparams(("arbitrary", "arbitrary"), VMEM_LIMIT),
        name="input_projection",
    )(a, w_in.reshape(-1, w_in.shape[-1]), w2)


def _mm_res_kernel(a_ref, w_ref, *refs, n_h, lat_tiles, f32w):
    h_refs = refs[:n_h]
    g_ref, o_ref = refs[n_h], refs[n_h + 1]
    if f32w:
        wb_ref = refs[n_h + 2]

        @pl.when(pl.program_id(1) == 0)
        def _():
            wb_ref[...] = w_ref[...].astype(BF16)

        w = wb_ref[...]
    else:
        w = w_ref[...]
    h = _read_stream(h_refs, pl.program_id(1), lat_tiles)
    o_ref[...] = h + g_ref[...] * _dot(a_ref[...], w)


def _layer_weight_spec(w, layer, tn, single_buffer=False):
    mode = pl.Buffered(1) if single_buffer else None
    if w.ndim == 3:
        return pl.BlockSpec((None, w.shape[1], tn), lambda j, i: (layer, 0, j), pipeline_mode=mode)
    return pl.BlockSpec((w.shape[0], tn), lambda j, i: (0, j), pipeline_mode=mode)


def _matmul_gated_residual(a, w, h, mod, k_gate, geo, n_tiles, tn, layer=None, single_buffer_w=False):
    m, k = a.shape
    n = w.shape[-1]
    tm = geo["tm"]
    cls = geo["tile_class"]
    f32w = w.dtype == F32
    parts = _stream_parts(h)
    return pl.pallas_call(
        functools.partial(_mm_res_kernel, n_h=len(parts), lat_tiles=parts[0].shape[0] // tm, f32w=f32w),
        grid=(n // tn, n_tiles),
        in_specs=[pl.BlockSpec((tm, k), lambda j, i: (i, 0)),
                  _layer_weight_spec(w, layer, tn, single_buffer_w)]
        + _stream_specs(parts, tm, tn, lambda j, i: i, lambda j, i: j)
        + [pl.BlockSpec((None, None, 1, tn), lambda j, i: (k_gate, cls(i), 0, j))],
        out_specs=pl.BlockSpec((tm, tn), lambda j, i: (i, j)),
        out_shape=jax.ShapeDtypeStruct((n_tiles * tm, n), F32),
        scratch_shapes=[pltpu.VMEM((k, tn), BF16)] if f32w else [],
        compiler_params=_cparams(("arbitrary", "arbitrary"), VMEM_LIMIT),
        name="matmul_gated_residual",
    )(a, w, *parts, mod)


def _ffn1_kernel(a_ref, w1_ref, w3_ref, o_ref, w1b_ref, w3b_ref):
    @pl.when(pl.program_id(1) == 0)
    def _():
        w1b_ref[...] = w1_ref[...].astype(BF16)
        w3b_ref[...] = w3_ref[...].astype(BF16)

    a = a_ref[...]
    u = _dot(a, w1b_ref[...])
    v = _dot(a, w3b_ref[...])
    o_ref[...] = (u * _sigmoid(u) * v).astype(o_ref.dtype)


def _ffn_in(a, w1, w3, layer, geo, n_tiles, tn):
    m, k = a.shape
    n = w1.shape[-1]
    tm = geo["tm"]
    return pl.pallas_call(
        _ffn1_kernel,
        grid=(pl.cdiv(n, tn), n_tiles),
        in_specs=[pl.BlockSpec((tm, k), lambda j, i: (i, 0)),
                  _layer_weight_spec(w1, layer, tn),
                  _layer_weight_spec(w3, layer, tn)],
        out_specs=pl.BlockSpec((tm, tn), lambda j, i: (i, j)),
        out_shape=jax.ShapeDtypeStruct((m, n), BF16),
        scratch_shapes=[pltpu.VMEM((k, tn), BF16), pltpu.VMEM((k, tn), BF16)],
        compiler_params=_cparams(("arbitrary", "arbitrary"), VMEM_LIMIT),
        name="ffn_in",
    )(a, w1, w3)


def _rope_tail(t, cos_ref, sina_ref, sinb_ref):
    q = MLA_ROPE // 4
    return (t * cos_ref[...] + pltpu.roll(t, LANE - q, 1) * sina_ref[...]
            + pltpu.roll(t, q, 1) * sinb_ref[...])


def _prep_kernel(na_ref, cq_ref, ckvkr_ref, naqw_ref, nakw_ref, cqw_ref, ckvw_ref,
                 wuq_ref, wukv_ref, mqw_ref, mkw_ref, cos_ref, sina_ref, sinb_ref,
                 naq_ref, nak_ref, nav_ref, mq_ref, mk_ref, mv_ref):
    hd = NA_HEAD_DIM
    tm = na_ref.shape[0]
    ones_col = jnp.where(lax.broadcasted_iota(jnp.int32, (tm, LANE), 1) == 0, 1.0, 0.0).astype(BF16)
    na_qs = naqw_ref[...] * (NA_HEAD_DIM ** -0.5 * LOG2E)
    for h in range(N_HEADS):
        naq_ref[:, h * hd:(h + 1) * hd] = _rms(na_ref[:, h * hd:(h + 1) * hd], na_qs).astype(BF16)
        nak_ref[:, h * hd:(h + 1) * hd] = _rms(
            na_ref[:, GROUP_WIDTH + h * hd:GROUP_WIDTH + (h + 1) * hd], nakw_ref[...]).astype(BF16)
        vo = h * V_HEAD_PAD
        nav_ref[:, vo:vo + hd] = na_ref[:, 2 * GROUP_WIDTH + h * hd:2 * GROUP_WIDTH + (h + 1) * hd].astype(BF16)
        nav_ref[:, vo + hd:vo + V_HEAD_PAD] = ones_col

    cq = _rms(cq_ref[...], cqw_ref[...]).astype(BF16)
    q = _dot(cq, wuq_ref[...])
    inv_qk = 1.0 / MLA_QK
    mqw = mqw_ref[...] * (MLA_QK ** -0.5 * LOG2E)
    for h in range(N_HEADS):
        o = h * MLA_HEAD_PAD
        nope = q[:, o:o + MLA_NOPE]
        tail = q[:, o + MLA_NOPE:o + MLA_HEAD_PAD]
        ss = jnp.sum(nope * nope, axis=-1, keepdims=True) + jnp.sum(tail * tail, axis=-1, keepdims=True)
        r = lax.rsqrt(ss * inv_qk + NORM_EPS)
        mq_ref[:, o:o + MLA_NOPE] = (nope * r * mqw[:, :MLA_NOPE]).astype(BF16)
        mq_ref[:, o + MLA_NOPE:o + MLA_HEAD_PAD] = _rope_tail(
            tail * r * mqw[:, MLA_NOPE:], cos_ref, sina_ref, sinb_ref).astype(BF16)

    blk = ckvkr_ref[...]
    lane = lax.broadcasted_iota(jnp.int32, blk.shape, 1)
    ckv = jnp.where(lane < MLA_KV_RANK, blk, 0.0)
    ckv_ms = jnp.sum(ckv * ckv, axis=-1, keepdims=True) * (1.0 / MLA_KV_RANK)
    ckv_n = (ckv * lax.rsqrt(ckv_ms + NORM_EPS) * ckvw_ref[...]).astype(BF16)
    kv = _dot(ckv_n, wukv_ref[...])
    for h in range(N_HEADS):
        vo = h * V_HEAD_PAD
        mv_ref[:, vo:vo + MLA_V] = kv[:, GROUP_WIDTH + h * MLA_V:GROUP_WIDTH + (h + 1) * MLA_V].astype(BF16)
        mv_ref[:, vo + MLA_V:vo + V_HEAD_PAD] = ones_col
    kr = pltpu.roll(blk[:, 2 * LANE:3 * LANE], LANE - MLA_ROPE, 1)
    kr = jnp.where(lax.broadcasted_iota(jnp.int32, kr.shape, 1) < MLA_ROPE, kr, 0.0)
    kr_ss = jnp.sum(kr * kr, axis=-1, keepdims=True)
    mkw = mkw_ref[...]
    for h in range(N_HEADS):
        o = h * MLA_HEAD_PAD
        kn = kv[:, h * MLA_NOPE:(h + 1) * MLA_NOPE]
        r = lax.rsqrt((jnp.sum(kn * kn, axis=-1, keepdims=True) + kr_ss) * inv_qk + NORM_EPS)
        mk_ref[:, o:o + MLA_NOPE] = (kn * r * mkw[:, :MLA_NOPE]).astype(BF16)
        mk_ref[:, o + MLA_NOPE:o + MLA_HEAD_PAD] = _rope_tail(
            kr * r * mkw[:, MLA_NOPE:], cos_ref, sina_ref, sinb_ref).astype(BF16)


def _prep_qkv(p, lw, layer, rope, tm):
    n_tok = p.shape[0]
    blk = lambda width, col: pl.BlockSpec((tm, width), lambda i: (i, col // width))
    full = lambda a: _layer_full_spec(a, layer)
    tab = pl.BlockSpec((tm, LANE), lambda i: (i, 0))
    out_w = [GROUP_WIDTH, GROUP_WIDTH, N_HEADS * V_HEAD_PAD,
             N_HEADS * MLA_HEAD_PAD, N_HEADS * MLA_HEAD_PAD, N_HEADS * V_HEAD_PAD]
    weights = [lw["na_q_norm_w"], lw["na_k_norm_w"], lw["mla_cq_norm_w"], lw["mla_ckv_norm_w"],
               lw["mla_w_uq"], lw["mla_w_ukv"], lw["mla_q_norm_w"], lw["mla_k_norm_w"]]
    return pl.pallas_call(
        _prep_kernel,
        grid=(n_tok // tm,),
        in_specs=[blk(3 * GROUP_WIDTH, COL_NA), blk(MLA_Q_RANK, COL_CQ), blk(MLA_KV_RANK_PAD, COL_CKV)]
        + [full(a) for a in weights] + [tab, tab, tab],
        out_specs=[pl.BlockSpec((tm, w), lambda i: (i, 0)) for w in out_w],
        out_shape=[jax.ShapeDtypeStruct((n_tok, w), BF16) for w in out_w],
        compiler_params=_cparams(("arbitrary",), 48 * 1024 * 1024),
        name="prep_qkv",
    )(p, p, p, *weights, *rope)


def _softmax_attend(q, pairs, bias=None):
    scores = []
    for idx, (k, _) in enumerate(pairs):
        s = _dot_nt(q, k)
        if idx == 0 and bias is not None:
            s = s + bias
        scores.append(s)
    m = scores[0].max(axis=-1, keepdims=True)
    for s in scores[1:]:
        m = jnp.maximum(m, s.max(axis=-1, keepdims=True))
    acc = None
    for s, (_, v) in zip(scores, pairs):
        o = _dot(jnp.exp2(s - m).astype(BF16), v)
        acc = o if acc is None else acc + o
    dv = acc.shape[-1] // 2
    return acc[:, :dv] / acc[:, dv:dv + 1]


NA_INVALID_ROW = 2 * NA_WIN_ROWS - 1


def _na_window_bias(bt_ref, rb, n_rows):
    half = NA_WIN_ROWS // 2
    ks_row = jnp.clip(rb * NA_Q_ROWS - half, 0, n_rows - NA_K_ROWS)
    lane_lo = lax.broadcasted_iota(jnp.int32, (GRID_W, LANE), 1) < GRID_W
    strips = []
    for i in range(NA_Q_ROWS):
        rq = rb * NA_Q_ROWS + i
        rs = jnp.clip(rq - half, 0, n_rows - NA_WIN_ROWS)

        def slab(j, rq=rq, rs=rs):
            rk = ks_row + j
            valid = (rk >= rs) & (rk < rs + NA_WIN_ROWS)
            return bt_ref[jnp.where(valid, rk - rq + (NA_WIN_ROWS - 1), NA_INVALID_ROW)]

        pieces = [jnp.where(lane_lo, slab(2 * m), slab(2 * m + 1)) for m in range(NA_K_ROWS // 2)]
        strips.append(jnp.concatenate(pieces, axis=-1))
    return jnp.concatenate(strips, axis=0)


def _na_kernel(q_ref, k_ref, v_ref, kc_ref, vc_ref, kca_ref, vca_ref, bt_ref, o_ref, *,
               n_lat_steps, steps_per_batch, n_rows, n_sub, ctx, hps):
    i = pl.program_id(1)
    sq = NA_Q_ROWS * GRID_W
    dq, dv = NA_HEAD_DIM, V_HEAD_PAD

    @pl.when(i < n_lat_steps)
    def _():
        step = i % steps_per_batch
        for hh in range(hps):
            qc, vc = slice(hh * dq, (hh + 1) * dq), slice(hh * dv, (hh + 1) * dv)
            for sb in range(n_sub):
                rb = step * n_sub + sb
                ks = jnp.clip(rb * NA_Q_ROWS - NA_WIN_ROWS // 2, 0, n_rows - NA_K_ROWS) * GRID_W
                ks = pl.multiple_of(ks, GRID_W)
                kw = k_ref[pl.ds(ks, NA_K_ROWS * GRID_W), qc]
                vw = v_ref[pl.ds(ks, NA_K_ROWS * GRID_W), vc]
                rows = slice(sb * sq, (sb + 1) * sq)
                o_ref[rows, qc] = _softmax_attend(q_ref[rows, qc], [(kw, vw), (kc_ref[:, qc], vc_ref[:, vc])],
                                                  _na_window_bias(bt_ref.at[hh], rb, n_rows))

    @pl.when(i == n_lat_steps)
    def _():
        for hh in range(hps):
            qc, vc = slice(hh * dq, (hh + 1) * dq), slice(hh * dv, (hh + 1) * dv)
            for bb in range(q_ref.shape[0] // ctx):
                rows = slice(bb * ctx, (bb + 1) * ctx)
                o_ref[rows, qc] = _softmax_attend(q_ref[rows, qc], [(kca_ref[rows, qc], vca_ref[rows, vc])])


def _na_attention(naq, nak, nav, bias_tab, layer, geo, with_ctx):
    b, t, ctx = geo["b"], geo["t"], geo["ctx"]
    n_tok = naq.shape[0]
    sq = NA_Q_ROWS * GRID_W
    tq = b * ctx
    assert tq % sq == 0 and t % tq == 0
    n_sub = tq // sq
    n_rows = t // GRID_W
    spb = t // tq
    lat_steps = b * spb
    ctx_block0 = b * t // ctx
    batch = lambda i: jnp.minimum(i // spb, b - 1)
    hps = 2
    dq, dv = hps * NA_HEAD_DIM, hps * V_HEAD_PAD

    return pl.pallas_call(
        functools.partial(_na_kernel, n_lat_steps=lat_steps, steps_per_batch=spb, n_rows=n_rows,
                          n_sub=n_sub, ctx=ctx, hps=hps),
        grid=(N_HEADS // hps, lat_steps + (1 if with_ctx else 0)),
        in_specs=[pl.BlockSpec((tq, dq), lambda h, i: (i, h)),
                  pl.BlockSpec((t, dq), lambda h, i: (batch(i), h)),
                  pl.BlockSpec((t, dv), lambda h, i: (batch(i), h)),
                  pl.BlockSpec((ctx, dq), lambda h, i: (ctx_block0 + batch(i), h)),
                  pl.BlockSpec((ctx, dv), lambda h, i: (ctx_block0 + batch(i), h)),
                  pl.BlockSpec((tq, dq), lambda h, i: (lat_steps, h)),
                  pl.BlockSpec((tq, dv), lambda h, i: (lat_steps, h)),
                  pl.BlockSpec((None, hps) + bias_tab.shape[2:], lambda h, i: (layer, h, 0, 0, 0))],
        out_specs=pl.BlockSpec((tq, dq), lambda h, i: (i, h)),
        out_shape=jax.ShapeDtypeStruct((n_tok, GROUP_WIDTH), F32),
        compiler_params=_cparams(("arbitrary", "arbitrary"), 40 * 1024 * 1024),
        name="na_attention",
    )(naq, nak, nav, nak, nav, nak, nav, bias_tab)


def _flash_attend(q, loads):
    m = None
    acc = None
    for load in loads:
        k, v = load()
        s = _dot_nt(q, k)
        mc = s.max(axis=-1, keepdims=True)
        if m is None:
            m = mc
            acc = _dot(jnp.exp2(s - m).astype(BF16), v)
        else:
            m_new = jnp.maximum(m, mc)
            acc = acc * jnp.exp2(m - m_new) + _dot(jnp.exp2(s - m_new).astype(BF16), v)
            m = m_new
    dv = acc.shape[-1] // 2
    return acc[:, :dv] / acc[:, dv:dv + 1]


def _mla_kernel(q_ref, k_ref, v_ref, kc_ref, vc_ref, kca_ref, vca_ref, *refs,
                n_q_blocks, key_chunk, ctx, hps, n_ada_blocks):
    i = pl.program_id(1)
    dq, dv = MLA_HEAD_PAD, V_HEAD_PAD
    if n_ada_blocks:
        c_ref, aw_ref, ab_ref, o_ref, mod_ref = refs
        step = pl.program_id(0) * pl.num_programs(1) + i

        @pl.when(step < n_ada_blocks)
        def _():
            _ada_kernel(c_ref, aw_ref, ab_ref, mod_ref)
    else:
        (o_ref,) = refs

    @pl.when(i < n_q_blocks)
    def _():
        for hh in range(hps):
            qc, vc = slice(hh * dq, (hh + 1) * dq), slice(hh * dv, (hh + 1) * dv)

            def lat_chunk(j, qc=qc, vc=vc):
                rows = slice(j * key_chunk, (j + 1) * key_chunk)
                return lambda: (k_ref[rows, qc], v_ref[rows, vc])

            loads = [lat_chunk(j) for j in range(k_ref.shape[0] // key_chunk)]
            loads.append(lambda qc=qc, vc=vc: (kc_ref[:, qc], vc_ref[:, vc]))
            o_ref[:, hh * MLA_V:(hh + 1) * MLA_V] = _flash_attend(q_ref[:, qc], loads)

    @pl.when(i == n_q_blocks)
    def _():
        for hh in range(hps):
            qc, vc = slice(hh * dq, (hh + 1) * dq), slice(hh * dv, (hh + 1) * dv)
            for bb in range(q_ref.shape[0] // ctx):
                rows = slice(bb * ctx, (bb + 1) * ctx)
                o_ref[rows, hh * MLA_V:(hh + 1) * MLA_V] = _flash_attend(
                    q_ref[rows, qc], [lambda: (kca_ref[rows, qc], vca_ref[rows, vc])])


def _mla_attention(mq, mk, mv, geo, with_ctx, ada=None):
    b, t, ctx = geo["b"], geo["t"], geo["ctx"]
    n_tok = mq.shape[0]
    tq = b * ctx
    assert t % tq == 0
    nqb = t // tq
    lat_blocks = b * nqb
    ctx_block0 = b * t // ctx
    batch = lambda i: jnp.minimum(i // nqb, b - 1)
    hps = 2
    dq, dv = hps * MLA_HEAD_PAD, hps * V_HEAD_PAD
    grid = (N_HEADS // hps, lat_blocks + (1 if with_ctx else 0))

    in_specs = [pl.BlockSpec((tq, dq), lambda h, i: (i, h)),
                pl.BlockSpec((t, dq), lambda h, i: (batch(i), h)),
                pl.BlockSpec((t, dv), lambda h, i: (batch(i), h)),
                pl.BlockSpec((ctx, dq), lambda h, i: (ctx_block0 + batch(i), h)),
                pl.BlockSpec((ctx, dv), lambda h, i: (ctx_block0 + batch(i), h)),
                pl.BlockSpec((tq, dq), lambda h, i: (lat_blocks, h)),
                pl.BlockSpec((tq, dv), lambda h, i: (lat_blocks, h))]
    operands = [mq, mk, mv, mk, mv, mk, mv]
    out_specs = [pl.BlockSpec((tq, hps * MLA_V), lambda h, i: (i, h))]
    out_shape = [jax.ShapeDtypeStruct((n_tok, GROUP_WIDTH), F32)]
    n_ada_blocks = 0
    if ada is not None:
        cvec, ada_w, ada_b, layer = ada
        n_layers, d, n = ada_w.shape
        n_steps = grid[0] * grid[1]
        n_ada_blocks = max(k for k in range(1, n_steps + 1) if n % (k * LANE) == 0)
        tn = n // n_ada_blocks
        slab = lambda h, i: jnp.minimum(h * grid[1] + i, n_ada_blocks - 1)
        in_specs += [pl.BlockSpec((8, d), lambda h, i: (0, 0)),
                     pl.BlockSpec((None, d, tn), lambda h, i: (layer, 0, slab(h, i))),
                     pl.BlockSpec((None, 1, tn), lambda h, i: (layer, 0, slab(h, i)))]
        operands += [cvec, ada_w, ada_b.reshape(n_layers, 1, n)]
        out_specs.append(pl.BlockSpec((8, tn), lambda h, i: (0, slab(h, i))))
        out_shape.append(jax.ShapeDtypeStruct((8, n), F32))

    outs = pl.pallas_call(
        functools.partial(_mla_kernel, n_q_blocks=lat_blocks, key_chunk=512, ctx=ctx, hps=hps,
                          n_ada_blocks=n_ada_blocks),
        grid=grid,
        in_specs=in_specs,
        out_specs=out_specs,
        out_shape=out_shape,
        compiler_params=_cparams(("arbitrary", "arbitrary"), 48 * 1024 * 1024),
        name="mla_attention",
    )(*operands)
    if ada is None:
        return outs[0], None
    return outs[0], _ada_rows(outs[1], ada[1].shape[1])


S5_OCTET = LANE // S5_GROUP


def _s5_gather_chunks(p_ref, u_ref):
    rows = u_ref.shape[0]
    for s in range(S5_CHUNK):
        xs = p_ref[pl.ds(s, rows, stride=S5_CHUNK), :]
        for k in range(S5_OCTET):
            dst = k * S5_BLOCK + s * S5_GROUP
            u_ref[:, dst:dst + S5_GROUP] = xs[:, k * S5_GROUP:(k + 1) * S5_GROUP]


def _s5_contrib_kernel(p_ref, b_ref, o_ref, ub_ref, u_ref):
    _s5_gather_chunks(p_ref, u_ref)
    ub_ref[...] = u_ref[...].astype(BF16)
    r = [_dot(ub_ref[:, k * S5_BLOCK:(k + 1) * S5_BLOCK], b_ref[k]) for k in range(S5_OCTET)]
    ps = S5_STATE
    for plane in range(4):
        o_ref[plane] = jnp.concatenate([rk[:, plane * ps:(plane + 1) * ps] for rk in r], axis=-1)


def _s5_scan_kernel(c_ref, lr_ref, li_ref, x_ref, *, n_tiles, n_ctx_tiles, nb):
    sub = 8
    n_lat_tiles = n_tiles - n_ctx_tiles
    lanes = c_ref.shape[-1]
    lr = jnp.broadcast_to(lr_ref[...], (sub, lanes))
    li = jnp.broadcast_to(li_ref[...], (sub, lanes))
    row = lax.broadcasted_iota(jnp.int32, (sub, lanes), 0)

    def run(reverse):
        order = range(sub - 1, -1, -1) if reverse else range(sub)

        def body(jt, carry):
            if reverse:
                tile = jnp.where(jt < n_ctx_tiles, n_ctx_tiles - 1 - jt, n_tiles - 1 - (jt - n_ctx_tiles))
            else:
                tile = jt
            new = []
            for bi in range(nb):
                xr, xi = carry[2 * bi], carry[2 * bi + 1]
                row_tile = jnp.where(tile < n_ctx_tiles, nb * n_lat_tiles + bi * n_ctx_tiles + tile,
                                     bi * n_lat_tiles + tile - n_ctx_tiles)
                r0 = pl.multiple_of(row_tile * sub, sub)
                cr = c_ref[0, pl.ds(r0, sub), :]
                ci = c_ref[1, pl.ds(r0, sub), :]
                out_r = jnp.zeros_like(cr)
                out_i = jnp.zeros_like(ci)
                for k in order:
                    out_r = jnp.where(row == k, xr, out_r)
                    out_i = jnp.where(row == k, xi, out_i)
                    ck_r = jnp.broadcast_to(cr[k:k + 1, :], (sub, lanes))
                    ck_i = jnp.broadcast_to(ci[k:k + 1, :], (sub, lanes))
                    xr, xi = lr * xr - li * xi + ck_r, lr * xi + li * xr + ck_i
                x_ref[0, pl.ds(r0, sub), :] = out_r
                x_ref[1, pl.ds(r0, sub), :] = out_i
                new += [xr, xi]
            return tuple(new)

        zero = jnp.zeros((sub, lanes), F32)
        lax.fori_loop(0, n_tiles, body, (zero,) * (2 * nb))

    @pl.when(pl.program_id(0) == 0)
    def _():
        run(False)

    @pl.when(pl.program_id(0) == 1)
    def _():
        run(True)


def _s5_out_kernel(u_ref, x_ref, t_ref, ct_ref, o_ref, y_ref):
    rows = u_ref.shape[0]
    ps = S5_STATE
    for k in range(S5_OCTET):
        x = jnp.concatenate([x_ref[plane][:, k * ps:(k + 1) * ps] for plane in range(4)], axis=-1)
        cols = slice(k * S5_BLOCK, (k + 1) * S5_BLOCK)
        y_ref[:, cols] = _dot(u_ref[:, cols], t_ref[k]) + _dot_nt(x.astype(BF16), ct_ref[k])
    for t in range(S5_CHUNK):
        piece = jnp.concatenate([y_ref[:, k * S5_BLOCK + t * S5_GROUP:k * S5_BLOCK + (t + 1) * S5_GROUP]
                                 for k in range(S5_OCTET)], axis=-1)
        o_ref[pl.ds(t, rows, stride=S5_CHUNK), :] = piece


def _s5_mixer(p, ops, layer, geo):
    b, t, ctx = geo["b"], geo["t"], geo["ctx"]
    n_chunks = (ctx + t) // S5_CHUNK
    rows = n_chunks * b
    n_tok = p.shape[0]
    assert rows * S5_CHUNK == n_tok
    n_oct = S5_GROUPS // S5_OCTET
    state_w = S5_GROUPS * S5_STATE
    oct_w = S5_OCTET * S5_BLOCK
    oct_states = S5_OCTET * S5_STATE
    sub = 8
    assert (t // S5_CHUNK) % sub == 0 and (ctx // S5_CHUNK) % sub == 0
    p_spec = pl.BlockSpec((n_tok, LANE), lambda o: (0, COL_S5 // LANE + o))
    op_spec = pl.BlockSpec((None, S5_OCTET, S5_BLOCK, S5_BLOCK), lambda o: (layer, o, 0, 0))
    plane_spec = pl.BlockSpec((4, rows, oct_states), lambda o: (0, 0, o))

    u_spec = pl.BlockSpec((rows, oct_w), lambda o: (0, o))
    contrib, u_chunks = pl.pallas_call(
        _s5_contrib_kernel,
        grid=(n_oct,),
        in_specs=[p_spec, op_spec],
        out_specs=[plane_spec, u_spec],
        out_shape=[jax.ShapeDtypeStruct((4, rows, state_w), F32),
                   jax.ShapeDtypeStruct((rows, S5_GROUPS * S5_BLOCK), BF16)],
        scratch_shapes=[pltpu.VMEM((rows, oct_w), F32)],
        compiler_params=_cparams(("arbitrary",), 48 * 1024 * 1024),
        name="s5_contrib",
    )(p, ops["b_mat"])

    lane_blk = 512
    states = pl.pallas_call(
        functools.partial(_s5_scan_kernel, n_tiles=n_chunks // sub, n_ctx_tiles=ctx // S5_CHUNK // sub, nb=b),
        grid=(2, state_w // lane_blk),
        in_specs=[pl.BlockSpec((None, 2, rows, lane_blk), lambda d, l: (d, 0, 0, l)),
                  pl.BlockSpec((None, None, 1, lane_blk), lambda d, l: (layer, d, 0, l)),
                  pl.BlockSpec((None, None, 1, lane_blk), lambda d, l: (layer, d, 0, l))],
        out_specs=pl.BlockSpec((None, 2, rows, lane_blk), lambda d, l: (d, 0, 0, l)),
        out_shape=jax.ShapeDtypeStruct((2, 2, rows, state_w), F32),
        compiler_params=_cparams(("arbitrary", "arbitrary"), 40 * 1024 * 1024),
        name="s5_scan",
    )(contrib.reshape(2, 2, rows, state_w), ops["decay_re"], ops["decay_im"])

    return pl.pallas_call(
        _s5_out_kernel,
        grid=(n_oct,),
        in_specs=[u_spec, plane_spec, op_spec, op_spec],
        out_specs=pl.BlockSpec((n_tok, LANE), lambda o: (0, o)),
        out_shape=jax.ShapeDtypeStruct((n_tok, GROUP_WIDTH), F32),
        scratch_shapes=[pltpu.VMEM((rows, oct_w), F32)],
        compiler_params=_cparams(("arbitrary",), 48 * 1024 * 1024),
        name="s5_out",
    )(u_chunks, states.reshape(4, rows, state_w), ops["t_sum"], ops["c_mat_t"])


def _s5_operators(lam_re, lam_im, log_dt, b_re, b_im, c_re, c_im):
    hp = lax.Precision.HIGHEST
    g, pn, ni, lc = S5_GROUPS, S5_STATE, S5_GROUP, S5_CHUNK
    dt = jnp.exp(log_dt)[..., None]
    zr, zi = lam_re * dt, lam_im * dt
    up = jnp.arange(lc, dtype=F32)
    down = (lc - 1) - up

    def powers(d, steps):
        mag = jnp.exp(zr[d][None] * steps[:, None, None])
        ang = zi[d][None] * steps[:, None, None]
        return mag * jnp.cos(ang), mag * jnp.sin(ang)

    one = jnp.ones((1,), F32)
    z1 = [powers(d, one) for d in range(2)]
    nr = jnp.stack([z1[0][0][0], z1[1][0][0]]) - 1.0
    nim = jnp.stack([z1[0][1][0], z1[1][1][0]])
    den = lam_re * lam_re + lam_im * lam_im
    cr_, ci_ = (nr * lam_re + nim * lam_im) / den, (nim * lam_re - nr * lam_im) / den
    bz_r = cr_[..., None] * b_re - ci_[..., None] * b_im
    bz_i = cr_[..., None] * b_im + ci_[..., None] * b_re

    def lag_kernel(d, steps):
        pr, pi = powers(d, steps)
        m_r = pr[..., None] * bz_r[d][None] - pi[..., None] * bz_i[d][None]
        m_i = pr[..., None] * bz_i[d][None] + pi[..., None] * bz_r[d][None]
        k = (jnp.einsum("gop,dgpi->dgoi", c_re[d], m_r, precision=hp)
             - jnp.einsum("gop,dgpi->dgoi", c_im[d], m_i, precision=hp))
        return k.transpose(1, 3, 0, 2)

    k_f = lag_kernel(0, up)
    k_b = lag_kernel(1, down)
    two_sided = jnp.concatenate([k_b[:, :, :lc - 1], k_f[:, :, :1] + k_b[:, :, lc - 1:], k_f[:, :, 1:]], axis=2)
    two_sided = two_sided.reshape(g, ni, (2 * lc - 1) * ni)
    t_sum = jnp.stack([two_sided[:, :, (lc - 1 - s) * ni:(lc - 1 - s) * ni + S5_BLOCK] for s in range(lc)], axis=1)
    t_sum = t_sum.reshape(g, S5_BLOCK, S5_BLOCK)

    per_dir = lambda a: jnp.stack([a[0], a[0], a[1], a[1]])
    by_pos = lambda planes: jnp.stack(planes).transpose(2, 1, 0, 3).reshape(g, lc, 1, 4 * pn)
    by_chan = lambda a: per_dir(a).transpose(1, 2, 0, 3).reshape(g, 1, ni, 4 * pn)

    ef_r, ef_i = powers(0, down)
    eb_r, eb_i = powers(1, up)
    bz_ri = (bz_r.transpose(0, 1, 3, 2), bz_i.transpose(0, 1, 3, 2))
    b_mat = (by_pos([ef_r, ef_i, eb_r, eb_i]) * by_chan(bz_ri[0])
             + by_pos([-ef_i, ef_r, -eb_i, eb_r]) * by_chan(bz_ri[1])).reshape(g, S5_BLOCK, 4 * pn)

    pf_r, pf_i = powers(0, up + 1.0)
    pb_r, pb_i = powers(1, lc - up)
    c_mat_t = (by_pos([pf_r, -pf_i, pb_r, -pb_i]) * by_chan(c_re)
               + by_pos([-pf_i, -pf_r, -pb_i, -pb_r]) * by_chan(c_im)).reshape(g, S5_BLOCK, 4 * pn)

    full = jnp.full((1,), float(lc), F32)
    decay = [powers(d, full) for d in range(2)]
    return {"t_sum": t_sum.astype(BF16), "b_mat": b_mat.astype(BF16), "c_mat_t": c_mat_t.astype(BF16),
            "decay_re": jnp.stack([decay[0][0], decay[1][0]]).reshape(2, 1, g * pn),
            "decay_im": jnp.stack([decay[0][1], decay[1][1]]).reshape(2, 1, g * pn)}


def _group_norm_store(o_ref, k, y, w_ref):
    cols = slice(k * GROUP_WIDTH, (k + 1) * GROUP_WIDTH)
    o_ref[:, cols] = _rms(y, w_ref[:, cols]).astype(o_ref.dtype)


def _mix_kernel(ya_ref, yb_ref, gb_ref, gc_ref, u_ref, gcp_ref, up_ref, gcn_ref, un_ref,
                cw_ref, cb_ref, ys_ref, us_ref, ds_ref, gw_ref, gbias_ref, mw_ref, o_ref, *, tm, lat_rows, t, ctx):
    i = pl.program_id(0)
    _group_norm_store(o_ref, 0, ya_ref[...], mw_ref)
    _group_norm_store(o_ref, 1, yb_ref[...], mw_ref)

    r0 = i * tm
    in_lat = r0 < lat_rows
    seq_len = jnp.where(in_lat, t, ctx)
    off = jnp.where(in_lat, r0, r0 - lat_rows) % seq_len
    has_prev = off != 0
    has_next = off + tm != seq_len
    v = gc_ref[...] * u_ref[...]
    v_prev_row = jnp.where(has_prev, gcp_ref[7:8, :] * up_ref[7:8, :], 0.0)
    v_next_row = jnp.where(has_next, gcn_ref[0:1, :] * un_ref[0:1, :], 0.0)
    row = lax.broadcasted_iota(jnp.int32, v.shape, 0)
    v_prev = jnp.where(row == 0, v_prev_row, pltpu.roll(v, 1, 0))
    v_next = jnp.where(row == tm - 1, v_next_row, pltpu.roll(v, tm - 1, 0))
    conv = cw_ref[0:1, :] * v_prev + cw_ref[1:2, :] * v + cw_ref[2:3, :] * v_next + cb_ref[...]
    _group_norm_store(o_ref, 2, gb_ref[...] * conv, mw_ref)

    y = ys_ref[...] + ds_ref[...] * us_ref[...]
    g = 0.5 * y * (1.0 + jnp.tanh(math.sqrt(2.0 / math.pi) * (y + 0.044715 * (y * y * y))))
    gate = _sigmoid(_dot(g.astype(BF16), gw_ref[...]) + gbias_ref[...])
    _group_norm_store(o_ref, 3, g * gate, mw_ref)


def _mix_outputs(ya, yb, p, ys5, lw, layer, geo, n_tiles):
    n_tok = ya.shape[0]
    tm = 256
    gw = GROUP_WIDTH
    halo = 8
    n_halo_blocks = n_tok // halo
    tile = lambda col: pl.BlockSpec((tm, gw), lambda i: (i, col // gw))
    prev = lambda col: pl.BlockSpec((halo, gw), lambda i: (jnp.maximum(i * (tm // halo) - 1, 0), col // gw))
    nxt = lambda col: pl.BlockSpec(
        (halo, gw), lambda i: (jnp.minimum((i + 1) * (tm // halo), n_halo_blocks - 1), col // gw))
    full = lambda a: _layer_full_spec(a, layer)
    c_gb, c_gc, c_u = COL_CONV, COL_CONV + gw, COL_CONV + 2 * gw
    weights = [lw["conv_w"], lw["conv_b"]]
    glu = [lw["s5_d"], lw["s5_glu_w"], lw["s5_glu_b"], lw["mix_norm_w"]]
    return pl.pallas_call(
        functools.partial(_mix_kernel, tm=tm, lat_rows=geo["b"] * geo["t"], t=geo["t"], ctx=geo["ctx"]),
        grid=(n_tiles * geo["tm"] // tm,),
        in_specs=[tile(0), tile(0), tile(c_gb), tile(c_gc), tile(c_u),
                  prev(c_gc), prev(c_u), nxt(c_gc), nxt(c_u)]
        + [full(a) for a in weights] + [tile(0), tile(COL_S5)] + [full(a) for a in glu],
        out_specs=pl.BlockSpec((tm, 4 * gw), lambda i: (i, 0)),
        out_shape=jax.ShapeDtypeStruct((n_tiles * geo["tm"], 4 * gw), BF16),
        compiler_params=_cparams(("arbitrary",), 48 * 1024 * 1024),
        name="mix_outputs",
    )(ya, yb, p, p, p, p, p, p, p, *weights, ys5, p, *glu)


def _pad_last(a, width):
    return jnp.pad(a, [(0, 0)] * (a.ndim - 1) + [(0, width - a.shape[-1])])


def _stacked_weights(pr):
    n_layers = pr["mla_w_uq"].shape[0]
    w_uq = pr["mla_w_uq"].reshape(n_layers, MLA_Q_RANK, N_HEADS, MLA_QK)
    w_uq = _pad_last(w_uq, MLA_HEAD_PAD).reshape(n_layers, MLA_Q_RANK, N_HEADS * MLA_HEAD_PAD)
    w_ukv = pr["mla_w_ukv"].reshape(n_layers, MLA_KV_RANK, N_HEADS, MLA_NOPE + MLA_V)
    w_ukv = jnp.concatenate([w_ukv[..., :MLA_NOPE].reshape(n_layers, MLA_KV_RANK, -1),
                             w_ukv[..., MLA_NOPE:].reshape(n_layers, MLA_KV_RANK, -1)], axis=2)
    w_ukv = jnp.pad(w_ukv, ((0, 0), (0, MLA_KV_RANK_PAD - MLA_KV_RANK), (0, 0)))
    row = lambda a: a[:, None, :]
    return {
        "na_q_norm_w": row(pr["na_q_norm_w"]),
        "na_k_norm_w": row(pr["na_k_norm_w"]),
        "mla_cq_norm_w": row(pr["mla_cq_norm_w"]),
        "mla_ckv_norm_w": row(_pad_last(pr["mla_ckv_norm_w"], MLA_KV_RANK_PAD)),
        "mla_w_uq": w_uq.astype(BF16),
        "mla_w_ukv": w_ukv.astype(BF16),
        "mla_q_norm_w": row(_pad_last(pr["mla_q_norm_w"], MLA_HEAD_PAD)),
        "mla_k_norm_w": row(_pad_last(pr["mla_k_norm_w"], MLA_HEAD_PAD)),
        "conv_w": pr["conv_w"],
        "conv_b": row(pr["conv_b"]),
        "s5_d": row(pr["s5_d"]),
        "s5_glu_w": pr["s5_glu_w"].astype(BF16),
        "s5_glu_b": row(pr["s5_glu_b"]),
        "mix_norm_w": row(pr["mix_norm_w"]),
    }


def _layer_full_spec(a, layer):
    return pl.BlockSpec((None,) + a.shape[1:], lambda i: (layer,) + (0,) * (a.ndim - 1))


def _rope_tables(b, t, n_ctx):
    pos = jnp.arange(t, dtype=jnp.int32)
    row = (pos // GRID_W).astype(F32)
    col = (pos % GRID_W).astype(F32)
    n_freq = MLA_ROPE // 4
    inv_freq = ROPE_THETA ** (-jnp.arange(n_freq, dtype=F32) / n_freq)
    ang_r, ang_c = row[:, None] * inv_freq[None, :], col[:, None] * inv_freq[None, :]
    cos_r, sin_r, cos_c, sin_c = jnp.cos(ang_r), jnp.sin(ang_r), jnp.cos(ang_c), jnp.sin(ang_c)
    zeros = jnp.zeros((t, n_freq), F32)
    rest = LANE - MLA_ROPE
    cos_t = jnp.concatenate([cos_r, cos_r, cos_c, cos_c, jnp.ones((t, rest), F32)], axis=1)
    sina_t = jnp.concatenate([-sin_r, zeros, -sin_c, zeros, jnp.zeros((t, rest), F32)], axis=1)
    sinb_t = jnp.concatenate([zeros, sin_r, zeros, sin_c, jnp.zeros((t, rest), F32)], axis=1)
    n_c = b * n_ctx
    tables = []
    for tab, fill in ((cos_t, 1.0), (sina_t, 0.0), (sinb_t, 0.0)):
        tables.append(jnp.concatenate([jnp.tile(tab, (b, 1)), jnp.full((n_c, LANE), fill, F32)], axis=0))
    return tables


def _na_bias_tables(rpb):
    w = GRID_W
    qc = jnp.arange(w)
    cs = jnp.clip(qc - NA_WIN_COLS // 2, 0, w - NA_WIN_COLS)
    v_col = (qc[None, :] >= cs[:, None]) & (qc[None, :] < cs[:, None] + NA_WIN_COLS)
    d_col = jnp.clip(qc[None, :] - qc[:, None], -(NA_WIN_COLS - 1), NA_WIN_COLS - 1) + (NA_WIN_COLS - 1)
    sel_c = jax.nn.one_hot(d_col, 2 * NA_WIN_COLS - 1, dtype=F32)
    tab = jnp.einsum("hab,qkb->haqk", rpb * LOG2E, sel_c, precision=lax.Precision.HIGHEST)
    tab = jnp.where(v_col[None, None], tab, NEG_INF)
    tab = jnp.concatenate([tab, jnp.full((rpb.shape[0], 1, w, w), NEG_INF, F32)], axis=1)
    return jnp.concatenate([tab, tab], axis=-1)


def kernel(x, c, ctx, c_ctx, ada_w, ada_b, norm1_w, norm2_w, w_in, na_q_norm_w, na_k_norm_w, na_rpb, mla_cq_norm_w, mla_ckv_norm_w, mla_w_uq, mla_w_ukv, mla_q_norm_w, mla_k_norm_w, conv_w, conv_b, s5_lambda_re, s5_lambda_im, s5_log_dt, s5_b_re, s5_b_im, s5_c_re, s5_c_im, s5_d, s5_glu_w, s5_glu_b, mix_norm_w, w_out, ffn_w1, ffn_w3, ffn_w2):
    b, t, d = x.shape
    n_ctx = ctx.shape[1]
    n_layers = ada_w.shape[0]
    n_rows = t // GRID_W
    assert t % GRID_W == 0 and n_rows >= NA_K_ROWS and n_rows % NA_Q_ROWS == 0
    assert n_ctx == NA_Q_ROWS * GRID_W and t % n_ctx == 0 and b + 1 <= 8
    tm = 512 if (b * n_ctx) % 512 == 0 and t % 512 == 0 else 256
    tiles_per_batch = t // tm
    geo = {"b": b, "t": t, "ctx": n_ctx, "tm": tm,
           "tile_class": lambda i: jnp.minimum(i // tiles_per_batch, b)}
    lat_tiles = b * t // tm
    all_tiles = lat_tiles + b * n_ctx // tm

    pr = dict(na_q_norm_w=na_q_norm_w, na_k_norm_w=na_k_norm_w, mla_cq_norm_w=mla_cq_norm_w,
              mla_ckv_norm_w=mla_ckv_norm_w, mla_w_uq=mla_w_uq, mla_w_ukv=mla_w_ukv,
              mla_q_norm_w=mla_q_norm_w, mla_k_norm_w=mla_k_norm_w, conv_w=conv_w, conv_b=conv_b,
              s5_d=s5_d, s5_glu_w=s5_glu_w, s5_glu_b=s5_glu_b, mix_norm_w=mix_norm_w)

    h = (x.reshape(b * t, d), ctx.reshape(b * n_ctx, d))
    cvec = jnp.zeros((8, d), F32).at[:b].set(c).at[b].set(c_ctx)
    mod = _ada_mod(cvec, ada_w, ada_b, 0)
    rope = _rope_tables(b, t, n_ctx)
    lw = _stacked_weights(pr)
    na_bias = jax.vmap(_na_bias_tables)(na_rpb)
    s5_ops = jax.vmap(_s5_operators)(s5_lambda_re, s5_lambda_im, s5_log_dt, s5_b_re, s5_b_im, s5_c_re, s5_c_im)
    m_in = 1088 if (b * t + b * n_ctx) % 1088 == 0 else tm

    for l in range(n_layers):
        last = l == n_layers - 1
        n_tiles = lat_tiles if last else all_tiles

        a = _norm_modulate(h, norm1_w[l], mod, 0, 1, geo, all_tiles)
        p, w2_bf16 = _input_projection(a, w_in, ffn_w2, l, m_in)
        naq, nak, nav, mq, mk, mv = _prep_qkv(p, lw, l, rope, 256)
        ya = _na_attention(naq, nak, nav, na_bias, l, geo, not last)
        yb, mod_next = _mla_attention(mq, mk, mv, geo, not last,
                                      None if last else (cvec, ada_w, ada_b, l + 1))
        ys5 = _s5_mixer(p, s5_ops, l, geo)
        y = _mix_outputs(ya, yb, p, ys5, lw, l, geo, n_tiles)
        h = _matmul_gated_residual(y, w_out, h, mod, 2, geo, n_tiles, 1024, layer=l, single_buffer_w=True)

        f = _norm_modulate(h, norm2_w[l], mod, 3, 4, geo, n_tiles)
        g = _ffn_in(f, ffn_w1, ffn_w3, l, geo, n_tiles, 512)
        h = _matmul_gated_residual(g, w2_bf16, h, mod, 5, geo, n_tiles, 512)
        mod = mod_next

    return h.reshape(b, t, d)
```

```python
import functools
import math

import jax
import jax.numpy as jnp
from jax import lax
from jax.experimental import pallas as pl
from jax.experimental.pallas import tpu as pltpu

F32 = jnp.float32
BF16 = jnp.bfloat16

NORM_EPS = 1e-6
NEG_INF = -1e30
GRID_W = 64

GROUP_WIDTH = 1024
N_HEADS = 8
NA_HEAD_DIM = 128
NA_WIN_ROWS = 8
NA_WIN_COLS = 16
NA_Q_ROWS = 4
NA_K_ROWS = NA_Q_ROWS + NA_WIN_ROWS

MLA_NOPE = 128
MLA_ROPE = 64
MLA_QK = MLA_NOPE + MLA_ROPE
MLA_V = 128
MLA_Q_RANK = 896
MLA_KV_RANK = 320
MLA_KV_RANK_PAD = 512
MLA_HEAD_PAD = 256
V_HEAD_PAD = 256
LOG2E = math.log2(math.e)
ROPE_THETA = 10000.0

CONV_K = 3

S5_GROUPS = 64
S5_GROUP = 16
S5_STATE = 64
S5_CHUNK = 16
S5_BLOCK = S5_CHUNK * S5_GROUP


COL_NA = 0
COL_CONV = 3072
COL_S5 = 6144
COL_CQ = 7168
COL_CKV = 8192
IN_WIDTH_PAD = 8704
IN_TILE = 512
SRC_CQ = 3072
SRC_CKV = SRC_CQ + MLA_Q_RANK
SRC_CONV = SRC_CKV + MLA_KV_RANK + MLA_ROPE
SRC_S5 = SRC_CONV + 3 * GROUP_WIDTH

LANE = 128
VMEM_LIMIT = 56 * 1024 * 1024


def _cparams(sem, vmem=None):
    return pltpu.CompilerParams(dimension_semantics=sem, vmem_limit_bytes=vmem)


def _sigmoid(x):
    return 1.0 / (1.0 + jnp.exp(-x))


def _rms(x, w):
    return x * lax.rsqrt(jnp.mean(x * x, axis=-1, keepdims=True) + NORM_EPS) * w


def _dot(a, b):
    return jnp.dot(a, b, preferred_element_type=F32)


def _dot_nt(a, b):
    return lax.dot_general(a, b, (((1,), (1,)), ((), ())), preferred_element_type=F32)


def _ada_kernel(c_ref, w_ref, b_ref, o_ref):
    c = c_ref[...]
    s = (c * _sigmoid(c)).astype(BF16)
    o_ref[...] = _dot(s, w_ref[...].astype(BF16)) + b_ref[...]


def _ada_rows(out, d):
    return out.reshape(8, 6, d).transpose(1, 0, 2)[:, :, None, :]


def _ada_mod(cvec, ada_w, ada_b, layer):
    n_layers, d, n = ada_w.shape
    tn = 512
    out = pl.pallas_call(
        _ada_kernel,
        grid=(n // tn,),
        in_specs=[pl.BlockSpec((8, d), lambda j: (0, 0)),
                  pl.BlockSpec((None, d, tn), lambda j: (layer, 0, j)),
                  pl.BlockSpec((None, 1, tn), lambda j: (layer, 0, j))],
        out_specs=pl.BlockSpec((8, tn), lambda j: (0, j)),
        out_shape=jax.ShapeDtypeStruct((8, n), F32),
        compiler_params=_cparams(("arbitrary",), 40 * 1024 * 1024),
        name="ada_mod",
    )(cvec, ada_w, ada_b.reshape(n_layers, 1, n))
    return _ada_rows(out, d)


def _stream_parts(h):
    return tuple(h) if isinstance(h, (tuple, list)) else (h,)


def _stream_specs(parts, tm, width, row_of, col_of):
    if len(parts) == 1:
        return [pl.BlockSpec((tm, width), lambda *g: (row_of(*g), col_of(*g)))]
    lat_tiles = parts[0].shape[0] // tm
    return [pl.BlockSpec((tm, width), lambda *g: (jnp.minimum(row_of(*g), lat_tiles - 1), col_of(*g))),
            pl.BlockSpec((tm, width), lambda *g: (jnp.maximum(row_of(*g) - lat_tiles, 0), col_of(*g)))]


def _read_stream(h_refs, row_tile, lat_tiles):
    if len(h_refs) == 1:
        return h_refs[0][...]
    return jnp.where(row_tile < lat_tiles, h_refs[0][...], h_refs[1][...])


def _normmod_kernel(*refs, n_h, lat_tiles):
    h_refs = refs[:n_h]
    w_ref, shift_ref, scale_ref, o_ref = refs[n_h:]
    y = _rms(_read_stream(h_refs, pl.program_id(0), lat_tiles), w_ref[...])
    o_ref[...] = (y * (1.0 + scale_ref[...]) + shift_ref[...]).astype(o_ref.dtype)


def _norm_modulate(h, w, mod, k_shift, k_scale, geo, n_tiles):
    parts = _stream_parts(h)
    d = parts[0].shape[1]
    tm = geo["tm"]
    cls = geo["tile_class"]
    return pl.pallas_call(
        functools.partial(_normmod_kernel, n_h=len(parts), lat_tiles=parts[0].shape[0] // tm),
        grid=(n_tiles,),
        in_specs=_stream_specs(parts, tm, d, lambda i: i, lambda i: 0)
        + [pl.BlockSpec((1, d), lambda i: (0, 0)),
           pl.BlockSpec((None, None, 1, d), lambda i: (k_shift, cls(i), 0, 0)),
           pl.BlockSpec((None, None, 1, d), lambda i: (k_scale, cls(i), 0, 0))],
        out_specs=pl.BlockSpec((tm, d), lambda i: (i, 0)),
        out_shape=jax.ShapeDtypeStruct((n_tiles * tm, d), BF16),
        compiler_params=_cparams(("arbitrary",), VMEM_LIMIT),
        name="norm_modulate",
    )(*parts, w.reshape(1, d), mod, mod)


def _in_proj_kernel(a_ref, w_ref, w2_ref, o_ref, w2b_ref, wb_ref, *, n_row_tiles, n_w2_blocks):
    @pl.when(pl.program_id(1) == 0)
    def _():
        wb_ref[...] = w_ref[...].astype(BF16)

    o_ref[...] = _dot(a_ref[...], wb_ref[...])

    step = pl.program_id(0) * n_row_tiles + pl.program_id(1)

    @pl.when(step < n_w2_blocks)
    def _():
        w2b_ref[...] = w2_ref[...].astype(BF16)


def _in_proj_source_col(j):
    t = IN_TILE // LANE
    unit = jnp.where(j < 6, j * t,
           jnp.where(j < 12, SRC_CONV // LANE + (j - 6) * t,
           jnp.where(j < 14, SRC_S5 // LANE + (j - 12) * t,
           jnp.where(j < 16, SRC_CQ // LANE + (j - 14) * t, SRC_CKV // LANE))))
    return unit * LANE


def _input_projection(a, w_in, w2, layer, tm):
    m, k = a.shape
    tn = IN_TILE
    assert m % tm == 0 and COL_CKV + tn == IN_WIDTH_PAD and SRC_CKV + tn <= w_in.shape[-1]
    n_col_tiles, n_row_tiles = IN_WIDTH_PAD // tn, m // tm
    hidden, d_out = w2.shape[1:]
    slab = next(r for r in range(16, hidden + 1, 16) if hidden % r == 0 and hidden // r <= n_col_tiles * n_row_tiles)
    n_w2_blocks = hidden // slab
    w2_block = lambda j, i: jnp.minimum(j * n_row_tiles + i, n_w2_blocks - 1)
    return pl.pallas_call(
        functools.partial(_in_proj_kernel, n_row_tiles=n_row_tiles, n_w2_blocks=n_w2_blocks),
        grid=(n_col_tiles, n_row_tiles),
        in_specs=[pl.BlockSpec((tm, k), lambda j, i: (i, 0)),
                  pl.BlockSpec((pl.Element(k), pl.Element(tn)),
                               lambda j, i: (layer * k, _in_proj_source_col(j))),
                  pl.BlockSpec((None, slab, d_out), lambda j, i: (layer, w2_block(j, i), 0))],
        out_specs=[pl.BlockSpec((tm, tn), lambda j, i: (i, j)),
                   pl.BlockSpec((slab, d_out), lambda j, i: (w2_block(j, i), 0))],
        out_shape=[jax.ShapeDtypeStruct((m, IN_WIDTH_PAD), F32), jax.ShapeDtypeStruct((hidden, d_out), BF16)],
        scratch_shapes=[pltpu.VMEM((k, tn), BF16)],
        compiler_params=_cparams(("arbitrary", "arbitrary"), VMEM_LIMIT),
        name="input_projection",
    )(a, w_in.reshape(-1, w_in.shape[-1]), w2)


def _mm_res_kernel(a_ref, w_ref, *refs, n_h, lat_tiles, f32w):
    h_refs = refs[:n_h]
    g_ref, o_ref = refs[n_h], refs[n_h + 1]
    if f32w:
        wb_ref = refs[n_h + 2]

        @pl.when(pl.program_id(1) == 0)
        def _():
            wb_ref[...] = w_ref[...].astype(BF16)

        w = wb_ref[...]
    else:
        w = w_ref[...]
    h = _read_stream(h_refs, pl.program_id(1), lat_tiles)
    o_ref[...] = h + g_ref[...] * _dot(a_ref[...], w)


def _layer_weight_spec(w, layer, tn, single_buffer=False):
    mode = pl.Buffered(1) if single_buffer else None
    if w.ndim == 3:
        return pl.BlockSpec((None, w.shape[1], tn), lambda j, i: (layer, 0, j), pipeline_mode=mode)
    return pl.BlockSpec((w.shape[0], tn), lambda j, i: (0, j), pipeline_mode=mode)


def _matmul_gated_residual(a, w, h, mod, k_gate, geo, n_tiles, tn, layer=None, single_buffer_w=False):
    m, k = a.shape
    n = w.shape[-1]
    tm = geo["tm"]
    cls = geo["tile_class"]
    f32w = w.dtype == F32
    parts = _stream_parts(h)
    return pl.pallas_call(
        functools.partial(_mm_res_kernel, n_h=len(parts), lat_tiles=parts[0].shape[0] // tm, f32w=f32w),
        grid=(n // tn, n_tiles),
        in_specs=[pl.BlockSpec((tm, k), lambda j, i: (i, 0)),
                  _layer_weight_spec(w, layer, tn, single_buffer_w)]
        + _stream_specs(parts, tm, tn, lambda j, i: i, lambda j, i: j)
        + [pl.BlockSpec((None, None, 1, tn), lambda j, i: (k_gate, cls(i), 0, j))],
        out_specs=pl.BlockSpec((tm, tn), lambda j, i: (i, j)),
        out_shape=jax.ShapeDtypeStruct((n_tiles * tm, n), F32),
        scratch_shapes=[pltpu.VMEM((k, tn), BF16)] if f32w else [],
        compiler_params=_cparams(("arbitrary", "arbitrary"), VMEM_LIMIT),
        name="matmul_gated_residual",
    )(a, w, *parts, mod)


def _ffn1_kernel(a_ref, w1_ref, w3_ref, o_ref, w1b_ref, w3b_ref):
    @pl.when(pl.program_id(1) == 0)
    def _():
        w1b_ref[...] = w1_ref[...].astype(BF16)
        w3b_ref[...] = w3_ref[...].astype(BF16)

    a = a_ref[...]
    u = _dot(a, w1b_ref[...])
    v = _dot(a, w3b_ref[...])
    o_ref[...] = (u * _sigmoid(u) * v).astype(o_ref.dtype)


def _ffn_in(a, w1, w3, layer, geo, n_tiles, tn):
    m, k = a.shape
    n = w1.shape[-1]
    tm = geo["tm"]
    return pl.pallas_call(
        _ffn1_kernel,
        grid=(pl.cdiv(n, tn), n_tiles),
        in_specs=[pl.BlockSpec((tm, k), lambda j, i: (i, 0)),
                  _layer_weight_spec(w1, layer, tn),
                  _layer_weight_spec(w3, layer, tn)],
        out_specs=pl.BlockSpec((tm, tn), lambda j, i: (i, j)),
        out_shape=jax.ShapeDtypeStruct((m, n), BF16),
        scratch_shapes=[pltpu.VMEM((k, tn), BF16), pltpu.VMEM((k, tn), BF16)],
        compiler_params=_cparams(("arbitrary", "arbitrary"), VMEM_LIMIT),
        name="ffn_in",
    )(a, w1, w3)


def _rope_tail(t, cos_ref, sina_ref, sinb_ref):
    q = MLA_ROPE // 4
    return (t * cos_ref[...] + pltpu.roll(t, LANE - q, 1) * sina_ref[...]
            + pltpu.roll(t, q, 1) * sinb_ref[...])


def _prep_kernel(na_ref, cq_ref, ckvkr_ref, naqw_ref, nakw_ref, cqw_ref, ckvw_ref,
                 wuq_ref, wukv_ref, mqw_ref, mkw_ref, cos_ref, sina_ref, sinb_ref,
                 naq_ref, nak_ref, nav_ref, mq_ref, mk_ref, mv_ref):
    hd = NA_HEAD_DIM
    tm = na_ref.shape[0]
    ones_col = jnp.where(lax.broadcasted_iota(jnp.int32, (tm, LANE), 1) == 0, 1.0, 0.0).astype(BF16)
    na_qs = naqw_ref[...] * (NA_HEAD_DIM ** -0.5 * LOG2E)
    for h in range(N_HEADS):
        naq_ref[:, h * hd:(h + 1) * hd] = _rms(na_ref[:, h * hd:(h + 1) * hd], na_qs).astype(BF16)
        nak_ref[:, h * hd:(h + 1) * hd] = _rms(
            na_ref[:, GROUP_WIDTH + h * hd:GROUP_WIDTH + (h + 1) * hd], nakw_ref[...]).astype(BF16)
        vo = h * V_HEAD_PAD
        nav_ref[:, vo:vo + hd] = na_ref[:, 2 * GROUP_WIDTH + h * hd:2 * GROUP_WIDTH + (h + 1) * hd].astype(BF16)
        nav_ref[:, vo + hd:vo + V_HEAD_PAD] = ones_col

    cq = _rms(cq_ref[...], cqw_ref[...]).astype(BF16)
    q = _dot(cq, wuq_ref[...])
    inv_qk = 1.0 / MLA_QK
    mqw = mqw_ref[...] * (MLA_QK ** -0.5 * LOG2E)
    for h in range(N_HEADS):
        o = h * MLA_HEAD_PAD
        nope = q[:, o:o + MLA_NOPE]
        tail = q[:, o + MLA_NOPE:o + MLA_HEAD_PAD]
        ss = jnp.sum(nope * nope, axis=-1, keepdims=True) + jnp.sum(tail * tail, axis=-1, keepdims=True)
        r = lax.rsqrt(ss * inv_qk + NORM_EPS)
        mq_ref[:, o:o + MLA_NOPE] = (nope * r * mqw[:, :MLA_NOPE]).astype(BF16)
        mq_ref[:, o + MLA_NOPE:o + MLA_HEAD_PAD] = _rope_tail(
            tail * r * mqw[:, MLA_NOPE:], cos_ref, sina_ref, sinb_ref).astype(BF16)

    blk = ckvkr_ref[...]
    lane = lax.broadcasted_iota(jnp.int32, blk.shape, 1)
    ckv = jnp.where(lane < MLA_KV_RANK, blk, 0.0)
    ckv_ms = jnp.sum(ckv * ckv, axis=-1, keepdims=True) * (1.0 / MLA_KV_RANK)
    ckv_n = (ckv * lax.rsqrt(ckv_ms + NORM_EPS) * ckvw_ref[...]).astype(BF16)
    kv = _dot(ckv_n, wukv_ref[...])
    for h in range(N_HEADS):
        vo = h * V_HEAD_PAD
        mv_ref[:, vo:vo + MLA_V] = kv[:, GROUP_WIDTH + h * MLA_V:GROUP_WIDTH + (h + 1) * MLA_V].astype(BF16)
        mv_ref[:, vo + MLA_V:vo + V_HEAD_PAD] = ones_col
    kr = pltpu.roll(blk[:, 2 * LANE:3 * LANE], LANE - MLA_ROPE, 1)
    kr = jnp.where(lax.broadcasted_iota(jnp.int32, kr.shape, 1) < MLA_ROPE, kr, 0.0)
    kr_ss = jnp.sum(kr * kr, axis=-1, keepdims=True)
    mkw = mkw_ref[...]
    for h in range(N_HEADS):
        o = h * MLA_HEAD_PAD
        kn = kv[:, h * MLA_NOPE:(h + 1) * MLA_NOPE]
        r = lax.rsqrt((jnp.sum(kn * kn, axis=-1, keepdims=True) + kr_ss) * inv_qk + NORM_EPS)
        mk_ref[:, o:o + MLA_NOPE] = (kn * r * mkw[:, :MLA_NOPE]).astype(BF16)
        mk_ref[:, o + MLA_NOPE:o + MLA_HEAD_PAD] = _rope_tail(
            kr * r * mkw[:, MLA_NOPE:], cos_ref, sina_ref, sinb_ref).astype(BF16)


def _prep_qkv(p, lw, layer, rope, tm):
    n_tok = p.shape[0]
    blk = lambda width, col: pl.BlockSpec((tm, width), lambda i: (i, col // width))
    full = lambda a: _layer_full_spec(a, layer)
    tab = pl.BlockSpec((tm, LANE), lambda i: (i, 0))
    out_w = [GROUP_WIDTH, GROUP_WIDTH, N_HEADS * V_HEAD_PAD,
             N_HEADS * MLA_HEAD_PAD, N_HEADS * MLA_HEAD_PAD, N_HEADS * V_HEAD_PAD]
    weights = [lw["na_q_norm_w"], lw["na_k_norm_w"], lw["mla_cq_norm_w"], lw["mla_ckv_norm_w"],
               lw["mla_w_uq"], lw["mla_w_ukv"], lw["mla_q_norm_w"], lw["mla_k_norm_w"]]
    return pl.pallas_call(
        _prep_kernel,
        grid=(n_tok // tm,),
        in_specs=[blk(3 * GROUP_WIDTH, COL_NA), blk(MLA_Q_RANK, COL_CQ), blk(MLA_KV_RANK_PAD, COL_CKV)]
        + [full(a) for a in weights] + [tab, tab, tab],
        out_specs=[pl.BlockSpec((tm, w), lambda i: (i, 0)) for w in out_w],
        out_shape=[jax.ShapeDtypeStruct((n_tok, w), BF16) for w in out_w],
        compiler_params=_cparams(("arbitrary",), 48 * 1024 * 1024),
        name="prep_qkv",
    )(p, p, p, *weights, *rope)


def _softmax_attend(q, pairs, bias=None):
    scores = []
    for idx, (k, _) in enumerate(pairs):
        s = _dot_nt(q, k)
        if idx == 0 and bias is not None:
            s = s + bias
        scores.append(s)
    m = scores[0].max(axis=-1, keepdims=True)
    for s in scores[1:]:
        m = jnp.maximum(m, s.max(axis=-1, keepdims=True))
    acc = None
    for s, (_, v) in zip(scores, pairs):
        o = _dot(jnp.exp2(s - m).astype(BF16), v)
        acc = o if acc is None else acc + o
    dv = acc.shape[-1] // 2
    return acc[:, :dv] / acc[:, dv:dv + 1]


NA_INVALID_ROW = 2 * NA_WIN_ROWS - 1


def _na_window_bias(bt_ref, rb, n_rows):
    half = NA_WIN_ROWS // 2
    ks_row = jnp.clip(rb * NA_Q_ROWS - half, 0, n_rows - NA_K_ROWS)
    lane_lo = lax.broadcasted_iota(jnp.int32, (GRID_W, LANE), 1) < GRID_W
    strips = []
    for i in range(NA_Q_ROWS):
        rq = rb * NA_Q_ROWS + i
        rs = jnp.clip(rq - half, 0, n_rows - NA_WIN_ROWS)

        def slab(j, rq=rq, rs=rs):
            rk = ks_row + j
            valid = (rk >= rs) & (rk < rs + NA_WIN_ROWS)
            return bt_ref[jnp.where(valid, rk - rq + (NA_WIN_ROWS - 1), NA_INVALID_ROW)]

        pieces = [jnp.where(lane_lo, slab(2 * m), slab(2 * m + 1)) for m in range(NA_K_ROWS // 2)]
        strips.append(jnp.concatenate(pieces, axis=-1))
    return jnp.concatenate(strips, axis=0)


def _na_kernel(q_ref, k_ref, v_ref, kc_ref, vc_ref, kca_ref, vca_ref, bt_ref, o_ref, *,
               n_lat_steps, steps_per_batch, n_rows, n_sub, ctx, hps):
    i = pl.program_id(1)
    sq = NA_Q_ROWS * GRID_W
    dq, dv = NA_HEAD_DIM, V_HEAD_PAD

    @pl.when(i < n_lat_steps)
    def _():
        step = i % steps_per_batch
        for hh in range(hps):
            qc, vc = slice(hh * dq, (hh + 1) * dq), slice(hh * dv, (hh + 1) * dv)
            for sb in range(n_sub):
                rb = step * n_sub + sb
                ks = jnp.clip(rb * NA_Q_ROWS - NA_WIN_ROWS // 2, 0, n_rows - NA_K_ROWS) * GRID_W
                ks = pl.multiple_of(ks, GRID_W)
                kw = k_ref[pl.ds(ks, NA_K_ROWS * GRID_W), qc]
                vw = v_ref[pl.ds(ks, NA_K_ROWS * GRID_W), vc]
                rows = slice(sb * sq, (sb + 1) * sq)
                o_ref[rows, qc] = _softmax_attend(q_ref[rows, qc], [(kw, vw), (kc_ref[:, qc], vc_ref[:, vc])],
                                                  _na_window_bias(bt_ref.at[hh], rb, n_rows))

    @pl.when(i == n_lat_steps)
    def _():
        for hh in range(hps):
            qc, vc = slice(hh * dq, (hh + 1) * dq), slice(hh * dv, (hh + 1) * dv)
            for bb in range(q_ref.shape[0] // ctx):
                rows = slice(bb * ctx, (bb + 1) * ctx)
                o_ref[rows, qc] = _softmax_attend(q_ref[rows, qc], [(kca_ref[rows, qc], vca_ref[rows, vc])])


def _na_attention(naq, nak, nav, bias_tab, layer, geo, with_ctx):
    b, t, ctx = geo["b"], geo["t"], geo["ctx"]
    n_tok = naq.shape[0]
    sq = NA_Q_ROWS * GRID_W
    tq = b * ctx
    assert tq % sq == 0 and t % tq == 0
    n_sub = tq // sq
    n_rows = t // GRID_W
    spb = t // tq
    lat_steps = b * spb
    ctx_block0 = b * t // ctx
    batch = lambda i: jnp.minimum(i // spb, b - 1)
    hps = 2
    dq, dv = hps * NA_HEAD_DIM, hps * V_HEAD_PAD

    return pl.pallas_call(
        functools.partial(_na_kernel, n_lat_steps=lat_steps, steps_per_batch=spb, n_rows=n_rows,
                          n_sub=n_sub, ctx=ctx, hps=hps),
        grid=(N_HEADS // hps, lat_steps + (1 if with_ctx else 0)),
        in_specs=[pl.BlockSpec((tq, dq), lambda h, i: (i, h)),
                  pl.BlockSpec((t, dq), lambda h, i: (batch(i), h)),
                  pl.BlockSpec((t, dv), lambda h, i: (batch(i), h)),
                  pl.BlockSpec((ctx, dq), lambda h, i: (ctx_block0 + batch(i), h)),
                  pl.BlockSpec((ctx, dv), lambda h, i: (ctx_block0 + batch(i), h)),
                  pl.BlockSpec((tq, dq), lambda h, i: (lat_steps, h)),
                  pl.BlockSpec((tq, dv), lambda h, i: (lat_steps, h)),
                  pl.BlockSpec((None, hps) + bias_tab.shape[2:], lambda h, i: (layer, h, 0, 0, 0))],
        out_specs=pl.BlockSpec((tq, dq), lambda h, i: (i, h)),
        out_shape=jax.ShapeDtypeStruct((n_tok, GROUP_WIDTH), F32),
        compiler_params=_cparams(("arbitrary", "arbitrary"), 40 * 1024 * 1024),
        name="na_attention",
    )(naq, nak, nav, nak, nav, nak, nav, bias_tab)


def _flash_attend(q, loads):
    m = None
    acc = None
    for load in loads:
        k, v = load()
        s = _dot_nt(q, k)
        mc = s.max(axis=-1, keepdims=True)
        if m is None:
            m = mc
            acc = _dot(jnp.exp2(s - m).astype(BF16), v)
        else:
            m_new = jnp.maximum(m, mc)
            acc = acc * jnp.exp2(m - m_new) + _dot(jnp.exp2(s - m_new).astype(BF16), v)
            m = m_new
    dv = acc.shape[-1] // 2
    return acc[:, :dv] / acc[:, dv:dv + 1]


def _mla_kernel(q_ref, k_ref, v_ref, kc_ref, vc_ref, kca_ref, vca_ref, *refs,
                n_q_blocks, key_chunk, ctx, hps, n_ada_blocks):
    i = pl.program_id(1)
    dq, dv = MLA_HEAD_PAD, V_HEAD_PAD
    if n_ada_blocks:
        c_ref, aw_ref, ab_ref, o_ref, mod_ref = refs
        step = pl.program_id(0) * pl.num_programs(1) + i

        @pl.when(step < n_ada_blocks)
        def _():
            _ada_kernel(c_ref, aw_ref, ab_ref, mod_ref)
    else:
        (o_ref,) = refs

    @pl.when(i < n_q_blocks)
    def _():
        for hh in range(hps):
            qc, vc = slice(hh * dq, (hh + 1) * dq), slice(hh * dv, (hh + 1) * dv)

            def lat_chunk(j, qc=qc, vc=vc):
                rows = slice(j * key_chunk, (j + 1) * key_chunk)
                return lambda: (k_ref[rows, qc], v_ref[rows, vc])

            loads = [lat_chunk(j) for j in range(k_ref.shape[0] // key_chunk)]
            loads.append(lambda qc=qc, vc=vc: (kc_ref[:, qc], vc_ref[:, vc]))
            o_ref[:, hh * MLA_V:(hh + 1) * MLA_V] = _flash_attend(q_ref[:, qc], loads)

    @pl.when(i == n_q_blocks)
    def _():
        for hh in range(hps):
            qc, vc = slice(hh * dq, (hh + 1) * dq), slice(hh * dv, (hh + 1) * dv)
            for bb in range(q_ref.shape[0] // ctx):
                rows = slice(bb * ctx, (bb + 1) * ctx)
                o_ref[rows, hh * MLA_V:(hh + 1) * MLA_V] = _flash_attend(
                    q_ref[rows, qc], [lambda: (kca_ref[rows, qc], vca_ref[rows, vc])])


def _mla_attention(mq, mk, mv, geo, with_ctx, ada=None):
    b, t, ctx = geo["b"], geo["t"], geo["ctx"]
    n_tok = mq.shape[0]
    tq = b * ctx
    assert t % tq == 0
    nqb = t // tq
    lat_blocks = b * nqb
    ctx_block0 = b * t // ctx
    batch = lambda i: jnp.minimum(i // nqb, b - 1)
    hps = 2
    dq, dv = hps * MLA_HEAD_PAD, hps * V_HEAD_PAD
    grid = (N_HEADS // hps, lat_blocks + (1 if with_ctx else 0))

    in_specs = [pl.BlockSpec((tq, dq), lambda h, i: (i, h)),
                pl.BlockSpec((t, dq), lambda h, i: (batch(i), h)),
                pl.BlockSpec((t, dv), lambda h, i: (batch(i), h)),
                pl.BlockSpec((ctx, dq), lambda h, i: (ctx_block0 + batch(i), h)),
                pl.BlockSpec((ctx, dv), lambda h, i: (ctx_block0 + batch(i), h)),
                pl.BlockSpec((tq, dq), lambda h, i: (lat_blocks, h)),
                pl.BlockSpec((tq, dv), lambda h, i: (lat_blocks, h))]
    operands = [mq, mk, mv, mk, mv, mk, mv]
    out_specs = [pl.BlockSpec((tq, hps * MLA_V), lambda h, i: (i, h))]
    out_shape = [jax.ShapeDtypeStruct((n_tok, GROUP_WIDTH), F32)]
    n_ada_blocks = 0
    if ada is not None:
        cvec, ada_w, ada_b, layer = ada
        n_layers, d, n = ada_w.shape
        n_steps = grid[0] * grid[1]
        n_ada_blocks = max(k for k in range(1, n_steps + 1) if n % (k * LANE) == 0)
        tn = n // n_ada_blocks
        slab = lambda h, i: jnp.minimum(h * grid[1] + i, n_ada_blocks - 1)
        in_specs += [pl.BlockSpec((8, d), lambda h, i: (0, 0)),
                     pl.BlockSpec((None, d, tn), lambda h, i: (layer, 0, slab(h, i))),
                     pl.BlockSpec((None, 1, tn), lambda h, i: (layer, 0, slab(h, i)))]
        operands += [cvec, ada_w, ada_b.reshape(n_layers, 1, n)]
        out_specs.append(pl.BlockSpec((8, tn), lambda h, i: (0, slab(h, i))))
        out_shape.append(jax.ShapeDtypeStruct((8, n), F32))

    outs = pl.pallas_call(
        functools.partial(_mla_kernel, n_q_blocks=lat_blocks, key_chunk=512, ctx=ctx, hps=hps,
                          n_ada_blocks=n_ada_blocks),
        grid=grid,
        in_specs=in_specs,
        out_specs=out_specs,
        out_shape=out_shape,
        compiler_params=_cparams(("arbitrary", "arbitrary"), 48 * 1024 * 1024),
        name="mla_attention",
    )(*operands)
    if ada is None:
        return outs[0], None
    return outs[0], _ada_rows(outs[1], ada[1].shape[1])


S5_OCTET = LANE // S5_GROUP


def _s5_chunk_permutation():
    r = jnp.arange(S5_CHUNK * LANE)
    s, k, i = r // LANE, (r % LANE) // S5_GROUP, r % S5_GROUP
    dst = k * S5_BLOCK + s * S5_GROUP + i
    return (dst[:, None] == jnp.arange(S5_OCTET * S5_BLOCK)[None, :]).astype(BF16)


def _s5_contrib_kernel(p_ref, perm_ref, b_ref, o_ref, ub_ref, x_ref):
    rows = x_ref.shape[0]
    for s in range(S5_CHUNK):
        x_ref[:, s * LANE:(s + 1) * LANE] = p_ref[pl.ds(s, rows, stride=S5_CHUNK), :].astype(BF16)
    ub_ref[...] = _dot(x_ref[...], perm_ref[...]).astype(BF16)
    r = [_dot(ub_ref[:, k * S5_BLOCK:(k + 1) * S5_BLOCK], b_ref[k]) for k in range(S5_OCTET)]
    ps = S5_STATE
    for plane in range(4):
        o_ref[plane] = jnp.concatenate([rk[:, plane * ps:(plane + 1) * ps] for rk in r], axis=-1)


def _s5_scan_kernel(c_ref, lr_ref, li_ref, x_ref, *, n_tiles, n_ctx_tiles, nb):
    sub = 8
    n_lat_tiles = n_tiles - n_ctx_tiles
    lanes = c_ref.shape[-1]
    lr = jnp.broadcast_to(lr_ref[...], (sub, lanes))
    li = jnp.broadcast_to(li_ref[...], (sub, lanes))
    row = lax.broadcasted_iota(jnp.int32, (sub, lanes), 0)

    def run(reverse):
        order = range(sub - 1, -1, -1) if reverse else range(sub)

        def body(jt, carry):
            if reverse:
                tile = jnp.where(jt < n_ctx_tiles, n_ctx_tiles - 1 - jt, n_tiles - 1 - (jt - n_ctx_tiles))
            else:
                tile = jt
            new = []
            for bi in range(nb):
                xr, xi = carry[2 * bi], carry[2 * bi + 1]
                row_tile = jnp.where(tile < n_ctx_tiles, nb * n_lat_tiles + bi * n_ctx_tiles + tile,
                                     bi * n_lat_tiles + tile - n_ctx_tiles)
                r0 = pl.multiple_of(row_tile * sub, sub)
                cr = c_ref[0, pl.ds(r0, sub), :]
                ci = c_ref[1, pl.ds(r0, sub), :]
                out_r = jnp.zeros_like(cr)
                out_i = jnp.zeros_like(ci)
                for k in order:
                    out_r = jnp.where(row == k, xr, out_r)
                    out_i = jnp.where(row == k, xi, out_i)
                    ck_r = jnp.broadcast_to(cr[k:k + 1, :], (sub, lanes))
                    ck_i = jnp.broadcast_to(ci[k:k + 1, :], (sub, lanes))
                    xr, xi = lr * xr - li * xi + ck_r, lr * xi + li * xr + ck_i
                x_ref[0, pl.ds(r0, sub), :] = out_r
                x_ref[1, pl.ds(r0, sub), :] = out_i
                new += [xr, xi]
            return tuple(new)

        zero = jnp.zeros((sub, lanes), F32)
        lax.fori_loop(0, n_tiles, body, (zero,) * (2 * nb))

    @pl.when(pl.program_id(0) == 0)
    def _():
        run(False)

    @pl.when(pl.program_id(0) == 1)
    def _():
        run(True)


def _s5_out_kernel(u_ref, x_ref, t_ref, ct_ref, o_ref, y_ref):
    rows = u_ref.shape[0]
    ps = S5_STATE
    for k in range(S5_OCTET):
        x = jnp.concatenate([x_ref[plane][:, k * ps:(k + 1) * ps] for plane in range(4)], axis=-1)
        cols = slice(k * S5_BLOCK, (k + 1) * S5_BLOCK)
        y_ref[:, cols] = _dot(u_ref[:, cols], t_ref[k]) + _dot_nt(x.astype(BF16), ct_ref[k])
    for t in range(S5_CHUNK):
        piece = jnp.concatenate([y_ref[:, k * S5_BLOCK + t * S5_GROUP:k * S5_BLOCK + (t + 1) * S5_GROUP]
                                 for k in range(S5_OCTET)], axis=-1)
        o_ref[pl.ds(t, rows, stride=S5_CHUNK), :] = piece


def _s5_mixer(p, ops, layer, geo):
    b, t, ctx = geo["b"], geo["t"], geo["ctx"]
    n_chunks = (ctx + t) // S5_CHUNK
    rows = n_chunks * b
    n_tok = p.shape[0]
    assert rows * S5_CHUNK == n_tok
    n_oct = S5_GROUPS // S5_OCTET
    state_w = S5_GROUPS * S5_STATE
    oct_w = S5_OCTET * S5_BLOCK
    oct_states = S5_OCTET * S5_STATE
    sub = 8
    assert (t // S5_CHUNK) % sub == 0 and (ctx // S5_CHUNK) % sub == 0
    p_spec = pl.BlockSpec((n_tok, LANE), lambda o: (0, COL_S5 // LANE + o))
    op_spec = pl.BlockSpec((None, S5_OCTET, S5_BLOCK, S5_BLOCK), lambda o: (layer, o, 0, 0))
    plane_spec = pl.BlockSpec((4, rows, oct_states), lambda o: (0, 0, o))

    u_spec = pl.BlockSpec((rows, oct_w), lambda o: (0, o))
    contrib, u_chunks = pl.pallas_call(
        _s5_contrib_kernel,
        grid=(n_oct,),
        in_specs=[p_spec, pl.BlockSpec((S5_CHUNK * LANE, oct_w), lambda o: (0, 0)), op_spec],
        out_specs=[plane_spec, u_spec],
        out_shape=[jax.ShapeDtypeStruct((4, rows, state_w), F32),
                   jax.ShapeDtypeStruct((rows, S5_GROUPS * S5_BLOCK), BF16)],
        scratch_shapes=[pltpu.VMEM((rows, S5_CHUNK * LANE), BF16)],
        compiler_params=_cparams(("arbitrary",), 48 * 1024 * 1024),
        name="s5_contrib",
    )(p, _s5_chunk_permutation(), ops["b_mat"])

    lane_blk = 512
    states = pl.pallas_call(
        functools.partial(_s5_scan_kernel, n_tiles=n_chunks // sub, n_ctx_tiles=ctx // S5_CHUNK // sub, nb=b),
        grid=(2, state_w // lane_blk),
        in_specs=[pl.BlockSpec((None, 2, rows, lane_blk), lambda d, l: (d, 0, 0, l)),
                  pl.BlockSpec((None, None, 1, lane_blk), lambda d, l: (layer, d, 0, l)),
                  pl.BlockSpec((None, None, 1, lane_blk), lambda d, l: (layer, d, 0, l))],
        out_specs=pl.BlockSpec((None, 2, rows, lane_blk), lambda d, l: (d, 0, 0, l)),
        out_shape=jax.ShapeDtypeStruct((2, 2, rows, state_w), F32),
        compiler_params=_cparams(("arbitrary", "arbitrary"), 40 * 1024 * 1024),
        name="s5_scan",
    )(contrib.reshape(2, 2, rows, state_w), ops["decay_re"], ops["decay_im"])

    return pl.pallas_call(
        _s5_out_kernel,
        grid=(n_oct,),
        in_specs=[u_spec, plane_spec, op_spec, op_spec],
        out_specs=pl.BlockSpec((n_tok, LANE), lambda o: (0, o)),
        out_shape=jax.ShapeDtypeStruct((n_tok, GROUP_WIDTH), F32),
        scratch_shapes=[pltpu.VMEM((rows, oct_w), F32)],
        compiler_params=_cparams(("arbitrary",), 48 * 1024 * 1024),
        name="s5_out",
    )(u_chunks, states.reshape(4, rows, state_w), ops["t_sum"], ops["c_mat_t"])


def _s5_operators(lam_re, lam_im, log_dt, b_re, b_im, c_re, c_im):
    hp = lax.Precision.HIGHEST
    g, pn, ni, lc = S5_GROUPS, S5_STATE, S5_GROUP, S5_CHUNK
    dt = jnp.exp(log_dt)[..., None]
    zr, zi = lam_re * dt, lam_im * dt
    up = jnp.arange(lc, dtype=F32)
    down = (lc - 1) - up

    def powers(d, steps):
        mag = jnp.exp(zr[d][None] * steps[:, None, None])
        ang = zi[d][None] * steps[:, None, None]
        return mag * jnp.cos(ang), mag * jnp.sin(ang)

    one = jnp.ones((1,), F32)
    z1 = [powers(d, one) for d in range(2)]
    nr = jnp.stack([z1[0][0][0], z1[1][0][0]]) - 1.0
    nim = jnp.stack([z1[0][1][0], z1[1][1][0]])
    den = lam_re * lam_re + lam_im * lam_im
    cr_, ci_ = (nr * lam_re + nim * lam_im) / den, (nim * lam_re - nr * lam_im) / den
    bz_r = cr_[..., None] * b_re - ci_[..., None] * b_im
    bz_i = cr_[..., None] * b_im + ci_[..., None] * b_re

    def lag_kernel(d, steps):
        pr, pi = powers(d, steps)
        m_r = pr[..., None] * bz_r[d][None] - pi[..., None] * bz_i[d][None]
        m_i = pr[..., None] * bz_i[d][None] + pi[..., None] * bz_r[d][None]
        k = (jnp.einsum("gop,dgpi->dgoi", c_re[d], m_r, precision=hp)
             - jnp.einsum("gop,dgpi->dgoi", c_im[d], m_i, precision=hp))
        return k.transpose(1, 3, 0, 2)

    k_f = lag_kernel(0, up)
    k_b = lag_kernel(1, down)
    two_sided = jnp.concatenate([k_b[:, :, :lc - 1], k_f[:, :, :1] + k_b[:, :, lc - 1:], k_f[:, :, 1:]], axis=2)
    two_sided = two_sided.reshape(g, ni, (2 * lc - 1) * ni)
    t_sum = jnp.stack([two_sided[:, :, (lc - 1 - s) * ni:(lc - 1 - s) * ni + S5_BLOCK] for s in range(lc)], axis=1)
    t_sum = t_sum.reshape(g, S5_BLOCK, S5_BLOCK)

    per_dir = lambda a: jnp.stack([a[0], a[0], a[1], a[1]])
    by_pos = lambda planes: jnp.stack(planes).transpose(2, 1, 0, 3).reshape(g, lc, 1, 4 * pn)
    by_chan = lambda a: per_dir(a).transpose(1, 2, 0, 3).reshape(g, 1, ni, 4 * pn)

    ef_r, ef_i = powers(0, down)
    eb_r, eb_i = powers(1, up)
    bz_ri = (bz_r.transpose(0, 1, 3, 2), bz_i.transpose(0, 1, 3, 2))
    b_mat = (by_pos([ef_r, ef_i, eb_r, eb_i]) * by_chan(bz_ri[0])
             + by_pos([-ef_i, ef_r, -eb_i, eb_r]) * by_chan(bz_ri[1])).reshape(g, S5_BLOCK, 4 * pn)

    pf_r, pf_i = powers(0, up + 1.0)
    pb_r, pb_i = powers(1, lc - up)
    c_mat_t = (by_pos([pf_r, -pf_i, pb_r, -pb_i]) * by_chan(c_re)
               + by_pos([-pf_i, -pf_r, -pb_i, -pb_r]) * by_chan(c_im)).reshape(g, S5_BLOCK, 4 * pn)

    full = jnp.full((1,), float(lc), F32)
    decay = [powers(d, full) for d in range(2)]
    return {"t_sum": t_sum.astype(BF16), "b_mat": b_mat.astype(BF16), "c_mat_t": c_mat_t.astype(BF16),
            "decay_re": jnp.stack([decay[0][0], decay[1][0]]).reshape(2, 1, g * pn),
            "decay_im": jnp.stack([decay[0][1], decay[1][1]]).reshape(2, 1, g * pn)}


def _group_norm_store(o_ref, k, y, w_ref):
    cols = slice(k * GROUP_WIDTH, (k + 1) * GROUP_WIDTH)
    o_ref[:, cols] = _rms(y, w_ref[:, cols]).astype(o_ref.dtype)


def _mix_kernel(ya_ref, yb_ref, gb_ref, gc_ref, u_ref, gcp_ref, up_ref, gcn_ref, un_ref,
                cw_ref, cb_ref, ys_ref, us_ref, ds_ref, gw_ref, gbias_ref, mw_ref, o_ref, *, tm, lat_rows, t, ctx):
    i = pl.program_id(0)
    _group_norm_store(o_ref, 0, ya_ref[...], mw_ref)
    _group_norm_store(o_ref, 1, yb_ref[...], mw_ref)

    r0 = i * tm
    in_lat = r0 < lat_rows
    seq_len = jnp.where(in_lat, t, ctx)
    off = jnp.where(in_lat, r0, r0 - lat_rows) % seq_len
    has_prev = off != 0
    has_next = off + tm != seq_len
    v = gc_ref[...] * u_ref[...]
    v_prev_row = jnp.where(has_prev, gcp_ref[7:8, :] * up_ref[7:8, :], 0.0)
    v_next_row = jnp.where(has_next, gcn_ref[0:1, :] * un_ref[0:1, :], 0.0)
    row = lax.broadcasted_iota(jnp.int32, v.shape, 0)
    v_prev = jnp.where(row == 0, v_prev_row, pltpu.roll(v, 1, 0))
    v_next = jnp.where(row == tm - 1, v_next_row, pltpu.roll(v, tm - 1, 0))
    conv = cw_ref[0:1, :] * v_prev + cw_ref[1:2, :] * v + cw_ref[2:3, :] * v_next + cb_ref[...]
    _group_norm_store(o_ref, 2, gb_ref[...] * conv, mw_ref)

    y = ys_ref[...] + ds_ref[...] * us_ref[...]
    g = 0.5 * y * (1.0 + jnp.tanh(math.sqrt(2.0 / math.pi) * (y + 0.044715 * (y * y * y))))
    gate = _sigmoid(_dot(g.astype(BF16), gw_ref[...]) + gbias_ref[...])
    _group_norm_store(o_ref, 3, g * gate, mw_ref)


def _mix_outputs(ya, yb, p, ys5, lw, layer, geo, n_tiles):
    n_tok = ya.shape[0]
    tm = 256
    gw = GROUP_WIDTH
    halo = 8
    n_halo_blocks = n_tok // halo
    tile = lambda col: pl.BlockSpec((tm, gw), lambda i: (i, col // gw))
    prev = lambda col: pl.BlockSpec((halo, gw), lambda i: (jnp.maximum(i * (tm // halo) - 1, 0), col // gw))
    nxt = lambda col: pl.BlockSpec(
        (halo, gw), lambda i: (jnp.minimum((i + 1) * (tm // halo), n_halo_blocks - 1), col // gw))
    full = lambda a: _layer_full_spec(a, layer)
    c_gb, c_gc, c_u = COL_CONV, COL_CONV + gw, COL_CONV + 2 * gw
    weights = [lw["conv_w"], lw["conv_b"]]
    glu = [lw["s5_d"], lw["s5_glu_w"], lw["s5_glu_b"], lw["mix_norm_w"]]
    return pl.pallas_call(
        functools.partial(_mix_kernel, tm=tm, lat_rows=geo["b"] * geo["t"], t=geo["t"], ctx=geo["ctx"]),
        grid=(n_tiles * geo["tm"] // tm,),
        in_specs=[tile(0), tile(0), tile(c_gb), tile(c_gc), tile(c_u),
                  prev(c_gc), prev(c_u), nxt(c_gc), nxt(c_u)]
        + [full(a) for a in weights] + [tile(0), tile(COL_S5)] + [full(a) for a in glu],
        out_specs=pl.BlockSpec((tm, 4 * gw), lambda i: (i, 0)),
        out_shape=jax.ShapeDtypeStruct((n_tiles * geo["tm"], 4 * gw), BF16),
        compiler_params=_cparams(("arbitrary",), 48 * 1024 * 1024),
        name="mix_outputs",
    )(ya, yb, p, p, p, p, p, p, p, *weights, ys5, p, *glu)


def _pad_last(a, width):
    return jnp.pad(a, [(0, 0)] * (a.ndim - 1) + [(0, width - a.shape[-1])])


def _stacked_weights(pr):
    n_layers = pr["mla_w_uq"].shape[0]
    w_uq = pr["mla_w_uq"].reshape(n_layers, MLA_Q_RANK, N_HEADS, MLA_QK)
    w_uq = _pad_last(w_uq, MLA_HEAD_PAD).reshape(n_layers, MLA_Q_RANK, N_HEADS * MLA_HEAD_PAD)
    w_ukv = pr["mla_w_ukv"].reshape(n_layers, MLA_KV_RANK, N_HEADS, MLA_NOPE + MLA_V)
    w_ukv = jnp.concatenate([w_ukv[..., :MLA_NOPE].reshape(n_layers, MLA_KV_RANK, -1),
                             w_ukv[..., MLA_NOPE:].reshape(n_layers, MLA_KV_RANK, -1)], axis=2)
    w_ukv = jnp.pad(w_ukv, ((0, 0), (0, MLA_KV_RANK_PAD - MLA_KV_RANK), (0, 0)))
    row = lambda a: a[:, None, :]
    return {
        "na_q_norm_w": row(pr["na_q_norm_w"]),
        "na_k_norm_w": row(pr["na_k_norm_w"]),
        "mla_cq_norm_w": row(pr["mla_cq_norm_w"]),
        "mla_ckv_norm_w": row(_pad_last(pr["mla_ckv_norm_w"], MLA_KV_RANK_PAD)),
        "mla_w_uq": w_uq.astype(BF16),
        "mla_w_ukv": w_ukv.astype(BF16),
        "mla_q_norm_w": row(_pad_last(pr["mla_q_norm_w"], MLA_HEAD_PAD)),
        "mla_k_norm_w": row(_pad_last(pr["mla_k_norm_w"], MLA_HEAD_PAD)),
        "conv_w": pr["conv_w"],
        "conv_b": row(pr["conv_b"]),
        "s5_d": row(pr["s5_d"]),
        "s5_glu_w": pr["s5_glu_w"].astype(BF16),
        "s5_glu_b": row(pr["s5_glu_b"]),
        "mix_norm_w": row(pr["mix_norm_w"]),
    }


def _layer_full_spec(a, layer):
    return pl.BlockSpec((None,) + a.shape[1:], lambda i: (layer,) + (0,) * (a.ndim - 1))


def _rope_tables(b, t, n_ctx):
    pos = jnp.arange(t, dtype=jnp.int32)
    row = (pos // GRID_W).astype(F32)
    col = (pos % GRID_W).astype(F32)
    n_freq = MLA_ROPE // 4
    inv_freq = ROPE_THETA ** (-jnp.arange(n_freq, dtype=F32) / n_freq)
    ang_r, ang_c = row[:, None] * inv_freq[None, :], col[:, None] * inv_freq[None, :]
    cos_r, sin_r, cos_c, sin_c = jnp.cos(ang_r), jnp.sin(ang_r), jnp.cos(ang_c), jnp.sin(ang_c)
    zeros = jnp.zeros((t, n_freq), F32)
    rest = LANE - MLA_ROPE
    cos_t = jnp.concatenate([cos_r, cos_r, cos_c, cos_c, jnp.ones((t, rest), F32)], axis=1)
    sina_t = jnp.concatenate([-sin_r, zeros, -sin_c, zeros, jnp.zeros((t, rest), F32)], axis=1)
    sinb_t = jnp.concatenate([zeros, sin_r, zeros, sin_c, jnp.zeros((t, rest), F32)], axis=1)
    n_c = b * n_ctx
    tables = []
    for tab, fill in ((cos_t, 1.0), (sina_t, 0.0), (sinb_t, 0.0)):
        tables.append(jnp.concatenate([jnp.tile(tab, (b, 1)), jnp.full((n_c, LANE), fill, F32)], axis=0))
    return tables


def _na_bias_tables(rpb):
    w = GRID_W
    qc = jnp.arange(w)
    cs = jnp.clip(qc - NA_WIN_COLS // 2, 0, w - NA_WIN_COLS)
    v_col = (qc[None, :] >= cs[:, None]) & (qc[None, :] < cs[:, None] + NA_WIN_COLS)
    d_col = jnp.clip(qc[None, :] - qc[:, None], -(NA_WIN_COLS - 1), NA_WIN_COLS - 1) + (NA_WIN_COLS - 1)
    sel_c = jax.nn.one_hot(d_col, 2 * NA_WIN_COLS - 1, dtype=F32)
    tab = jnp.einsum("hab,qkb->haqk", rpb * LOG2E, sel_c, precision=lax.Precision.HIGHEST)
    tab = jnp.where(v_col[None, None], tab, NEG_INF)
    tab = jnp.concatenate([tab, jnp.full((rpb.shape[0], 1, w, w), NEG_INF, F32)], axis=1)
    return jnp.concatenate([tab, tab], axis=-1)


def kernel(x, c, ctx, c_ctx, ada_w, ada_b, norm1_w, norm2_w, w_in, na_q_norm_w, na_k_norm_w, na_rpb, mla_cq_norm_w, mla_ckv_norm_w, mla_w_uq, mla_w_ukv, mla_q_norm_w, mla_k_norm_w, conv_w, conv_b, s5_lambda_re, s5_lambda_im, s5_log_dt, s5_b_re, s5_b_im, s5_c_re, s5_c_im, s5_d, s5_glu_w, s5_glu_b, mix_norm_w, w_out, ffn_w1, ffn_w3, ffn_w2):
    b, t, d = x.shape
    n_ctx = ctx.shape[1]
    n_layers = ada_w.shape[0]
    n_rows = t // GRID_W
    assert t % GRID_W == 0 and n_rows >= NA_K_ROWS and n_rows % NA_Q_ROWS == 0
    assert n_ctx == NA_Q_ROWS * GRID_W and t % n_ctx == 0 and b + 1 <= 8
    tm = 512 if (b * n_ctx) % 512 == 0 and t % 512 == 0 else 256
    tiles_per_batch = t // tm
    geo = {"b": b, "t": t, "ctx": n_ctx, "tm": tm,
           "tile_class": lambda i: jnp.minimum(i // tiles_per_batch, b)}
    lat_tiles = b * t // tm
    all_tiles = lat_tiles + b * n_ctx // tm

    pr = dict(na_q_norm_w=na_q_norm_w, na_k_norm_w=na_k_norm_w, mla_cq_norm_w=mla_cq_norm_w,
              mla_ckv_norm_w=mla_ckv_norm_w, mla_w_uq=mla_w_uq, mla_w_ukv=mla_w_ukv,
              mla_q_norm_w=mla_q_norm_w, mla_k_norm_w=mla_k_norm_w, conv_w=conv_w, conv_b=conv_b,
              s5_d=s5_d, s5_glu_w=s5_glu_w, s5_glu_b=s5_glu_b, mix_norm_w=mix_norm_w)

    h = (x.reshape(b * t, d), ctx.reshape(b * n_ctx, d))
    cvec = jnp.zeros((8, d), F32).at[:b].set(c).at[b].set(c_ctx)
    mod = _ada_mod(cvec, ada_w, ada_b, 0)
    rope = _rope_tables(b, t, n_ctx)
    lw = _stacked_weights(pr)
    na_bias = jax.vmap(_na_bias_tables)(na_rpb)
    s5_ops = jax.vmap(_s5_operators)(s5_lambda_re, s5_lambda_im, s5_log_dt, s5_b_re, s5_b_im, s5_c_re, s5_c_im)
    m_in = 1088 if (b * t + b * n_ctx) % 1088 == 0 else tm

    for l in range(n_layers):
        last = l == n_layers - 1
        n_tiles = lat_tiles if last else all_tiles

        a = _norm_modulate(h, norm1_w[l], mod, 0, 1, geo, all_tiles)
        p, w2_bf16 = _input_projection(a, w_in, ffn_w2, l, m_in)
        naq, nak, nav, mq, mk, mv = _prep_qkv(p, lw, l, rope, 256)
        ya = _na_attention(naq, nak, nav, na_bias, l, geo, not last)
        yb, mod_next = _mla_attention(mq, mk, mv, geo, not last,
                                      None if last else (cvec, ada_w, ada_b, l + 1))
        ys5 = _s5_mixer(p, s5_ops, l, geo)
        y = _mix_outputs(ya, yb, p, ys5, lw, l, geo, n_tiles)
        h = _matmul_gated_residual(y, w_out, h, mod, 2, geo, n_tiles, 1024, layer=l, single_buffer_w=True)

        f = _norm_modulate(h, norm2_w[l], mod, 3, 4, geo, n_tiles)
        g = _ffn_in(f, ffn_w1, ffn_w3, l, geo, n_tiles, 512)
        h = _matmul_gated_residual(g, w2_bf16, h, mod, 5, geo, n_tiles, 512)
        mod = mod_next

    return h.reshape(b, t, d)
```

```python
import functools
import math

import jax
import jax.numpy as jnp
from jax import lax
from jax.experimental import pallas as pl
from jax.experimental.pallas import tpu as pltpu

F32 = jnp.float32
BF16 = jnp.bfloat16

NORM_EPS = 1e-6
NEG_INF = -1e30
GRID_W = 64

GROUP_WIDTH = 1024
N_HEADS = 8
NA_HEAD_DIM = 128
NA_WIN_ROWS = 8
NA_WIN_COLS = 16
NA_Q_ROWS = 4
NA_K_ROWS = NA_Q_ROWS + NA_WIN_ROWS

MLA_NOPE = 128
MLA_ROPE = 64
MLA_QK = MLA_NOPE + MLA_ROPE
MLA_V = 128
MLA_Q_RANK = 896
MLA_KV_RANK = 320
MLA_KV_RANK_PAD = 512
MLA_HEAD_PAD = 256
V_HEAD_PAD = 256
LOG2E = math.log2(math.e)
ROPE_THETA = 10000.0

CONV_K = 3

S5_GROUPS = 64
S5_GROUP = 16
S5_STATE = 64
S5_CHUNK = 16
S5_BLOCK = S5_CHUNK * S5_GROUP


COL_NA = 0
COL_CONV = 3072
COL_S5 = 6144
COL_CQ = 7168
COL_CKV = 8192
IN_WIDTH_PAD = 8704
IN_TILE = 512
SRC_CQ = 3072
SRC_CKV = SRC_CQ + MLA_Q_RANK
SRC_CONV = SRC_CKV + MLA_KV_RANK + MLA_ROPE
SRC_S5 = SRC_CONV + 3 * GROUP_WIDTH

LANE = 128
VMEM_LIMIT = 56 * 1024 * 1024


def _cparams(sem, vmem=None):
    return pltpu.CompilerParams(dimension_semantics=sem, vmem_limit_bytes=vmem)


def _sigmoid(x):
    return 1.0 / (1.0 + jnp.exp(-x))


def _rms(x, w):
    return x * lax.rsqrt(jnp.mean(x * x, axis=-1, keepdims=True) + NORM_EPS) * w


def _dot(a, b):
    return jnp.dot(a, b, preferred_element_type=F32)


def _dot_nt(a, b):
    return lax.dot_general(a, b, (((1,), (1,)), ((), ())), preferred_element_type=F32)


def _ada_kernel(c_ref, w_ref, b_ref, o_ref):
    c = c_ref[...]
    s = (c * _sigmoid(c)).astype(BF16)
    o_ref[...] = _dot(s, w_ref[...].astype(BF16)) + b_ref[...]


def _ada_rows(out, d):
    return out.reshape(8, 6, d).transpose(1, 0, 2)[:, :, None, :]


def _ada_mod(cvec, ada_w, ada_b, layer):
    n_layers, d, n = ada_w.shape
    tn = 512
    out = pl.pallas_call(
        _ada_kernel,
        grid=(n // tn,),
        in_specs=[pl.BlockSpec((8, d), lambda j: (0, 0)),
                  pl.BlockSpec((None, d, tn), lambda j: (layer, 0, j)),
                  pl.BlockSpec((None, 1, tn), lambda j: (layer, 0, j))],
        out_specs=pl.BlockSpec((8, tn), lambda j: (0, j)),
        out_shape=jax.ShapeDtypeStruct((8, n), F32),
        compiler_params=_cparams(("arbitrary",), 40 * 1024 * 1024),
        name="ada_mod",
    )(cvec, ada_w, ada_b.reshape(n_layers, 1, n))
    return _ada_rows(out, d)


def _stream_parts(h):
    return tuple(h) if isinstance(h, (tuple, list)) else (h,)


def _stream_specs(parts, tm, width, row_of, col_of):
    if len(parts) == 1:
        return [pl.BlockSpec((tm, width), lambda *g: (row_of(*g), col_of(*g)))]
    lat_tiles = parts[0].shape[0] // tm
    return [pl.BlockSpec((tm, width), lambda *g: (jnp.minimum(row_of(*g), lat_tiles - 1), col_of(*g))),
            pl.BlockSpec((tm, width), lambda *g: (jnp.maximum(row_of(*g) - lat_tiles, 0), col_of(*g)))]


def _read_stream(h_refs, row_tile, lat_tiles):
    if len(h_refs) == 1:
        return h_refs[0][...]
    return jnp.where(row_tile < lat_tiles, h_refs[0][...], h_refs[1][...])


def _normmod_kernel(*refs, n_h, lat_tiles):
    h_refs = refs[:n_h]
    w_ref, shift_ref, scale_ref, o_ref = refs[n_h:]
    y = _rms(_read_stream(h_refs, pl.program_id(0), lat_tiles), w_ref[...])
    o_ref[...] = (y * (1.0 + scale_ref[...]) + shift_ref[...]).astype(o_ref.dtype)


def _norm_modulate(h, w, mod, k_shift, k_scale, geo, n_tiles):
    parts = _stream_parts(h)
    d = parts[0].shape[1]
    tm = geo["tm"]
    cls = geo["tile_class"]
    return pl.pallas_call(
        functools.partial(_normmod_kernel, n_h=len(parts), lat_tiles=parts[0].shape[0] // tm),
        grid=(n_tiles,),
        in_specs=_stream_specs(parts, tm, d, lambda i: i, lambda i: 0)
        + [pl.BlockSpec((1, d), lambda i: (0, 0)),
           pl.BlockSpec((None, None, 1, d), lambda i: (k_shift, cls(i), 0, 0)),
           pl.BlockSpec((None, None, 1, d), lambda i: (k_scale, cls(i), 0, 0))],
        out_specs=pl.BlockSpec((tm, d), lambda i: (i, 0)),
        out_shape=jax.ShapeDtypeStruct((n_tiles * tm, d), BF16),
        compiler_params=_cparams(("arbitrary",), VMEM_LIMIT),
        name="norm_modulate",
    )(*parts, w.reshape(1, d), mod, mod)


def _in_proj_kernel(a_ref, w_ref, o_ref, wb_ref):
    @pl.when(pl.program_id(1) == 0)
    def _():
        wb_ref[...] = w_ref[...].astype(BF16)

    o_ref[...] = _dot(a_ref[...], wb_ref[...])


def _in_proj_source_col(j):
    t = IN_TILE // LANE
    unit = jnp.where(j < 6, j * t,
           jnp.where(j < 12, SRC_CONV // LANE + (j - 6) * t,
           jnp.where(j < 14, SRC_S5 // LANE + (j - 12) * t,
           jnp.where(j < 16, SRC_CQ // LANE + (j - 14) * t, SRC_CKV // LANE))))
    return unit * LANE


def _input_projection(a, w_in, layer, tm):
    m, k = a.shape
    tn = IN_TILE
    assert m % tm == 0 and COL_CKV + tn == IN_WIDTH_PAD and SRC_CKV + tn <= w_in.shape[-1]
    return pl.pallas_call(
        _in_proj_kernel,
        grid=(IN_WIDTH_PAD // tn, m // tm),
        in_specs=[pl.BlockSpec((tm, k), lambda j, i: (i, 0)),
                  pl.BlockSpec((pl.Element(k), pl.Element(tn)),
                               lambda j, i: (layer * k, _in_proj_source_col(j)))],
        out_specs=pl.BlockSpec((tm, tn), lambda j, i: (i, j)),
        out_shape=jax.ShapeDtypeStruct((m, IN_WIDTH_PAD), F32),
        scratch_shapes=[pltpu.VMEM((k, tn), BF16)],
        compiler_params=_cparams(("arbitrary", "arbitrary"), VMEM_LIMIT),
        name="input_projection",
    )(a, w_in.reshape(-1, w_in.shape[-1]))


def _mm_res_kernel(a_ref, w_ref, *refs, n_h, lat_tiles, f32w):
    h_refs = refs[:n_h]
    g_ref, o_ref = refs[n_h], refs[n_h + 1]
    if f32w:
        wb_ref = refs[n_h + 2]

        @pl.when(pl.program_id(1) == 0)
        def _():
            wb_ref[...] = w_ref[...].astype(BF16)

        w = wb_ref[...]
    else:
        w = w_ref[...]
    h = _read_stream(h_refs, pl.program_id(1), lat_tiles)
    o_ref[...] = h + g_ref[...] * _dot(a_ref[...], w)


def _layer_weight_spec(w, layer, tn, single_buffer=False):
    mode = pl.Buffered(1) if single_buffer else None
    if w.ndim == 3:
        return pl.BlockSpec((None, w.shape[1], tn), lambda j, i: (layer, 0, j), pipeline_mode=mode)
    return pl.BlockSpec((w.shape[0], tn), lambda j, i: (0, j), pipeline_mode=mode)


def _matmul_gated_residual(a, w, h, mod, k_gate, geo, n_tiles, tn, layer=None, single_buffer_w=False):
    m, k = a.shape
    n = w.shape[-1]
    tm = geo["tm"]
    cls = geo["tile_class"]
    f32w = w.dtype == F32
    parts = _stream_parts(h)
    return pl.pallas_call(
        functools.partial(_mm_res_kernel, n_h=len(parts), lat_tiles=parts[0].shape[0] // tm, f32w=f32w),
        grid=(n // tn, n_tiles),
        in_specs=[pl.BlockSpec((tm, k), lambda j, i: (i, 0)),
                  _layer_weight_spec(w, layer, tn, single_buffer_w)]
        + _stream_specs(parts, tm, tn, lambda j, i: i, lambda j, i: j)
        + [pl.BlockSpec((None, None, 1, tn), lambda j, i: (k_gate, cls(i), 0, j))],
        out_specs=pl.BlockSpec((tm, tn), lambda j, i: (i, j)),
        out_shape=jax.ShapeDtypeStruct((n_tiles * tm, n), F32),
        scratch_shapes=[pltpu.VMEM((k, tn), BF16)] if f32w else [],
        compiler_params=_cparams(("arbitrary", "arbitrary"), VMEM_LIMIT),
        name="matmul_gated_residual",
    )(a, w, *parts, mod)


def _ffn1_kernel(a_ref, w1_ref, w3_ref, o_ref, w1b_ref, w3b_ref):
    @pl.when(pl.program_id(1) == 0)
    def _():
        w1b_ref[...] = w1_ref[...].astype(BF16)
        w3b_ref[...] = w3_ref[...].astype(BF16)

    a = a_ref[...]
    u = _dot(a, w1b_ref[...])
    v = _dot(a, w3b_ref[...])
    o_ref[...] = (u * _sigmoid(u) * v).astype(o_ref.dtype)


def _ffn_in(a, w1, w3, layer, geo, n_tiles, tn):
    m, k = a.shape
    n = w1.shape[-1]
    tm = geo["tm"]
    return pl.pallas_call(
        _ffn1_kernel,
        grid=(pl.cdiv(n, tn), n_tiles),
        in_specs=[pl.BlockSpec((tm, k), lambda j, i: (i, 0)),
                  _layer_weight_spec(w1, layer, tn),
                  _layer_weight_spec(w3, layer, tn)],
        out_specs=pl.BlockSpec((tm, tn), lambda j, i: (i, j)),
        out_shape=jax.ShapeDtypeStruct((m, n), BF16),
        scratch_shapes=[pltpu.VMEM((k, tn), BF16), pltpu.VMEM((k, tn), BF16)],
        compiler_params=_cparams(("arbitrary", "arbitrary"), VMEM_LIMIT),
        name="ffn_in",
    )(a, w1, w3)


def _rope_tail(t, cos_ref, sina_ref, sinb_ref):
    q = MLA_ROPE // 4
    return (t * cos_ref[...] + pltpu.roll(t, LANE - q, 1) * sina_ref[...]
            + pltpu.roll(t, q, 1) * sinb_ref[...])


def _prep_kernel(na_ref, cq_ref, ckvkr_ref, naqw_ref, nakw_ref, cqw_ref, ckvw_ref,
                 wuq_ref, wukv_ref, mqw_ref, mkw_ref, cos_ref, sina_ref, sinb_ref,
                 naq_ref, nak_ref, nav_ref, mq_ref, mk_ref, mv_ref):
    hd = NA_HEAD_DIM
    tm = na_ref.shape[0]
    ones_col = jnp.where(lax.broadcasted_iota(jnp.int32, (tm, LANE), 1) == 0, 1.0, 0.0).astype(BF16)
    na_qs = naqw_ref[...] * (NA_HEAD_DIM ** -0.5 * LOG2E)
    for h in range(N_HEADS):
        naq_ref[:, h * hd:(h + 1) * hd] = _rms(na_ref[:, h * hd:(h + 1) * hd], na_qs).astype(BF16)
        nak_ref[:, h * hd:(h + 1) * hd] = _rms(
            na_ref[:, GROUP_WIDTH + h * hd:GROUP_WIDTH + (h + 1) * hd], nakw_ref[...]).astype(BF16)
        vo = h * V_HEAD_PAD
        nav_ref[:, vo:vo + hd] = na_ref[:, 2 * GROUP_WIDTH + h * hd:2 * GROUP_WIDTH + (h + 1) * hd].astype(BF16)
        nav_ref[:, vo + hd:vo + V_HEAD_PAD] = ones_col

    cq = _rms(cq_ref[...], cqw_ref[...]).astype(BF16)
    q = _dot(cq, wuq_ref[...])
    inv_qk = 1.0 / MLA_QK
    mqw = mqw_ref[...] * (MLA_QK ** -0.5 * LOG2E)
    for h in range(N_HEADS):
        o = h * MLA_HEAD_PAD
        nope = q[:, o:o + MLA_NOPE]
        tail = q[:, o + MLA_NOPE:o + MLA_HEAD_PAD]
        ss = jnp.sum(nope * nope, axis=-1, keepdims=True) + jnp.sum(tail * tail, axis=-1, keepdims=True)
        r = lax.rsqrt(ss * inv_qk + NORM_EPS)
        mq_ref[:, o:o + MLA_NOPE] = (nope * r * mqw[:, :MLA_NOPE]).astype(BF16)
        mq_ref[:, o + MLA_NOPE:o + MLA_HEAD_PAD] = _rope_tail(
            tail * r * mqw[:, MLA_NOPE:], cos_ref, sina_ref, sinb_ref).astype(BF16)

    blk = ckvkr_ref[...]
    lane = lax.broadcasted_iota(jnp.int32, blk.shape, 1)
    ckv = jnp.where(lane < MLA_KV_RANK, blk, 0.0)
    ckv_ms = jnp.sum(ckv * ckv, axis=-1, keepdims=True) * (1.0 / MLA_KV_RANK)
    ckv_n = (ckv * lax.rsqrt(ckv_ms + NORM_EPS) * ckvw_ref[...]).astype(BF16)
    kv = _dot(ckv_n, wukv_ref[...])
    for h in range(N_HEADS):
        vo = h * V_HEAD_PAD
        mv_ref[:, vo:vo + MLA_V] = kv[:, GROUP_WIDTH + h * MLA_V:GROUP_WIDTH + (h + 1) * MLA_V].astype(BF16)
        mv_ref[:, vo + MLA_V:vo + V_HEAD_PAD] = ones_col
    kr = pltpu.roll(blk[:, 2 * LANE:3 * LANE], LANE - MLA_ROPE, 1)
    kr = jnp.where(lax.broadcasted_iota(jnp.int32, kr.shape, 1) < MLA_ROPE, kr, 0.0)
    kr_ss = jnp.sum(kr * kr, axis=-1, keepdims=True)
    mkw = mkw_ref[...]
    for h in range(N_HEADS):
        o = h * MLA_HEAD_PAD
        kn = kv[:, h * MLA_NOPE:(h + 1) * MLA_NOPE]
        r = lax.rsqrt((jnp.sum(kn * kn, axis=-1, keepdims=True) + kr_ss) * inv_qk + NORM_EPS)
        mk_ref[:, o:o + MLA_NOPE] = (kn * r * mkw[:, :MLA_NOPE]).astype(BF16)
        mk_ref[:, o + MLA_NOPE:o + MLA_HEAD_PAD] = _rope_tail(
            kr * r * mkw[:, MLA_NOPE:], cos_ref, sina_ref, sinb_ref).astype(BF16)


def _prep_qkv(p, lw, layer, rope, tm):
    n_tok = p.shape[0]
    blk = lambda width, col: pl.BlockSpec((tm, width), lambda i: (i, col // width))
    full = lambda a: _layer_full_spec(a, layer)
    tab = pl.BlockSpec((tm, LANE), lambda i: (i, 0))
    out_w = [GROUP_WIDTH, GROUP_WIDTH, N_HEADS * V_HEAD_PAD,
             N_HEADS * MLA_HEAD_PAD, N_HEADS * MLA_HEAD_PAD, N_HEADS * V_HEAD_PAD]
    weights = [lw["na_q_norm_w"], lw["na_k_norm_w"], lw["mla_cq_norm_w"], lw["mla_ckv_norm_w"],
               lw["mla_w_uq"], lw["mla_w_ukv"], lw["mla_q_norm_w"], lw["mla_k_norm_w"]]
    return pl.pallas_call(
        _prep_kernel,
        grid=(n_tok // tm,),
        in_specs=[blk(3 * GROUP_WIDTH, COL_NA), blk(MLA_Q_RANK, COL_CQ), blk(MLA_KV_RANK_PAD, COL_CKV)]
        + [full(a) for a in weights] + [tab, tab, tab],
        out_specs=[pl.BlockSpec((tm, w), lambda i: (i, 0)) for w in out_w],
        out_shape=[jax.ShapeDtypeStruct((n_tok, w), BF16) for w in out_w],
        compiler_params=_cparams(("arbitrary",), 48 * 1024 * 1024),
        name="prep_qkv",
    )(p, p, p, *weights, *rope)


def _softmax_attend(q, pairs, bias=None):
    scores = []
    for idx, (k, _) in enumerate(pairs):
        s = _dot_nt(q, k)
        if idx == 0 and bias is not None:
            s = s + bias
        scores.append(s)
    m = scores[0].max(axis=-1, keepdims=True)
    for s in scores[1:]:
        m = jnp.maximum(m, s.max(axis=-1, keepdims=True))
    acc = None
    for s, (_, v) in zip(scores, pairs):
        o = _dot(jnp.exp2(s - m).astype(BF16), v)
        acc = o if acc is None else acc + o
    dv = acc.shape[-1] // 2
    return acc[:, :dv] / acc[:, dv:dv + 1]


NA_INVALID_ROW = 2 * NA_WIN_ROWS - 1


def _na_window_bias(bt_ref, rb, n_rows):
    half = NA_WIN_ROWS // 2
    ks_row = jnp.clip(rb * NA_Q_ROWS - half, 0, n_rows - NA_K_ROWS)
    lane_lo = lax.broadcasted_iota(jnp.int32, (GRID_W, LANE), 1) < GRID_W
    strips = []
    for i in range(NA_Q_ROWS):
        rq = rb * NA_Q_ROWS + i
        rs = jnp.clip(rq - half, 0, n_rows - NA_WIN_ROWS)

        def slab(j, rq=rq, rs=rs):
            rk = ks_row + j
            valid = (rk >= rs) & (rk < rs + NA_WIN_ROWS)
            return bt_ref[jnp.where(valid, rk - rq + (NA_WIN_ROWS - 1), NA_INVALID_ROW)]

        pieces = [jnp.where(lane_lo, slab(2 * m), slab(2 * m + 1)) for m in range(NA_K_ROWS // 2)]
        strips.append(jnp.concatenate(pieces, axis=-1))
    return jnp.concatenate(strips, axis=0)


def _na_kernel(q_ref, k_ref, v_ref, kc_ref, vc_ref, kca_ref, vca_ref, bt_ref, o_ref, *,
               n_lat_steps, steps_per_batch, n_rows, n_sub, ctx, hps):
    i = pl.program_id(1)
    sq = NA_Q_ROWS * GRID_W
    dq, dv = NA_HEAD_DIM, V_HEAD_PAD

    @pl.when(i < n_lat_steps)
    def _():
        step = i % steps_per_batch
        for hh in range(hps):
            qc, vc = slice(hh * dq, (hh + 1) * dq), slice(hh * dv, (hh + 1) * dv)
            for sb in range(n_sub):
                rb = step * n_sub + sb
                ks = jnp.clip(rb * NA_Q_ROWS - NA_WIN_ROWS // 2, 0, n_rows - NA_K_ROWS) * GRID_W
                ks = pl.multiple_of(ks, GRID_W)
                kw = k_ref[pl.ds(ks, NA_K_ROWS * GRID_W), qc]
                vw = v_ref[pl.ds(ks, NA_K_ROWS * GRID_W), vc]
                rows = slice(sb * sq, (sb + 1) * sq)
                o_ref[rows, qc] = _softmax_attend(q_ref[rows, qc], [(kw, vw), (kc_ref[:, qc], vc_ref[:, vc])],
                                                  _na_window_bias(bt_ref.at[hh], rb, n_rows))

    @pl.when(i == n_lat_steps)
    def _():
        for hh in range(hps):
            qc, vc = slice(hh * dq, (hh + 1) * dq), slice(hh * dv, (hh + 1) * dv)
            for bb in range(q_ref.shape[0] // ctx):
                rows = slice(bb * ctx, (bb + 1) * ctx)
                o_ref[rows, qc] = _softmax_attend(q_ref[rows, qc], [(kca_ref[rows, qc], vca_ref[rows, vc])])


def _na_attention(naq, nak, nav, bias_tab, layer, geo, with_ctx):
    b, t, ctx = geo["b"], geo["t"], geo["ctx"]
    n_tok = naq.shape[0]
    sq = NA_Q_ROWS * GRID_W
    tq = b * ctx
    assert tq % sq == 0 and t % tq == 0
    n_sub = tq // sq
    n_rows = t // GRID_W
    spb = t // tq
    lat_steps = b * spb
    ctx_block0 = b * t // ctx
    batch = lambda i: jnp.minimum(i // spb, b - 1)
    hps = 2
    dq, dv = hps * NA_HEAD_DIM, hps * V_HEAD_PAD

    return pl.pallas_call(
        functools.partial(_na_kernel, n_lat_steps=lat_steps, steps_per_batch=spb, n_rows=n_rows,
                          n_sub=n_sub, ctx=ctx, hps=hps),
        grid=(N_HEADS // hps, lat_steps + (1 if with_ctx else 0)),
        in_specs=[pl.BlockSpec((tq, dq), lambda h, i: (i, h)),
                  pl.BlockSpec((t, dq), lambda h, i: (batch(i), h)),
                  pl.BlockSpec((t, dv), lambda h, i: (batch(i), h)),
                  pl.BlockSpec((ctx, dq), lambda h, i: (ctx_block0 + batch(i), h)),
                  pl.BlockSpec((ctx, dv), lambda h, i: (ctx_block0 + batch(i), h)),
                  pl.BlockSpec((tq, dq), lambda h, i: (lat_steps, h)),
                  pl.BlockSpec((tq, dv), lambda h, i: (lat_steps, h)),
                  pl.BlockSpec((None, hps) + bias_tab.shape[2:], lambda h, i: (layer, h, 0, 0, 0))],
        out_specs=pl.BlockSpec((tq, dq), lambda h, i: (i, h)),
        out_shape=jax.ShapeDtypeStruct((n_tok, GROUP_WIDTH), F32),
        compiler_params=_cparams(("arbitrary", "arbitrary"), 40 * 1024 * 1024),
        name="na_attention",
    )(naq, nak, nav, nak, nav, nak, nav, bias_tab)


def _flash_attend(q, loads):
    m = None
    acc = None
    for load in loads:
        k, v = load()
        s = _dot_nt(q, k)
        mc = s.max(axis=-1, keepdims=True)
        if m is None:
            m = mc
            acc = _dot(jnp.exp2(s - m).astype(BF16), v)
        else:
            m_new = jnp.maximum(m, mc)
            acc = acc * jnp.exp2(m - m_new) + _dot(jnp.exp2(s - m_new).astype(BF16), v)
            m = m_new
    dv = acc.shape[-1] // 2
    return acc[:, :dv] / acc[:, dv:dv + 1]


def _mla_kernel(q_ref, k_ref, v_ref, kc_ref, vc_ref, kca_ref, vca_ref, *refs,
                n_q_blocks, key_chunk, ctx, hps, n_ada_blocks, n_w2_blocks):
    i = pl.program_id(1)
    dq, dv = MLA_HEAD_PAD, V_HEAD_PAD
    step = pl.program_id(0) * pl.num_programs(1) + i
    if n_ada_blocks:
        c_ref, aw_ref, ab_ref, w2_ref, o_ref, w2b_ref, mod_ref = refs

        @pl.when(step < n_ada_blocks)
        def _():
            _ada_kernel(c_ref, aw_ref, ab_ref, mod_ref)
    else:
        w2_ref, o_ref, w2b_ref = refs

    @pl.when(step < n_w2_blocks)
    def _():
        w2b_ref[...] = w2_ref[...].astype(BF16)

    @pl.when(i < n_q_blocks)
    def _():
        for hh in range(hps):
            qc, vc = slice(hh * dq, (hh + 1) * dq), slice(hh * dv, (hh + 1) * dv)

            def lat_chunk(j, qc=qc, vc=vc):
                rows = slice(j * key_chunk, (j + 1) * key_chunk)
                return lambda: (k_ref[rows, qc], v_ref[rows, vc])

            loads = [lat_chunk(j) for j in range(k_ref.shape[0] // key_chunk)]
            loads.append(lambda qc=qc, vc=vc: (kc_ref[:, qc], vc_ref[:, vc]))
            o_ref[:, hh * MLA_V:(hh + 1) * MLA_V] = _flash_attend(q_ref[:, qc], loads)

    @pl.when(i == n_q_blocks)
    def _():
        for hh in range(hps):
            qc, vc = slice(hh * dq, (hh + 1) * dq), slice(hh * dv, (hh + 1) * dv)
            for bb in range(q_ref.shape[0] // ctx):
                rows = slice(bb * ctx, (bb + 1) * ctx)
                o_ref[rows, hh * MLA_V:(hh + 1) * MLA_V] = _flash_attend(
                    q_ref[rows, qc], [lambda: (kca_ref[rows, qc], vca_ref[rows, vc])])


def _mla_attention(mq, mk, mv, w2, layer, geo, with_ctx, ada=None):
    b, t, ctx = geo["b"], geo["t"], geo["ctx"]
    n_tok = mq.shape[0]
    tq = b * ctx
    assert t % tq == 0
    nqb = t // tq
    lat_blocks = b * nqb
    ctx_block0 = b * t // ctx
    batch = lambda i: jnp.minimum(i // nqb, b - 1)
    hps = 2
    dq, dv = hps * MLA_HEAD_PAD, hps * V_HEAD_PAD
    grid = (N_HEADS // hps, lat_blocks + (1 if with_ctx else 0))
    n_steps = grid[0] * grid[1]
    step = lambda h, i: h * grid[1] + i

    in_specs = [pl.BlockSpec((tq, dq), lambda h, i: (i, h)),
                pl.BlockSpec((t, dq), lambda h, i: (batch(i), h)),
                pl.BlockSpec((t, dv), lambda h, i: (batch(i), h)),
                pl.BlockSpec((ctx, dq), lambda h, i: (ctx_block0 + batch(i), h)),
                pl.BlockSpec((ctx, dv), lambda h, i: (ctx_block0 + batch(i), h)),
                pl.BlockSpec((tq, dq), lambda h, i: (lat_blocks, h)),
                pl.BlockSpec((tq, dv), lambda h, i: (lat_blocks, h))]
    operands = [mq, mk, mv, mk, mv, mk, mv]
    out_specs = [pl.BlockSpec((tq, hps * MLA_V), lambda h, i: (i, h))]
    out_shape = [jax.ShapeDtypeStruct((n_tok, GROUP_WIDTH), F32)]
    n_ada_blocks = 0
    if ada is not None:
        cvec, ada_w, ada_b, ada_layer = ada
        n_layers, d, n = ada_w.shape
        n_ada_blocks = max(k for k in range(1, n_steps + 1) if n % (k * LANE) == 0)
        tn = n // n_ada_blocks
        ada_slab = lambda h, i: jnp.minimum(step(h, i), n_ada_blocks - 1)
        in_specs += [pl.BlockSpec((8, d), lambda h, i: (0, 0)),
                     pl.BlockSpec((None, d, tn), lambda h, i: (ada_layer, 0, ada_slab(h, i))),
                     pl.BlockSpec((None, 1, tn), lambda h, i: (ada_layer, 0, ada_slab(h, i)))]
        operands += [cvec, ada_w, ada_b.reshape(n_layers, 1, n)]

    hidden, d_out = w2.shape[1:]
    slab = next(r for r in range(16, hidden + 1, 16) if hidden % r == 0 and hidden // r <= n_steps)
    n_w2_blocks = hidden // slab
    w2_block = lambda h, i: jnp.minimum(step(h, i), n_w2_blocks - 1)
    in_specs.append(pl.BlockSpec((None, slab, d_out), lambda h, i: (layer, w2_block(h, i), 0)))
    operands.append(w2)
    out_specs.append(pl.BlockSpec((slab, d_out), lambda h, i: (w2_block(h, i), 0)))
    out_shape.append(jax.ShapeDtypeStruct((hidden, d_out), BF16))
    if ada is not None:
        out_specs.append(pl.BlockSpec((8, tn), lambda h, i: (0, ada_slab(h, i))))
        out_shape.append(jax.ShapeDtypeStruct((8, n), F32))

    outs = pl.pallas_call(
        functools.partial(_mla_kernel, n_q_blocks=lat_blocks, key_chunk=512, ctx=ctx, hps=hps,
                          n_ada_blocks=n_ada_blocks, n_w2_blocks=n_w2_blocks),
        grid=grid,
        in_specs=in_specs,
        out_specs=out_specs,
        out_shape=out_shape,
        compiler_params=_cparams(("arbitrary", "arbitrary"), VMEM_LIMIT),
        name="mla_attention",
    )(*operands)
    mod_next = None if ada is None else _ada_rows(outs[2], ada[1].shape[1])
    return outs[0], outs[1], mod_next


S5_OCTET = LANE // S5_GROUP


def _s5_chunk_permutation():
    r = jnp.arange(S5_CHUNK * LANE)
    s, k, i = r // LANE, (r % LANE) // S5_GROUP, r % S5_GROUP
    dst = k * S5_BLOCK + s * S5_GROUP + i
    return (dst[:, None] == jnp.arange(S5_OCTET * S5_BLOCK)[None, :]).astype(BF16)


def _s5_contrib_kernel(p_ref, perm_ref, b_ref, o_ref, ub_ref, x_ref):
    rows = x_ref.shape[0]
    for s in range(S5_CHUNK):
        x_ref[:, s * LANE:(s + 1) * LANE] = p_ref[pl.ds(s, rows, stride=S5_CHUNK), :].astype(BF16)
    ub_ref[...] = _dot(x_ref[...], perm_ref[...]).astype(BF16)
    r = [_dot(ub_ref[:, k * S5_BLOCK:(k + 1) * S5_BLOCK], b_ref[k]) for k in range(S5_OCTET)]
    ps = S5_STATE
    for plane in range(4):
        o_ref[plane] = jnp.concatenate([rk[:, plane * ps:(plane + 1) * ps] for rk in r], axis=-1)


def _s5_scan_kernel(c_ref, lr_ref, li_ref, x_ref, *, n_tiles, n_ctx_tiles, nb):
    sub = 8
    n_lat_tiles = n_tiles - n_ctx_tiles
    lanes = c_ref.shape[-1]
    lr = jnp.broadcast_to(lr_ref[...], (sub, lanes))
    li = jnp.broadcast_to(li_ref[...], (sub, lanes))
    row = lax.broadcasted_iota(jnp.int32, (sub, lanes), 0)

    def run(reverse):
        order = range(sub - 1, -1, -1) if reverse else range(sub)

        def body(jt, carry):
            if reverse:
                tile = jnp.where(jt < n_ctx_tiles, n_ctx_tiles - 1 - jt, n_tiles - 1 - (jt - n_ctx_tiles))
            else:
                tile = jt
            new = []
            for bi in range(nb):
                xr, xi = carry[2 * bi], carry[2 * bi + 1]
                row_tile = jnp.where(tile < n_ctx_tiles, nb * n_lat_tiles + bi * n_ctx_tiles + tile,
                                     bi * n_lat_tiles + tile - n_ctx_tiles)
                r0 = pl.multiple_of(row_tile * sub, sub)
                cr = c_ref[0, pl.ds(r0, sub), :]
                ci = c_ref[1, pl.ds(r0, sub), :]
                out_r = jnp.zeros_like(cr)
                out_i = jnp.zeros_like(ci)
                for k in order:
                    out_r = jnp.where(row == k, xr, out_r)
                    out_i = jnp.where(row == k, xi, out_i)
                    ck_r = jnp.broadcast_to(cr[k:k + 1, :], (sub, lanes))
                    ck_i = jnp.broadcast_to(ci[k:k + 1, :], (sub, lanes))
                    xr, xi = lr * xr - li * xi + ck_r, lr * xi + li * xr + ck_i
                x_ref[0, pl.ds(r0, sub), :] = out_r
                x_ref[1, pl.ds(r0, sub), :] = out_i
                new += [xr, xi]
            return tuple(new)

        zero = jnp.zeros((sub, lanes), F32)
        lax.fori_loop(0, n_tiles, body, (zero,) * (2 * nb))

    @pl.when(pl.program_id(0) == 0)
    def _():
        run(False)

    @pl.when(pl.program_id(0) == 1)
    def _():
        run(True)


def _s5_out_kernel(u_ref, x_ref, t_ref, ct_ref, o_ref, y_ref):
    rows = u_ref.shape[0]
    ps = S5_STATE
    for k in range(S5_OCTET):
        x = jnp.concatenate([x_ref[plane][:, k * ps:(k + 1) * ps] for plane in range(4)], axis=-1)
        cols = slice(k * S5_BLOCK, (k + 1) * S5_BLOCK)
        y_ref[:, cols] = _dot(u_ref[:, cols], t_ref[k]) + _dot_nt(x.astype(BF16), ct_ref[k])
    for t in range(S5_CHUNK):
        piece = jnp.concatenate([y_ref[:, k * S5_BLOCK + t * S5_GROUP:k * S5_BLOCK + (t + 1) * S5_GROUP]
                                 for k in range(S5_OCTET)], axis=-1)
        o_ref[pl.ds(t, rows, stride=S5_CHUNK), :] = piece


def _s5_mixer(p, ops, layer, geo):
    b, t, ctx = geo["b"], geo["t"], geo["ctx"]
    n_chunks = (ctx + t) // S5_CHUNK
    rows = n_chunks * b
    n_tok = p.shape[0]
    assert rows * S5_CHUNK == n_tok
    n_oct = S5_GROUPS // S5_OCTET
    state_w = S5_GROUPS * S5_STATE
    oct_w = S5_OCTET * S5_BLOCK
    oct_states = S5_OCTET * S5_STATE
    sub = 8
    assert (t // S5_CHUNK) % sub == 0 and (ctx // S5_CHUNK) % sub == 0
    p_spec = pl.BlockSpec((n_tok, LANE), lambda o: (0, COL_S5 // LANE + o))
    op_spec = pl.BlockSpec((None, S5_OCTET, S5_BLOCK, S5_BLOCK), lambda o: (layer, o, 0, 0))
    plane_spec = pl.BlockSpec((4, rows, oct_states), lambda o: (0, 0, o))

    u_spec = pl.BlockSpec((rows, oct_w), lambda o: (0, o))
    contrib, u_chunks = pl.pallas_call(
        _s5_contrib_kernel,
        grid=(n_oct,),
        in_specs=[p_spec, pl.BlockSpec((S5_CHUNK * LANE, oct_w), lambda o: (0, 0)), op_spec],
        out_specs=[plane_spec, u_spec],
        out_shape=[jax.ShapeDtypeStruct((4, rows, state_w), F32),
                   jax.ShapeDtypeStruct((rows, S5_GROUPS * S5_BLOCK), BF16)],
        scratch_shapes=[pltpu.VMEM((rows, S5_CHUNK * LANE), BF16)],
        compiler_params=_cparams(("arbitrary",), 48 * 1024 * 1024),
        name="s5_contrib",
    )(p, _s5_chunk_permutation(), ops["b_mat"])

    lane_blk = 512
    states = pl.pallas_call(
        functools.partial(_s5_scan_kernel, n_tiles=n_chunks // sub, n_ctx_tiles=ctx // S5_CHUNK // sub, nb=b),
        grid=(2, state_w // lane_blk),
        in_specs=[pl.BlockSpec((None, 2, rows, lane_blk), lambda d, l: (d, 0, 0, l)),
                  pl.BlockSpec((None, None, 1, lane_blk), lambda d, l: (layer, d, 0, l)),
                  pl.BlockSpec((None, None, 1, lane_blk), lambda d, l: (layer, d, 0, l))],
        out_specs=pl.BlockSpec((None, 2, rows, lane_blk), lambda d, l: (d, 0, 0, l)),
        out_shape=jax.ShapeDtypeStruct((2, 2, rows, state_w), F32),
        compiler_params=_cparams(("arbitrary", "arbitrary"), 40 * 1024 * 1024),
        name="s5_scan",
    )(contrib.reshape(2, 2, rows, state_w), ops["decay_re"], ops["decay_im"])

    return pl.pallas_call(
        _s5_out_kernel,
        grid=(n_oct,),
        in_specs=[u_spec, plane_spec, op_spec, op_spec],
        out_specs=pl.BlockSpec((n_tok, LANE), lambda o: (0, o)),
        out_shape=jax.ShapeDtypeStruct((n_tok, GROUP_WIDTH), F32),
        scratch_shapes=[pltpu.VMEM((rows, oct_w), F32)],
        compiler_params=_cparams(("arbitrary",), 48 * 1024 * 1024),
        name="s5_out",
    )(u_chunks, states.reshape(4, rows, state_w), ops["t_sum"], ops["c_mat_t"])


def _s5_operators(lam_re, lam_im, log_dt, b_re, b_im, c_re, c_im):
    hp = lax.Precision.HIGHEST
    g, pn, ni, lc = S5_GROUPS, S5_STATE, S5_GROUP, S5_CHUNK
    dt = jnp.exp(log_dt)[..., None]
    zr, zi = lam_re * dt, lam_im * dt
    up = jnp.arange(lc, dtype=F32)
    down = (lc - 1) - up

    def powers(d, steps):
        mag = jnp.exp(zr[d][None] * steps[:, None, None])
        ang = zi[d][None] * steps[:, None, None]
        return mag * jnp.cos(ang), mag * jnp.sin(ang)

    one = jnp.ones((1,), F32)
    z1 = [powers(d, one) for d in range(2)]
    nr = jnp.stack([z1[0][0][0], z1[1][0][0]]) - 1.0
    nim = jnp.stack([z1[0][1][0], z1[1][1][0]])
    den = lam_re * lam_re + lam_im * lam_im
    cr_, ci_ = (nr * lam_re + nim * lam_im) / den, (nim * lam_re - nr * lam_im) / den
    bz_r = cr_[..., None] * b_re - ci_[..., None] * b_im
    bz_i = cr_[..., None] * b_im + ci_[..., None] * b_re

    def lag_kernel(d, steps):
        pr, pi = powers(d, steps)
        m_r = pr[..., None] * bz_r[d][None] - pi[..., None] * bz_i[d][None]
        m_i = pr[..., None] * bz_i[d][None] + pi[..., None] * bz_r[d][None]
        k = (jnp.einsum("gop,dgpi->dgoi", c_re[d], m_r, precision=hp)
             - jnp.einsum("gop,dgpi->dgoi", c_im[d], m_i, precision=hp))
        return k.transpose(1, 3, 0, 2)

    k_f = lag_kernel(0, up)
    k_b = lag_kernel(1, down)
    two_sided = jnp.concatenate([k_b[:, :, :lc - 1], k_f[:, :, :1] + k_b[:, :, lc - 1:], k_f[:, :, 1:]], axis=2)
    two_sided = two_sided.reshape(g, ni, (2 * lc - 1) * ni)
    t_sum = jnp.stack([two_sided[:, :, (lc - 1 - s) * ni:(lc - 1 - s) * ni + S5_BLOCK] for s in range(lc)], axis=1)
    t_sum = t_sum.reshape(g, S5_BLOCK, S5_BLOCK)

    per_dir = lambda a: jnp.stack([a[0], a[0], a[1], a[1]])
    by_pos = lambda planes: jnp.stack(planes).transpose(2, 1, 0, 3).reshape(g, lc, 1, 4 * pn)
    by_chan = lambda a: per_dir(a).transpose(1, 2, 0, 3).reshape(g, 1, ni, 4 * pn)

    ef_r, ef_i = powers(0, down)
    eb_r, eb_i = powers(1, up)
    bz_ri = (bz_r.transpose(0, 1, 3, 2), bz_i.transpose(0, 1, 3, 2))
    b_mat = (by_pos([ef_r, ef_i, eb_r, eb_i]) * by_chan(bz_ri[0])
             + by_pos([-ef_i, ef_r, -eb_i, eb_r]) * by_chan(bz_ri[1])).reshape(g, S5_BLOCK, 4 * pn)

    pf_r, pf_i = powers(0, up + 1.0)
    pb_r, pb_i = powers(1, lc - up)
    c_mat_t = (by_pos([pf_r, -pf_i, pb_r, -pb_i]) * by_chan(c_re)
               + by_pos([-pf_i, -pf_r, -pb_i, -pb_r]) * by_chan(c_im)).reshape(g, S5_BLOCK, 4 * pn)

    full = jnp.full((1,), float(lc), F32)
    decay = [powers(d, full) for d in range(2)]
    return {"t_sum": t_sum.astype(BF16), "b_mat": b_mat.astype(BF16), "c_mat_t": c_mat_t.astype(BF16),
            "decay_re": jnp.stack([decay[0][0], decay[1][0]]).reshape(2, 1, g * pn),
            "decay_im": jnp.stack([decay[0][1], decay[1][1]]).reshape(2, 1, g * pn)}


def _group_norm_store(o_ref, k, y, w_ref):
    cols = slice(k * GROUP_WIDTH, (k + 1) * GROUP_WIDTH)
    o_ref[:, cols] = _rms(y, w_ref[:, cols]).astype(o_ref.dtype)


def _mix_kernel(ya_ref, yb_ref, gb_ref, gc_ref, u_ref, gcp_ref, up_ref, gcn_ref, un_ref,
                cw_ref, cb_ref, ys_ref, us_ref, ds_ref, gw_ref, gbias_ref, mw_ref, o_ref, *, tm, lat_rows, t, ctx):
    i = pl.program_id(0)
    _group_norm_store(o_ref, 0, ya_ref[...], mw_ref)
    _group_norm_store(o_ref, 1, yb_ref[...], mw_ref)

    r0 = i * tm
    in_lat = r0 < lat_rows
    seq_len = jnp.where(in_lat, t, ctx)
    off = jnp.where(in_lat, r0, r0 - lat_rows) % seq_len
    has_prev = off != 0
    has_next = off + tm != seq_len
    v = gc_ref[...] * u_ref[...]
    v_prev_row = jnp.where(has_prev, gcp_ref[7:8, :] * up_ref[7:8, :], 0.0)
    v_next_row = jnp.where(has_next, gcn_ref[0:1, :] * un_ref[0:1, :], 0.0)
    row = lax.broadcasted_iota(jnp.int32, v.shape, 0)
    v_prev = jnp.where(row == 0, v_prev_row, pltpu.roll(v, 1, 0))
    v_next = jnp.where(row == tm - 1, v_next_row, pltpu.roll(v, tm - 1, 0))
    conv = cw_ref[0:1, :] * v_prev + cw_ref[1:2, :] * v + cw_ref[2:3, :] * v_next + cb_ref[...]
    _group_norm_store(o_ref, 2, gb_ref[...] * conv, mw_ref)

    y = ys_ref[...] + ds_ref[...] * us_ref[...]
    g = 0.5 * y * (1.0 + jnp.tanh(math.sqrt(2.0 / math.pi) * (y + 0.044715 * (y * y * y))))
    gate = _sigmoid(_dot(g.astype(BF16), gw_ref[...]) + gbias_ref[...])
    _group_norm_store(o_ref, 3, g * gate, mw_ref)


def _mix_outputs(ya, yb, p, ys5, lw, layer, geo, n_tiles):
    n_tok = ya.shape[0]
    tm = 256
    gw = GROUP_WIDTH
    halo = 8
    n_halo_blocks = n_tok // halo
    tile = lambda col: pl.BlockSpec((tm, gw), lambda i: (i, col // gw))
    prev = lambda col: pl.BlockSpec((halo, gw), lambda i: (jnp.maximum(i * (tm // halo) - 1, 0), col // gw))
    nxt = lambda col: pl.BlockSpec(
        (halo, gw), lambda i: (jnp.minimum((i + 1) * (tm // halo), n_halo_blocks - 1), col // gw))
    full = lambda a: _layer_full_spec(a, layer)
    c_gb, c_gc, c_u = COL_CONV, COL_CONV + gw, COL_CONV + 2 * gw
    weights = [lw["conv_w"], lw["conv_b"]]
    glu = [lw["s5_d"], lw["s5_glu_w"], lw["s5_glu_b"], lw["mix_norm_w"]]
    return pl.pallas_call(
        functools.partial(_mix_kernel, tm=tm, lat_rows=geo["b"] * geo["t"], t=geo["t"], ctx=geo["ctx"]),
        grid=(n_tiles * geo["tm"] // tm,),
        in_specs=[tile(0), tile(0), tile(c_gb), tile(c_gc), tile(c_u),
                  prev(c_gc), prev(c_u), nxt(c_gc), nxt(c_u)]
        + [full(a) for a in weights] + [tile(0), tile(COL_S5)] + [full(a) for a in glu],
        out_specs=pl.BlockSpec((tm, 4 * gw), lambda i: (i, 0)),
        out_shape=jax.ShapeDtypeStruct((n_tiles * geo["tm"], 4 * gw), BF16),
        compiler_params=_cparams(("arbitrary",), 48 * 1024 * 1024),
        name="mix_outputs",
    )(ya, yb, p, p, p, p, p, p, p, *weights, ys5, p, *glu)


def _pad_last(a, width):
    return jnp.pad(a, [(0, 0)] * (a.ndim - 1) + [(0, width - a.shape[-1])])


def _stacked_weights(pr):
    n_layers = pr["mla_w_uq"].shape[0]
    w_uq = pr["mla_w_uq"].reshape(n_layers, MLA_Q_RANK, N_HEADS, MLA_QK)
    w_uq = _pad_last(w_uq, MLA_HEAD_PAD).reshape(n_layers, MLA_Q_RANK, N_HEADS * MLA_HEAD_PAD)
    w_ukv = pr["mla_w_ukv"].reshape(n_layers, MLA_KV_RANK, N_HEADS, MLA_NOPE + MLA_V)
    w_ukv = jnp.concatenate([w_ukv[..., :MLA_NOPE].reshape(n_layers, MLA_KV_RANK, -1),
                             w_ukv[..., MLA_NOPE:].reshape(n_layers, MLA_KV_RANK, -1)], axis=2)
    w_ukv = jnp.pad(w_ukv, ((0, 0), (0, MLA_KV_RANK_PAD - MLA_KV_RANK), (0, 0)))
    row = lambda a: a[:, None, :]
    return {
        "na_q_norm_w": row(pr["na_q_norm_w"]),
        "na_k_norm_w": row(pr["na_k_norm_w"]),
        "mla_cq_norm_w": row(pr["mla_cq_norm_w"]),
        "mla_ckv_norm_w": row(_pad_last(pr["mla_ckv_norm_w"], MLA_KV_RANK_PAD)),
        "mla_w_uq": w_uq.astype(BF16),
        "mla_w_ukv": w_ukv.astype(BF16),
        "mla_q_norm_w": row(_pad_last(pr["mla_q_norm_w"], MLA_HEAD_PAD)),
        "mla_k_norm_w": row(_pad_last(pr["mla_k_norm_w"], MLA_HEAD_PAD)),
        "conv_w": pr["conv_w"],
        "conv_b": row(pr["conv_b"]),
        "s5_d": row(pr["s5_d"]),
        "s5_glu_w": pr["s5_glu_w"].astype(BF16),
        "s5_glu_b": row(pr["s5_glu_b"]),
        "mix_norm_w": row(pr["mix_norm_w"]),
    }


def _layer_full_spec(a, layer):
    return pl.BlockSpec((None,) + a.shape[1:], lambda i: (layer,) + (0,) * (a.ndim - 1))


def _rope_tables(b, t, n_ctx):
    pos = jnp.arange(t, dtype=jnp.int32)
    row = (pos // GRID_W).astype(F32)
    col = (pos % GRID_W).astype(F32)
    n_freq = MLA_ROPE // 4
    inv_freq = ROPE_THETA ** (-jnp.arange(n_freq, dtype=F32) / n_freq)
    ang_r, ang_c = row[:, None] * inv_freq[None, :], col[:, None] * inv_freq[None, :]
    cos_r, sin_r, cos_c, sin_c = jnp.cos(ang_r), jnp.sin(ang_r), jnp.cos(ang_c), jnp.sin(ang_c)
    zeros = jnp.zeros((t, n_freq), F32)
    rest = LANE - MLA_ROPE
    cos_t = jnp.concatenate([cos_r, cos_r, cos_c, cos_c, jnp.ones((t, rest), F32)], axis=1)
    sina_t = jnp.concatenate([-sin_r, zeros, -sin_c, zeros, jnp.zeros((t, rest), F32)], axis=1)
    sinb_t = jnp.concatenate([zeros, sin_r, zeros, sin_c, jnp.zeros((t, rest), F32)], axis=1)
    n_c = b * n_ctx
    tables = []
    for tab, fill in ((cos_t, 1.0), (sina_t, 0.0), (sinb_t, 0.0)):
        tables.append(jnp.concatenate([jnp.tile(tab, (b, 1)), jnp.full((n_c, LANE), fill, F32)], axis=0))
    return tables


def _na_bias_tables(rpb):
    w = GRID_W
    qc = jnp.arange(w)
    cs = jnp.clip(qc - NA_WIN_COLS // 2, 0, w - NA_WIN_COLS)
    v_col = (qc[None, :] >= cs[:, None]) & (qc[None, :] < cs[:, None] + NA_WIN_COLS)
    d_col = jnp.clip(qc[None, :] - qc[:, None], -(NA_WIN_COLS - 1), NA_WIN_COLS - 1) + (NA_WIN_COLS - 1)
    sel_c = jax.nn.one_hot(d_col, 2 * NA_WIN_COLS - 1, dtype=F32)
    tab = jnp.einsum("hab,qkb->haqk", rpb * LOG2E, sel_c, precision=lax.Precision.HIGHEST)
    tab = jnp.where(v_col[None, None], tab, NEG_INF)
    tab = jnp.concatenate([tab, jnp.full((rpb.shape[0], 1, w, w), NEG_INF, F32)], axis=1)
    return jnp.concatenate([tab, tab], axis=-1)


def kernel(x, c, ctx, c_ctx, ada_w, ada_b, norm1_w, norm2_w, w_in, na_q_norm_w, na_k_norm_w, na_rpb, mla_cq_norm_w, mla_ckv_norm_w, mla_w_uq, mla_w_ukv, mla_q_norm_w, mla_k_norm_w, conv_w, conv_b, s5_lambda_re, s5_lambda_im, s5_log_dt, s5_b_re, s5_b_im, s5_c_re, s5_c_im, s5_d, s5_glu_w, s5_glu_b, mix_norm_w, w_out, ffn_w1, ffn_w3, ffn_w2):
    b, t, d = x.shape
    n_ctx = ctx.shape[1]
    n_layers = ada_w.shape[0]
    n_rows = t // GRID_W
    assert t % GRID_W == 0 and n_rows >= NA_K_ROWS and n_rows % NA_Q_ROWS == 0
    assert n_ctx == NA_Q_ROWS * GRID_W and t % n_ctx == 0 and b + 1 <= 8
    tm = 512 if (b * n_ctx) % 512 == 0 and t % 512 == 0 else 256
    tiles_per_batch = t // tm
    geo = {"b": b, "t": t, "ctx": n_ctx, "tm": tm,
           "tile_class": lambda i: jnp.minimum(i // tiles_per_batch, b)}
    lat_tiles = b * t // tm
    all_tiles = lat_tiles + b * n_ctx // tm

    pr = dict(na_q_norm_w=na_q_norm_w, na_k_norm_w=na_k_norm_w, mla_cq_norm_w=mla_cq_norm_w,
              mla_ckv_norm_w=mla_ckv_norm_w, mla_w_uq=mla_w_uq, mla_w_ukv=mla_w_ukv,
              mla_q_norm_w=mla_q_norm_w, mla_k_norm_w=mla_k_norm_w, conv_w=conv_w, conv_b=conv_b,
              s5_d=s5_d, s5_glu_w=s5_glu_w, s5_glu_b=s5_glu_b, mix_norm_w=mix_norm_w)

    h = (x.reshape(b * t, d), ctx.reshape(b * n_ctx, d))
    cvec = jnp.zeros((8, d), F32).at[:b].set(c).at[b].set(c_ctx)
    mod = _ada_mod(cvec, ada_w, ada_b, 0)
    rope = _rope_tables(b, t, n_ctx)
    lw = _stacked_weights(pr)
    na_bias = jax.vmap(_na_bias_tables)(na_rpb)
    s5_ops = jax.vmap(_s5_operators)(s5_lambda_re, s5_lambda_im, s5_log_dt, s5_b_re, s5_b_im, s5_c_re, s5_c_im)
    m_in = 1088 if (b * t + b * n_ctx) % 1088 == 0 else tm

    for l in range(n_layers):
        last = l == n_layers - 1
        n_tiles = lat_tiles if last else all_tiles

        a = _norm_modulate(h, norm1_w[l], mod, 0, 1, geo, all_tiles)
        p = _input_projection(a, w_in, l, m_in)
        naq, nak, nav, mq, mk, mv = _prep_qkv(p, lw, l, rope, 256)
        ya = _na_attention(naq, nak, nav, na_bias, l, geo, not last)
        yb, w2_bf16, mod_next = _mla_attention(mq, mk, mv, ffn_w2, l, geo, not last,
                                               None if last else (cvec, ada_w, ada_b, l + 1))
        ys5 = _s5_mixer(p, s5_ops, l, geo)
        y = _mix_outputs(ya, yb, p, ys5, lw, l, geo, n_tiles)
        h = _matmul_gated_residual(y, w_out, h, mod, 2, geo, n_tiles, 1024, layer=l, single_buffer_w=True)

        f = _norm_modulate(h, norm2_w[l], mod, 3, 4, geo, n_tiles)
        g = _ffn_in(f, ffn_w1, ffn_w3, l, geo, n_tiles, 512)
        h = _matmul_gated_residual(g, w2_bf16, h, mod, 5, geo, n_tiles, 512)
        mod = mod_next

    return h.reshape(b, t, d)
```

```python
import functools
import math

import jax
import jax.numpy as jnp
from jax import lax
from jax.experimental import pallas as pl
from jax.experimental.pallas import tpu as pltpu

F32 = jnp.float32
BF16 = jnp.bfloat16

NORM_EPS = 1e-6
NEG_INF = -1e30
GRID_W = 64

GROUP_WIDTH = 1024
N_HEADS = 8
NA_HEAD_DIM = 128
NA_WIN_ROWS = 8
NA_WIN_COLS = 16
NA_Q_ROWS = 4
NA_K_ROWS = NA_Q_ROWS + NA_WIN_ROWS

MLA_NOPE = 128
MLA_ROPE = 64
MLA_QK = MLA_NOPE + MLA_ROPE
MLA_V = 128
MLA_Q_RANK = 896
MLA_KV_RANK = 320
MLA_KV_RANK_PAD = 512
MLA_HEAD_PAD = 256
V_HEAD_PAD = 256
LOG2E = math.log2(math.e)
ROPE_THETA = 10000.0

CONV_K = 3

S5_GROUPS = 64
S5_GROUP = 16
S5_STATE = 64
S5_CHUNK = 16
S5_BLOCK = S5_CHUNK * S5_GROUP


COL_NA = 0
COL_CONV = 3072
COL_S5 = 6144
COL_CQ = 7168
COL_CKV = 8192
IN_WIDTH_PAD = 8704
IN_TILE = 512
SRC_CQ = 3072
SRC_CKV = SRC_CQ + MLA_Q_RANK
SRC_CONV = SRC_CKV + MLA_KV_RANK + MLA_ROPE
SRC_S5 = SRC_CONV + 3 * GROUP_WIDTH

LANE = 128
VMEM_LIMIT = 56 * 1024 * 1024


def _cparams(sem, vmem=None):
    return pltpu.CompilerParams(dimension_semantics=sem, vmem_limit_bytes=vmem)


def _sigmoid(x):
    return 1.0 / (1.0 + jnp.exp(-x))


def _rms(x, w):
    return x * lax.rsqrt(jnp.mean(x * x, axis=-1, keepdims=True) + NORM_EPS) * w


def _dot(a, b):
    return jnp.dot(a, b, preferred_element_type=F32)


def _dot_nt(a, b):
    return lax.dot_general(a, b, (((1,), (1,)), ((), ())), preferred_element_type=F32)


def _ada_kernel(c_ref, w_ref, b_ref, o_ref):
    c = c_ref[...]
    s = (c * _sigmoid(c)).astype(BF16)
    o_ref[...] = _dot(s, w_ref[...].astype(BF16)) + b_ref[...]


def _ada_rows(out, d):
    return out.reshape(8, 6, d).transpose(1, 0, 2)[:, :, None, :]


def _ada_mod(cvec, ada_w, ada_b, layer):
    n_layers, d, n = ada_w.shape
    tn = 512
    out = pl.pallas_call(
        _ada_kernel,
        grid=(n // tn,),
        in_specs=[pl.BlockSpec((8, d), lambda j: (0, 0)),
                  pl.BlockSpec((None, d, tn), lambda j: (layer, 0, j)),
                  pl.BlockSpec((None, 1, tn), lambda j: (layer, 0, j))],
        out_specs=pl.BlockSpec((8, tn), lambda j: (0, j)),
        out_shape=jax.ShapeDtypeStruct((8, n), F32),
        compiler_params=_cparams(("arbitrary",), 40 * 1024 * 1024),
        name="ada_mod",
    )(cvec, ada_w, ada_b.reshape(n_layers, 1, n))
    return _ada_rows(out, d)


def _stream_parts(h):
    return tuple(h) if isinstance(h, (tuple, list)) else (h,)


def _stream_specs(parts, tm, width, row_of, col_of):
    if len(parts) == 1:
        return [pl.BlockSpec((tm, width), lambda *g: (row_of(*g), col_of(*g)))]
    lat_tiles = parts[0].shape[0] // tm
    return [pl.BlockSpec((tm, width), lambda *g: (jnp.minimum(row_of(*g), lat_tiles - 1), col_of(*g))),
            pl.BlockSpec((tm, width), lambda *g: (jnp.maximum(row_of(*g) - lat_tiles, 0), col_of(*g)))]


def _read_stream(h_refs, row_tile, lat_tiles):
    if len(h_refs) == 1:
        return h_refs[0][...]
    return jnp.where(row_tile < lat_tiles, h_refs[0][...], h_refs[1][...])


def _normmod_kernel(*refs, n_h, lat_tiles):
    h_refs = refs[:n_h]
    w_ref, shift_ref, scale_ref, o_ref = refs[n_h:]
    y = _rms(_read_stream(h_refs, pl.program_id(0), lat_tiles), w_ref[...])
    o_ref[...] = (y * (1.0 + scale_ref[...]) + shift_ref[...]).astype(o_ref.dtype)


def _norm_modulate(h, w, mod, k_shift, k_scale, geo, n_tiles):
    parts = _stream_parts(h)
    d = parts[0].shape[1]
    tm = geo["tm"]
    cls = geo["tile_class"]
    return pl.pallas_call(
        functools.partial(_normmod_kernel, n_h=len(parts), lat_tiles=parts[0].shape[0] // tm),
        grid=(n_tiles,),
        in_specs=_stream_specs(parts, tm, d, lambda i: i, lambda i: 0)
        + [pl.BlockSpec((1, d), lambda i: (0, 0)),
           pl.BlockSpec((None, None, 1, d), lambda i: (k_shift, cls(i), 0, 0)),
           pl.BlockSpec((None, None, 1, d), lambda i: (k_scale, cls(i), 0, 0))],
        out_specs=pl.BlockSpec((tm, d), lambda i: (i, 0)),
        out_shape=jax.ShapeDtypeStruct((n_tiles * tm, d), BF16),
        compiler_params=_cparams(("arbitrary",), VMEM_LIMIT),
        name="norm_modulate",
    )(*parts, w.reshape(1, d), mod, mod)


def _in_proj_kernel(a_ref, w_ref, o_ref, wb_ref):
    @pl.when(pl.program_id(1) == 0)
    def _():
        wb_ref[...] = w_ref[...].astype(BF16)

    o_ref[...] = _dot(a_ref[...], wb_ref[...])


def _in_proj_source_col(j):
    t = IN_TILE // LANE
    unit = jnp.where(j < 6, j * t,
           jnp.where(j < 12, SRC_CONV // LANE + (j - 6) * t,
           jnp.where(j < 14, SRC_S5 // LANE + (j - 12) * t,
           jnp.where(j < 16, SRC_CQ // LANE + (j - 14) * t, SRC_CKV // LANE))))
    return unit * LANE


def _input_projection(a, w_in, layer, tm):
    m, k = a.shape
    tn = IN_TILE
    assert m % tm == 0 and COL_CKV + tn == IN_WIDTH_PAD and SRC_CKV + tn <= w_in.shape[-1]
    return pl.pallas_call(
        _in_proj_kernel,
        grid=(IN_WIDTH_PAD // tn, m // tm),
        in_specs=[pl.BlockSpec((tm, k), lambda j, i: (i, 0)),
                  pl.BlockSpec((pl.Element(k), pl.Element(tn)),
                               lambda j, i: (layer * k, _in_proj_source_col(j)))],
        out_specs=pl.BlockSpec((tm, tn), lambda j, i: (i, j)),
        out_shape=jax.ShapeDtypeStruct((m, IN_WIDTH_PAD), F32),
        scratch_shapes=[pltpu.VMEM((k, tn), BF16)],
        compiler_params=_cparams(("arbitrary", "arbitrary"), VMEM_LIMIT),
        name="input_projection",
    )(a, w_in.reshape(-1, w_in.shape[-1]))


def _mm_res_kernel(a_ref, w_ref, *refs, n_h, lat_tiles, f32w):
    h_refs = refs[:n_h]
    g_ref, o_ref = refs[n_h], refs[n_h + 1]
    if f32w:
        wb_ref = refs[n_h + 2]

        @pl.when(pl.program_id(1) == 0)
        def _():
            wb_ref[...] = w_ref[...].astype(BF16)

        w = wb_ref[...]
    else:
        w = w_ref[...]
    h = _read_stream(h_refs, pl.program_id(1), lat_tiles)
    o_ref[...] = h + g_ref[...] * _dot(a_ref[...], w)


def _layer_weight_spec(w, layer, tn, single_buffer=False):
    mode = pl.Buffered(1) if single_buffer else None
    if w.ndim == 3:
        return pl.BlockSpec((None, w.shape[1], tn), lambda j, i: (layer, 0, j), pipeline_mode=mode)
    return pl.BlockSpec((w.shape[0], tn), lambda j, i: (0, j), pipeline_mode=mode)


def _matmul_gated_residual(a, w, h, mod, k_gate, geo, n_tiles, tn, layer=None, single_buffer_w=False):
    m, k = a.shape
    n = w.shape[-1]
    tm = geo["tm"]
    cls = geo["tile_class"]
    f32w = w.dtype == F32
    parts = _stream_parts(h)
    return pl.pallas_call(
        functools.partial(_mm_res_kernel, n_h=len(parts), lat_tiles=parts[0].shape[0] // tm, f32w=f32w),
        grid=(n // tn, n_tiles),
        in_specs=[pl.BlockSpec((tm, k), lambda j, i: (i, 0)),
                  _layer_weight_spec(w, layer, tn, single_buffer_w)]
        + _stream_specs(parts, tm, tn, lambda j, i: i, lambda j, i: j)
        + [pl.BlockSpec((None, None, 1, tn), lambda j, i: (k_gate, cls(i), 0, j))],
        out_specs=pl.BlockSpec((tm, tn), lambda j, i: (i, j)),
        out_shape=jax.ShapeDtypeStruct((n_tiles * tm, n), F32),
        scratch_shapes=[pltpu.VMEM((k, tn), BF16)] if f32w else [],
        compiler_params=_cparams(("arbitrary", "arbitrary"), VMEM_LIMIT),
        name="matmul_gated_residual",
    )(a, w, *parts, mod)


def _ffn1_kernel(a_ref, w1_ref, w3_ref, o_ref, w1b_ref, w3b_ref):
    @pl.when(pl.program_id(1) == 0)
    def _():
        w1b_ref[...] = w1_ref[...].astype(BF16)
        w3b_ref[...] = w3_ref[...].astype(BF16)

    a = a_ref[...]
    u = _dot(a, w1b_ref[...])
    v = _dot(a, w3b_ref[...])
    o_ref[...] = (u * _sigmoid(u) * v).astype(o_ref.dtype)


def _ffn_in(a, w1, w3, layer, geo, n_tiles, tn):
    m, k = a.shape
    n = w1.shape[-1]
    tm = geo["tm"]
    return pl.pallas_call(
        _ffn1_kernel,
        grid=(pl.cdiv(n, tn), n_tiles),
        in_specs=[pl.BlockSpec((tm, k), lambda j, i: (i, 0)),
                  _layer_weight_spec(w1, layer, tn),
                  _layer_weight_spec(w3, layer, tn)],
        out_specs=pl.BlockSpec((tm, tn), lambda j, i: (i, j)),
        out_shape=jax.ShapeDtypeStruct((m, n), BF16),
        scratch_shapes=[pltpu.VMEM((k, tn), BF16), pltpu.VMEM((k, tn), BF16)],
        compiler_params=_cparams(("arbitrary", "arbitrary"), VMEM_LIMIT),
        name="ffn_in",
    )(a, w1, w3)


def _rope_tail(t, cos_ref, sina_ref, sinb_ref):
    q = MLA_ROPE // 4
    return (t * cos_ref[...] + pltpu.roll(t, LANE - q, 1) * sina_ref[...]
            + pltpu.roll(t, q, 1) * sinb_ref[...])


def _prep_kernel(na_ref, cq_ref, ckvkr_ref, naqw_ref, nakw_ref, cqw_ref, ckvw_ref,
                 wuq_ref, wukv_ref, mqw_ref, mkw_ref, cos_ref, sina_ref, sinb_ref,
                 naq_ref, nak_ref, nav_ref, mq_ref, mk_ref, mv_ref):
    hd = NA_HEAD_DIM
    tm = na_ref.shape[0]
    ones_col = jnp.where(lax.broadcasted_iota(jnp.int32, (tm, LANE), 1) == 0, 1.0, 0.0).astype(BF16)
    na_qs = naqw_ref[...] * (NA_HEAD_DIM ** -0.5 * LOG2E)
    for h in range(N_HEADS):
        naq_ref[:, h * hd:(h + 1) * hd] = _rms(na_ref[:, h * hd:(h + 1) * hd], na_qs).astype(BF16)
        nak_ref[:, h * hd:(h + 1) * hd] = _rms(
            na_ref[:, GROUP_WIDTH + h * hd:GROUP_WIDTH + (h + 1) * hd], nakw_ref[...]).astype(BF16)
        vo = h * V_HEAD_PAD
        nav_ref[:, vo:vo + hd] = na_ref[:, 2 * GROUP_WIDTH + h * hd:2 * GROUP_WIDTH + (h + 1) * hd].astype(BF16)
        nav_ref[:, vo + hd:vo + V_HEAD_PAD] = ones_col

    cq = _rms(cq_ref[...], cqw_ref[...]).astype(BF16)
    q = _dot(cq, wuq_ref[...])
    inv_qk = 1.0 / MLA_QK
    mqw = mqw_ref[...] * (MLA_QK ** -0.5 * LOG2E)
    for h in range(N_HEADS):
        o = h * MLA_HEAD_PAD
        nope = q[:, o:o + MLA_NOPE]
        tail = q[:, o + MLA_NOPE:o + MLA_HEAD_PAD]
        ss = jnp.sum(nope * nope, axis=-1, keepdims=True) + jnp.sum(tail * tail, axis=-1, keepdims=True)
        r = lax.rsqrt(ss * inv_qk + NORM_EPS)
        mq_ref[:, o:o + MLA_NOPE] = (nope * r * mqw[:, :MLA_NOPE]).astype(BF16)
        mq_ref[:, o + MLA_NOPE:o + MLA_HEAD_PAD] = _rope_tail(
            tail * r * mqw[:, MLA_NOPE:], cos_ref, sina_ref, sinb_ref).astype(BF16)

    blk = ckvkr_ref[...]
    lane = lax.broadcasted_iota(jnp.int32, blk.shape, 1)
    ckv = jnp.where(lane < MLA_KV_RANK, blk, 0.0)
    ckv_ms = jnp.sum(ckv * ckv, axis=-1, keepdims=True) * (1.0 / MLA_KV_RANK)
    ckv_n = (ckv * lax.rsqrt(ckv_ms + NORM_EPS) * ckvw_ref[...]).astype(BF16)
    kv = _dot(ckv_n, wukv_ref[...])
    for h in range(N_HEADS):
        vo = h * V_HEAD_PAD
        mv_ref[:, vo:vo + MLA_V] = kv[:, GROUP_WIDTH + h * MLA_V:GROUP_WIDTH + (h + 1) * MLA_V].astype(BF16)
        mv_ref[:, vo + MLA_V:vo + V_HEAD_PAD] = ones_col
    kr = pltpu.roll(blk[:, 2 * LANE:3 * LANE], LANE - MLA_ROPE, 1)
    kr = jnp.where(lax.broadcasted_iota(jnp.int32, kr.shape, 1) < MLA_ROPE, kr, 0.0)
    kr_ss = jnp.sum(kr * kr, axis=-1, keepdims=True)
    mkw = mkw_ref[...]
    for h in range(N_HEADS):
        o = h * MLA_HEAD_PAD
        kn = kv[:, h * MLA_NOPE:(h + 1) * MLA_NOPE]
        r = lax.rsqrt((jnp.sum(kn * kn, axis=-1, keepdims=True) + kr_ss) * inv_qk + NORM_EPS)
        mk_ref[:, o:o + MLA_NOPE] = (kn * r * mkw[:, :MLA_NOPE]).astype(BF16)
        mk_ref[:, o + MLA_NOPE:o + MLA_HEAD_PAD] = _rope_tail(
            kr * r * mkw[:, MLA_NOPE:], cos_ref, sina_ref, sinb_ref).astype(BF16)


def _prep_qkv(p, lw, layer, rope, tm):
    n_tok = p.shape[0]
    blk = lambda width, col: pl.BlockSpec((tm, width), lambda i: (i, col // width))
    full = lambda a: _layer_full_spec(a, layer)
    tab = pl.BlockSpec((tm, LANE), lambda i: (i, 0))
    out_w = [GROUP_WIDTH, GROUP_WIDTH, N_HEADS * V_HEAD_PAD,
             N_HEADS * MLA_HEAD_PAD, N_HEADS * MLA_HEAD_PAD, N_HEADS * V_HEAD_PAD]
    weights = [lw["na_q_norm_w"], lw["na_k_norm_w"], lw["mla_cq_norm_w"], lw["mla_ckv_norm_w"],
               lw["mla_w_uq"], lw["mla_w_ukv"], lw["mla_q_norm_w"], lw["mla_k_norm_w"]]
    return pl.pallas_call(
        _prep_kernel,
        grid=(n_tok // tm,),
        in_specs=[blk(3 * GROUP_WIDTH, COL_NA), blk(MLA_Q_RANK, COL_CQ), blk(MLA_KV_RANK_PAD, COL_CKV)]
        + [full(a) for a in weights] + [tab, tab, tab],
        out_specs=[pl.BlockSpec((tm, w), lambda i: (i, 0)) for w in out_w],
        out_shape=[jax.ShapeDtypeStruct((n_tok, w), BF16) for w in out_w],
        compiler_params=_cparams(("arbitrary",), 48 * 1024 * 1024),
        name="prep_qkv",
    )(p, p, p, *weights, *rope)


def _softmax_attend(q, pairs, bias=None):
    scores = []
    for idx, (k, _) in enumerate(pairs):
        s = _dot_nt(q, k)
        if idx == 0 and bias is not None:
            s = s + bias
        scores.append(s)
    m = scores[0].max(axis=-1, keepdims=True)
    for s in scores[1:]:
        m = jnp.maximum(m, s.max(axis=-1, keepdims=True))
    acc = None
    for s, (_, v) in zip(scores, pairs):
        o = _dot(jnp.exp2(s - m).astype(BF16), v)
        acc = o if acc is None else acc + o
    dv = acc.shape[-1] // 2
    return acc[:, :dv] / acc[:, dv:dv + 1]


NA_INVALID_ROW = 2 * NA_WIN_ROWS - 1


def _na_window_bias(bt_ref, rb, n_rows):
    half = NA_WIN_ROWS // 2
    ks_row = jnp.clip(rb * NA_Q_ROWS - half, 0, n_rows - NA_K_ROWS)
    lane_lo = lax.broadcasted_iota(jnp.int32, (GRID_W, LANE), 1) < GRID_W
    strips = []
    for i in range(NA_Q_ROWS):
        rq = rb * NA_Q_ROWS + i
        rs = jnp.clip(rq - half, 0, n_rows - NA_WIN_ROWS)

        def slab(j, rq=rq, rs=rs):
            rk = ks_row + j
            valid = (rk >= rs) & (rk < rs + NA_WIN_ROWS)
            return bt_ref[jnp.where(valid, rk - rq + (NA_WIN_ROWS - 1), NA_INVALID_ROW)]

        pieces = [jnp.where(lane_lo, slab(2 * m), slab(2 * m + 1)) for m in range(NA_K_ROWS // 2)]
        strips.append(jnp.concatenate(pieces, axis=-1))
    return jnp.concatenate(strips, axis=0)


def _na_kernel(q_ref, k_ref, v_ref, kc_ref, vc_ref, kca_ref, vca_ref, bt_ref, o_ref, *,
               n_lat_steps, steps_per_batch, n_rows, n_sub, ctx, hps):
    i = pl.program_id(1)
    sq = NA_Q_ROWS * GRID_W
    dq, dv = NA_HEAD_DIM, V_HEAD_PAD

    @pl.when(i < n_lat_steps)
    def _():
        step = i % steps_per_batch
        for hh in range(hps):
            qc, vc = slice(hh * dq, (hh + 1) * dq), slice(hh * dv, (hh + 1) * dv)
            for sb in range(n_sub):
                rb = step * n_sub + sb
                ks = jnp.clip(rb * NA_Q_ROWS - NA_WIN_ROWS // 2, 0, n_rows - NA_K_ROWS) * GRID_W
                ks = pl.multiple_of(ks, GRID_W)
                kw = k_ref[pl.ds(ks, NA_K_ROWS * GRID_W), qc]
                vw = v_ref[pl.ds(ks, NA_K_ROWS * GRID_W), vc]
                rows = slice(sb * sq, (sb + 1) * sq)
                o_ref[rows, qc] = _softmax_attend(q_ref[rows, qc], [(kw, vw), (kc_ref[:, qc], vc_ref[:, vc])],
                                                  _na_window_bias(bt_ref.at[hh], rb, n_rows))

    @pl.when(i == n_lat_steps)
    def _():
        for hh in range(hps):
            qc, vc = slice(hh * dq, (hh + 1) * dq), slice(hh * dv, (hh + 1) * dv)
            for bb in range(q_ref.shape[0] // ctx):
                rows = slice(bb * ctx, (bb + 1) * ctx)
                o_ref[rows, qc] = _softmax_attend(q_ref[rows, qc], [(kca_ref[rows, qc], vca_ref[rows, vc])])


def _na_attention(naq, nak, nav, bias_tab, layer, geo, with_ctx):
    b, t, ctx = geo["b"], geo["t"], geo["ctx"]
    n_tok = naq.shape[0]
    sq = NA_Q_ROWS * GRID_W
    tq = b * ctx
    assert tq % sq == 0 and t % tq == 0
    n_sub = tq // sq
    n_rows = t // GRID_W
    spb = t // tq
    lat_steps = b * spb
    ctx_block0 = b * t // ctx
    batch = lambda i: jnp.minimum(i // spb, b - 1)
    hps = 2
    dq, dv = hps * NA_HEAD_DIM, hps * V_HEAD_PAD

    return pl.pallas_call(
        functools.partial(_na_kernel, n_lat_steps=lat_steps, steps_per_batch=spb, n_rows=n_rows,
                          n_sub=n_sub, ctx=ctx, hps=hps),
        grid=(N_HEADS // hps, lat_steps + (1 if with_ctx else 0)),
        in_specs=[pl.BlockSpec((tq, dq), lambda h, i: (i, h)),
                  pl.BlockSpec((t, dq), lambda h, i: (batch(i), h)),
                  pl.BlockSpec((t, dv), lambda h, i: (batch(i), h)),
                  pl.BlockSpec((ctx, dq), lambda h, i: (ctx_block0 + batch(i), h)),
                  pl.BlockSpec((ctx, dv), lambda h, i: (ctx_block0 + batch(i), h)),
                  pl.BlockSpec((tq, dq), lambda h, i: (lat_steps, h)),
                  pl.BlockSpec((tq, dv), lambda h, i: (lat_steps, h)),
                  pl.BlockSpec((None, hps) + bias_tab.shape[2:], lambda h, i: (layer, h, 0, 0, 0))],
        out_specs=pl.BlockSpec((tq, dq), lambda h, i: (i, h)),
        out_shape=jax.ShapeDtypeStruct((n_tok, GROUP_WIDTH), F32),
        compiler_params=_cparams(("arbitrary", "arbitrary"), 40 * 1024 * 1024),
        name="na_attention",
    )(naq, nak, nav, nak, nav, nak, nav, bias_tab)


def _flash_attend(q, loads):
    m = None
    acc = None
    for load in loads:
        k, v = load()
        s = _dot_nt(q, k)
        mc = s.max(axis=-1, keepdims=True)
        if m is None:
            m = mc
            acc = _dot(jnp.exp2(s - m).astype(BF16), v)
        else:
            m_new = jnp.maximum(m, mc)
            acc = acc * jnp.exp2(m - m_new) + _dot(jnp.exp2(s - m_new).astype(BF16), v)
            m = m_new
    dv = acc.shape[-1] // 2
    return acc[:, :dv] / acc[:, dv:dv + 1]


def _mla_kernel(q_ref, k_ref, v_ref, kc_ref, vc_ref, kca_ref, vca_ref, *refs,
                n_q_blocks, key_chunk, ctx, hps, n_ada_blocks, n_w2_blocks):
    i = pl.program_id(1)
    dq, dv = MLA_HEAD_PAD, V_HEAD_PAD
    step = pl.program_id(0) * pl.num_programs(1) + i
    if n_ada_blocks:
        c_ref, aw_ref, ab_ref, w2_ref, o_ref, w2b_ref, mod_ref = refs

        @pl.when(step < n_ada_blocks)
        def _():
            _ada_kernel(c_ref, aw_ref, ab_ref, mod_ref)
    else:
        w2_ref, o_ref, w2b_ref = refs

    @pl.when(step < n_w2_blocks)
    def _():
        w2b_ref[...] = w2_ref[...].astype(BF16)

    @pl.when(i < n_q_blocks)
    def _():
        for hh in range(hps):
            qc, vc = slice(hh * dq, (hh + 1) * dq), slice(hh * dv, (hh + 1) * dv)

            def lat_chunk(j, qc=qc, vc=vc):
                rows = slice(j * key_chunk, (j + 1) * key_chunk)
                return lambda: (k_ref[rows, qc], v_ref[rows, vc])

            loads = [lat_chunk(j) for j in range(k_ref.shape[0] // key_chunk)]
            loads.append(lambda qc=qc, vc=vc: (kc_ref[:, qc], vc_ref[:, vc]))
            o_ref[:, hh * MLA_V:(hh + 1) * MLA_V] = _flash_attend(q_ref[:, qc], loads)

    @pl.when(i == n_q_blocks)
    def _():
        for hh in range(hps):
            qc, vc = slice(hh * dq, (hh + 1) * dq), slice(hh * dv, (hh + 1) * dv)
            for bb in range(q_ref.shape[0] // ctx):
                rows = slice(bb * ctx, (bb + 1) * ctx)
                o_ref[rows, hh * MLA_V:(hh + 1) * MLA_V] = _flash_attend(
                    q_ref[rows, qc], [lambda: (kca_ref[rows, qc], vca_ref[rows, vc])])


def _mla_attention(mq, mk, mv, w2, layer, geo, with_ctx, ada=None):
    b, t, ctx = geo["b"], geo["t"], geo["ctx"]
    n_tok = mq.shape[0]
    tq = b * ctx
    assert t % tq == 0
    nqb = t // tq
    lat_blocks = b * nqb
    ctx_block0 = b * t // ctx
    batch = lambda i: jnp.minimum(i // nqb, b - 1)
    hps = 2
    dq, dv = hps * MLA_HEAD_PAD, hps * V_HEAD_PAD
    grid = (N_HEADS // hps, lat_blocks + (1 if with_ctx else 0))
    n_steps = grid[0] * grid[1]
    step = lambda h, i: h * grid[1] + i

    in_specs = [pl.BlockSpec((tq, dq), lambda h, i: (i, h)),
                pl.BlockSpec((t, dq), lambda h, i: (batch(i), h)),
                pl.BlockSpec((t, dv), lambda h, i: (batch(i), h)),
                pl.BlockSpec((ctx, dq), lambda h, i: (ctx_block0 + batch(i), h)),
                pl.BlockSpec((ctx, dv), lambda h, i: (ctx_block0 + batch(i), h)),
                pl.BlockSpec((tq, dq), lambda h, i: (lat_blocks, h)),
                pl.BlockSpec((tq, dv), lambda h, i: (lat_blocks, h))]
    operands = [mq, mk, mv, mk, mv, mk, mv]
    out_specs = [pl.BlockSpec((tq, hps * MLA_V), lambda h, i: (i, h))]
    out_shape = [jax.ShapeDtypeStruct((n_tok, GROUP_WIDTH), F32)]
    n_ada_blocks = 0
    if ada is not None:
        cvec, ada_w, ada_b, ada_layer = ada
        n_layers, d, n = ada_w.shape
        n_ada_blocks = max(k for k in range(1, n_steps + 1) if n % (k * LANE) == 0)
        tn = n // n_ada_blocks
        ada_slab = lambda h, i: jnp.minimum(step(h, i), n_ada_blocks - 1)
        in_specs += [pl.BlockSpec((8, d), lambda h, i: (0, 0)),
                     pl.BlockSpec((None, d, tn), lambda h, i: (ada_layer, 0, ada_slab(h, i))),
                     pl.BlockSpec((None, 1, tn), lambda h, i: (ada_layer, 0, ada_slab(h, i)))]
        operands += [cvec, ada_w, ada_b.reshape(n_layers, 1, n)]

    hidden, d_out = w2.shape[1:]
    slab = next(r for r in range(16, hidden + 1, 16) if hidden % r == 0 and hidden // r <= n_steps)
    n_w2_blocks = hidden // slab
    w2_block = lambda h, i: jnp.minimum(step(h, i), n_w2_blocks - 1)
    in_specs.append(pl.BlockSpec((None, slab, d_out), lambda h, i: (layer, w2_block(h, i), 0)))
    operands.append(w2)
    out_specs.append(pl.BlockSpec((slab, d_out), lambda h, i: (w2_block(h, i), 0)))
    out_shape.append(jax.ShapeDtypeStruct((hidden, d_out), BF16))
    if ada is not None:
        out_specs.append(pl.BlockSpec((8, tn), lambda h, i: (0, ada_slab(h, i))))
        out_shape.append(jax.ShapeDtypeStruct((8, n), F32))

    outs = pl.pallas_call(
        functools.partial(_mla_kernel, n_q_blocks=lat_blocks, key_chunk=512, ctx=ctx, hps=hps,
                          n_ada_blocks=n_ada_blocks, n_w2_blocks=n_w2_blocks),
        grid=grid,
        in_specs=in_specs,
        out_specs=out_specs,
        out_shape=out_shape,
        compiler_params=_cparams(("arbitrary", "arbitrary"), VMEM_LIMIT),
        name="mla_attention",
    )(*operands)
    mod_next = None if ada is None else _ada_rows(outs[2], ada[1].shape[1])
    return outs[0], outs[1], mod_next


S5_OCTET = LANE // S5_GROUP


def _s5_chunk_permutation():
    r = jnp.arange(S5_CHUNK * LANE)
    s, k, i = r // LANE, (r % LANE) // S5_GROUP, r % S5_GROUP
    dst = k * S5_BLOCK + s * S5_GROUP + i
    return (dst[:, None] == jnp.arange(S5_OCTET * S5_BLOCK)[None, :]).astype(BF16)


def _s5_contrib_kernel(p_ref, perm_ref, b_ref, o_ref, ub_ref, x_ref):
    rows = x_ref.shape[0]
    for s in range(S5_CHUNK):
        x_ref[:, s * LANE:(s + 1) * LANE] = p_ref[pl.ds(s, rows, stride=S5_CHUNK), :].astype(BF16)
    ub_ref[...] = _dot(x_ref[...], perm_ref[...]).astype(BF16)
    r = [_dot(ub_ref[:, k * S5_BLOCK:(k + 1) * S5_BLOCK], b_ref[k]) for k in range(S5_OCTET)]
    ps = S5_STATE
    for plane in range(4):
        o_ref[plane] = jnp.concatenate([rk[:, plane * ps:(plane + 1) * ps] for rk in r], axis=-1)


def _s5_scan_kernel(c_ref, lr_ref, li_ref, x_ref, *, n_tiles, n_ctx_tiles, nb):
    sub = 8
    n_lat_tiles = n_tiles - n_ctx_tiles
    lanes = c_ref.shape[-1]
    lr = jnp.broadcast_to(lr_ref[...], (sub, lanes))
    li = jnp.broadcast_to(li_ref[...], (sub, lanes))
    row = lax.broadcasted_iota(jnp.int32, (sub, lanes), 0)

    def run(reverse):
        order = range(sub - 1, -1, -1) if reverse else range(sub)

        def body(jt, carry):
            if reverse:
                tile = jnp.where(jt < n_ctx_tiles, n_ctx_tiles - 1 - jt, n_tiles - 1 - (jt - n_ctx_tiles))
            else:
                tile = jt
            new = []
            for bi in range(nb):
                xr, xi = carry[2 * bi], carry[2 * bi + 1]
                row_tile = jnp.where(tile < n_ctx_tiles, nb * n_lat_tiles + bi * n_ctx_tiles + tile,
                                     bi * n_lat_tiles + tile - n_ctx_tiles)
                r0 = pl.multiple_of(row_tile * sub, sub)
                cr = c_ref[0, pl.ds(r0, sub), :]
                ci = c_ref[1, pl.ds(r0, sub), :]
                out_r = jnp.zeros_like(cr)
                out_i = jnp.zeros_like(ci)
                for k in order:
                    out_r = jnp.where(row == k, xr, out_r)
                    out_i = jnp.where(row == k, xi, out_i)
                    ck_r = jnp.broadcast_to(cr[k:k + 1, :], (sub, lanes))
                    ck_i = jnp.broadcast_to(ci[k:k + 1, :], (sub, lanes))
                    xr, xi = lr * xr - li * xi + ck_r, lr * xi + li * xr + ck_i
                x_ref[0, pl.ds(r0, sub), :] = out_r
                x_ref[1, pl.ds(r0, sub), :] = out_i
                new += [xr, xi]
            return tuple(new)

        zero = jnp.zeros((sub, lanes), F32)
        lax.fori_loop(0, n_tiles, body, (zero,) * (2 * nb))

    @pl.when(pl.program_id(0) == 0)
    def _():
        run(False)

    @pl.when(pl.program_id(0) == 1)
    def _():
        run(True)


def _s5_out_kernel(u_ref, x_ref, t_ref, ct_ref, o_ref, y_ref):
    rows = u_ref.shape[0]
    ps = S5_STATE
    for k in range(S5_OCTET):
        x = jnp.concatenate([x_ref[plane][:, k * ps:(k + 1) * ps] for plane in range(4)], axis=-1)
        cols = slice(k * S5_BLOCK, (k + 1) * S5_BLOCK)
        y_ref[:, cols] = _dot(u_ref[:, cols], t_ref[k]) + _dot_nt(x.astype(BF16), ct_ref[k])
    for t in range(S5_CHUNK):
        piece = jnp.concatenate([y_ref[:, k * S5_BLOCK + t * S5_GROUP:k * S5_BLOCK + (t + 1) * S5_GROUP]
                                 for k in range(S5_OCTET)], axis=-1)
        o_ref[pl.ds(t, rows, stride=S5_CHUNK), :] = piece


def _s5_mixer(p, ops, layer, geo):
    b, t, ctx = geo["b"], geo["t"], geo["ctx"]
    n_chunks = (ctx + t) // S5_CHUNK
    rows = n_chunks * b
    n_tok = p.shape[0]
    assert rows * S5_CHUNK == n_tok
    n_oct = S5_GROUPS // S5_OCTET
    state_w = S5_GROUPS * S5_STATE
    oct_w = S5_OCTET * S5_BLOCK
    oct_states = S5_OCTET * S5_STATE
    sub = 8
    assert (t // S5_CHUNK) % sub == 0 and (ctx // S5_CHUNK) % sub == 0
    p_spec = pl.BlockSpec((n_tok, LANE), lambda o: (0, COL_S5 // LANE + o))
    op_spec = pl.BlockSpec((None, S5_OCTET, S5_BLOCK, S5_BLOCK), lambda o: (layer, o, 0, 0))
    plane_spec = pl.BlockSpec((4, rows, oct_states), lambda o: (0, 0, o))

    u_spec = pl.BlockSpec((rows, oct_w), lambda o: (0, o))
    contrib, u_chunks = pl.pallas_call(
        _s5_contrib_kernel,
        grid=(n_oct,),
        in_specs=[p_spec, pl.BlockSpec((S5_CHUNK * LANE, oct_w), lambda o: (0, 0)), op_spec],
        out_specs=[plane_spec, u_spec],
        out_shape=[jax.ShapeDtypeStruct((4, rows, state_w), F32),
                   jax.ShapeDtypeStruct((rows, S5_GROUPS * S5_BLOCK), BF16)],
        scratch_shapes=[pltpu.VMEM((rows, S5_CHUNK * LANE), BF16)],
        compiler_params=_cparams(("arbitrary",), 48 * 1024 * 1024),
        name="s5_contrib",
    )(p, _s5_chunk_permutation(), ops["b_mat"])

    lane_blk = 512
    states = pl.pallas_call(
        functools.partial(_s5_scan_kernel, n_tiles=n_chunks // sub, n_ctx_tiles=ctx // S5_CHUNK // sub, nb=b),
        grid=(2, state_w // lane_blk),
        in_specs=[pl.BlockSpec((None, 2, rows, lane_blk), lambda d, l: (d, 0, 0, l)),
                  pl.BlockSpec((None, None, 1, lane_blk), lambda d, l: (layer, d, 0, l)),
                  pl.BlockSpec((None, None, 1, lane_blk), lambda d, l: (layer, d, 0, l))],
        out_specs=pl.BlockSpec((None, 2, rows, lane_blk), lambda d, l: (d, 0, 0, l)),
        out_shape=jax.ShapeDtypeStruct((2, 2, rows, state_w), F32),
        compiler_params=_cparams(("arbitrary", "arbitrary"), 40 * 1024 * 1024),
        name="s5_scan",
    )(contrib.reshape(2, 2, rows, state_w), ops["decay_re"], ops["decay_im"])

    return pl.pallas_call(
        _s5_out_kernel,
        grid=(n_oct,),
        in_specs=[u_spec, plane_spec, op_spec, op_spec],
        out_specs=pl.BlockSpec((n_tok, LANE), lambda o: (0, o)),
        out_shape=jax.ShapeDtypeStruct((n_tok, GROUP_WIDTH), F32),
        scratch_shapes=[pltpu.VMEM((rows, oct_w), F32)],
        compiler_params=_cparams(("arbitrary",), 48 * 1024 * 1024),
        name="s5_out",
    )(u_chunks, states.reshape(4, rows, state_w), ops["t_sum"], ops["c_mat_t"])


def _s5_operators(lam_re, lam_im, log_dt, b_re, b_im, c_re, c_im):
    hp = lax.Precision.HIGHEST
    g, pn, ni, lc = S5_GROUPS, S5_STATE, S5_GROUP, S5_CHUNK
    dt = jnp.exp(log_dt)[..., None]
    zr, zi = lam_re * dt, lam_im * dt
    up = jnp.arange(lc, dtype=F32)
    down = (lc - 1) - up

    def powers(d, steps):
        mag = jnp.exp(zr[d][None] * steps[:, None, None])
        ang = zi[d][None] * steps[:, None, None]
        return mag * jnp.cos(ang), mag * jnp.sin(ang)

    one = jnp.ones((1,), F32)
    z1 = [powers(d, one) for d in range(2)]
    nr = jnp.stack([z1[0][0][0], z1[1][0][0]]) - 1.0
    nim = jnp.stack([z1[0][1][0], z1[1][1][0]])
    den = lam_re * lam_re + lam_im * lam_im
    cr_, ci_ = (nr * lam_re + nim * lam_im) / den, (nim * lam_re - nr * lam_im) / den
    bz_r = cr_[..., None] * b_re - ci_[..., None] * b_im
    bz_i = cr_[..., None] * b_im + ci_[..., None] * b_re

    def lag_kernel(d, steps):
        pr, pi = powers(d, steps)
        m_r = pr[..., None] * bz_r[d][None] - pi[..., None] * bz_i[d][None]
        m_i = pr[..., None] * bz_i[d][None] + pi[..., None] * bz_r[d][None]
        k = (jnp.einsum("gop,dgpi->dgoi", c_re[d], m_r, precision=hp)
             - jnp.einsum("gop,dgpi->dgoi", c_im[d], m_i, precision=hp))
        return k.transpose(1, 3, 0, 2)

    k_f = lag_kernel(0, up)
    k_b = lag_kernel(1, down)
    two_sided = jnp.concatenate([k_b[:, :, :lc - 1], k_f[:, :, :1] + k_b[:, :, lc - 1:], k_f[:, :, 1:]], axis=2)
    two_sided = two_sided.reshape(g, ni, (2 * lc - 1) * ni)
    t_sum = jnp.stack([two_sided[:, :, (lc - 1 - s) * ni:(lc - 1 - s) * ni + S5_BLOCK] for s in range(lc)], axis=1)
    t_sum = t_sum.reshape(g, S5_BLOCK, S5_BLOCK)

    per_dir = lambda a: jnp.stack([a[0], a[0], a[1], a[1]])
    by_pos = lambda planes: jnp.stack(planes).transpose(2, 1, 0, 3).reshape(g, lc, 1, 4 * pn)
    by_chan = lambda a: per_dir(a).transpose(1, 2, 0, 3).reshape(g, 1, ni, 4 * pn)

    ef_r, ef_i = powers(0, down)
    eb_r, eb_i = powers(1, up)
    bz_ri = (bz_r.transpose(0, 1, 3, 2), bz_i.transpose(0, 1, 3, 2))
    b_mat = (by_pos([ef_r, ef_i, eb_r, eb_i]) * by_chan(bz_ri[0])
             + by_pos([-ef_i, ef_r, -eb_i, eb_r]) * by_chan(bz_ri[1])).reshape(g, S5_BLOCK, 4 * pn)

    pf_r, pf_i = powers(0, up + 1.0)
    pb_r, pb_i = powers(1, lc - up)
    c_mat_t = (by_pos([pf_r, -pf_i, pb_r, -pb_i]) * by_chan(c_re)
               + by_pos([-pf_i, -pf_r, -pb_i, -pb_r]) * by_chan(c_im)).reshape(g, S5_BLOCK, 4 * pn)

    full = jnp.full((1,), float(lc), F32)
    decay = [powers(d, full) for d in range(2)]
    return {"t_sum": t_sum.astype(BF16), "b_mat": b_mat.astype(BF16), "c_mat_t": c_mat_t.astype(BF16),
            "decay_re": jnp.stack([decay[0][0], decay[1][0]]).reshape(2, 1, g * pn),
            "decay_im": jnp.stack([decay[0][1], decay[1][1]]).reshape(2, 1, g * pn)}


def _group_norm_store(o_ref, k, y, w_ref):
    cols = slice(k * GROUP_WIDTH, (k + 1) * GROUP_WIDTH)
    o_ref[:, cols] = _rms(y, w_ref[:, cols]).astype(o_ref.dtype)


def _mix_kernel(ya_ref, yb_ref, gb_ref, gc_ref, u_ref, gcp_ref, up_ref, gcn_ref, un_ref,
                cw_ref, cb_ref, ys_ref, us_ref, ds_ref, gw_ref, gbias_ref, mw_ref, o_ref, *, tm, lat_rows, t, ctx):
    i = pl.program_id(0)
    _group_norm_store(o_ref, 0, ya_ref[...], mw_ref)
    _group_norm_store(o_ref, 1, yb_ref[...], mw_ref)

    r0 = i * tm
    in_lat = r0 < lat_rows
    seq_len = jnp.where(in_lat, t, ctx)
    off = jnp.where(in_lat, r0, r0 - lat_rows) % seq_len
    has_prev = off != 0
    has_next = off + tm != seq_len
    v = gc_ref[...] * u_ref[...]
    v_prev_row = jnp.where(has_prev, gcp_ref[7:8, :] * up_ref[7:8, :], 0.0)
    v_next_row = jnp.where(has_next, gcn_ref[0:1, :] * un_ref[0:1, :], 0.0)
    row = lax.broadcasted_iota(jnp.int32, v.shape, 0)
    v_prev = jnp.where(row == 0, v_prev_row, pltpu.roll(v, 1, 0))
    v_next = jnp.where(row == tm - 1, v_next_row, pltpu.roll(v, tm - 1, 0))
    conv = cw_ref[0:1, :] * v_prev + cw_ref[1:2, :] * v + cw_ref[2:3, :] * v_next + cb_ref[...]
    _group_norm_store(o_ref, 2, gb_ref[...] * conv, mw_ref)

    y = ys_ref[...] + ds_ref[...] * us_ref[...]
    g = 0.5 * y * (1.0 + jnp.tanh(math.sqrt(2.0 / math.pi) * (y + 0.044715 * (y * y * y))))
    gate = _sigmoid(_dot(g.astype(BF16), gw_ref[...]) + gbias_ref[...])
    _group_norm_store(o_ref, 3, g * gate, mw_ref)


def _mix_outputs(ya, yb, p, ys5, lw, layer, geo, n_tiles):
    n_tok = ya.shape[0]
    tm = 256
    gw = GROUP_WIDTH
    halo = 8
    n_halo_blocks = n_tok // halo
    tile = lambda col: pl.BlockSpec((tm, gw), lambda i: (i, col // gw))
    prev = lambda col: pl.BlockSpec((halo, gw), lambda i: (jnp.maximum(i * (tm // halo) - 1, 0), col // gw))
    nxt = lambda col: pl.BlockSpec(
        (halo, gw), lambda i: (jnp.minimum((i + 1) * (tm // halo), n_halo_blocks - 1), col // gw))
    full = lambda a: _layer_full_spec(a, layer)
    c_gb, c_gc, c_u = COL_CONV, COL_CONV + gw, COL_CONV + 2 * gw
    weights = [lw["conv_w"], lw["conv_b"]]
    glu = [lw["s5_d"], lw["s5_glu_w"], lw["s5_glu_b"], lw["mix_norm_w"]]
    return pl.pallas_call(
        functools.partial(_mix_kernel, tm=tm, lat_rows=geo["b"] * geo["t"], t=geo["t"], ctx=geo["ctx"]),
        grid=(n_tiles * geo["tm"] // tm,),
        in_specs=[tile(0), tile(0), tile(c_gb), tile(c_gc), tile(c_u),
                  prev(c_gc), prev(c_u), nxt(c_gc), nxt(c_u)]
        + [full(a) for a in weights] + [tile(0), tile(COL_S5)] + [full(a) for a in glu],
        out_specs=pl.BlockSpec((tm, 4 * gw), lambda i: (i, 0)),
        out_shape=jax.ShapeDtypeStruct((n_tiles * geo["tm"], 4 * gw), BF16),
        compiler_params=_cparams(("arbitrary",), 48 * 1024 * 1024),
        name="mix_outputs",
    )(ya, yb, p, p, p, p, p, p, p, *weights, ys5, p, *glu)


def _pad_last(a, width):
    return jnp.pad(a, [(0, 0)] * (a.ndim - 1) + [(0, width - a.shape[-1])])


def _stacked_weights(pr):
    n_layers = pr["mla_w_uq"].shape[0]
    w_uq = pr["mla_w_uq"].reshape(n_layers, MLA_Q_RANK, N_HEADS, MLA_QK)
    w_uq = _pad_last(w_uq, MLA_HEAD_PAD).reshape(n_layers, MLA_Q_RANK, N_HEADS * MLA_HEAD_PAD)
    w_ukv = pr["mla_w_ukv"].reshape(n_layers, MLA_KV_RANK, N_HEADS, MLA_NOPE + MLA_V)
    w_ukv = jnp.concatenate([w_ukv[..., :MLA_NOPE].reshape(n_layers, MLA_KV_RANK, -1),
                             w_ukv[..., MLA_NOPE:].reshape(n_layers, MLA_KV_RANK, -1)], axis=2)
    w_ukv = jnp.pad(w_ukv, ((0, 0), (0, MLA_KV_RANK_PAD - MLA_KV_RANK), (0, 0)))
    row = lambda a: a[:, None, :]
    return {
        "na_q_norm_w": row(pr["na_q_norm_w"]),
        "na_k_norm_w": row(pr["na_k_norm_w"]),
        "mla_cq_norm_w": row(pr["mla_cq_norm_w"]),
        "mla_ckv_norm_w": row(_pad_last(pr["mla_ckv_norm_w"], MLA_KV_RANK_PAD)),
        "mla_w_uq": w_uq.astype(BF16),
        "mla_w_ukv": w_ukv.astype(BF16),
        "mla_q_norm_w": row(_pad_last(pr["mla_q_norm_w"], MLA_HEAD_PAD)),
        "mla_k_norm_w": row(_pad_last(pr["mla_k_norm_w"], MLA_HEAD_PAD)),
        "conv_w": pr["conv_w"],
        "conv_b": row(pr["conv_b"]),
        "s5_d": row(pr["s5_d"]),
        "s5_glu_w": pr["s5_glu_w"].astype(BF16),
        "s5_glu_b": row(pr["s5_glu_b"]),
        "mix_norm_w": row(pr["mix_norm_w"]),
    }


def _layer_full_spec(a, layer):
    return pl.BlockSpec((None,) + a.shape[1:], lambda i: (layer,) + (0,) * (a.ndim - 1))


def _rope_tables(b, t, n_ctx):
    pos = jnp.arange(t, dtype=jnp.int32)
    row = (pos // GRID_W).astype(F32)
    col = (pos % GRID_W).astype(F32)
    n_freq = MLA_ROPE // 4
    inv_freq = ROPE_THETA ** (-jnp.arange(n_freq, dtype=F32) / n_freq)
    ang_r, ang_c = row[:, None] * inv_freq[None, :], col[:, None] * inv_freq[None, :]
    cos_r, sin_r, cos_c, sin_c = jnp.cos(ang_r), jnp.sin(ang_r), jnp.cos(ang_c), jnp.sin(ang_c)
    zeros = jnp.zeros((t, n_freq), F32)
    rest = LANE - MLA_ROPE
    cos_t = jnp.concatenate([cos_r, cos_r, cos_c, cos_c, jnp.ones((t, rest), F32)], axis=1)
    sina_t = jnp.concatenate([-sin_r, zeros, -sin_c, zeros, jnp.zeros((t, rest), F32)], axis=1)
    sinb_t = jnp.concatenate([zeros, sin_r, zeros, sin_c, jnp.zeros((t, rest), F32)], axis=1)
    n_c = b * n_ctx
    tables = []
    for tab, fill in ((cos_t, 1.0), (sina_t, 0.0), (sinb_t, 0.0)):
        tables.append(jnp.concatenate([jnp.tile(tab, (b, 1)), jnp.full((n_c, LANE), fill, F32)], axis=0))
    return tables


def _na_bias_tables(rpb):
    w = GRID_W
    qc = jnp.arange(w)
    cs = jnp.clip(qc - NA_WIN_COLS // 2, 0, w - NA_WIN_COLS)
    v_col = (qc[None, :] >= cs[:, None]) & (qc[None, :] < cs[:, None] + NA_WIN_COLS)
    d_col = jnp.clip(qc[None, :] - qc[:, None], -(NA_WIN_COLS - 1), NA_WIN_COLS - 1) + (NA_WIN_COLS - 1)
    sel_c = jax.nn.one_hot(d_col, 2 * NA_WIN_COLS - 1, dtype=F32)
    tab = jnp.einsum("hab,qkb->haqk", rpb * LOG2E, sel_c, precision=lax.Precision.HIGHEST)
    tab = jnp.where(v_col[None, None], tab, NEG_INF)
    tab = jnp.concatenate([tab, jnp.full((rpb.shape[0], 1, w, w), NEG_INF, F32)], axis=1)
    return jnp.concatenate([tab, tab], axis=-1)


def kernel(x, c, ctx, c_ctx, ada_w, ada_b, norm1_w, norm2_w, w_in, na_q_norm_w, na_k_norm_w, na_rpb, mla_cq_norm_w, mla_ckv_norm_w, mla_w_uq, mla_w_ukv, mla_q_norm_w, mla_k_norm_w, conv_w, conv_b, s5_lambda_re, s5_lambda_im, s5_log_dt, s5_b_re, s5_b_im, s5_c_re, s5_c_im, s5_d, s5_glu_w, s5_glu_b, mix_norm_w, w_out, ffn_w1, ffn_w3, ffn_w2):
    b, t, d = x.shape
    n_ctx = ctx.shape[1]
    n_layers = ada_w.shape[0]
    n_rows = t // GRID_W
    assert t % GRID_W == 0 and n_rows >= NA_K_ROWS and n_rows % NA_Q_ROWS == 0
    assert n_ctx == NA_Q_ROWS * GRID_W and t % n_ctx == 0 and b + 1 <= 8
    tm = 512 if (b * n_ctx) % 512 == 0 and t % 512 == 0 else 256
    tiles_per_batch = t // tm
    geo = {"b": b, "t": t, "ctx": n_ctx, "tm": tm,
           "tile_class": lambda i: jnp.minimum(i // tiles_per_batch, b)}
    lat_tiles = b * t // tm
    all_tiles = lat_tiles + b * n_ctx // tm

    pr = dict(na_q_norm_w=na_q_norm_w, na_k_norm_w=na_k_norm_w, mla_cq_norm_w=mla_cq_norm_w,
              mla_ckv_norm_w=mla_ckv_norm_w, mla_w_uq=mla_w_uq, mla_w_ukv=mla_w_ukv,
              mla_q_norm_w=mla_q_norm_w, mla_k_norm_w=mla_k_norm_w, conv_w=conv_w, conv_b=conv_b,
              s5_d=s5_d, s5_glu_w=s5_glu_w, s5_glu_b=s5_glu_b, mix_norm_w=mix_norm_w)

    h = (x.reshape(b * t, d), ctx.reshape(b * n_ctx, d))
    cvec = jnp.zeros((8, d), F32).at[:b].set(c).at[b].set(c_ctx)
    mod = _ada_mod(cvec, ada_w, ada_b, 0)
    rope = _rope_tables(b, t, n_ctx)
    lw = _stacked_weights(pr)
    na_bias = jax.vmap(_na_bias_tables)(na_rpb)
    s5_ops = jax.vmap(_s5_operators)(s5_lambda_re, s5_lambda_im, s5_log_dt, s5_b_re, s5_b_im, s5_c_re, s5_c_im)
    m_in = 1088 if (b * t + b * n_ctx) % 1088 == 0 else tm

    for l in range(n_layers):
        last = l == n_layers - 1
        n_tiles = lat_tiles if last else all_tiles

        a = _norm_modulate(h, norm1_w[l], mod, 0, 1, geo, all_tiles)
        p = _input_projection(a, w_in, l, m_in)
        naq, nak, nav, mq, mk, mv = _prep_qkv(p, lw, l, rope, 256)
        ya = _na_attention(naq, nak, nav, na_bias, l, geo, not last)
        yb, w2_bf16, mod_next = _mla_attention(mq, mk, mv, ffn_w2, l, geo, not last,
                                               None if last else (cvec, ada_w, ada_b, l + 1))
        ys5 = _s5_mixer(p, s5_ops, l, geo)
        y = _mix_outputs(ya, yb, p, ys5, lw, l, geo, n_tiles)
        h = _matmul_gated_residual(y, w_out, h, mod, 2, geo, n_tiles, 1024, layer=l, single_buffer_w=True)

        f = _norm_modulate(h, norm2_w[l], mod, 3, 4, geo, n_tiles)
        g = _ffn_in(f, ffn_w1, ffn_w3, l, geo, n_tiles, 512)
        h = _matmul_gated_residual(g, w2_bf16, h, mod, 5, geo, n_tiles, 1024, single_buffer_w=True)
        mod = mod_next

    return h.reshape(b, t, d)
```
